```python
import jax, jax.numpy as jnp
from jax import lax
import numpy as np

D_MODEL = 2048
BATCH = 4
SEQ = 2048
DEPTH = 1
DEC_BATCH = 128
DEC_SEQ = 8
PAST_LEN = 16384
PAGE_SIZE = 128

CHUNK = 128
D_A = D_MODEL // 2
N_GROUPS_A = 8
GW_A = D_A // N_GROUPS_A
D_B = D_MODEL // 2
CONV_W = 3
N_EXPERTS = 32
TOP_K = 4
D_FF = D_MODEL
SWIGLU_LIMIT = 7.0
SWIGLU_ALPHA = 1.702
MOE_BLOCK = 128
PLE_DIM = 256
EPS = 1e-6
IN_WIDTHS = (D_A, D_A, D_B, D_B, D_B, D_MODEL, D_MODEL)
IN_SPLITS = tuple(int(s) for s in np.cumsum(IN_WIDTHS)[:-1])
D_IN = int(sum(IN_WIDTHS))

kernel_name = "hybrid_chunkmlp_shortconv_moe_step"


def rms_norm(x, g):
    xf = x.astype(jnp.float32)
    xf = xf * lax.rsqrt(jnp.mean(xf * xf, axis=-1, keepdims=True) + EPS)
    return (xf * g.astype(jnp.float32)).astype(x.dtype)


def layer_norm(x, g, b):
    xf = x.astype(jnp.float32)
    mu = jnp.mean(xf, axis=-1, keepdims=True)
    xc = xf - mu
    xf = xc * lax.rsqrt(jnp.mean(xc * xc, axis=-1, keepdims=True) + EPS)
    return (xf * g.astype(jnp.float32) + b.astype(jnp.float32)).astype(x.dtype)


def spatial_gating(u, v, w_s, b_s):
    bt, t, _ = v.shape
    length = min(t, CHUNK)
    n_chunks = t // length
    vg = v.reshape(bt, n_chunks, length, N_GROUPS_A, GW_A)
    mask = jnp.tril(jnp.ones((length, length), dtype=bool))
    w = jnp.where(mask[None], w_s[:, :length, :length], jnp.zeros_like(w_s[:, :length, :length])).astype(v.dtype)
    bias = b_s[:, :length].T.astype(v.dtype)[None, None, :, :, None]
    s = jnp.einsum('gts,bnsgc->bntgc', w, vg) + bias
    return u * s.reshape(bt, t, D_A)


def short_conv(z, prev, conv_w, conv_b):
    t = z.shape[1]
    zp = jnp.concatenate([prev.astype(z.dtype), z], axis=1)
    out = conv_b.astype(z.dtype)
    for k in range(CONV_W):
        out = out + zp[:, k:k + t] * conv_w[k].astype(z.dtype)
    return out, zp[:, -(CONV_W - 1):]


def moe(h, w_router, b_router, w_gate_up, b_gate_up, w_down, b_down):
    n_tok, d = h.shape
    logits = h.astype(jnp.float32) @ w_router.astype(jnp.float32) + b_router.astype(jnp.float32)
    top_vals, top_idx = lax.top_k(logits, TOP_K)
    gates = jax.nn.softmax(top_vals, axis=-1).astype(h.dtype)
    n_assign = n_tok * TOP_K
    flat_e = top_idx.reshape(-1)
    order = jnp.argsort(flat_e)
    sorted_e = flat_e[order]
    sizes = jnp.bincount(flat_e, length=N_EXPERTS)
    starts = jnp.cumsum(sizes) - sizes
    padded = (sizes + MOE_BLOCK - 1) // MOE_BLOCK * MOE_BLOCK
    padded_end = jnp.cumsum(padded)
    padded_start = padded_end - padded
    dest = padded_start[sorted_e] + jnp.arange(n_assign) - starts[sorted_e]
    n_blocks = -(-n_assign // MOE_BLOCK) + N_EXPERTS
    n_slots = n_blocks * MOE_BLOCK
    slot_tok = jnp.full((n_slots,), n_tok, jnp.int32).at[dest].set((order // TOP_K).astype(jnp.int32))
    slot_w = jnp.zeros((n_slots,), h.dtype).at[dest].set(gates.reshape(-1)[order])
    block_e = jnp.minimum(jnp.searchsorted(padded_end, jnp.arange(n_blocks) * MOE_BLOCK, side='right'),
                          N_EXPERTS - 1)
    h_pad = jnp.concatenate([h, jnp.zeros((1, d), h.dtype)], axis=0)
    xb = h_pad[slot_tok].reshape(n_blocks, MOE_BLOCK, d)

    def expert_block(args):
        xblk, e = args
        gu = xblk @ w_gate_up[e] + b_gate_up[e]
        gate, up = jnp.split(gu, 2, axis=-1)
        gate = jnp.minimum(gate, SWIGLU_LIMIT)
        up = jnp.clip(up, -SWIGLU_LIMIT, SWIGLU_LIMIT)
        act = (up + 1) * (gate * jax.nn.sigmoid(gate * SWIGLU_ALPHA))
        return act @ w_down[e] + b_down[e]

    yb = lax.map(expert_block, (xb, block_e)).reshape(n_slots, d)
    return jax.ops.segment_sum(yb * slot_w[:, None], slot_tok, num_segments=n_tok + 1)[:n_tok]


def hybrid_layer(x, p, conv_prev, g_mix, w_in, ln_v_g, ln_v_b, w_s, b_s, conv_w, conv_b,
                 w_proj_a, w_proj_b, w_o, g_moe, w_router, b_router, w_gate_up, b_gate_up,
                 w_down, b_down, g_ple, w_ple, w_ple_gate):
    bt, t, d = x.shape
    h = rms_norm(x, g_mix)
    z = h @ w_in
    u, v, b_gate, c_gate, xin, g_a, g_b = jnp.split(z, IN_SPLITS, axis=-1)
    u = jax.nn.gelu(u)
    v = layer_norm(jax.nn.gelu(v), ln_v_g, ln_v_b)
    y_a = spatial_gating(u, v, w_s, b_s) @ w_proj_a
    conv_out, conv_new = short_conv(c_gate * xin, conv_prev, conv_w, conv_b)
    y_b = (b_gate * conv_out) @ w_proj_b
    mix = jax.nn.sigmoid(g_a) * y_a + jax.nn.sigmoid(g_b) * y_b
    x = x + mix @ w_o
    x = x + moe(rms_norm(x, g_moe).reshape(bt * t, d), w_router, b_router, w_gate_up, b_gate_up,
                w_down, b_down).reshape(bt, t, d)
    x = x + (p @ w_ple) * jax.nn.sigmoid(rms_norm(x, g_ple) @ w_ple_gate)
    return x, conv_new, v


def setup_inputs(seed: int = 0) -> dict:
    key = jax.random.key(seed)
    ks = jax.random.split(key, 32)
    f32 = jnp.float32

    def nrm(k, shape, scale):
        return jax.random.normal(k, shape, f32) * scale

    def gain(k, shape):
        return 1.0 + 0.05 * jax.random.normal(k, shape, f32)

    L = DEPTH
    return {
        "x_prompt": nrm(ks[0], (BATCH, SEQ, D_MODEL), 1.0),
        "x_sample": nrm(ks[1], (DEC_BATCH, DEC_SEQ, D_MODEL), 1.0),
        "state_conv": nrm(ks[2], (L, DEC_BATCH, CONV_W - 1, D_B), 0.5),
        "p_prompt": nrm(ks[3], (L, BATCH, SEQ, PLE_DIM), 1.0),
        "p_sample": nrm(ks[4], (L, DEC_BATCH, DEC_SEQ, PLE_DIM), 1.0),
        "g_mix": gain(ks[5], (L, D_MODEL)),
        "w_in": nrm(ks[6], (L, D_MODEL, D_IN), D_MODEL ** -0.5),
        "ln_v_g": gain(ks[7], (L, D_A)),
        "ln_v_b": nrm(ks[8], (L, D_A), 0.02),
        "w_s": nrm(ks[9], (L, N_GROUPS_A, CHUNK, CHUNK), CHUNK ** -0.5),
        "b_s": gain(ks[10], (L, N_GROUPS_A, CHUNK)),
        "conv_w": nrm(ks[11], (L, CONV_W, D_B), CONV_W ** -0.5),
        "conv_b": nrm(ks[12], (L, D_B), 0.02),
        "w_proj_a": nrm(ks[13], (L, D_A, D_MODEL), D_A ** -0.5),
        "w_proj_b": nrm(ks[14], (L, D_B, D_MODEL), D_B ** -0.5),
        "w_o": nrm(ks[15], (L, D_MODEL, D_MODEL), D_MODEL ** -0.5),
        "g_moe": gain(ks[16], (L, D_MODEL)),
        "w_router": nrm(ks[17], (L, D_MODEL, N_EXPERTS), D_MODEL ** -0.5),
        "b_router": nrm(ks[18], (L, N_EXPERTS), 0.01),
        "w_gate_up": nrm(ks[19], (L, N_EXPERTS, D_MODEL, 2 * D_FF), D_MODEL ** -0.5),
        "b_gate_up": nrm(ks[20], (L, N_EXPERTS, 2 * D_FF), 0.02),
        "w_down": nrm(ks[21], (L, N_EXPERTS, D_FF, D_MODEL), D_FF ** -0.5),
        "b_down": nrm(ks[22], (L, N_EXPERTS, D_MODEL), 0.02),
        "g_ple": gain(ks[23], (L, D_MODEL)),
        "w_ple": nrm(ks[24], (L, PLE_DIM, D_MODEL), PLE_DIM ** -0.5),
        "w_ple_gate": nrm(ks[25], (L, D_MODEL, D_MODEL), D_MODEL ** -0.5),
        "g_final": gain(ks[26], (D_MODEL,)),
    }


def reference(x_prompt, x_sample, state_conv, p_prompt, p_sample, g_mix, w_in, ln_v_g, ln_v_b,
              w_s, b_s, conv_w, conv_b, w_proj_a, w_proj_b, w_o, g_moe, w_router, b_router,
              w_gate_up, b_gate_up, w_down, b_down, g_ple, w_ple, w_ple_gate, g_final):
    xp = x_prompt
    xs = x_sample
    conv_p_list, conv_s_list, v_s_list = [], [], []
    for i in range(DEPTH):
        params = (g_mix[i], w_in[i], ln_v_g[i], ln_v_b[i], w_s[i], b_s[i], conv_w[i], conv_b[i],
                  w_proj_a[i], w_proj_b[i], w_o[i], g_moe[i], w_router[i], b_router[i],
                  w_gate_up[i], b_gate_up[i], w_down[i], b_down[i], g_ple[i], w_ple[i], w_ple_gate[i])
        prev_p = jnp.zeros((xp.shape[0], CONV_W - 1, D_B), xp.dtype)
        xp, conv_p, _ = hybrid_layer(xp, p_prompt[i], prev_p, *params)
        xs, conv_s, v_s = hybrid_layer(xs, p_sample[i], state_conv[i], *params)
        conv_p_list.append(conv_p)
        conv_s_list.append(conv_s)
        v_s_list.append(v_s)
    y_prompt = rms_norm(xp, g_final)
    y_sample = rms_norm(xs, g_final)
    state_conv_prompt = jnp.stack(conv_p_list, axis=0)
    state_conv_sample = jnp.stack(conv_s_list, axis=0)
    state_chunk_v_sample = jnp.stack(v_s_list, axis=0)
    return (y_prompt, y_sample, state_conv_prompt, state_conv_sample, state_chunk_v_sample)
```

```python
import functools

import jax
import jax.numpy as jnp
from jax import lax
from jax.experimental import pallas as pl
from jax.experimental.pallas import tpu as pltpu

F32 = jnp.float32
BF16 = jnp.bfloat16
I32 = jnp.int32

D_MODEL = 2048
BATCH = 4
SEQ = 2048
DEC_BATCH = 128
DEC_SEQ = 8
CHUNK = 128
D_A = D_MODEL // 2
N_GROUPS_A = 8
GW_A = D_A // N_GROUPS_A
D_B = D_MODEL // 2
CONV_W = 3
N_EXPERTS = 32
TOP_K = 4
D_FF = D_MODEL
SWIGLU_LIMIT = 7.0
SWIGLU_ALPHA = 1.702
PLE_DIM = 256
EPS = 1e-6

T_S = DEC_BATCH * DEC_SEQ
T_P = BATCH * SEQ
T = T_S + T_P

LANES = 128
V7X_VMEM_BYTES = 64 * 1024 * 1024
MIB = 1024 * 1024

NORM_TM = 512
IN_TM = 1536
IN_TN = 512
MIX_TM = CHUNK
MIX_NS = T_S // MIX_TM
MIX_SEQ_TILES = SEQ // MIX_TM
MOE_RB = 256
MOE_BMAX = 6
MOE_RMAX = MOE_RB * MOE_BMAX
MOE_TF = 256
MOE_TN = 512
MOE_P1 = D_FF // MOE_TF
MOE_P2 = D_MODEL // MOE_TN
N_SLOTS = T * TOP_K + N_EXPERTS * MOE_RB
N_BLOCKS = N_SLOTS // MOE_RB
MOE_NI = (N_BLOCKS + N_EXPERTS * (MOE_BMAX - 1)) // MOE_BMAX
GATHER_UNROLL = 8
CMB_TM = 256
CMB_NS = T_S // CMB_TM
NEG_BIG = -1e30


def _rms(x, g):
    return x * lax.rsqrt(jnp.mean(x * x, axis=-1, keepdims=True) + EPS) * g


def _cparams(sem, vmem_mib):
    return pltpu.CompilerParams(dimension_semantics=sem, vmem_limit_bytes=vmem_mib * MIB)


def _resident(shape):
    zeros = (0,) * len(shape)
    return pl.BlockSpec(shape, lambda *_: zeros, pipeline_mode=pl.Buffered(1))


def _norm_kernel(xs_ref, xp_ref, g_ref, h_ref, *, ns):
    m = pl.program_id(0)
    x = jnp.where(m < ns, xs_ref[...], xp_ref[...])
    h_ref[...] = _rms(x, g_ref[...]).astype(BF16)


def _norm_call(xs, xp, g):
    ns = T_S // NORM_TM
    return pl.pallas_call(
        functools.partial(_norm_kernel, ns=ns),
        grid=(T // NORM_TM,),
        in_specs=[
            pl.BlockSpec((NORM_TM, D_MODEL), lambda m: (jnp.minimum(m, ns - 1), 0)),
            pl.BlockSpec((NORM_TM, D_MODEL), lambda m: (jnp.maximum(m - ns, 0), 0)),
            pl.BlockSpec((1, D_MODEL), lambda m: (0, 0)),
        ],
        out_specs=pl.BlockSpec((NORM_TM, D_MODEL), lambda m: (m, 0)),
        out_shape=jax.ShapeDtypeStruct((T, D_MODEL), BF16),
        compiler_params=_cparams(("arbitrary",), 32),
        name="norm",
    )(xs, xp, g)


IN_N_GELU = 2 * D_A // IN_TN
IN_N_V0 = D_A // IN_TN
IN_N_LIN = (2 * D_A + 3 * D_B) // IN_TN


def _in_proj_kernel(h_ref, w_ref, z_ref, vpre_ref, acc_ref):
    n = pl.program_id(1)
    acc_ref[...] = jnp.dot(h_ref[...], w_ref[...].astype(BF16), preferred_element_type=F32)

    @pl.when(n < IN_N_GELU)
    def _():
        g = jax.nn.gelu(acc_ref[...], approximate=True)
        z_ref[...] = g.astype(BF16)

        @pl.when(n >= IN_N_V0)
        def _():
            vpre_ref[...] = g

    @pl.when((n >= IN_N_GELU) & (n < IN_N_LIN))
    def _():
        z_ref[...] = acc_ref[...].astype(BF16)

    @pl.when(n >= IN_N_LIN)
    def _():
        z_ref[...] = jax.nn.sigmoid(acc_ref[...]).astype(BF16)


def _in_proj_call(h, w_in):
    d_in = w_in.shape[1]
    return pl.pallas_call(
        _in_proj_kernel,
        grid=(T // IN_TM, d_in // IN_TN),
        in_specs=[
            pl.BlockSpec((IN_TM, D_MODEL), lambda m, n: (m, 0)),
            pl.BlockSpec((D_MODEL, IN_TN), lambda m, n: (0, n)),
        ],
        out_specs=[
            pl.BlockSpec((IN_TM, IN_TN), lambda m, n: (m, n)),
            pl.BlockSpec((IN_TM, IN_TN), lambda m, n: (m, jnp.clip(n - IN_N_V0, 0, IN_N_GELU - IN_N_V0 - 1))),
        ],
        out_shape=[
            jax.ShapeDtypeStruct((T, d_in), BF16),
            jax.ShapeDtypeStruct((T, D_A), F32),
        ],
        scratch_shapes=[pltpu.VMEM((IN_TM, IN_TN), F32)],
        compiler_params=_cparams(("arbitrary", "arbitrary"), 44),
        name="in_proj",
    )(h, w_in)


def _mixer_kernel(xs_ref, xp_ref, z_ref, vpre_ref, ext_ref, wsg_ref, bsg_ref, lng_ref, lnb_ref,
                  cw_ref, cb_ref, wa_ref, wb_ref, wo_ref, gmoe_ref, wrh_ref, wrl_ref, br_ref,
                  x1_ref, xn_ref, ri_ref, rg_ref, cnt_ref, vln_ref, cxs_ref, tail_ref,
                  prev_ref, carry_ref):
    m = pl.program_id(0)
    is_s = m < MIX_NS
    tm = MIX_TM

    @pl.when(m == 0)
    def _():
        prev_ref[...] = jnp.zeros_like(prev_ref)
        carry_ref[...] = jnp.zeros_like(carry_ref)

    vg = vpre_ref[...]
    mu = jnp.mean(vg, axis=-1, keepdims=True)
    vc = vg - mu
    v = vc * lax.rsqrt(jnp.mean(vc * vc, axis=-1, keepdims=True) + EPS) * lng_ref[...] + lnb_ref[...]

    @pl.when(is_s)
    def _():
        vln_ref[...] = v

    vb = v.astype(BF16)
    s_parts = []
    for g in range(N_GROUPS_A):
        s_parts.append(jnp.dot(wsg_ref[0, g], vb[:, g * GW_A:(g + 1) * GW_A], preferred_element_type=F32))
    s = jnp.concatenate(s_parts, axis=1) + bsg_ref[0]
    u = z_ref[:, 0:D_A].astype(F32)
    a_in = (u * s).astype(BF16)

    o_b = 2 * D_A
    bg = z_ref[:, o_b:o_b + D_B].astype(F32)
    cg = z_ref[:, o_b + D_B:o_b + 2 * D_B].astype(F32)
    xin = z_ref[:, o_b + 2 * D_B:o_b + 3 * D_B].astype(F32)
    cx = cg * xin

    @pl.when(is_s)
    def _():
        cxs_ref[...] = cx

    @pl.when(jnp.logical_not(is_s))
    def _():
        tail_ref[0] = cx[tm - 8:tm]

    row = lax.broadcasted_iota(I32, (tm, D_B), 0)
    seq_start = ((m - MIX_NS) % MIX_SEQ_TILES) == 0
    prev = jnp.where(seq_start, 0.0, prev_ref[...])
    row8 = lax.broadcasted_iota(I32, (8, D_B), 0)
    top = jnp.where(row8 < CONV_W - 1, pltpu.roll(prev, CONV_W - 1, 0), 0.0)
    ext_p = jnp.concatenate([top, jnp.zeros((tm - 8, D_B), F32)], axis=0)
    ext = jnp.where(is_s, ext_ref[...], ext_p)
    t_in = jnp.where(is_s, row & (DEC_SEQ - 1), row)
    s1 = jnp.where(t_in < 1, pltpu.roll(ext, tm - 1, 0), pltpu.roll(cx, 1, 0))
    s2 = jnp.where(t_in < 2, ext, pltpu.roll(cx, 2, 0))
    prev_ref[...] = cx[tm - 8:tm]
    conv = cb_ref[...] + s2 * cw_ref[0:1, :] + s1 * cw_ref[1:2, :] + cx * cw_ref[2:3, :]
    b_in = (bg * conv).astype(BF16)

    y_a = jnp.dot(a_in, wa_ref[...], preferred_element_type=F32)
    y_b = jnp.dot(b_in, wb_ref[...], preferred_element_type=F32)
    o_g = 2 * D_A + 3 * D_B
    ga = z_ref[:, o_g:o_g + D_MODEL].astype(F32)
    gb = z_ref[:, o_g + D_MODEL:o_g + 2 * D_MODEL].astype(F32)
    mix = (ga * y_a + gb * y_b).astype(BF16)
    x = jnp.where(is_s, xs_ref[...], xp_ref[...])
    x1 = x + jnp.dot(mix, wo_ref[...], preferred_element_type=F32)
    x1_ref[...] = x1

    xn = _rms(x1, gmoe_ref[...])
    xn_ref[...] = xn
    hi = xn.astype(BF16)
    lo = (xn - hi.astype(F32)).astype(BF16)
    logits = (jnp.dot(hi, wrh_ref[...], preferred_element_type=F32)
              + jnp.dot(lo, wrh_ref[...], preferred_element_type=F32)
              + jnp.dot(hi, wrl_ref[...], preferred_element_type=F32)) + br_ref[...]
    lane = lax.broadcasted_iota(I32, (tm, LANES), 1)
    lane_f = lane.astype(F32)
    vals, idxs, hots = [], [], []
    work = logits
    for _ in range(TOP_K):
        mx = jnp.max(work, axis=-1, keepdims=True)
        idx = jnp.min(jnp.where(work == mx, lane_f, float(LANES)), axis=-1, keepdims=True)
        hot = lane_f == idx
        work = jnp.where(hot, -jnp.inf, work)
        vals.append(mx)
        idxs.append(idx)
        hots.append(hot)
    exps = [jnp.exp(vk - vals[0]) for vk in vals]
    den = exps[0] + exps[1] + exps[2] + exps[3]

    chosen = jnp.zeros((tm, LANES), F32)
    for hot in hots:
        chosen = chosen + jnp.where(hot, 1.0, 0.0)
    r_i = lax.broadcasted_iota(I32, (tm, tm), 0)
    c_i = lax.broadcasted_iota(I32, (tm, tm), 1)
    tri = jnp.where(c_i < r_i, 1.0, 0.0).astype(BF16)
    before = jnp.dot(tri, chosen.astype(BF16), preferred_element_type=F32) + carry_ref[0:1, :]
    total = carry_ref[0:1, :] + jnp.sum(chosen, axis=0, keepdims=True)
    carry_ref[...] = jnp.broadcast_to(total, carry_ref.shape)
    cnt_ref[...] = jnp.broadcast_to(total, cnt_ref.shape)

    ri = jnp.zeros((tm, LANES), F32)
    rg = jnp.zeros((tm, LANES), F32)
    for k in range(TOP_K):
        rank = jnp.sum(jnp.where(hots[k], before, 0.0), axis=-1, keepdims=True)
        ri = ri + jnp.where(lane == k, idxs[k], 0.0) + jnp.where(lane == TOP_K + k, rank, 0.0)
        rg = rg + jnp.where(lane == k, exps[k] / den, 0.0)
    ri_ref[...] = ri.astype(I32)
    rg_ref[...] = rg


def _mixer_call(xs, xp, z, vpre, ext, wsg, bsg, ln_g, ln_b, conv_w, conv_b, wa, wb, wo, g_moe,
                wr_hi, wr_lo, b_r):
    tm = MIX_TM
    ns = MIX_NS
    d_in = z.shape[1]
    s_idx = lambda m: (jnp.minimum(m, ns - 1), 0)
    p_idx = lambda m: (jnp.maximum(m - ns, 0), 0)
    row = lambda m: (m, 0)
    return pl.pallas_call(
        _mixer_kernel,
        grid=(T // tm,),
        in_specs=[
            pl.BlockSpec((tm, D_MODEL), s_idx),
            pl.BlockSpec((tm, D_MODEL), p_idx),
            pl.BlockSpec((tm, d_in), row),
            pl.BlockSpec((tm, D_A), row),
            pl.BlockSpec((tm, D_B), s_idx),
            pl.BlockSpec((1, N_GROUPS_A, CHUNK, CHUNK), lambda m: (jnp.minimum(m // ns, 1), 0, 0, 0)),
            pl.BlockSpec((1, CHUNK, D_A), lambda m: (jnp.minimum(m // ns, 1), 0, 0)),
            _resident((1, D_A)),
            _resident((1, D_A)),
            _resident((CONV_W, D_B)),
            _resident((1, D_B)),
            _resident((D_A, D_MODEL)),
            _resident((D_B, D_MODEL)),
            _resident((D_MODEL, D_MODEL)),
            _resident((1, D_MODEL)),
            _resident((D_MODEL, LANES)),
            _resident((D_MODEL, LANES)),
            _resident((1, LANES)),
        ],
        out_specs=[
            pl.BlockSpec((tm, D_MODEL), row),
            pl.BlockSpec((tm, D_MODEL), row),
            pl.BlockSpec((tm, LANES), row),
            pl.BlockSpec((tm, LANES), row),
            pl.BlockSpec((8, LANES), lambda m: (0, 0)),
            pl.BlockSpec((tm, D_A), s_idx),
            pl.BlockSpec((tm, D_B), s_idx),
            pl.BlockSpec((1, 8, D_B), lambda m: (jnp.maximum(m - ns, 0), 0, 0)),
        ],
        out_shape=[
            jax.ShapeDtypeStruct((T, D_MODEL), F32),
            jax.ShapeDtypeStruct((T, D_MODEL), F32),
            jax.ShapeDtypeStruct((T, LANES), I32),
            jax.ShapeDtypeStruct((T, LANES), F32),
            jax.ShapeDtypeStruct((8, LANES), F32),
            jax.ShapeDtypeStruct((T_S, D_A), F32),
            jax.ShapeDtypeStruct((T_S, D_B), F32),
            jax.ShapeDtypeStruct((T_P // tm, 8, D_B), F32),
        ],
        scratch_shapes=[pltpu.VMEM((8, D_B), F32), pltpu.VMEM((8, LANES), F32)],
        compiler_params=_cparams(("arbitrary",), 52),
        name="mixer",
    )(xs, xp, z, vpre, ext, wsg, bsg, ln_g, ln_b, conv_w, conv_b, wa, wb, wo, g_moe, wr_hi, wr_lo, b_r)


def _moe_kernel(tok_ref, ite_ref, iten_ref, row0_ref, nblk_ref, used_ref,
                xn_hbm, wg_ref, wu_ref, wd_ref, bg_ref, bu_ref, bd_ref,
                y_hbm,
                xraw_ref, act_ref, wgu_ref, wdb_ref, ystage_ref, zbuf_ref, pend_ref, gsem, ysem, zsem):
    i = pl.program_id(0)
    s = pl.program_id(1)
    nblk = nblk_ref[i]
    row0 = row0_ref[i]
    valid = nblk > 0

    def tail_copy(b):
        r = pl.multiple_of(b * MOE_RB, MOE_RB)
        return pltpu.make_async_copy(zbuf_ref, y_hbm.at[pl.ds(r, MOE_RB), :], zsem)

    def x_block_copy():
        return pltpu.make_async_copy(xn_hbm.at[pl.ds(0, MOE_RB), :], xraw_ref.at[pl.ds(0, MOE_RB), :], gsem)

    def issue_gather(item):
        base = row0_ref[item]

        def body(c, carry):
            for j in range(GATHER_UNROLL):
                r = c * GATHER_UNROLL + j
                tok = tok_ref[base + r]
                pltpu.make_async_copy(xn_hbm.at[pl.ds(tok, 1), :], xraw_ref.at[pl.ds(r, 1), :], gsem).start()
            return carry

        lax.fori_loop(0, nblk_ref[item] * (MOE_RB // GATHER_UNROLL), body, 0)

    def y_copy(slot, r, col):
        return pltpu.make_async_copy(
            ystage_ref.at[slot],
            y_hbm.at[pl.ds(pl.multiple_of(row0 + r, MOE_RB), MOE_RB), pl.ds(col, MOE_TN)],
            ysem.at[slot])

    @pl.when((i == 0) & (s == 0))
    def _():
        pend_ref[0] = 0
        pend_ref[1] = 0
        pend_ref[2] = 0
        issue_gather(0)
        zbuf_ref[...] = jnp.zeros_like(zbuf_ref)

        def fill(b, carry):
            tail_copy(b).start()
            return carry

        lax.fori_loop(used_ref[0], N_BLOCKS, fill, 0)

    @pl.when(valid & (s == 0))
    def _():
        def body(b, carry):
            x_block_copy().wait()
            return carry

        lax.fori_loop(0, nblk, body, 0)

    @pl.when(valid & (s < MOE_P1))
    def _():
        wgu_ref[:, 0:MOE_TF] = wg_ref[0].astype(BF16)
        wgu_ref[:, MOE_TF:2 * MOE_TF] = wu_ref[0].astype(BF16)
        b_g = bg_ref[0]
        b_u = bu_ref[0]

        def body(b, carry):
            r = pl.multiple_of(b * MOE_RB, MOE_RB)
            xb = xraw_ref[pl.ds(r, MOE_RB), :].astype(BF16)
            gu = jnp.dot(xb, wgu_ref[...], preferred_element_type=F32)
            gate = jnp.minimum(gu[:, 0:MOE_TF] + b_g, SWIGLU_LIMIT)
            up = jnp.clip(gu[:, MOE_TF:2 * MOE_TF] + b_u, -SWIGLU_LIMIT, SWIGLU_LIMIT)
            act = (up + 1) * (gate * jax.nn.sigmoid(gate * SWIGLU_ALPHA))
            act_ref[s, pl.ds(r, MOE_RB), :] = act.astype(BF16)
            return carry

        lax.fori_loop(0, nblk, body, 0)

    @pl.when(valid & (s == MOE_P1) & (i + 1 < MOE_NI))
    def _():
        issue_gather(i + 1)

    @pl.when(valid & (s >= MOE_P1))
    def _():
        wdb_ref[...] = wd_ref[0].astype(BF16)
        b_d = bd_ref[0]
        col = pl.multiple_of((s - MOE_P1) * MOE_TN, MOE_TN)

        def body(b, carry):
            r = pl.multiple_of(b * MOE_RB, MOE_RB)
            a = jnp.concatenate([act_ref[j, pl.ds(r, MOE_RB), :] for j in range(MOE_P1)], axis=1)
            y = jnp.dot(a, wdb_ref[...], preferred_element_type=F32) + b_d
            ring = pend_ref[2]
            slot = ring % 2

            @pl.when(pend_ref[slot] > 0)
            def _():
                y_copy(slot, r, col).wait()

            ystage_ref[slot] = y
            y_copy(slot, r, col).start()
            pend_ref[slot] = 1
            pend_ref[2] = ring + 1
            return carry

        lax.fori_loop(0, nblk, body, 0)

    @pl.when((i == MOE_NI - 1) & (s == MOE_P1 + MOE_P2 - 1))
    def _():
        for slot in range(2):
            @pl.when(pend_ref[slot] > 0)
            def _():
                y_copy(slot, 0, 0).wait()

        def drain(b, carry):
            tail_copy(b).wait()
            return carry

        lax.fori_loop(used_ref[0], N_BLOCKS, drain, 0)


def _moe_call(slot_tok, it_e, it_enext, it_row0, it_nblk, n_used, xn, w_gate_up, w_down, b_gate_up, b_down):
    up0 = D_FF // MOE_TF

    def phase1_tile(i, s, nb):
        return jnp.where((s < MOE_P1) & (nb[i] > 0), s, 0)

    def gu_expert(i, s, e, en):
        return jnp.where(s < MOE_P1, e[i], en[i])

    def g_map(i, s, tok, e, en, r0, nb, used):
        return (gu_expert(i, s, e, en), 0, phase1_tile(i, s, nb))

    def u_map(i, s, tok, e, en, r0, nb, used):
        return (gu_expert(i, s, e, en), 0, up0 + phase1_tile(i, s, nb))

    def d_map(i, s, tok, e, en, r0, nb, used):
        return (e[i], 0, jnp.where((s >= MOE_P1) & (nb[i] > 0), s - MOE_P1, 0))

    grid_spec = pltpu.PrefetchScalarGridSpec(
        num_scalar_prefetch=6,
        grid=(MOE_NI, MOE_P1 + MOE_P2),
        in_specs=[
            pl.BlockSpec(memory_space=pl.ANY),
            pl.BlockSpec((1, D_MODEL, MOE_TF), g_map),
            pl.BlockSpec((1, D_MODEL, MOE_TF), u_map),
            pl.BlockSpec((1, D_FF, MOE_TN), d_map),
            pl.BlockSpec((1, 1, MOE_TF), g_map),
            pl.BlockSpec((1, 1, MOE_TF), u_map),
            pl.BlockSpec((1, 1, MOE_TN), d_map),
        ],
        out_specs=pl.BlockSpec(memory_space=pl.ANY),
        scratch_shapes=[
            pltpu.VMEM((MOE_RMAX, D_MODEL), F32),
            pltpu.VMEM((MOE_P1, MOE_RMAX, MOE_TF), BF16),
            pltpu.VMEM((D_MODEL, 2 * MOE_TF), BF16),
            pltpu.VMEM((D_FF, MOE_TN), BF16),
            pltpu.VMEM((2, MOE_RB, MOE_TN), F32),
            pltpu.VMEM((MOE_RB, D_MODEL), F32),
            pltpu.SMEM((3,), I32),
            pltpu.SemaphoreType.DMA(()),
            pltpu.SemaphoreType.DMA((2,)),
            pltpu.SemaphoreType.DMA(()),
        ],
    )
    return pl.pallas_call(
        _moe_kernel,
        grid_spec=grid_spec,
        out_shape=jax.ShapeDtypeStruct((N_SLOTS, D_MODEL), F32),
        compiler_params=_cparams(("arbitrary", "arbitrary"), 52),
        name="moe",
    )(slot_tok, it_e, it_enext, it_row0, it_nblk, n_used, xn, w_gate_up, w_gate_up, w_down,
      b_gate_up, b_gate_up, b_down)


def _combine_kernel(dest_ref, y_hbm, x1_ref, rg_ref, ps_ref, pp_ref, wple_ref, wpg_ref, gple_ref,
                    gfin_ref, ys_ref, yp_ref, gbuf_ref, gsem):
    m = pl.program_id(0)
    nm = pl.num_programs(0)
    tm = CMB_TM
    slot = m % 2

    def issue(tile, slot_):
        def body(c, carry):
            for j in range(GATHER_UNROLL // TOP_K):
                r = c * (GATHER_UNROLL // TOP_K) + j
                for k in range(TOP_K):
                    d = dest_ref[(tile * tm + r) * TOP_K + k]
                    pltpu.make_async_copy(y_hbm.at[pl.ds(d, 1), :], gbuf_ref.at[slot_, k, pl.ds(r, 1), :],
                                          gsem.at[slot_]).start()
            return carry

        lax.fori_loop(0, tm // (GATHER_UNROLL // TOP_K), body, 0)

    @pl.when(m == 0)
    def _():
        issue(0, 0)

    @pl.when(m + 1 < nm)
    def _():
        issue(m + 1, 1 - slot)

    for k in range(TOP_K):
        pltpu.make_async_copy(y_hbm.at[pl.ds(0, tm), :], gbuf_ref.at[slot, k], gsem.at[slot]).wait()

    gates = rg_ref[...]
    moe = gates[:, 0:1] * gbuf_ref[slot, 0]
    for k in range(1, TOP_K):
        moe = moe + gates[:, k:k + 1] * gbuf_ref[slot, k]
    x2 = x1_ref[...] + moe
    is_s = m < CMB_NS
    p = jnp.where(is_s, ps_ref[...], pp_ref[...]).astype(BF16)
    pe = jnp.dot(p, wple_ref[...], preferred_element_type=F32)
    hn = _rms(x2, gple_ref[...]).astype(BF16)
    gate = jax.nn.sigmoid(jnp.dot(hn, wpg_ref[...], preferred_element_type=F32))
    x3 = x2 + pe * gate
    y = _rms(x3, gfin_ref[...])

    @pl.when(is_s)
    def _():
        ys_ref[...] = y

    @pl.when(jnp.logical_not(is_s))
    def _():
        yp_ref[...] = y


def _combine_call(dest, y_sorted, x1, rg, ps, pp, wple, wpg, g_ple, g_final):
    tm = CMB_TM
    ns = CMB_NS
    s_idx = lambda m, d: (jnp.minimum(m, ns - 1), 0)
    p_idx = lambda m, d: (jnp.maximum(m - ns, 0), 0)
    row = lambda m, d: (m, 0)
    const2 = lambda m, d: (0, 0)
    grid_spec = pltpu.PrefetchScalarGridSpec(
        num_scalar_prefetch=1,
        grid=(T // tm,),
        in_specs=[
            pl.BlockSpec(memory_space=pl.ANY),
            pl.BlockSpec((tm, D_MODEL), row),
            pl.BlockSpec((tm, LANES), row),
            pl.BlockSpec((tm, PLE_DIM), s_idx),
            pl.BlockSpec((tm, PLE_DIM), p_idx),
            pl.BlockSpec((PLE_DIM, D_MODEL), const2),
            pl.BlockSpec((D_MODEL, D_MODEL), const2),
            pl.BlockSpec((1, D_MODEL), const2),
            pl.BlockSpec((1, D_MODEL), const2),
        ],
        out_specs=[
            pl.BlockSpec((tm, D_MODEL), s_idx),
            pl.BlockSpec((tm, D_MODEL), p_idx),
        ],
        scratch_shapes=[
            pltpu.VMEM((2, TOP_K, tm, D_MODEL), F32),
            pltpu.SemaphoreType.DMA((2,)),
        ],
    )
    return pl.pallas_call(
        _combine_kernel,
        grid_spec=grid_spec,
        out_shape=[
            jax.ShapeDtypeStruct((T_S, D_MODEL), F32),
            jax.ShapeDtypeStruct((T_P, D_MODEL), F32),
        ],
        compiler_params=_cparams(("arbitrary",), 56),
        name="combine",
    )(dest, y_sorted, x1, rg, ps, pp, wple, wpg, g_ple, g_final)


def _routing_tables(route_i, counts):
    e_idx = route_i[:, 0:TOP_K]
    rank = route_i[:, TOP_K:2 * TOP_K]
    cnt = counts[0, :N_EXPERTS].astype(I32)
    padded = (cnt + MOE_RB - 1) // MOE_RB * MOE_RB
    gend = jnp.cumsum(padded)
    gstart = gend - padded
    dest = (gstart[e_idx] + rank).astype(I32)
    tok_of = jnp.repeat(jnp.arange(T, dtype=I32), TOP_K)
    slot_tok = jnp.zeros((N_SLOTS,), I32).at[dest.reshape(-1)].set(tok_of)
    nblk_e = padded // MOE_RB
    items_e = (nblk_e + MOE_BMAX - 1) // MOE_BMAX
    iend = jnp.cumsum(items_e)
    istart = iend - items_e
    n_items = iend[-1]
    ids = jnp.arange(MOE_NI, dtype=I32)
    valid = ids < n_items
    e_of = jnp.minimum(jnp.searchsorted(iend, ids, side="right"), N_EXPERTS - 1).astype(I32)
    e_last = jnp.minimum(jnp.searchsorted(iend, n_items - 1, side="right"), N_EXPERTS - 1).astype(I32)
    it_e = jnp.where(valid, e_of, e_last)
    local = ids - istart[it_e]
    it_nblk = jnp.where(valid, jnp.minimum(MOE_BMAX, nblk_e[it_e] - local * MOE_BMAX), 0).astype(I32)
    it_row0 = jnp.where(valid, gstart[it_e] + local * MOE_RMAX, 0).astype(I32)
    it_enext = it_e[jnp.minimum(ids + 1, MOE_NI - 1)]
    n_used = jnp.sum(nblk_e).astype(I32).reshape(1)
    return dest.reshape(-1), slot_tok, it_e.astype(I32), it_enext.astype(I32), it_row0, it_nblk, n_used


def kernel(x_prompt, x_sample, state_conv, p_prompt, p_sample, g_mix, w_in, ln_v_g, ln_v_b, w_s, b_s,
           conv_w, conv_b, w_proj_a, w_proj_b, w_o, g_moe, w_router, b_router, w_gate_up, b_gate_up,
           w_down, b_down, g_ple, w_ple, w_ple_gate, g_final):
    assert g_mix.shape[0] == 1, "one layer"
    xs = x_sample.reshape(T_S, D_MODEL)
    xp = x_prompt.reshape(T_P, D_MODEL)

    tril = jnp.tril(jnp.ones((CHUNK, CHUNK), bool))
    w_prompt = jnp.where(tril[None], w_s[0], 0.0)
    small = jnp.where(tril[None, :DEC_SEQ, :DEC_SEQ], w_s[0, :, :DEC_SEQ, :DEC_SEQ], 0.0)
    reps = CHUNK // DEC_SEQ
    blockdiag = jnp.kron(jnp.eye(reps, dtype=F32), jnp.ones((DEC_SEQ, DEC_SEQ), F32))
    w_sample = jnp.tile(small, (1, reps, reps)) * blockdiag[None]
    wsg = jnp.stack([w_sample, w_prompt]).astype(BF16)
    bias_p = jnp.repeat(b_s[0].T, GW_A, axis=1)
    bias_s = jnp.tile(jnp.repeat(b_s[0, :, :DEC_SEQ].T, GW_A, axis=1), (reps, 1))
    bsg = jnp.stack([bias_s, bias_p])
    ext = jnp.pad(state_conv[0], ((0, 0), (0, DEC_SEQ - (CONV_W - 1)), (0, 0))).reshape(T_S, D_B)

    wr = jnp.pad(w_router[0], ((0, 0), (0, LANES - N_EXPERTS)))
    wr_hi = wr.astype(BF16)
    wr_lo = (wr - wr_hi.astype(F32)).astype(BF16)
    b_r = jnp.pad(b_router[0], (0, LANES - N_EXPERTS), constant_values=NEG_BIG).reshape(1, LANES)

    h = _norm_call(xs, xp, g_mix)
    z, vpre = _in_proj_call(h, w_in[0])
    x1, xn, route_i, route_g, counts, vln, cxs, tail = _mixer_call(
        xs, xp, z, vpre, ext, wsg, bsg, ln_v_g, ln_v_b, conv_w[0], conv_b,
        w_proj_a[0].astype(BF16), w_proj_b[0].astype(BF16), w_o[0].astype(BF16), g_moe,
        wr_hi, wr_lo, b_r)

    dest, slot_tok, it_e, it_enext, it_row0, it_nblk, n_used = _routing_tables(route_i, counts)
    y_sorted = _moe_call(slot_tok, it_e, it_enext, it_row0, it_nblk, n_used, xn, w_gate_up[0], w_down[0],
                         b_gate_up[0].reshape(N_EXPERTS, 1, 2 * D_FF), b_down[0].reshape(N_EXPERTS, 1, D_MODEL))
    ys, yp = _combine_call(dest, y_sorted, x1, route_g,
                           p_sample[0].reshape(T_S, PLE_DIM), p_prompt[0].reshape(T_P, PLE_DIM),
                           w_ple[0].astype(BF16), w_ple_gate[0].astype(BF16), g_ple, g_final.reshape(1, D_MODEL))

    y_prompt = yp.reshape(BATCH, SEQ, D_MODEL)
    y_sample = ys.reshape(DEC_BATCH, DEC_SEQ, D_MODEL)
    last = tail.reshape(BATCH, MIX_SEQ_TILES, 8, D_B)[:, -1, 8 - (CONV_W - 1):, :]
    state_conv_prompt = last[None]
    state_conv_sample = cxs.reshape(DEC_BATCH, DEC_SEQ, D_B)[:, DEC_SEQ - (CONV_W - 1):, :][None]
    state_chunk_v_sample = vln.reshape(DEC_BATCH, DEC_SEQ, D_A)[None]
    return (y_prompt, y_sample, state_conv_prompt, state_conv_sample, state_chunk_v_sample)
```

```python
import functools

import jax
import jax.numpy as jnp
from jax import lax
from jax.experimental import pallas as pl
from jax.experimental.pallas import tpu as pltpu

F32 = jnp.float32
BF16 = jnp.bfloat16
I32 = jnp.int32

D_MODEL = 2048
BATCH = 4
SEQ = 2048
DEC_BATCH = 128
DEC_SEQ = 8
CHUNK = 128
D_A = D_MODEL // 2
N_GROUPS_A = 8
GW_A = D_A // N_GROUPS_A
D_B = D_MODEL // 2
CONV_W = 3
N_EXPERTS = 32
TOP_K = 4
D_FF = D_MODEL
SWIGLU_LIMIT = 7.0
SWIGLU_ALPHA = 1.702
PLE_DIM = 256
EPS = 1e-6

T_S = DEC_BATCH * DEC_SEQ
T_P = BATCH * SEQ
T = T_S + T_P

LANES = 128
V7X_VMEM_BYTES = 64 * 1024 * 1024
MIB = 1024 * 1024

NORM_TM = 512
IN_TM = 1536
IN_TN = 512
MIX_TM = CHUNK
MIX_NS = T_S // MIX_TM
MIX_SEQ_TILES = SEQ // MIX_TM
ROW_TILES = D_MODEL // LANES
MOE_RB = 128
MOE_BMAX = 12
MOE_RMAX = MOE_RB * MOE_BMAX
MOE_CHUNKS = (8, 4, 2, 1)
MOE_TF = 256
MOE_TN = 512
MOE_P1 = D_FF // MOE_TF
MOE_P2 = D_MODEL // MOE_TN
MOE_G = MOE_RB // MOE_P1
MOE_YSLOTS = 8
N_SLOTS = T * TOP_K + N_EXPERTS * MOE_RB
N_BLOCKS = N_SLOTS // MOE_RB
MOE_NI = (N_BLOCKS + N_EXPERTS * (MOE_BMAX - 1)) // MOE_BMAX
GATHER_UNROLL = 8
CMB_TM = 256
CMB_NS = T_S // CMB_TM
NEG_BIG = -1e30


def _rms(x, g):
    return x * lax.rsqrt(jnp.mean(x * x, axis=-1, keepdims=True) + EPS) * g


def _cparams(sem, vmem_mib):
    return pltpu.CompilerParams(dimension_semantics=sem, vmem_limit_bytes=vmem_mib * MIB)


def _resident(shape):
    zeros = (0,) * len(shape)
    return pl.BlockSpec(shape, lambda *_: zeros, pipeline_mode=pl.Buffered(1))


def _norm_kernel(xs_ref, xp_ref, g_ref, h_ref, *, ns):
    m = pl.program_id(0)
    x = jnp.where(m < ns, xs_ref[...], xp_ref[...])
    h_ref[...] = _rms(x, g_ref[...]).astype(BF16)


def _norm_call(xs, xp, g):
    ns = T_S // NORM_TM
    return pl.pallas_call(
        functools.partial(_norm_kernel, ns=ns),
        grid=(T // NORM_TM,),
        in_specs=[
            pl.BlockSpec((NORM_TM, D_MODEL), lambda m: (jnp.minimum(m, ns - 1), 0)),
            pl.BlockSpec((NORM_TM, D_MODEL), lambda m: (jnp.maximum(m - ns, 0), 0)),
            pl.BlockSpec((1, D_MODEL), lambda m: (0, 0)),
        ],
        out_specs=pl.BlockSpec((NORM_TM, D_MODEL), lambda m: (m, 0)),
        out_shape=jax.ShapeDtypeStruct((T, D_MODEL), BF16),
        compiler_params=_cparams(("arbitrary",), 32),
        name="norm",
    )(xs, xp, g)


IN_N_GELU = 2 * D_A // IN_TN
IN_N_V0 = D_A // IN_TN
IN_N_LIN = (2 * D_A + 3 * D_B) // IN_TN


def _in_proj_kernel(h_ref, w_ref, z_ref, vpre_ref, acc_ref):
    n = pl.program_id(1)
    acc_ref[...] = jnp.dot(h_ref[...], w_ref[...].astype(BF16), preferred_element_type=F32)

    @pl.when(n < IN_N_GELU)
    def _():
        g = jax.nn.gelu(acc_ref[...], approximate=True)
        z_ref[...] = g.astype(BF16)

        @pl.when(n >= IN_N_V0)
        def _():
            vpre_ref[...] = g

    @pl.when((n >= IN_N_GELU) & (n < IN_N_LIN))
    def _():
        z_ref[...] = acc_ref[...].astype(BF16)

    @pl.when(n >= IN_N_LIN)
    def _():
        z_ref[...] = jax.nn.sigmoid(acc_ref[...]).astype(BF16)


def _in_proj_call(h, w_in):
    d_in = w_in.shape[1]
    return pl.pallas_call(
        _in_proj_kernel,
        grid=(T // IN_TM, d_in // IN_TN),
        in_specs=[
            pl.BlockSpec((IN_TM, D_MODEL), lambda m, n: (m, 0)),
            pl.BlockSpec((D_MODEL, IN_TN), lambda m, n: (0, n)),
        ],
        out_specs=[
            pl.BlockSpec((IN_TM, IN_TN), lambda m, n: (m, n)),
            pl.BlockSpec((IN_TM, IN_TN), lambda m, n: (m, jnp.clip(n - IN_N_V0, 0, IN_N_GELU - IN_N_V0 - 1))),
        ],
        out_shape=[
            jax.ShapeDtypeStruct((T, d_in), BF16),
            jax.ShapeDtypeStruct((T, D_A), F32),
        ],
        scratch_shapes=[pltpu.VMEM((IN_TM, IN_TN), F32)],
        compiler_params=_cparams(("arbitrary", "arbitrary"), 44),
        name="in_proj",
    )(h, w_in)


def _mixer_kernel(xs_ref, xp_ref, z_ref, vpre_ref, ext_ref, wsg_ref, bsg_ref, lng_ref, lnb_ref,
                  cw_ref, cb_ref, wa_ref, wb_ref, wo_ref, gmoe_ref, wrh_ref, wrl_ref, br_ref,
                  x1_ref, xn_ref, ri_ref, rg_ref, cnt_ref, vln_ref, cxs_ref, tail_ref,
                  prev_ref, carry_ref):
    m = pl.program_id(0)
    is_s = m < MIX_NS
    tm = MIX_TM

    @pl.when(m == 0)
    def _():
        prev_ref[...] = jnp.zeros_like(prev_ref)
        carry_ref[...] = jnp.zeros_like(carry_ref)

    vg = vpre_ref[...]
    mu = jnp.mean(vg, axis=-1, keepdims=True)
    vc = vg - mu
    v = vc * lax.rsqrt(jnp.mean(vc * vc, axis=-1, keepdims=True) + EPS) * lng_ref[...] + lnb_ref[...]

    @pl.when(is_s)
    def _():
        vln_ref[...] = v

    vb = v.astype(BF16)
    s_parts = []
    for g in range(N_GROUPS_A):
        s_parts.append(jnp.dot(wsg_ref[0, g], vb[:, g * GW_A:(g + 1) * GW_A], preferred_element_type=F32))
    s = jnp.concatenate(s_parts, axis=1) + bsg_ref[0]
    u = z_ref[:, 0:D_A].astype(F32)
    a_in = (u * s).astype(BF16)

    o_b = 2 * D_A
    bg = z_ref[:, o_b:o_b + D_B].astype(F32)
    cg = z_ref[:, o_b + D_B:o_b + 2 * D_B].astype(F32)
    xin = z_ref[:, o_b + 2 * D_B:o_b + 3 * D_B].astype(F32)
    cx = cg * xin

    @pl.when(is_s)
    def _():
        cxs_ref[...] = cx

    @pl.when(jnp.logical_not(is_s))
    def _():
        tail_ref[0] = cx[tm - 8:tm]

    row = lax.broadcasted_iota(I32, (tm, D_B), 0)
    seq_start = ((m - MIX_NS) % MIX_SEQ_TILES) == 0
    prev = jnp.where(seq_start, 0.0, prev_ref[...])
    row8 = lax.broadcasted_iota(I32, (8, D_B), 0)
    top = jnp.where(row8 < CONV_W - 1, pltpu.roll(prev, CONV_W - 1, 0), 0.0)
    ext_p = jnp.concatenate([top, jnp.zeros((tm - 8, D_B), F32)], axis=0)
    ext = jnp.where(is_s, ext_ref[...], ext_p)
    t_in = jnp.where(is_s, row & (DEC_SEQ - 1), row)
    s1 = jnp.where(t_in < 1, pltpu.roll(ext, tm - 1, 0), pltpu.roll(cx, 1, 0))
    s2 = jnp.where(t_in < 2, ext, pltpu.roll(cx, 2, 0))
    prev_ref[...] = cx[tm - 8:tm]
    conv = cb_ref[...] + s2 * cw_ref[0:1, :] + s1 * cw_ref[1:2, :] + cx * cw_ref[2:3, :]
    b_in = (bg * conv).astype(BF16)

    y_a = jnp.dot(a_in, wa_ref[...], preferred_element_type=F32)
    y_b = jnp.dot(b_in, wb_ref[...], preferred_element_type=F32)
    o_g = 2 * D_A + 3 * D_B
    ga = z_ref[:, o_g:o_g + D_MODEL].astype(F32)
    gb = z_ref[:, o_g + D_MODEL:o_g + 2 * D_MODEL].astype(F32)
    mix = (ga * y_a + gb * y_b).astype(BF16)
    x = jnp.where(is_s, xs_ref[...], xp_ref[...])
    x1 = x + jnp.dot(mix, wo_ref[...], preferred_element_type=F32)
    x1_ref[...] = x1

    xn = _rms(x1, gmoe_ref[...])
    for j in range(ROW_TILES):
        xn_ref[pl.ds(j, tm, stride=ROW_TILES), :] = xn[:, j * LANES:(j + 1) * LANES]
    hi = xn.astype(BF16)
    lo = (xn - hi.astype(F32)).astype(BF16)
    logits = (jnp.dot(hi, wrh_ref[...], preferred_element_type=F32)
              + jnp.dot(lo, wrh_ref[...], preferred_element_type=F32)
              + jnp.dot(hi, wrl_ref[...], preferred_element_type=F32)) + br_ref[...]
    lane = lax.broadcasted_iota(I32, (tm, LANES), 1)
    lane_f = lane.astype(F32)
    vals, idxs, hots = [], [], []
    work = logits
    for _ in range(TOP_K):
        mx = jnp.max(work, axis=-1, keepdims=True)
        idx = jnp.min(jnp.where(work == mx, lane_f, float(LANES)), axis=-1, keepdims=True)
        hot = lane_f == idx
        work = jnp.where(hot, -jnp.inf, work)
        vals.append(mx)
        idxs.append(idx)
        hots.append(hot)
    exps = [jnp.exp(vk - vals[0]) for vk in vals]
    den = exps[0] + exps[1] + exps[2] + exps[3]

    chosen = jnp.zeros((tm, LANES), F32)
    for hot in hots:
        chosen = chosen + jnp.where(hot, 1.0, 0.0)
    r_i = lax.broadcasted_iota(I32, (tm, tm), 0)
    c_i = lax.broadcasted_iota(I32, (tm, tm), 1)
    tri = jnp.where(c_i < r_i, 1.0, 0.0).astype(BF16)
    before = jnp.dot(tri, chosen.astype(BF16), preferred_element_type=F32) + carry_ref[0:1, :]
    total = carry_ref[0:1, :] + jnp.sum(chosen, axis=0, keepdims=True)
    carry_ref[...] = jnp.broadcast_to(total, carry_ref.shape)
    cnt_ref[...] = jnp.broadcast_to(total, cnt_ref.shape)

    ri = jnp.zeros((tm, LANES), F32)
    rg = jnp.zeros((tm, LANES), F32)
    for k in range(TOP_K):
        rank = jnp.sum(jnp.where(hots[k], before, 0.0), axis=-1, keepdims=True)
        ri = ri + jnp.where(lane == k, idxs[k], 0.0) + jnp.where(lane == TOP_K + k, rank, 0.0)
        rg = rg + jnp.where(lane == k, exps[k] / den, 0.0)
    ri_ref[...] = ri.astype(I32)
    rg_ref[...] = rg


def _mixer_call(xs, xp, z, vpre, ext, wsg, bsg, ln_g, ln_b, conv_w, conv_b, wa, wb, wo, g_moe,
                wr_hi, wr_lo, b_r):
    tm = MIX_TM
    ns = MIX_NS
    d_in = z.shape[1]
    s_idx = lambda m: (jnp.minimum(m, ns - 1), 0)
    p_idx = lambda m: (jnp.maximum(m - ns, 0), 0)
    row = lambda m: (m, 0)
    return pl.pallas_call(
        _mixer_kernel,
        grid=(T // tm,),
        in_specs=[
            pl.BlockSpec((tm, D_MODEL), s_idx),
            pl.BlockSpec((tm, D_MODEL), p_idx),
            pl.BlockSpec((tm, d_in), row),
            pl.BlockSpec((tm, D_A), row),
            pl.BlockSpec((tm, D_B), s_idx),
            pl.BlockSpec((1, N_GROUPS_A, CHUNK, CHUNK), lambda m: (jnp.minimum(m // ns, 1), 0, 0, 0)),
            pl.BlockSpec((1, CHUNK, D_A), lambda m: (jnp.minimum(m // ns, 1), 0, 0)),
            _resident((1, D_A)),
            _resident((1, D_A)),
            _resident((CONV_W, D_B)),
            _resident((1, D_B)),
            _resident((D_A, D_MODEL)),
            _resident((D_B, D_MODEL)),
            _resident((D_MODEL, D_MODEL)),
            _resident((1, D_MODEL)),
            _resident((D_MODEL, LANES)),
            _resident((D_MODEL, LANES)),
            _resident((1, LANES)),
        ],
        out_specs=[
            pl.BlockSpec((tm, D_MODEL), row),
            pl.BlockSpec((tm * ROW_TILES, LANES), row),
            pl.BlockSpec((tm, LANES), row),
            pl.BlockSpec((tm, LANES), row),
            pl.BlockSpec((8, LANES), lambda m: (0, 0)),
            pl.BlockSpec((tm, D_A), s_idx),
            pl.BlockSpec((tm, D_B), s_idx),
            pl.BlockSpec((1, 8, D_B), lambda m: (jnp.maximum(m - ns, 0), 0, 0)),
        ],
        out_shape=[
            jax.ShapeDtypeStruct((T, D_MODEL), F32),
            jax.ShapeDtypeStruct((T * ROW_TILES, LANES), F32),
            jax.ShapeDtypeStruct((T, LANES), I32),
            jax.ShapeDtypeStruct((T, LANES), F32),
            jax.ShapeDtypeStruct((8, LANES), F32),
            jax.ShapeDtypeStruct((T_S, D_A), F32),
            jax.ShapeDtypeStruct((T_S, D_B), F32),
            jax.ShapeDtypeStruct((T_P // tm, 8, D_B), F32),
        ],
        scratch_shapes=[pltpu.VMEM((8, D_B), F32), pltpu.VMEM((8, LANES), F32)],
        compiler_params=_cparams(("arbitrary",), 52),
        name="mixer",
    )(xs, xp, z, vpre, ext, wsg, bsg, ln_g, ln_b, conv_w, conv_b, wa, wb, wo, g_moe, wr_hi, wr_lo, b_r)


MOE_CHUNK_SLOT0 = {8: 0, 4: 0, 2: 4, 1: 6}


def _for_units(n, unit_fn, chunk_begin=None):
    big = MOE_CHUNKS[0]

    def chunk(u0, count):
        slot0 = MOE_CHUNK_SLOT0[count]
        if chunk_begin is not None:
            chunk_begin(range(slot0, slot0 + count))
        for j in range(count):
            unit_fn(u0 + j, slot0 + j)

    def body(c, carry):
        chunk(c * big, big)
        return carry

    n_big = lax.shift_right_logical(n, big.bit_length() - 1)
    lax.fori_loop(0, n_big, body, 0)
    base = n_big * big
    for count in MOE_CHUNKS[1:]:
        @pl.when((n & count) != 0)
        def _():
            chunk(base, count)

        base = base + (n & count)


def _moe_kernel(tok_ref, ite_ref, iten_ref, row0_ref, nblk_ref, wait_ref, used_ref,
                xn_hbm, wg_ref, wu_ref, wd_ref, bg_ref, bu_ref, bd_ref,
                y_hbm,
                xraw_ref, xb_ref, act_ref, wgu_ref, wdb_ref, ystage_ref, zbuf_ref, gsem, ysem, zsem):
    i = pl.program_id(0)
    s = pl.program_id(1)
    nblk = nblk_ref[i]
    row0 = row0_ref[i]
    valid = nblk > 0
    last_step = (i == MOE_NI - 1) & (s == MOE_P1 + MOE_P2 - 1)

    def tail_copy(b):
        r = pl.multiple_of(b * MOE_RB, MOE_RB)
        return pltpu.make_async_copy(zbuf_ref, y_hbm.at[pl.ds(r, MOE_RB), :], zsem)

    def row_copy(tok, r):
        return pltpu.make_async_copy(
            xn_hbm.at[pl.ds(pl.multiple_of(tok * ROW_TILES, ROW_TILES), ROW_TILES), :],
            xraw_ref.at[pl.ds(pl.multiple_of(r * ROW_TILES, ROW_TILES), ROW_TILES), :],
            gsem)

    def unit_wait():
        n = MOE_RB * ROW_TILES
        return pltpu.make_async_copy(xn_hbm.at[pl.ds(0, n), :], xraw_ref.at[pl.ds(0, n), :], gsem)

    def gather_rows(item, lo_unit, hi_unit):
        base = row0_ref[item]

        def body(c, carry):
            for j in range(GATHER_UNROLL):
                r = c * GATHER_UNROLL + j
                row_copy(tok_ref[base + r], r).start()
            return carry

        per_unit = MOE_RB // GATHER_UNROLL
        lax.fori_loop(lo_unit * per_unit, hi_unit * per_unit, body, 0)

    def y_copy(slot, u, col):
        r = pl.multiple_of(row0 + u * MOE_RB, MOE_RB)
        return pltpu.make_async_copy(ystage_ref.at[slot], y_hbm.at[pl.ds(r, MOE_RB), pl.ds(col, MOE_TN)],
                                     ysem.at[slot])

    def dump_copy(slot):
        r = N_SLOTS + (slot // MOE_P2) * MOE_RB
        c = (slot % MOE_P2) * MOE_TN
        return pltpu.make_async_copy(ystage_ref.at[slot], y_hbm.at[pl.ds(r, MOE_RB), pl.ds(c, MOE_TN)],
                                     ysem.at[slot])

    @pl.when((i == 0) & (s == 0))
    def _():
        gather_rows(0, 0, nblk_ref[0])
        ystage_ref[...] = jnp.zeros_like(ystage_ref)
        for slot in range(MOE_YSLOTS):
            dump_copy(slot).start()
        zbuf_ref[...] = jnp.zeros_like(zbuf_ref)

        def fill(b, carry):
            tail_copy(b).start()
            return carry

        lax.fori_loop(used_ref[0], N_BLOCKS, fill, 0)

    @pl.when(s == 0)
    def _():
        def body(b, carry):
            unit_wait().wait()
            return carry

        lax.fori_loop(0, wait_ref[i], body, 0)

        def conv(u, carry):
            r = pl.multiple_of(u * MOE_RB, MOE_RB)
            parts = [xraw_ref[pl.ds(r * ROW_TILES + j, MOE_RB, stride=ROW_TILES), :] for j in range(ROW_TILES)]
            xb_ref[pl.ds(r, MOE_RB), :] = jnp.concatenate(parts, axis=1).astype(BF16)
            return carry

        lax.fori_loop(0, nblk, conv, 0)

    @pl.when(valid & (s < MOE_P1))
    def _():
        wgu_ref[:, 0:MOE_TF] = wg_ref[0].astype(BF16)
        wgu_ref[:, MOE_TF:2 * MOE_TF] = wu_ref[0].astype(BF16)
        b_g = bg_ref[0]
        b_u = bu_ref[0]
        next_base = row0_ref[i + 1]

        def unit(u, slot):
            r = pl.multiple_of(u * MOE_RB, MOE_RB)
            gu = jnp.dot(xb_ref[pl.ds(r, MOE_RB), :], wgu_ref[...], preferred_element_type=F32)
            gate = jnp.minimum(gu[:, 0:MOE_TF] + b_g, SWIGLU_LIMIT)
            up = jnp.clip(gu[:, MOE_TF:2 * MOE_TF] + b_u, -SWIGLU_LIMIT, SWIGLU_LIMIT)
            act = (up + 1) * (gate * jax.nn.sigmoid(gate * SWIGLU_ALPHA))
            act_ref[s, pl.ds(r, MOE_RB), :] = act.astype(BF16)
            for j in range(MOE_G):
                rr = r + s * MOE_G + j
                row_copy(tok_ref[next_base + rr], rr).start()

        _for_units(nblk, unit)

    @pl.when(valid & (s == MOE_P1))
    def _():
        gather_rows(i + 1, nblk, nblk_ref[i + 1])

    @pl.when(valid & (s >= MOE_P1))
    def _():
        wdb_ref[...] = wd_ref[0].astype(BF16)
        b_d = bd_ref[0]
        col = pl.multiple_of((s - MOE_P1) * MOE_TN, MOE_TN)

        def free_slots(slots):
            for slot in slots:
                y_copy(slot, 0, col).wait()

        def unit(u, slot):
            r = pl.multiple_of(u * MOE_RB, MOE_RB)
            a = jnp.concatenate([act_ref[j, pl.ds(r, MOE_RB), :] for j in range(MOE_P1)], axis=1)
            ystage_ref[slot] = jnp.dot(a, wdb_ref[...], preferred_element_type=F32) + b_d
            y_copy(slot, u, col).start()

        _for_units(nblk, unit, free_slots)

    @pl.when(last_step)
    def _():
        for slot in range(MOE_YSLOTS):
            dump_copy(slot).wait()

        def wait_rows(b, carry):
            unit_wait().wait()
            return carry

        lax.fori_loop(0, wait_ref[MOE_NI], wait_rows, 0)

        def drain(b, carry):
            tail_copy(b).wait()
            return carry

        lax.fori_loop(used_ref[0], N_BLOCKS, drain, 0)


def _moe_call(slot_tok, it_e, it_enext, it_row0, it_nblk, it_wait, n_used, xn, w_gate_up, w_down,
              b_gate_up, b_down):
    up0 = D_FF // MOE_TF

    def phase1_tile(i, s, nb):
        return jnp.where((s < MOE_P1) & (nb[i] > 0), s, 0)

    def gu_expert(i, s, e, en):
        return jnp.where(s < MOE_P1, e[i], en[i])

    def g_map(i, s, tok, e, en, r0, nb, wt, used):
        return (gu_expert(i, s, e, en), 0, phase1_tile(i, s, nb))

    def u_map(i, s, tok, e, en, r0, nb, wt, used):
        return (gu_expert(i, s, e, en), 0, up0 + phase1_tile(i, s, nb))

    def d_map(i, s, tok, e, en, r0, nb, wt, used):
        return (e[i], 0, jnp.where((s >= MOE_P1) & (nb[i] > 0), s - MOE_P1, 0))

    grid_spec = pltpu.PrefetchScalarGridSpec(
        num_scalar_prefetch=7,
        grid=(MOE_NI, MOE_P1 + MOE_P2),
        in_specs=[
            pl.BlockSpec(memory_space=pl.ANY),
            pl.BlockSpec((1, D_MODEL, MOE_TF), g_map),
            pl.BlockSpec((1, D_MODEL, MOE_TF), u_map),
            pl.BlockSpec((1, D_FF, MOE_TN), d_map),
            pl.BlockSpec((1, 1, MOE_TF), g_map),
            pl.BlockSpec((1, 1, MOE_TF), u_map),
            pl.BlockSpec((1, 1, MOE_TN), d_map),
        ],
        out_specs=pl.BlockSpec(memory_space=pl.ANY),
        scratch_shapes=[
            pltpu.VMEM((MOE_RMAX * ROW_TILES, LANES), F32),
            pltpu.VMEM((MOE_RMAX, D_MODEL), BF16),
            pltpu.VMEM((MOE_P1, MOE_RMAX, MOE_TF), BF16),
            pltpu.VMEM((D_MODEL, 2 * MOE_TF), BF16),
            pltpu.VMEM((D_FF, MOE_TN), BF16),
            pltpu.VMEM((MOE_YSLOTS, MOE_RB, MOE_TN), F32),
            pltpu.VMEM((MOE_RB, D_MODEL), F32),
            pltpu.SemaphoreType.DMA(()),
            pltpu.SemaphoreType.DMA((MOE_YSLOTS,)),
            pltpu.SemaphoreType.DMA(()),
        ],
    )
    spare_blocks = MOE_YSLOTS // MOE_P2
    return pl.pallas_call(
        _moe_kernel,
        grid_spec=grid_spec,
        out_shape=jax.ShapeDtypeStruct((N_SLOTS + spare_blocks * MOE_RB, D_MODEL), F32),
        compiler_params=_cparams(("arbitrary", "arbitrary"), 56),
        name="moe",
    )(slot_tok, it_e, it_enext, it_row0, it_nblk, it_wait, n_used, xn, w_gate_up, w_gate_up, w_down,
      b_gate_up, b_gate_up, b_down)


def _combine_kernel(dest_ref, y_hbm, x1_ref, rg_ref, ps_ref, pp_ref, wple_ref, wpg_ref, gple_ref,
                    gfin_ref, ys_ref, yp_ref, gbuf_ref, gsem):
    m = pl.program_id(0)
    nm = pl.num_programs(0)
    tm = CMB_TM
    slot = m % 2

    def issue(tile, slot_):
        def body(c, carry):
            for j in range(GATHER_UNROLL // TOP_K):
                r = c * (GATHER_UNROLL // TOP_K) + j
                for k in range(TOP_K):
                    d = dest_ref[(tile * tm + r) * TOP_K + k]
                    pltpu.make_async_copy(y_hbm.at[pl.ds(d, 1), :], gbuf_ref.at[slot_, k, pl.ds(r, 1), :],
                                          gsem.at[slot_]).start()
            return carry

        lax.fori_loop(0, tm // (GATHER_UNROLL // TOP_K), body, 0)

    @pl.when(m == 0)
    def _():
        issue(0, 0)

    @pl.when(m + 1 < nm)
    def _():
        issue(m + 1, 1 - slot)

    for k in range(TOP_K):
        pltpu.make_async_copy(y_hbm.at[pl.ds(0, tm), :], gbuf_ref.at[slot, k], gsem.at[slot]).wait()

    gates = rg_ref[...]
    moe = gates[:, 0:1] * gbuf_ref[slot, 0]
    for k in range(1, TOP_K):
        moe = moe + gates[:, k:k + 1] * gbuf_ref[slot, k]
    x2 = x1_ref[...] + moe
    is_s = m < CMB_NS
    p = jnp.where(is_s, ps_ref[...], pp_ref[...]).astype(BF16)
    pe = jnp.dot(p, wple_ref[...], preferred_element_type=F32)
    hn = _rms(x2, gple_ref[...]).astype(BF16)
    gate = jax.nn.sigmoid(jnp.dot(hn, wpg_ref[...], preferred_element_type=F32))
    x3 = x2 + pe * gate
    y = _rms(x3, gfin_ref[...])

    @pl.when(is_s)
    def _():
        ys_ref[...] = y

    @pl.when(jnp.logical_not(is_s))
    def _():
        yp_ref[...] = y


def _combine_call(dest, y_sorted, x1, rg, ps, pp, wple, wpg, g_ple, g_final):
    tm = CMB_TM
    ns = CMB_NS
    s_idx = lambda m, d: (jnp.minimum(m, ns - 1), 0)
    p_idx = lambda m, d: (jnp.maximum(m - ns, 0), 0)
    row = lambda m, d: (m, 0)
    const2 = lambda m, d: (0, 0)
    grid_spec = pltpu.PrefetchScalarGridSpec(
        num_scalar_prefetch=1,
        grid=(T // tm,),
        in_specs=[
            pl.BlockSpec(memory_space=pl.ANY),
            pl.BlockSpec((tm, D_MODEL), row),
            pl.BlockSpec((tm, LANES), row),
            pl.BlockSpec((tm, PLE_DIM), s_idx),
            pl.BlockSpec((tm, PLE_DIM), p_idx),
            pl.BlockSpec((PLE_DIM, D_MODEL), const2),
            pl.BlockSpec((D_MODEL, D_MODEL), const2),
            pl.BlockSpec((1, D_MODEL), const2),
            pl.BlockSpec((1, D_MODEL), const2),
        ],
        out_specs=[
            pl.BlockSpec((tm, D_MODEL), s_idx),
            pl.BlockSpec((tm, D_MODEL), p_idx),
        ],
        scratch_shapes=[
            pltpu.VMEM((2, TOP_K, tm, D_MODEL), F32),
            pltpu.SemaphoreType.DMA((2,)),
        ],
    )
    return pl.pallas_call(
        _combine_kernel,
        grid_spec=grid_spec,
        out_shape=[
            jax.ShapeDtypeStruct((T_S, D_MODEL), F32),
            jax.ShapeDtypeStruct((T_P, D_MODEL), F32),
        ],
        compiler_params=_cparams(("arbitrary",), 56),
        name="combine",
    )(dest, y_sorted, x1, rg, ps, pp, wple, wpg, g_ple, g_final)


def _routing_tables(route_i, counts):
    e_idx = route_i[:, 0:TOP_K]
    rank = route_i[:, TOP_K:2 * TOP_K]
    cnt = counts[0, :N_EXPERTS].astype(I32)
    padded = (cnt + MOE_RB - 1) // MOE_RB * MOE_RB
    gend = jnp.cumsum(padded)
    gstart = gend - padded
    dest = (gstart[e_idx] + rank).astype(I32)
    tok_of = jnp.repeat(jnp.arange(T, dtype=I32), TOP_K)
    slot_tok = jnp.zeros((N_SLOTS + MOE_RMAX,), I32).at[dest.reshape(-1)].set(tok_of)
    nblk_e = padded // MOE_RB
    items_e = (nblk_e + MOE_BMAX - 1) // MOE_BMAX
    iend = jnp.cumsum(items_e)
    istart = iend - items_e
    n_items = iend[-1]
    ids = jnp.arange(MOE_NI + 1, dtype=I32)
    valid = ids < n_items
    e_of = jnp.minimum(jnp.searchsorted(iend, ids, side="right"), N_EXPERTS - 1).astype(I32)
    e_last = jnp.minimum(jnp.searchsorted(iend, n_items - 1, side="right"), N_EXPERTS - 1).astype(I32)
    it_e = jnp.where(valid, e_of, e_last)
    local = ids - istart[it_e]
    it_nblk = jnp.where(valid, jnp.minimum(MOE_BMAX, nblk_e[it_e] - local * MOE_BMAX), 0).astype(I32)
    it_row0 = jnp.where(valid, gstart[it_e] + local * MOE_RMAX, 0).astype(I32)
    it_enext = it_e[jnp.minimum(ids + 1, MOE_NI)]
    prev_nblk = jnp.concatenate([jnp.zeros((1,), I32), it_nblk[:-1]])
    it_wait = jnp.maximum(it_nblk, prev_nblk)
    n_used = jnp.sum(nblk_e).astype(I32).reshape(1)
    return (dest.reshape(-1), slot_tok, it_e.astype(I32), it_enext.astype(I32), it_row0, it_nblk, it_wait,
            n_used)


def kernel(x_prompt, x_sample, state_conv, p_prompt, p_sample, g_mix, w_in, ln_v_g, ln_v_b, w_s, b_s,
           conv_w, conv_b, w_proj_a, w_proj_b, w_o, g_moe, w_router, b_router, w_gate_up, b_gate_up,
           w_down, b_down, g_ple, w_ple, w_ple_gate, g_final):
    assert g_mix.shape[0] == 1, "one layer"
    xs = x_sample.reshape(T_S, D_MODEL)
    xp = x_prompt.reshape(T_P, D_MODEL)

    tril = jnp.tril(jnp.ones((CHUNK, CHUNK), bool))
    w_prompt = jnp.where(tril[None], w_s[0], 0.0)
    small = jnp.where(tril[None, :DEC_SEQ, :DEC_SEQ], w_s[0, :, :DEC_SEQ, :DEC_SEQ], 0.0)
    reps = CHUNK // DEC_SEQ
    blockdiag = jnp.kron(jnp.eye(reps, dtype=F32), jnp.ones((DEC_SEQ, DEC_SEQ), F32))
    w_sample = jnp.tile(small, (1, reps, reps)) * blockdiag[None]
    wsg = jnp.stack([w_sample, w_prompt]).astype(BF16)
    bias_p = jnp.repeat(b_s[0].T, GW_A, axis=1)
    bias_s = jnp.tile(jnp.repeat(b_s[0, :, :DEC_SEQ].T, GW_A, axis=1), (reps, 1))
    bsg = jnp.stack([bias_s, bias_p])
    ext = jnp.pad(state_conv[0], ((0, 0), (0, DEC_SEQ - (CONV_W - 1)), (0, 0))).reshape(T_S, D_B)

    wr = jnp.pad(w_router[0], ((0, 0), (0, LANES - N_EXPERTS)))
    wr_hi = wr.astype(BF16)
    wr_lo = (wr - wr_hi.astype(F32)).astype(BF16)
    b_r = jnp.pad(b_router[0], (0, LANES - N_EXPERTS), constant_values=NEG_BIG).reshape(1, LANES)

    h = _norm_call(xs, xp, g_mix)
    z, vpre = _in_proj_call(h, w_in[0])
    x1, xn, route_i, route_g, counts, vln, cxs, tail = _mixer_call(
        xs, xp, z, vpre, ext, wsg, bsg, ln_v_g, ln_v_b, conv_w[0], conv_b,
        w_proj_a[0].astype(BF16), w_proj_b[0].astype(BF16), w_o[0].astype(BF16), g_moe,
        wr_hi, wr_lo, b_r)

    dest, slot_tok, it_e, it_enext, it_row0, it_nblk, it_wait, n_used = _routing_tables(route_i, counts)
    y_sorted = _moe_call(slot_tok, it_e, it_enext, it_row0, it_nblk, it_wait, n_used, xn, w_gate_up[0], w_down[0],
                         b_gate_up[0].reshape(N_EXPERTS, 1, 2 * D_FF), b_down[0].reshape(N_EXPERTS, 1, D_MODEL))
    ys, yp = _combine_call(dest, y_sorted, x1, route_g,
                           p_sample[0].reshape(T_S, PLE_DIM), p_prompt[0].reshape(T_P, PLE_DIM),
                           w_ple[0].astype(BF16), w_ple_gate[0].astype(BF16), g_ple, g_final.reshape(1, D_MODEL))

    y_prompt = yp.reshape(BATCH, SEQ, D_MODEL)
    y_sample = ys.reshape(DEC_BATCH, DEC_SEQ, D_MODEL)
    last = tail.reshape(BATCH, MIX_SEQ_TILES, 8, D_B)[:, -1, 8 - (CONV_W - 1):, :]
    state_conv_prompt = last[None]
    state_conv_sample = cxs.reshape(DEC_BATCH, DEC_SEQ, D_B)[:, DEC_SEQ - (CONV_W - 1):, :][None]
    state_chunk_v_sample = vln.reshape(DEC_BATCH, DEC_SEQ, D_A)[None]
    return (y_prompt, y_sample, state_conv_prompt, state_conv_sample, state_chunk_v_sample)
```

```python
import functools

import jax
import jax.numpy as jnp
from jax import lax
from jax.experimental import pallas as pl
from jax.experimental.pallas import tpu as pltpu

F32 = jnp.float32
BF16 = jnp.bfloat16
I32 = jnp.int32

D_MODEL = 2048
BATCH = 4
SEQ = 2048
DEC_BATCH = 128
DEC_SEQ = 8
CHUNK = 128
D_A = D_MODEL // 2
N_GROUPS_A = 8
GW_A = D_A // N_GROUPS_A
D_B = D_MODEL // 2
CONV_W = 3
N_EXPERTS = 32
TOP_K = 4
D_FF = D_MODEL
SWIGLU_LIMIT = 7.0
SWIGLU_ALPHA = 1.702
PLE_DIM = 256
EPS = 1e-6

T_S = DEC_BATCH * DEC_SEQ
T_P = BATCH * SEQ
T = T_S + T_P

LANES = 128
V7X_VMEM_BYTES = 64 * 1024 * 1024
MIB = 1024 * 1024

NORM_TM = 512
IN_TM = 1536
IN_TN = 512
IN_SUB = 256
MIX_TM = CHUNK
MIX_NS = T_S // MIX_TM
MIX_SEQ_TILES = SEQ // MIX_TM
ROW_TILES = D_MODEL // LANES
MOE_RB = 128
MOE_BMAX = 12
MOE_RMAX = MOE_RB * MOE_BMAX
MOE_CHUNKS = (8, 4, 2, 1)
MOE_TF = 256
MOE_TN = 512
MOE_P1 = D_FF // MOE_TF
MOE_P2 = D_MODEL // MOE_TN
MOE_G = MOE_RB // MOE_P1
MOE_YSLOTS = 8
N_SLOTS = T * TOP_K + N_EXPERTS * MOE_RB
N_BLOCKS = N_SLOTS // MOE_RB
MOE_NI = (N_BLOCKS + N_EXPERTS * (MOE_BMAX - 1)) // MOE_BMAX
GATHER_UNROLL = 8
CMB_TM = 256
CMB_NS = T_S // CMB_TM
CMB_CHUNKS = 8
NEG_BIG = -1e30


def _rms(x, g):
    return x * lax.rsqrt(jnp.mean(x * x, axis=-1, keepdims=True) + EPS) * g


def _cparams(sem, vmem_mib):
    return pltpu.CompilerParams(dimension_semantics=sem, vmem_limit_bytes=vmem_mib * MIB)


def _resident(shape):
    zeros = (0,) * len(shape)
    return pl.BlockSpec(shape, lambda *_: zeros, pipeline_mode=pl.Buffered(1))


def _norm_kernel(xs_ref, xp_ref, g_ref, h_ref, *, ns):
    m = pl.program_id(0)
    x = jnp.where(m < ns, xs_ref[...], xp_ref[...])
    h_ref[...] = _rms(x, g_ref[...]).astype(BF16)


def _norm_call(xs, xp, g):
    ns = T_S // NORM_TM
    return pl.pallas_call(
        functools.partial(_norm_kernel, ns=ns),
        grid=(T // NORM_TM,),
        in_specs=[
            pl.BlockSpec((NORM_TM, D_MODEL), lambda m: (jnp.minimum(m, ns - 1), 0)),
            pl.BlockSpec((NORM_TM, D_MODEL), lambda m: (jnp.maximum(m - ns, 0), 0)),
            pl.BlockSpec((1, D_MODEL), lambda m: (0, 0)),
        ],
        out_specs=pl.BlockSpec((NORM_TM, D_MODEL), lambda m: (m, 0)),
        out_shape=jax.ShapeDtypeStruct((T, D_MODEL), BF16),
        compiler_params=_cparams(("arbitrary",), 32),
        name="norm",
    )(xs, xp, g)


IN_N_GELU = 2 * D_A // IN_TN
IN_N_V0 = D_A // IN_TN
IN_N_LIN = (2 * D_A + 3 * D_B) // IN_TN


def _in_proj_kernel(h_ref, w_ref, z_ref, vpre_ref, wb_ref):
    n = pl.program_id(1)
    wb_ref[...] = w_ref[...].astype(BF16)

    def blocks(epilogue):
        for b in range(IN_TM // IN_SUB):
            rows = pl.ds(b * IN_SUB, IN_SUB)
            epilogue(rows, jnp.dot(h_ref[rows, :], wb_ref[...], preferred_element_type=F32))

    @pl.when(n < IN_N_V0)
    def _():
        def ep(rows, acc):
            z_ref[rows, :] = jax.nn.gelu(acc, approximate=True).astype(BF16)

        blocks(ep)

    @pl.when((n >= IN_N_V0) & (n < IN_N_GELU))
    def _():
        def ep(rows, acc):
            g = jax.nn.gelu(acc, approximate=True)
            z_ref[rows, :] = g.astype(BF16)
            vpre_ref[rows, :] = g

        blocks(ep)

    @pl.when((n >= IN_N_GELU) & (n < IN_N_LIN))
    def _():
        def ep(rows, acc):
            z_ref[rows, :] = acc.astype(BF16)

        blocks(ep)

    @pl.when(n >= IN_N_LIN)
    def _():
        def ep(rows, acc):
            z_ref[rows, :] = jax.nn.sigmoid(acc).astype(BF16)

        blocks(ep)


def _in_proj_call(h, w_in):
    d_in = w_in.shape[1]
    return pl.pallas_call(
        _in_proj_kernel,
        grid=(T // IN_TM, d_in // IN_TN),
        in_specs=[
            pl.BlockSpec((IN_TM, D_MODEL), lambda m, n: (m, 0)),
            pl.BlockSpec((D_MODEL, IN_TN), lambda m, n: (0, n)),
        ],
        out_specs=[
            pl.BlockSpec((IN_TM, IN_TN), lambda m, n: (m, n)),
            pl.BlockSpec((IN_TM, IN_TN), lambda m, n: (m, jnp.clip(n - IN_N_V0, 0, IN_N_GELU - IN_N_V0 - 1))),
        ],
        out_shape=[
            jax.ShapeDtypeStruct((T, d_in), BF16),
            jax.ShapeDtypeStruct((T, D_A), F32),
        ],
        scratch_shapes=[pltpu.VMEM((D_MODEL, IN_TN), BF16)],
        compiler_params=_cparams(("arbitrary", "arbitrary"), 44),
        name="in_proj",
    )(h, w_in)


def _mixer_kernel(xs_ref, xp_ref, z_ref, vpre_ref, ext_ref, wsg_ref, bsg_ref, lng_ref, lnb_ref,
                  cw_ref, cb_ref, wa_ref, wb_ref, wo_ref, gmoe_ref, wrh_ref, wrl_ref, br_ref,
                  x1_ref, xn_ref, ri_ref, rg_ref, cnt_ref, vln_ref, cxs_ref, tail_ref,
                  prev_ref, carry_ref):
    m = pl.program_id(0)
    is_s = m < MIX_NS
    tm = MIX_TM

    @pl.when(m == 0)
    def _():
        prev_ref[...] = jnp.zeros_like(prev_ref)
        carry_ref[...] = jnp.zeros_like(carry_ref)

    vg = vpre_ref[...]
    mu = jnp.mean(vg, axis=-1, keepdims=True)
    vc = vg - mu
    v = vc * lax.rsqrt(jnp.mean(vc * vc, axis=-1, keepdims=True) + EPS) * lng_ref[...] + lnb_ref[...]

    @pl.when(is_s)
    def _():
        vln_ref[...] = v

    vb = v.astype(BF16)
    s_parts = []
    for g in range(N_GROUPS_A):
        s_parts.append(jnp.dot(wsg_ref[0, g], vb[:, g * GW_A:(g + 1) * GW_A], preferred_element_type=F32))
    s = jnp.concatenate(s_parts, axis=1) + bsg_ref[0]
    u = z_ref[:, 0:D_A].astype(F32)
    a_in = (u * s).astype(BF16)

    o_b = 2 * D_A
    bg = z_ref[:, o_b:o_b + D_B].astype(F32)
    cg = z_ref[:, o_b + D_B:o_b + 2 * D_B].astype(F32)
    xin = z_ref[:, o_b + 2 * D_B:o_b + 3 * D_B].astype(F32)
    cx = cg * xin

    @pl.when(is_s)
    def _():
        cxs_ref[...] = cx

    @pl.when(jnp.logical_not(is_s))
    def _():
        tail_ref[0] = cx[tm - 8:tm]

    row = lax.broadcasted_iota(I32, (tm, D_B), 0)
    seq_start = ((m - MIX_NS) % MIX_SEQ_TILES) == 0
    prev = jnp.where(seq_start, 0.0, prev_ref[...])
    row8 = lax.broadcasted_iota(I32, (8, D_B), 0)
    top = jnp.where(row8 < CONV_W - 1, pltpu.roll(prev, CONV_W - 1, 0), 0.0)
    ext_p = jnp.concatenate([top, jnp.zeros((tm - 8, D_B), F32)], axis=0)
    ext = jnp.where(is_s, ext_ref[...], ext_p)
    t_in = jnp.where(is_s, row & (DEC_SEQ - 1), row)
    s1 = jnp.where(t_in < 1, pltpu.roll(ext, tm - 1, 0), pltpu.roll(cx, 1, 0))
    s2 = jnp.where(t_in < 2, ext, pltpu.roll(cx, 2, 0))
    prev_ref[...] = cx[tm - 8:tm]
    conv = cb_ref[...] + s2 * cw_ref[0:1, :] + s1 * cw_ref[1:2, :] + cx * cw_ref[2:3, :]
    b_in = (bg * conv).astype(BF16)

    y_a = jnp.dot(a_in, wa_ref[...], preferred_element_type=F32)
    y_b = jnp.dot(b_in, wb_ref[...], preferred_element_type=F32)
    o_g = 2 * D_A + 3 * D_B
    ga = z_ref[:, o_g:o_g + D_MODEL].astype(F32)
    gb = z_ref[:, o_g + D_MODEL:o_g + 2 * D_MODEL].astype(F32)
    mix = (ga * y_a + gb * y_b).astype(BF16)
    x = jnp.where(is_s, xs_ref[...], xp_ref[...])
    x1 = x + jnp.dot(mix, wo_ref[...], preferred_element_type=F32)
    x1_ref[...] = x1

    xn = _rms(x1, gmoe_ref[...])
    for j in range(ROW_TILES):
        xn_ref[pl.ds(j, tm, stride=ROW_TILES), :] = xn[:, j * LANES:(j + 1) * LANES]
    hi = xn.astype(BF16)
    lo = (xn - hi.astype(F32)).astype(BF16)
    logits = (jnp.dot(hi, wrh_ref[...], preferred_element_type=F32)
              + jnp.dot(lo, wrh_ref[...], preferred_element_type=F32)
              + jnp.dot(hi, wrl_ref[...], preferred_element_type=F32)) + br_ref[...]
    lane = lax.broadcasted_iota(I32, (tm, LANES), 1)
    lane_f = lane.astype(F32)
    vals, idxs, hots = [], [], []
    work = logits
    for _ in range(TOP_K):
        mx = jnp.max(work, axis=-1, keepdims=True)
        idx = jnp.min(jnp.where(work == mx, lane_f, float(LANES)), axis=-1, keepdims=True)
        hot = lane_f == idx
        work = jnp.where(hot, -jnp.inf, work)
        vals.append(mx)
        idxs.append(idx)
        hots.append(hot)
    exps = [jnp.exp(vk - vals[0]) for vk in vals]
    den = exps[0] + exps[1] + exps[2] + exps[3]

    chosen = jnp.zeros((tm, LANES), F32)
    for hot in hots:
        chosen = chosen + jnp.where(hot, 1.0, 0.0)
    r_i = lax.broadcasted_iota(I32, (tm, tm), 0)
    c_i = lax.broadcasted_iota(I32, (tm, tm), 1)
    tri = jnp.where(c_i < r_i, 1.0, 0.0).astype(BF16)
    before = jnp.dot(tri, chosen.astype(BF16), preferred_element_type=F32) + carry_ref[0:1, :]
    total = carry_ref[0:1, :] + jnp.sum(chosen, axis=0, keepdims=True)
    carry_ref[...] = jnp.broadcast_to(total, carry_ref.shape)
    cnt_ref[...] = jnp.broadcast_to(total, cnt_ref.shape)

    ri = jnp.zeros((tm, LANES), F32)
    rg = jnp.zeros((tm, LANES), F32)
    for k in range(TOP_K):
        rank = jnp.sum(jnp.where(hots[k], before, 0.0), axis=-1, keepdims=True)
        ri = ri + jnp.where(lane == k, idxs[k], 0.0) + jnp.where(lane == TOP_K + k, rank, 0.0)
        rg = rg + jnp.where(lane == k, exps[k] / den, 0.0)
    ri_ref[...] = ri.astype(I32)
    rg_ref[...] = rg


def _mixer_call(xs, xp, z, vpre, ext, wsg, bsg, ln_g, ln_b, conv_w, conv_b, wa, wb, wo, g_moe,
                wr_hi, wr_lo, b_r):
    tm = MIX_TM
    ns = MIX_NS
    d_in = z.shape[1]
    s_idx = lambda m: (jnp.minimum(m, ns - 1), 0)
    p_idx = lambda m: (jnp.maximum(m - ns, 0), 0)
    row = lambda m: (m, 0)
    return pl.pallas_call(
        _mixer_kernel,
        grid=(T // tm,),
        in_specs=[
            pl.BlockSpec((tm, D_MODEL), s_idx),
            pl.BlockSpec((tm, D_MODEL), p_idx),
            pl.BlockSpec((tm, d_in), row),
            pl.BlockSpec((tm, D_A), row),
            pl.BlockSpec((tm, D_B), s_idx),
            pl.BlockSpec((1, N_GROUPS_A, CHUNK, CHUNK), lambda m: (jnp.minimum(m // ns, 1), 0, 0, 0)),
            pl.BlockSpec((1, CHUNK, D_A), lambda m: (jnp.minimum(m // ns, 1), 0, 0)),
            _resident((1, D_A)),
            _resident((1, D_A)),
            _resident((CONV_W, D_B)),
            _resident((1, D_B)),
            _resident((D_A, D_MODEL)),
            _resident((D_B, D_MODEL)),
            _resident((D_MODEL, D_MODEL)),
            _resident((1, D_MODEL)),
            _resident((D_MODEL, LANES)),
            _resident((D_MODEL, LANES)),
            _resident((1, LANES)),
        ],
        out_specs=[
            pl.BlockSpec((tm, D_MODEL), row),
            pl.BlockSpec((tm * ROW_TILES, LANES), row),
            pl.BlockSpec((tm, LANES), row),
            pl.BlockSpec((tm, LANES), row),
            pl.BlockSpec((8, LANES), lambda m: (0, 0)),
            pl.BlockSpec((tm, D_A), s_idx),
            pl.BlockSpec((tm, D_B), s_idx),
            pl.BlockSpec((1, 8, D_B), lambda m: (jnp.maximum(m - ns, 0), 0, 0)),
        ],
        out_shape=[
            jax.ShapeDtypeStruct((T, D_MODEL), F32),
            jax.ShapeDtypeStruct((T * ROW_TILES, LANES), F32),
            jax.ShapeDtypeStruct((T, LANES), I32),
            jax.ShapeDtypeStruct((T, LANES), F32),
            jax.ShapeDtypeStruct((8, LANES), F32),
            jax.ShapeDtypeStruct((T_S, D_A), F32),
            jax.ShapeDtypeStruct((T_S, D_B), F32),
            jax.ShapeDtypeStruct((T_P // tm, 8, D_B), F32),
        ],
        scratch_shapes=[pltpu.VMEM((8, D_B), F32), pltpu.VMEM((8, LANES), F32)],
        compiler_params=_cparams(("arbitrary",), 52),
        name="mixer",
    )(xs, xp, z, vpre, ext, wsg, bsg, ln_g, ln_b, conv_w, conv_b, wa, wb, wo, g_moe, wr_hi, wr_lo, b_r)


MOE_CHUNK_SLOT0 = {8: 0, 4: 0, 2: 4, 1: 6}


def _for_units(n, unit_fn, chunk_begin=None):
    big = MOE_CHUNKS[0]

    def chunk(u0, count):
        slot0 = MOE_CHUNK_SLOT0[count]
        if chunk_begin is not None:
            chunk_begin(range(slot0, slot0 + count))
        for j in range(count):
            unit_fn(u0 + j, slot0 + j)

    def body(c, carry):
        chunk(c * big, big)
        return carry

    n_big = lax.shift_right_logical(n, big.bit_length() - 1)
    lax.fori_loop(0, n_big, body, 0)
    base = n_big * big
    for count in MOE_CHUNKS[1:]:
        @pl.when((n & count) != 0)
        def _():
            chunk(base, count)

        base = base + (n & count)


def _moe_kernel(tok_ref, ite_ref, iten_ref, row0_ref, nblk_ref, wait_ref, used_ref,
                xn_hbm, wg_ref, wu_ref, wd_ref, bg_ref, bu_ref, bd_ref,
                y_hbm,
                xraw_ref, xb_ref, act_ref, wgu_ref, wdb_ref, ystage_ref, zbuf_ref, gsem, ysem, zsem):
    i = pl.program_id(0)
    s = pl.program_id(1)
    nblk = nblk_ref[i]
    row0 = row0_ref[i]
    valid = nblk > 0
    last_step = (i == MOE_NI - 1) & (s == MOE_P1 + MOE_P2 - 1)

    def tail_copy(b):
        r = pl.multiple_of(b * MOE_RB, MOE_RB)
        return pltpu.make_async_copy(zbuf_ref, y_hbm.at[pl.ds(r, MOE_RB), :], zsem)

    def row_copy(tok, r):
        return pltpu.make_async_copy(
            xn_hbm.at[pl.ds(pl.multiple_of(tok * ROW_TILES, ROW_TILES), ROW_TILES), :],
            xraw_ref.at[pl.ds(pl.multiple_of(r * ROW_TILES, ROW_TILES), ROW_TILES), :],
            gsem)

    def unit_wait():
        n = MOE_RB * ROW_TILES
        return pltpu.make_async_copy(xn_hbm.at[pl.ds(0, n), :], xraw_ref.at[pl.ds(0, n), :], gsem)

    def gather_rows(item, lo_unit, hi_unit):
        base = row0_ref[item]

        def body(c, carry):
            for j in range(GATHER_UNROLL):
                r = c * GATHER_UNROLL + j
                row_copy(tok_ref[base + r], r).start()
            return carry

        per_unit = MOE_RB // GATHER_UNROLL
        lax.fori_loop(lo_unit * per_unit, hi_unit * per_unit, body, 0)

    def y_copy(slot, u, col):
        r = pl.multiple_of(row0 + u * MOE_RB, MOE_RB)
        return pltpu.make_async_copy(ystage_ref.at[slot], y_hbm.at[pl.ds(r, MOE_RB), pl.ds(col, MOE_TN)],
                                     ysem.at[slot])

    def dump_copy(slot):
        r = N_SLOTS + (slot // MOE_P2) * MOE_RB
        c = (slot % MOE_P2) * MOE_TN
        return pltpu.make_async_copy(ystage_ref.at[slot], y_hbm.at[pl.ds(r, MOE_RB), pl.ds(c, MOE_TN)],
                                     ysem.at[slot])

    @pl.when((i == 0) & (s == 0))
    def _():
        gather_rows(0, 0, nblk_ref[0])
        ystage_ref[...] = jnp.zeros_like(ystage_ref)
        for slot in range(MOE_YSLOTS):
            dump_copy(slot).start()
        zbuf_ref[...] = jnp.zeros_like(zbuf_ref)

        def fill(b, carry):
            tail_copy(b).start()
            return carry

        lax.fori_loop(used_ref[0], N_BLOCKS, fill, 0)

    @pl.when(s == 0)
    def _():
        def body(b, carry):
            unit_wait().wait()
            return carry

        lax.fori_loop(0, wait_ref[i], body, 0)

        def conv(u, carry):
            r = pl.multiple_of(u * MOE_RB, MOE_RB)
            parts = [xraw_ref[pl.ds(r * ROW_TILES + j, MOE_RB, stride=ROW_TILES), :] for j in range(ROW_TILES)]
            xb_ref[pl.ds(r, MOE_RB), :] = jnp.concatenate(parts, axis=1).astype(BF16)
            return carry

        lax.fori_loop(0, nblk, conv, 0)

    @pl.when(valid & (s < MOE_P1))
    def _():
        wgu_ref[:, 0:MOE_TF] = wg_ref[0].astype(BF16)
        wgu_ref[:, MOE_TF:2 * MOE_TF] = wu_ref[0].astype(BF16)
        b_g = bg_ref[0]
        b_u = bu_ref[0]
        next_base = row0_ref[i + 1]

        def unit(u, slot):
            r = pl.multiple_of(u * MOE_RB, MOE_RB)
            gu = jnp.dot(xb_ref[pl.ds(r, MOE_RB), :], wgu_ref[...], preferred_element_type=F32)
            gate = jnp.minimum(gu[:, 0:MOE_TF] + b_g, SWIGLU_LIMIT)
            up = jnp.clip(gu[:, MOE_TF:2 * MOE_TF] + b_u, -SWIGLU_LIMIT, SWIGLU_LIMIT)
            act = (up + 1) * (gate * jax.nn.sigmoid(gate * SWIGLU_ALPHA))
            act_ref[s, pl.ds(r, MOE_RB), :] = act.astype(BF16)
            for j in range(MOE_G):
                rr = r + s * MOE_G + j
                row_copy(tok_ref[next_base + rr], rr).start()

        _for_units(nblk, unit)

    @pl.when(valid & (s == MOE_P1))
    def _():
        gather_rows(i + 1, nblk, nblk_ref[i + 1])

    @pl.when(valid & (s >= MOE_P1))
    def _():
        wdb_ref[...] = wd_ref[0].astype(BF16)
        b_d = bd_ref[0]
        col = pl.multiple_of((s - MOE_P1) * MOE_TN, MOE_TN)

        def free_slots(slots):
            for slot in slots:
                y_copy(slot, 0, col).wait()

        def unit(u, slot):
            r = pl.multiple_of(u * MOE_RB, MOE_RB)
            a = jnp.concatenate([act_ref[j, pl.ds(r, MOE_RB), :] for j in range(MOE_P1)], axis=1)
            ystage_ref[slot] = jnp.dot(a, wdb_ref[...], preferred_element_type=F32) + b_d
            y_copy(slot, u, col).start()

        _for_units(nblk, unit, free_slots)

    @pl.when(last_step)
    def _():
        for slot in range(MOE_YSLOTS):
            dump_copy(slot).wait()

        def wait_rows(b, carry):
            unit_wait().wait()
            return carry

        lax.fori_loop(0, wait_ref[MOE_NI], wait_rows, 0)

        def drain(b, carry):
            tail_copy(b).wait()
            return carry

        lax.fori_loop(used_ref[0], N_BLOCKS, drain, 0)


def _moe_call(slot_tok, it_e, it_enext, it_row0, it_nblk, it_wait, n_used, xn, w_gate_up, w_down,
              b_gate_up, b_down):
    up0 = D_FF // MOE_TF

    def phase1_tile(i, s, nb):
        return jnp.where((s < MOE_P1) & (nb[i] > 0), s, 0)

    def gu_expert(i, s, e, en):
        return jnp.where(s < MOE_P1, e[i], en[i])

    def g_map(i, s, tok, e, en, r0, nb, wt, used):
        return (gu_expert(i, s, e, en), 0, phase1_tile(i, s, nb))

    def u_map(i, s, tok, e, en, r0, nb, wt, used):
        return (gu_expert(i, s, e, en), 0, up0 + phase1_tile(i, s, nb))

    def d_map(i, s, tok, e, en, r0, nb, wt, used):
        return (e[i], 0, jnp.where((s >= MOE_P1) & (nb[i] > 0), s - MOE_P1, 0))

    grid_spec = pltpu.PrefetchScalarGridSpec(
        num_scalar_prefetch=7,
        grid=(MOE_NI, MOE_P1 + MOE_P2),
        in_specs=[
            pl.BlockSpec(memory_space=pl.ANY),
            pl.BlockSpec((1, D_MODEL, MOE_TF), g_map),
            pl.BlockSpec((1, D_MODEL, MOE_TF), u_map),
            pl.BlockSpec((1, D_FF, MOE_TN), d_map),
            pl.BlockSpec((1, 1, MOE_TF), g_map),
            pl.BlockSpec((1, 1, MOE_TF), u_map),
            pl.BlockSpec((1, 1, MOE_TN), d_map),
        ],
        out_specs=pl.BlockSpec(memory_space=pl.ANY),
        scratch_shapes=[
            pltpu.VMEM((MOE_RMAX * ROW_TILES, LANES), F32),
            pltpu.VMEM((MOE_RMAX, D_MODEL), BF16),
            pltpu.VMEM((MOE_P1, MOE_RMAX, MOE_TF), BF16),
            pltpu.VMEM((D_MODEL, 2 * MOE_TF), BF16),
            pltpu.VMEM((D_FF, MOE_TN), BF16),
            pltpu.VMEM((MOE_YSLOTS, MOE_RB, MOE_TN), F32),
            pltpu.VMEM((MOE_RB, D_MODEL), F32),
            pltpu.SemaphoreType.DMA(()),
            pltpu.SemaphoreType.DMA((MOE_YSLOTS,)),
            pltpu.SemaphoreType.DMA(()),
        ],
    )
    spare_blocks = MOE_YSLOTS // MOE_P2
    return pl.pallas_call(
        _moe_kernel,
        grid_spec=grid_spec,
        out_shape=jax.ShapeDtypeStruct((N_SLOTS + spare_blocks * MOE_RB, D_MODEL), F32),
        compiler_params=_cparams(("arbitrary", "arbitrary"), 56),
        name="moe",
    )(slot_tok, it_e, it_enext, it_row0, it_nblk, it_wait, n_used, xn, w_gate_up, w_gate_up, w_down,
      b_gate_up, b_gate_up, b_down)


def _combine_kernel(dest_ref, y_hbm, x1_ref, rg_ref, ps_ref, pp_ref, wple_ref, wpg_ref, gple_ref,
                    gfin_ref, ys_ref, yp_ref, gbuf_ref, gsem):
    m = pl.program_id(0)
    nm = pl.num_programs(0)
    tm = CMB_TM
    slot = m % 2

    def row_copy(tile, slot_, r, k):
        d = dest_ref[(tile * tm + r) * TOP_K + k]
        return pltpu.make_async_copy(y_hbm.at[pl.ds(d, 1), :], gbuf_ref.at[slot_, k, pl.ds(r, 1), :],
                                     gsem.at[slot_])

    def wait_tile(slot_):
        for k in range(TOP_K):
            pltpu.make_async_copy(y_hbm.at[pl.ds(0, tm), :], gbuf_ref.at[slot_, k], gsem.at[slot_]).wait()

    @pl.when(m == 0)
    def _():
        def body(c, carry):
            for j in range(GATHER_UNROLL // TOP_K):
                for k in range(TOP_K):
                    row_copy(0, 0, c * (GATHER_UNROLL // TOP_K) + j, k).start()
            return carry

        lax.fori_loop(0, tm // (GATHER_UNROLL // TOP_K), body, 0)

    wait_tile(slot)
    gates = rg_ref[...]
    moe = gates[:, 0:1] * gbuf_ref[slot, 0]
    for k in range(1, TOP_K):
        moe = moe + gates[:, k:k + 1] * gbuf_ref[slot, k]
    x2 = x1_ref[...] + moe
    is_s = m < CMB_NS
    p = jnp.where(is_s, ps_ref[...], pp_ref[...]).astype(BF16)
    hn = _rms(x2, gple_ref[...]).astype(BF16)
    nxt = jnp.minimum(m + 1, nm - 1)
    rows_per_chunk = tm // CMB_CHUNKS
    cw = D_MODEL // CMB_CHUNKS
    x3_parts = []
    for c in range(CMB_CHUNKS):
        cols = slice(c * cw, (c + 1) * cw)
        pe = jnp.dot(p, wple_ref[:, cols], preferred_element_type=F32)
        gate = jax.nn.sigmoid(jnp.dot(hn, wpg_ref[:, cols], preferred_element_type=F32))
        x3_parts.append(x2[:, cols] + pe * gate)
        for r in range(c * rows_per_chunk, (c + 1) * rows_per_chunk):
            for k in range(TOP_K):
                row_copy(nxt, 1 - slot, r, k).start()
    x3 = jnp.concatenate(x3_parts, axis=1)
    y = _rms(x3, gfin_ref[...])

    @pl.when(is_s)
    def _():
        ys_ref[...] = y

    @pl.when(jnp.logical_not(is_s))
    def _():
        yp_ref[...] = y

    @pl.when(m == nm - 1)
    def _():
        wait_tile(1 - slot)


def _combine_call(dest, y_sorted, x1, rg, ps, pp, wple, wpg, g_ple, g_final):
    tm = CMB_TM
    ns = CMB_NS
    s_idx = lambda m, d: (jnp.minimum(m, ns - 1), 0)
    p_idx = lambda m, d: (jnp.maximum(m - ns, 0), 0)
    row = lambda m, d: (m, 0)
    const2 = lambda m, d: (0, 0)
    grid_spec = pltpu.PrefetchScalarGridSpec(
        num_scalar_prefetch=1,
        grid=(T // tm,),
        in_specs=[
            pl.BlockSpec(memory_space=pl.ANY),
            pl.BlockSpec((tm, D_MODEL), row),
            pl.BlockSpec((tm, LANES), row),
            pl.BlockSpec((tm, PLE_DIM), s_idx),
            pl.BlockSpec((tm, PLE_DIM), p_idx),
            pl.BlockSpec((PLE_DIM, D_MODEL), const2),
            pl.BlockSpec((D_MODEL, D_MODEL), const2),
            pl.BlockSpec((1, D_MODEL), const2),
            pl.BlockSpec((1, D_MODEL), const2),
        ],
        out_specs=[
            pl.BlockSpec((tm, D_MODEL), s_idx),
            pl.BlockSpec((tm, D_MODEL), p_idx),
        ],
        scratch_shapes=[
            pltpu.VMEM((2, TOP_K, tm, D_MODEL), F32),
            pltpu.SemaphoreType.DMA((2,)),
        ],
    )
    return pl.pallas_call(
        _combine_kernel,
        grid_spec=grid_spec,
        out_shape=[
            jax.ShapeDtypeStruct((T_S, D_MODEL), F32),
            jax.ShapeDtypeStruct((T_P, D_MODEL), F32),
        ],
        compiler_params=_cparams(("arbitrary",), 56),
        name="combine",
    )(dest, y_sorted, x1, rg, ps, pp, wple, wpg, g_ple, g_final)


def _routing_tables(route_i, counts):
    e_idx = route_i[:, 0:TOP_K]
    rank = route_i[:, TOP_K:2 * TOP_K]
    cnt = counts[0, :N_EXPERTS].astype(I32)
    padded = (cnt + MOE_RB - 1) // MOE_RB * MOE_RB
    gend = jnp.cumsum(padded)
    gstart = gend - padded
    dest = (gstart[e_idx] + rank).astype(I32)
    tok_of = jnp.repeat(jnp.arange(T, dtype=I32), TOP_K)
    slot_tok = jnp.zeros((N_SLOTS + MOE_RMAX,), I32).at[dest.reshape(-1)].set(tok_of)
    nblk_e = padded // MOE_RB
    items_e = (nblk_e + MOE_BMAX - 1) // MOE_BMAX
    iend = jnp.cumsum(items_e)
    istart = iend - items_e
    n_items = iend[-1]
    ids = jnp.arange(MOE_NI + 1, dtype=I32)
    valid = ids < n_items
    e_of = jnp.minimum(jnp.searchsorted(iend, ids, side="right"), N_EXPERTS - 1).astype(I32)
    e_last = jnp.minimum(jnp.searchsorted(iend, n_items - 1, side="right"), N_EXPERTS - 1).astype(I32)
    it_e = jnp.where(valid, e_of, e_last)
    local = ids - istart[it_e]
    it_nblk = jnp.where(valid, jnp.minimum(MOE_BMAX, nblk_e[it_e] - local * MOE_BMAX), 0).astype(I32)
    it_row0 = jnp.where(valid, gstart[it_e] + local * MOE_RMAX, 0).astype(I32)
    it_enext = it_e[jnp.minimum(ids + 1, MOE_NI)]
    prev_nblk = jnp.concatenate([jnp.zeros((1,), I32), it_nblk[:-1]])
    it_wait = jnp.maximum(it_nblk, prev_nblk)
    n_used = jnp.sum(nblk_e).astype(I32).reshape(1)
    return (dest.reshape(-1), slot_tok, it_e.astype(I32), it_enext.astype(I32), it_row0, it_nblk, it_wait,
            n_used)


def kernel(x_prompt, x_sample, state_conv, p_prompt, p_sample, g_mix, w_in, ln_v_g, ln_v_b, w_s, b_s,
           conv_w, conv_b, w_proj_a, w_proj_b, w_o, g_moe, w_router, b_router, w_gate_up, b_gate_up,
           w_down, b_down, g_ple, w_ple, w_ple_gate, g_final):
    assert g_mix.shape[0] == 1, "one layer"
    xs = x_sample.reshape(T_S, D_MODEL)
    xp = x_prompt.reshape(T_P, D_MODEL)

    tril = jnp.tril(jnp.ones((CHUNK, CHUNK), bool))
    w_prompt = jnp.where(tril[None], w_s[0], 0.0)
    small = jnp.where(tril[None, :DEC_SEQ, :DEC_SEQ], w_s[0, :, :DEC_SEQ, :DEC_SEQ], 0.0)
    reps = CHUNK // DEC_SEQ
    blockdiag = jnp.kron(jnp.eye(reps, dtype=F32), jnp.ones((DEC_SEQ, DEC_SEQ), F32))
    w_sample = jnp.tile(small, (1, reps, reps)) * blockdiag[None]
    wsg = jnp.stack([w_sample, w_prompt]).astype(BF16)
    bias_p = jnp.repeat(b_s[0].T, GW_A, axis=1)
    bias_s = jnp.tile(jnp.repeat(b_s[0, :, :DEC_SEQ].T, GW_A, axis=1), (reps, 1))
    bsg = jnp.stack([bias_s, bias_p])
    ext = jnp.pad(state_conv[0], ((0, 0), (0, DEC_SEQ - (CONV_W - 1)), (0, 0))).reshape(T_S, D_B)

    wr = jnp.pad(w_router[0], ((0, 0), (0, LANES - N_EXPERTS)))
    wr_hi = wr.astype(BF16)
    wr_lo = (wr - wr_hi.astype(F32)).astype(BF16)
    b_r = jnp.pad(b_router[0], (0, LANES - N_EXPERTS), constant_values=NEG_BIG).reshape(1, LANES)

    h = _norm_call(xs, xp, g_mix)
    z, vpre = _in_proj_call(h, w_in[0])
    x1, xn, route_i, route_g, counts, vln, cxs, tail = _mixer_call(
        xs, xp, z, vpre, ext, wsg, bsg, ln_v_g, ln_v_b, conv_w[0], conv_b,
        w_proj_a[0].astype(BF16), w_proj_b[0].astype(BF16), w_o[0].astype(BF16), g_moe,
        wr_hi, wr_lo, b_r)

    dest, slot_tok, it_e, it_enext, it_row0, it_nblk, it_wait, n_used = _routing_tables(route_i, counts)
    y_sorted = _moe_call(slot_tok, it_e, it_enext, it_row0, it_nblk, it_wait, n_used, xn, w_gate_up[0], w_down[0],
                         b_gate_up[0].reshape(N_EXPERTS, 1, 2 * D_FF), b_down[0].reshape(N_EXPERTS, 1, D_MODEL))
    ys, yp = _combine_call(dest, y_sorted, x1, route_g,
                           p_sample[0].reshape(T_S, PLE_DIM), p_prompt[0].reshape(T_P, PLE_DIM),
                           w_ple[0].astype(BF16), w_ple_gate[0].astype(BF16), g_ple, g_final.reshape(1, D_MODEL))

    y_prompt = yp.reshape(BATCH, SEQ, D_MODEL)
    y_sample = ys.reshape(DEC_BATCH, DEC_SEQ, D_MODEL)
    last = tail.reshape(BATCH, MIX_SEQ_TILES, 8, D_B)[:, -1, 8 - (CONV_W - 1):, :]
    state_conv_prompt = last[None]
    state_conv_sample = cxs.reshape(DEC_BATCH, DEC_SEQ, D_B)[:, DEC_SEQ - (CONV_W - 1):, :][None]
    state_chunk_v_sample = vln.reshape(DEC_BATCH, DEC_SEQ, D_A)[None]
    return (y_prompt, y_sample, state_conv_prompt, state_conv_sample, state_chunk_v_sample)
```

```python
import functools

import jax
import jax.numpy as jnp
from jax import lax
from jax.experimental import pallas as pl
from jax.experimental.pallas import tpu as pltpu

F32 = jnp.float32
BF16 = jnp.bfloat16
I32 = jnp.int32

D_MODEL = 2048
BATCH = 4
SEQ = 2048
DEC_BATCH = 128
DEC_SEQ = 8
CHUNK = 128
D_A = D_MODEL // 2
N_GROUPS_A = 8
GW_A = D_A // N_GROUPS_A
D_B = D_MODEL // 2
CONV_W = 3
N_EXPERTS = 32
TOP_K = 4
D_FF = D_MODEL
SWIGLU_LIMIT = 7.0
SWIGLU_ALPHA = 1.702
PLE_DIM = 256
EPS = 1e-6

T_S = DEC_BATCH * DEC_SEQ
T_P = BATCH * SEQ
T = T_S + T_P

LANES = 128
V7X_VMEM_BYTES = 64 * 1024 * 1024
MIB = 1024 * 1024

NORM_TM = 512
IN_TM = 1536
IN_TN = 512
IN_SUB = 256
MIX_TM = CHUNK
MIX_NS = T_S // MIX_TM
MIX_SEQ_TILES = SEQ // MIX_TM
ROW_TILES = D_MODEL // LANES
MOE_RB = 128
MOE_BMAX = 12
MOE_RMAX = MOE_RB * MOE_BMAX
MOE_CHUNKS = (8, 4, 2, 1)
MOE_TF = 256
MOE_TN = 512
MOE_P1 = D_FF // MOE_TF
MOE_P2 = D_MODEL // MOE_TN
MOE_G = MOE_RB // MOE_P1
MOE_YSLOTS = 8
N_SLOTS = T * TOP_K + N_EXPERTS * MOE_RB
N_BLOCKS = N_SLOTS // MOE_RB
MOE_NI = (N_BLOCKS + N_EXPERTS * (MOE_BMAX - 1)) // MOE_BMAX
GATHER_UNROLL = 8
CMB_TM = 256
CMB_NS = T_S // CMB_TM
CMB_CHUNKS = 8
NEG_BIG = -1e30


def _rms(x, g):
    return x * lax.rsqrt(jnp.mean(x * x, axis=-1, keepdims=True) + EPS) * g


def _cparams(sem, vmem_mib):
    return pltpu.CompilerParams(dimension_semantics=sem, vmem_limit_bytes=vmem_mib * MIB)


def _resident(shape):
    zeros = (0,) * len(shape)
    return pl.BlockSpec(shape, lambda *_: zeros, pipeline_mode=pl.Buffered(1))


def _norm_kernel(xs_ref, xp_ref, g_ref, h_ref, *, ns):
    m = pl.program_id(0)
    x = jnp.where(m < ns, xs_ref[...], xp_ref[...])
    h_ref[...] = _rms(x, g_ref[...]).astype(BF16)


def _norm_call(xs, xp, g):
    ns = T_S // NORM_TM
    return pl.pallas_call(
        functools.partial(_norm_kernel, ns=ns),
        grid=(T // NORM_TM,),
        in_specs=[
            pl.BlockSpec((NORM_TM, D_MODEL), lambda m: (jnp.minimum(m, ns - 1), 0)),
            pl.BlockSpec((NORM_TM, D_MODEL), lambda m: (jnp.maximum(m - ns, 0), 0)),
            pl.BlockSpec((1, D_MODEL), lambda m: (0, 0)),
        ],
        out_specs=pl.BlockSpec((NORM_TM, D_MODEL), lambda m: (m, 0)),
        out_shape=jax.ShapeDtypeStruct((T, D_MODEL), BF16),
        compiler_params=_cparams(("arbitrary",), 32),
        name="norm",
    )(xs, xp, g)


IN_N_GELU = 2 * D_A // IN_TN
IN_N_V0 = D_A // IN_TN
IN_N_LIN = (2 * D_A + 3 * D_B) // IN_TN


def _in_proj_kernel(h_ref, w_ref, z_ref, vpre_ref, wb_ref):
    n = pl.program_id(1)
    wb_ref[...] = w_ref[...].astype(BF16)

    def blocks(epilogue):
        for b in range(IN_TM // IN_SUB):
            rows = pl.ds(b * IN_SUB, IN_SUB)
            epilogue(rows, jnp.dot(h_ref[rows, :], wb_ref[...], preferred_element_type=F32))

    @pl.when(n < IN_N_V0)
    def _():
        def ep(rows, acc):
            z_ref[rows, :] = jax.nn.gelu(acc, approximate=True).astype(BF16)

        blocks(ep)

    @pl.when((n >= IN_N_V0) & (n < IN_N_GELU))
    def _():
        def ep(rows, acc):
            g = jax.nn.gelu(acc, approximate=True)
            z_ref[rows, :] = g.astype(BF16)
            vpre_ref[rows, :] = g

        blocks(ep)

    @pl.when((n >= IN_N_GELU) & (n < IN_N_LIN))
    def _():
        def ep(rows, acc):
            z_ref[rows, :] = acc.astype(BF16)

        blocks(ep)

    @pl.when(n >= IN_N_LIN)
    def _():
        def ep(rows, acc):
            z_ref[rows, :] = jax.nn.sigmoid(acc).astype(BF16)

        blocks(ep)


def _in_proj_call(h, w_in):
    d_in = w_in.shape[1]
    return pl.pallas_call(
        _in_proj_kernel,
        grid=(T // IN_TM, d_in // IN_TN),
        in_specs=[
            pl.BlockSpec((IN_TM, D_MODEL), lambda m, n: (m, 0)),
            pl.BlockSpec((D_MODEL, IN_TN), lambda m, n: (0, n)),
        ],
        out_specs=[
            pl.BlockSpec((IN_TM, IN_TN), lambda m, n: (m, n)),
            pl.BlockSpec((IN_TM, IN_TN), lambda m, n: (m, jnp.clip(n - IN_N_V0, 0, IN_N_GELU - IN_N_V0 - 1))),
        ],
        out_shape=[
            jax.ShapeDtypeStruct((T, d_in), BF16),
            jax.ShapeDtypeStruct((T, D_A), F32),
        ],
        scratch_shapes=[pltpu.VMEM((D_MODEL, IN_TN), BF16)],
        compiler_params=_cparams(("arbitrary", "arbitrary"), 44),
        name="in_proj",
    )(h, w_in)


def _mixer_kernel(xs_ref, xp_ref, z_ref, vpre_ref, ext_ref, wsg_ref, bsg_ref, lng_ref, lnb_ref,
                  cw_ref, cb_ref, wa_ref, wb_ref, wo_ref, gmoe_ref, wrh_ref, wrl_ref, br_ref,
                  x1_ref, xn_ref, ri_ref, rg_ref, cnt_ref, vln_ref, cxs_ref, tail_ref,
                  prev_ref, carry_ref):
    m = pl.program_id(0)
    is_s = m < MIX_NS
    tm = MIX_TM

    @pl.when(m == 0)
    def _():
        prev_ref[...] = jnp.zeros_like(prev_ref)
        carry_ref[...] = jnp.zeros_like(carry_ref)

    vg = vpre_ref[...]
    mu = jnp.mean(vg, axis=-1, keepdims=True)
    vc = vg - mu
    v = vc * lax.rsqrt(jnp.mean(vc * vc, axis=-1, keepdims=True) + EPS) * lng_ref[...] + lnb_ref[...]

    @pl.when(is_s)
    def _():
        vln_ref[...] = v

    vb = v.astype(BF16)
    s_parts = []
    for g in range(N_GROUPS_A):
        s_parts.append(jnp.dot(wsg_ref[0, g], vb[:, g * GW_A:(g + 1) * GW_A], preferred_element_type=F32))
    s = jnp.concatenate(s_parts, axis=1) + bsg_ref[0]
    u = z_ref[:, 0:D_A].astype(F32)
    a_in = (u * s).astype(BF16)

    o_b = 2 * D_A
    bg = z_ref[:, o_b:o_b + D_B].astype(F32)
    cg = z_ref[:, o_b + D_B:o_b + 2 * D_B].astype(F32)
    xin = z_ref[:, o_b + 2 * D_B:o_b + 3 * D_B].astype(F32)
    cx = cg * xin

    @pl.when(is_s)
    def _():
        cxs_ref[...] = cx

    @pl.when(jnp.logical_not(is_s))
    def _():
        tail_ref[0] = cx[tm - 8:tm]

    row = lax.broadcasted_iota(I32, (tm, D_B), 0)
    seq_start = ((m - MIX_NS) % MIX_SEQ_TILES) == 0
    prev = jnp.where(seq_start, 0.0, prev_ref[...])
    row8 = lax.broadcasted_iota(I32, (8, D_B), 0)
    top = jnp.where(row8 < CONV_W - 1, pltpu.roll(prev, CONV_W - 1, 0), 0.0)
    ext_p = jnp.concatenate([top, jnp.zeros((tm - 8, D_B), F32)], axis=0)
    ext = jnp.where(is_s, ext_ref[...], ext_p)
    t_in = jnp.where(is_s, row & (DEC_SEQ - 1), row)
    s1 = jnp.where(t_in < 1, pltpu.roll(ext, tm - 1, 0), pltpu.roll(cx, 1, 0))
    s2 = jnp.where(t_in < 2, ext, pltpu.roll(cx, 2, 0))
    prev_ref[...] = cx[tm - 8:tm]
    conv = cb_ref[...] + s2 * cw_ref[0:1, :] + s1 * cw_ref[1:2, :] + cx * cw_ref[2:3, :]
    b_in = (bg * conv).astype(BF16)

    y_a = jnp.dot(a_in, wa_ref[...], preferred_element_type=F32)
    y_b = jnp.dot(b_in, wb_ref[...], preferred_element_type=F32)
    o_g = 2 * D_A + 3 * D_B
    ga = z_ref[:, o_g:o_g + D_MODEL].astype(F32)
    gb = z_ref[:, o_g + D_MODEL:o_g + 2 * D_MODEL].astype(F32)
    mix = (ga * y_a + gb * y_b).astype(BF16)
    x = jnp.where(is_s, xs_ref[...], xp_ref[...])
    x1 = x + jnp.dot(mix, wo_ref[...], preferred_element_type=F32)
    x1_ref[...] = x1

    xn = _rms(x1, gmoe_ref[...])
    for j in range(ROW_TILES):
        xn_ref[pl.ds(j, tm, stride=ROW_TILES), :] = xn[:, j * LANES:(j + 1) * LANES]
    hi = xn.astype(BF16)
    lo = (xn - hi.astype(F32)).astype(BF16)
    logits = (jnp.dot(hi, wrh_ref[...], preferred_element_type=F32)
              + jnp.dot(lo, wrh_ref[...], preferred_element_type=F32)
              + jnp.dot(hi, wrl_ref[...], preferred_element_type=F32)) + br_ref[...]
    lane = lax.broadcasted_iota(I32, (tm, LANES), 1)
    lane_f = lane.astype(F32)
    vals, idxs, hots = [], [], []
    work = logits
    for _ in range(TOP_K):
        mx = jnp.max(work, axis=-1, keepdims=True)
        idx = jnp.min(jnp.where(work == mx, lane_f, float(LANES)), axis=-1, keepdims=True)
        hot = lane_f == idx
        work = jnp.where(hot, -jnp.inf, work)
        vals.append(mx)
        idxs.append(idx)
        hots.append(hot)
    exps = [jnp.exp(vk - vals[0]) for vk in vals]
    den = exps[0] + exps[1] + exps[2] + exps[3]

    chosen = jnp.zeros((tm, LANES), F32)
    for hot in hots:
        chosen = chosen + jnp.where(hot, 1.0, 0.0)
    r_i = lax.broadcasted_iota(I32, (tm, tm), 0)
    c_i = lax.broadcasted_iota(I32, (tm, tm), 1)
    tri = jnp.where(c_i < r_i, 1.0, 0.0).astype(BF16)
    before = jnp.dot(tri, chosen.astype(BF16), preferred_element_type=F32) + carry_ref[0:1, :]
    total = carry_ref[0:1, :] + jnp.sum(chosen, axis=0, keepdims=True)
    carry_ref[...] = jnp.broadcast_to(total, carry_ref.shape)
    cnt_ref[...] = jnp.broadcast_to(total, cnt_ref.shape)

    ri = jnp.zeros((tm, LANES), F32)
    rg = jnp.zeros((tm, LANES), F32)
    for k in range(TOP_K):
        rank = jnp.sum(jnp.where(hots[k], before, 0.0), axis=-1, keepdims=True)
        ri = ri + jnp.where(lane == k, idxs[k], 0.0) + jnp.where(lane == TOP_K + k, rank, 0.0)
        rg = rg + jnp.where(lane == k, exps[k] / den, 0.0)
    ri_ref[...] = ri.astype(I32)
    rg_ref[...] = rg


def _mixer_call(xs, xp, z, vpre, ext, wsg, bsg, ln_g, ln_b, conv_w, conv_b, wa, wb, wo, g_moe,
                wr_hi, wr_lo, b_r):
    tm = MIX_TM
    ns = MIX_NS
    d_in = z.shape[1]
    s_idx = lambda m: (jnp.minimum(m, ns - 1), 0)
    p_idx = lambda m: (jnp.maximum(m - ns, 0), 0)
    row = lambda m: (m, 0)
    return pl.pallas_call(
        _mixer_kernel,
        grid=(T // tm,),
        in_specs=[
            pl.BlockSpec((tm, D_MODEL), s_idx),
            pl.BlockSpec((tm, D_MODEL), p_idx),
            pl.BlockSpec((tm, d_in), row),
            pl.BlockSpec((tm, D_A), row),
            pl.BlockSpec((tm, D_B), s_idx),
            pl.BlockSpec((1, N_GROUPS_A, CHUNK, CHUNK), lambda m: (jnp.minimum(m // ns, 1), 0, 0, 0)),
            pl.BlockSpec((1, CHUNK, D_A), lambda m: (jnp.minimum(m // ns, 1), 0, 0)),
            _resident((1, D_A)),
            _resident((1, D_A)),
            _resident((CONV_W, D_B)),
            _resident((1, D_B)),
            _resident((D_A, D_MODEL)),
            _resident((D_B, D_MODEL)),
            _resident((D_MODEL, D_MODEL)),
            _resident((1, D_MODEL)),
            _resident((D_MODEL, LANES)),
            _resident((D_MODEL, LANES)),
            _resident((1, LANES)),
        ],
        out_specs=[
            pl.BlockSpec((tm, D_MODEL), row),
            pl.BlockSpec((tm * ROW_TILES, LANES), row),
            pl.BlockSpec((tm, LANES), row),
            pl.BlockSpec((tm, LANES), row),
            pl.BlockSpec((8, LANES), lambda m: (0, 0)),
            pl.BlockSpec((tm, D_A), s_idx),
            pl.BlockSpec((tm, D_B), s_idx),
            pl.BlockSpec((1, 8, D_B), lambda m: (jnp.maximum(m - ns, 0), 0, 0)),
        ],
        out_shape=[
            jax.ShapeDtypeStruct((T, D_MODEL), F32),
            jax.ShapeDtypeStruct((T * ROW_TILES, LANES), F32),
            jax.ShapeDtypeStruct((T, LANES), I32),
            jax.ShapeDtypeStruct((T, LANES), F32),
            jax.ShapeDtypeStruct((8, LANES), F32),
            jax.ShapeDtypeStruct((T_S, D_A), F32),
            jax.ShapeDtypeStruct((T_S, D_B), F32),
            jax.ShapeDtypeStruct((T_P // tm, 8, D_B), F32),
        ],
        scratch_shapes=[pltpu.VMEM((8, D_B), F32), pltpu.VMEM((8, LANES), F32)],
        compiler_params=_cparams(("arbitrary",), 52),
        name="mixer",
    )(xs, xp, z, vpre, ext, wsg, bsg, ln_g, ln_b, conv_w, conv_b, wa, wb, wo, g_moe, wr_hi, wr_lo, b_r)


MOE_CHUNK_SLOT0 = {8: 0, 4: 0, 2: 4, 1: 6}


def _for_units(n, unit_fn, chunk_begin=None):
    big = MOE_CHUNKS[0]

    def chunk(u0, count):
        slot0 = MOE_CHUNK_SLOT0[count]
        if chunk_begin is not None:
            chunk_begin(range(slot0, slot0 + count))
        for j in range(count):
            unit_fn(u0 + j, slot0 + j)

    def body(c, carry):
        chunk(c * big, big)
        return carry

    n_big = lax.shift_right_logical(n, big.bit_length() - 1)
    lax.fori_loop(0, n_big, body, 0)
    base = n_big * big
    for count in MOE_CHUNKS[1:]:
        @pl.when((n & count) != 0)
        def _():
            chunk(base, count)

        base = base + (n & count)


def _moe_kernel(tok_ref, ite_ref, row0_ref, nblk_ref, wait_ref, used_ref,
                xn_hbm, wgu_hbm, wd_hbm, bgu_hbm, bd_hbm,
                y_hbm,
                xraw_ref, xb_ref, act_ref, wg_st, wu_st, wd_st, wgu_ref, wdb_ref, bgu_ref, bd_ref,
                ystage_ref, zbuf_ref, gsem, ysem, zsem, wsem_gu, wsem_d, bsem):
    i = pl.program_id(0)
    nblk = nblk_ref[i]
    row0 = row0_ref[i]
    valid = nblk > 0
    par = i & 1

    def gu_copies(e, tile, slot):
        col = pl.multiple_of(tile * MOE_TF, MOE_TF)
        return (pltpu.make_async_copy(wgu_hbm.at[e, :, pl.ds(col, MOE_TF)], wg_st.at[slot], wsem_gu.at[slot]),
                pltpu.make_async_copy(wgu_hbm.at[e, :, pl.ds(D_FF + col, MOE_TF)], wu_st.at[slot],
                                      wsem_gu.at[slot]))

    def d_copy(e, tile, slot):
        col = pl.multiple_of(tile * MOE_TN, MOE_TN)
        return pltpu.make_async_copy(wd_hbm.at[e, :, pl.ds(col, MOE_TN)], wd_st.at[slot], wsem_d.at[slot])

    def bias_copies(e, slot):
        return (pltpu.make_async_copy(bgu_hbm.at[e], bgu_ref.at[slot], bsem.at[slot]),
                pltpu.make_async_copy(bd_hbm.at[e], bd_ref.at[slot], bsem.at[slot]))

    def tail_copy(b):
        r = pl.multiple_of(b * MOE_RB, MOE_RB)
        return pltpu.make_async_copy(zbuf_ref, y_hbm.at[pl.ds(r, MOE_RB), :], zsem)

    def row_copy(tok, r):
        return pltpu.make_async_copy(
            xn_hbm.at[pl.ds(pl.multiple_of(tok * ROW_TILES, ROW_TILES), ROW_TILES), :],
            xraw_ref.at[pl.ds(pl.multiple_of(r * ROW_TILES, ROW_TILES), ROW_TILES), :],
            gsem)

    def unit_wait():
        n = MOE_RB * ROW_TILES
        return pltpu.make_async_copy(xn_hbm.at[pl.ds(0, n), :], xraw_ref.at[pl.ds(0, n), :], gsem)

    def gather_rows(item, lo_unit, hi_unit):
        base = row0_ref[item]

        def body(c, carry):
            for j in range(GATHER_UNROLL):
                r = c * GATHER_UNROLL + j
                row_copy(tok_ref[base + r], r).start()
            return carry

        per_unit = MOE_RB // GATHER_UNROLL
        lax.fori_loop(lo_unit * per_unit, hi_unit * per_unit, body, 0)

    def y_copy(slot, u, col):
        r = pl.multiple_of(row0 + u * MOE_RB, MOE_RB)
        return pltpu.make_async_copy(ystage_ref.at[slot], y_hbm.at[pl.ds(r, MOE_RB), pl.ds(col, MOE_TN)],
                                     ysem.at[slot])

    def dump_copy(slot):
        r = N_SLOTS + (slot // MOE_P2) * MOE_RB
        c = (slot % MOE_P2) * MOE_TN
        return pltpu.make_async_copy(ystage_ref.at[slot], y_hbm.at[pl.ds(r, MOE_RB), pl.ds(c, MOE_TN)],
                                     ysem.at[slot])

    @pl.when(i == 0)
    def _():
        for c in gu_copies(ite_ref[0], 0, 0) + bias_copies(ite_ref[0], 0):
            c.start()
        gather_rows(0, 0, nblk_ref[0])
        ystage_ref[...] = jnp.zeros_like(ystage_ref)
        for slot in range(MOE_YSLOTS):
            dump_copy(slot).start()
        zbuf_ref[...] = jnp.zeros_like(zbuf_ref)

        def fill(b, carry):
            tail_copy(b).start()
            return carry

        lax.fori_loop(used_ref[0], N_BLOCKS, fill, 0)

    def wait_rows(b, carry):
        unit_wait().wait()
        return carry

    lax.fori_loop(0, wait_ref[i], wait_rows, 0)

    def conv(u, carry):
        r = pl.multiple_of(u * MOE_RB, MOE_RB)
        parts = [xraw_ref[pl.ds(r * ROW_TILES + j, MOE_RB, stride=ROW_TILES), :] for j in range(ROW_TILES)]
        xb_ref[pl.ds(r, MOE_RB), :] = jnp.concatenate(parts, axis=1).astype(BF16)
        return carry

    lax.fori_loop(0, nblk, conv, 0)

    e = ite_ref[i]
    e_next = ite_ref[i + 1]
    next_valid = nblk_ref[i + 1] > 0
    next_base = row0_ref[i + 1]

    @pl.when(valid)
    def _():
        for c in bias_copies(e, par):
            c.wait()

    def gate_up_step(s, carry):
        slot = s & 1
        for c in gu_copies(e, s, slot):
            c.wait()

        @pl.when(s + 1 < MOE_P1)
        def _():
            for c in gu_copies(e, s + 1, 1 - slot):
                c.start()

        @pl.when((s + 1 == MOE_P1) & next_valid)
        def _():
            for c in gu_copies(e_next, 0, 1 - slot) + bias_copies(e_next, 1 - par):
                c.start()

        @pl.when(s == 0)
        def _():
            d_copy(e, 0, 0).start()

        wgu_ref[:, 0:MOE_TF] = wg_st[slot].astype(BF16)
        wgu_ref[:, MOE_TF:2 * MOE_TF] = wu_st[slot].astype(BF16)
        b_g = bgu_ref[par, pl.ds(s, 1), :]
        b_u = bgu_ref[par, pl.ds(MOE_P1 + s, 1), :]

        def unit(u, slot):
            r = pl.multiple_of(u * MOE_RB, MOE_RB)
            gu = jnp.dot(xb_ref[pl.ds(r, MOE_RB), :], wgu_ref[...], preferred_element_type=F32)
            gate = jnp.minimum(gu[:, 0:MOE_TF] + b_g, SWIGLU_LIMIT)
            up = jnp.clip(gu[:, MOE_TF:2 * MOE_TF] + b_u, -SWIGLU_LIMIT, SWIGLU_LIMIT)
            act = (up + 1) * (gate * jax.nn.sigmoid(gate * SWIGLU_ALPHA))
            act_ref[s, pl.ds(r, MOE_RB), :] = act.astype(BF16)
            for j in range(MOE_G):
                rr = r + s * MOE_G + j
                row_copy(tok_ref[next_base + rr], rr).start()

        _for_units(nblk, unit)
        return carry

    def down_step(s, carry):
        slot = s & 1
        d_copy(e, s, slot).wait()

        @pl.when(s + 1 < MOE_P2)
        def _():
            d_copy(e, s + 1, 1 - slot).start()

        wdb_ref[...] = wd_st[slot].astype(BF16)
        b_d = bd_ref[par, pl.ds(s, 1), :]
        col = pl.multiple_of(s * MOE_TN, MOE_TN)

        def free_slots(slots):
            for slot in slots:
                y_copy(slot, 0, col).wait()

        def unit(u, slot):
            r = pl.multiple_of(u * MOE_RB, MOE_RB)
            a = jnp.concatenate([act_ref[j, pl.ds(r, MOE_RB), :] for j in range(MOE_P1)], axis=1)
            ystage_ref[slot] = jnp.dot(a, wdb_ref[...], preferred_element_type=F32) + b_d
            y_copy(slot, u, col).start()

        _for_units(nblk, unit, free_slots)
        return carry

    @pl.when(valid)
    def _():
        lax.fori_loop(0, MOE_P1, gate_up_step, 0)
        gather_rows(i + 1, nblk, nblk_ref[i + 1])
        lax.fori_loop(0, MOE_P2, down_step, 0)

    @pl.when(i == MOE_NI - 1)
    def _():
        for slot in range(MOE_YSLOTS):
            dump_copy(slot).wait()

        lax.fori_loop(0, wait_ref[MOE_NI], wait_rows, 0)

        def drain(b, carry):
            tail_copy(b).wait()
            return carry

        lax.fori_loop(used_ref[0], N_BLOCKS, drain, 0)


def _moe_call(slot_tok, it_e, it_row0, it_nblk, it_wait, n_used, xn, w_gate_up, w_down, b_gate_up, b_down):
    any_spec = pl.BlockSpec(memory_space=pl.ANY)
    grid_spec = pltpu.PrefetchScalarGridSpec(
        num_scalar_prefetch=6,
        grid=(MOE_NI,),
        in_specs=[any_spec] * 5,
        out_specs=any_spec,
        scratch_shapes=[
            pltpu.VMEM((MOE_RMAX * ROW_TILES, LANES), F32),
            pltpu.VMEM((MOE_RMAX, D_MODEL), BF16),
            pltpu.VMEM((MOE_P1, MOE_RMAX, MOE_TF), BF16),
            pltpu.VMEM((2, D_MODEL, MOE_TF), F32),
            pltpu.VMEM((2, D_MODEL, MOE_TF), F32),
            pltpu.VMEM((2, D_FF, MOE_TN), F32),
            pltpu.VMEM((D_MODEL, 2 * MOE_TF), BF16),
            pltpu.VMEM((D_FF, MOE_TN), BF16),
            pltpu.VMEM((2, 2 * MOE_P1, MOE_TF), F32),
            pltpu.VMEM((2, MOE_P2, MOE_TN), F32),
            pltpu.VMEM((MOE_YSLOTS, MOE_RB, MOE_TN), F32),
            pltpu.VMEM((MOE_RB, D_MODEL), F32),
            pltpu.SemaphoreType.DMA(()),
            pltpu.SemaphoreType.DMA((MOE_YSLOTS,)),
            pltpu.SemaphoreType.DMA(()),
            pltpu.SemaphoreType.DMA((2,)),
            pltpu.SemaphoreType.DMA((2,)),
            pltpu.SemaphoreType.DMA((2,)),
        ],
    )
    spare_blocks = MOE_YSLOTS // MOE_P2
    return pl.pallas_call(
        _moe_kernel,
        grid_spec=grid_spec,
        out_shape=jax.ShapeDtypeStruct((N_SLOTS + spare_blocks * MOE_RB, D_MODEL), F32),
        compiler_params=_cparams(("arbitrary",), 56),
        name="moe",
    )(slot_tok, it_e, it_row0, it_nblk, it_wait, n_used, xn, w_gate_up, w_down, b_gate_up, b_down)


def _combine_kernel(dest_ref, y_hbm, x1_ref, rg_ref, ps_ref, pp_ref, wple_ref, wpg_ref, gple_ref,
                    gfin_ref, ys_ref, yp_ref, gbuf_ref, gsem):
    m = pl.program_id(0)
    nm = pl.num_programs(0)
    tm = CMB_TM
    slot = m % 2

    def row_copy(tile, slot_, r, k):
        d = dest_ref[(tile * tm + r) * TOP_K + k]
        return pltpu.make_async_copy(y_hbm.at[pl.ds(d, 1), :], gbuf_ref.at[slot_, k, pl.ds(r, 1), :],
                                     gsem.at[slot_])

    def wait_tile(slot_):
        for k in range(TOP_K):
            pltpu.make_async_copy(y_hbm.at[pl.ds(0, tm), :], gbuf_ref.at[slot_, k], gsem.at[slot_]).wait()

    @pl.when(m == 0)
    def _():
        def body(c, carry):
            for j in range(GATHER_UNROLL // TOP_K):
                for k in range(TOP_K):
                    row_copy(0, 0, c * (GATHER_UNROLL // TOP_K) + j, k).start()
            return carry

        lax.fori_loop(0, tm // (GATHER_UNROLL // TOP_K), body, 0)

    wait_tile(slot)
    gates = rg_ref[...]
    moe = gates[:, 0:1] * gbuf_ref[slot, 0]
    for k in range(1, TOP_K):
        moe = moe + gates[:, k:k + 1] * gbuf_ref[slot, k]
    x2 = x1_ref[...] + moe
    is_s = m < CMB_NS
    p = jnp.where(is_s, ps_ref[...], pp_ref[...]).astype(BF16)
    hn = _rms(x2, gple_ref[...]).astype(BF16)
    nxt = jnp.minimum(m + 1, nm - 1)
    rows_per_chunk = tm // CMB_CHUNKS
    cw = D_MODEL // CMB_CHUNKS
    x3_parts = []
    for c in range(CMB_CHUNKS):
        cols = slice(c * cw, (c + 1) * cw)
        pe = jnp.dot(p, wple_ref[:, cols], preferred_element_type=F32)
        gate = jax.nn.sigmoid(jnp.dot(hn, wpg_ref[:, cols], preferred_element_type=F32))
        x3_parts.append(x2[:, cols] + pe * gate)
        for r in range(c * rows_per_chunk, (c + 1) * rows_per_chunk):
            for k in range(TOP_K):
                row_copy(nxt, 1 - slot, r, k).start()
    x3 = jnp.concatenate(x3_parts, axis=1)
    y = _rms(x3, gfin_ref[...])

    @pl.when(is_s)
    def _():
        ys_ref[...] = y

    @pl.when(jnp.logical_not(is_s))
    def _():
        yp_ref[...] = y

    @pl.when(m == nm - 1)
    def _():
        wait_tile(1 - slot)


def _combine_call(dest, y_sorted, x1, rg, ps, pp, wple, wpg, g_ple, g_final):
    tm = CMB_TM
    ns = CMB_NS
    s_idx = lambda m, d: (jnp.minimum(m, ns - 1), 0)
    p_idx = lambda m, d: (jnp.maximum(m - ns, 0), 0)
    row = lambda m, d: (m, 0)
    const2 = lambda m, d: (0, 0)
    grid_spec = pltpu.PrefetchScalarGridSpec(
        num_scalar_prefetch=1,
        grid=(T // tm,),
        in_specs=[
            pl.BlockSpec(memory_space=pl.ANY),
            pl.BlockSpec((tm, D_MODEL), row),
            pl.BlockSpec((tm, LANES), row),
            pl.BlockSpec((tm, PLE_DIM), s_idx),
            pl.BlockSpec((tm, PLE_DIM), p_idx),
            pl.BlockSpec((PLE_DIM, D_MODEL), const2),
            pl.BlockSpec((D_MODEL, D_MODEL), const2),
            pl.BlockSpec((1, D_MODEL), const2),
            pl.BlockSpec((1, D_MODEL), const2),
        ],
        out_specs=[
            pl.BlockSpec((tm, D_MODEL), s_idx),
            pl.BlockSpec((tm, D_MODEL), p_idx),
        ],
        scratch_shapes=[
            pltpu.VMEM((2, TOP_K, tm, D_MODEL), F32),
            pltpu.SemaphoreType.DMA((2,)),
        ],
    )
    return pl.pallas_call(
        _combine_kernel,
        grid_spec=grid_spec,
        out_shape=[
            jax.ShapeDtypeStruct((T_S, D_MODEL), F32),
            jax.ShapeDtypeStruct((T_P, D_MODEL), F32),
        ],
        compiler_params=_cparams(("arbitrary",), 56),
        name="combine",
    )(dest, y_sorted, x1, rg, ps, pp, wple, wpg, g_ple, g_final)


def _routing_tables(route_i, counts):
    e_idx = route_i[:, 0:TOP_K]
    rank = route_i[:, TOP_K:2 * TOP_K]
    cnt = counts[0, :N_EXPERTS].astype(I32)
    padded = (cnt + MOE_RB - 1) // MOE_RB * MOE_RB
    gend = jnp.cumsum(padded)
    gstart = gend - padded
    dest = (gstart[e_idx] + rank).astype(I32)
    tok_of = jnp.repeat(jnp.arange(T, dtype=I32), TOP_K)
    slot_tok = jnp.zeros((N_SLOTS + MOE_RMAX,), I32).at[dest.reshape(-1)].set(tok_of)
    nblk_e = padded // MOE_RB
    items_e = (nblk_e + MOE_BMAX - 1) // MOE_BMAX
    iend = jnp.cumsum(items_e)
    istart = iend - items_e
    n_items = iend[-1]
    ids = jnp.arange(MOE_NI + 1, dtype=I32)
    valid = ids < n_items
    e_of = jnp.minimum(jnp.searchsorted(iend, ids, side="right"), N_EXPERTS - 1).astype(I32)
    e_last = jnp.minimum(jnp.searchsorted(iend, n_items - 1, side="right"), N_EXPERTS - 1).astype(I32)
    it_e = jnp.where(valid, e_of, e_last)
    local = ids - istart[it_e]
    it_nblk = jnp.where(valid, jnp.minimum(MOE_BMAX, nblk_e[it_e] - local * MOE_BMAX), 0).astype(I32)
    it_row0 = jnp.where(valid, gstart[it_e] + local * MOE_RMAX, 0).astype(I32)
    prev_nblk = jnp.concatenate([jnp.zeros((1,), I32), it_nblk[:-1]])
    it_wait = jnp.maximum(it_nblk, prev_nblk)
    n_used = jnp.sum(nblk_e).astype(I32).reshape(1)
    return dest.reshape(-1), slot_tok, it_e.astype(I32), it_row0, it_nblk, it_wait, n_used


def kernel(x_prompt, x_sample, state_conv, p_prompt, p_sample, g_mix, w_in, ln_v_g, ln_v_b, w_s, b_s,
           conv_w, conv_b, w_proj_a, w_proj_b, w_o, g_moe, w_router, b_router, w_gate_up, b_gate_up,
           w_down, b_down, g_ple, w_ple, w_ple_gate, g_final):
    assert g_mix.shape[0] == 1, "one layer"
    xs = x_sample.reshape(T_S, D_MODEL)
    xp = x_prompt.reshape(T_P, D_MODEL)

    tril = jnp.tril(jnp.ones((CHUNK, CHUNK), bool))
    w_prompt = jnp.where(tril[None], w_s[0], 0.0)
    small = jnp.where(tril[None, :DEC_SEQ, :DEC_SEQ], w_s[0, :, :DEC_SEQ, :DEC_SEQ], 0.0)
    reps = CHUNK // DEC_SEQ
    blockdiag = jnp.kron(jnp.eye(reps, dtype=F32), jnp.ones((DEC_SEQ, DEC_SEQ), F32))
    w_sample = jnp.tile(small, (1, reps, reps)) * blockdiag[None]
    wsg = jnp.stack([w_sample, w_prompt]).astype(BF16)
    bias_p = jnp.repeat(b_s[0].T, GW_A, axis=1)
    bias_s = jnp.tile(jnp.repeat(b_s[0, :, :DEC_SEQ].T, GW_A, axis=1), (reps, 1))
    bsg = jnp.stack([bias_s, bias_p])
    ext = jnp.pad(state_conv[0], ((0, 0), (0, DEC_SEQ - (CONV_W - 1)), (0, 0))).reshape(T_S, D_B)

    wr = jnp.pad(w_router[0], ((0, 0), (0, LANES - N_EXPERTS)))
    wr_hi = wr.astype(BF16)
    wr_lo = (wr - wr_hi.astype(F32)).astype(BF16)
    b_r = jnp.pad(b_router[0], (0, LANES - N_EXPERTS), constant_values=NEG_BIG).reshape(1, LANES)

    h = _norm_call(xs, xp, g_mix)
    z, vpre = _in_proj_call(h, w_in[0])
    x1, xn, route_i, route_g, counts, vln, cxs, tail = _mixer_call(
        xs, xp, z, vpre, ext, wsg, bsg, ln_v_g, ln_v_b, conv_w[0], conv_b,
        w_proj_a[0].astype(BF16), w_proj_b[0].astype(BF16), w_o[0].astype(BF16), g_moe,
        wr_hi, wr_lo, b_r)

    dest, slot_tok, it_e, it_row0, it_nblk, it_wait, n_used = _routing_tables(route_i, counts)
    y_sorted = _moe_call(slot_tok, it_e, it_row0, it_nblk, it_wait, n_used, xn, w_gate_up[0], w_down[0],
                         b_gate_up[0].reshape(N_EXPERTS, 2 * MOE_P1, MOE_TF),
                         b_down[0].reshape(N_EXPERTS, MOE_P2, MOE_TN))
    ys, yp = _combine_call(dest, y_sorted, x1, route_g,
                           p_sample[0].reshape(T_S, PLE_DIM), p_prompt[0].reshape(T_P, PLE_DIM),
                           w_ple[0].astype(BF16), w_ple_gate[0].astype(BF16), g_ple, g_final.reshape(1, D_MODEL))

    y_prompt = yp.reshape(BATCH, SEQ, D_MODEL)
    y_sample = ys.reshape(DEC_BATCH, DEC_SEQ, D_MODEL)
    last = tail.reshape(BATCH, MIX_SEQ_TILES, 8, D_B)[:, -1, 8 - (CONV_W - 1):, :]
    state_conv_prompt = last[None]
    state_conv_sample = cxs.reshape(DEC_BATCH, DEC_SEQ, D_B)[:, DEC_SEQ - (CONV_W - 1):, :][None]
    state_chunk_v_sample = vln.reshape(DEC_BATCH, DEC_SEQ, D_A)[None]
    return (y_prompt, y_sample, state_conv_prompt, state_conv_sample, state_chunk_v_sample)
```

```python
import functools

import jax
import jax.numpy as jnp
from jax import lax
from jax.experimental import pallas as pl
from jax.experimental.pallas import tpu as pltpu

F32 = jnp.float32
BF16 = jnp.bfloat16
I32 = jnp.int32

D_MODEL = 2048
BATCH = 4
SEQ = 2048
DEC_BATCH = 128
DEC_SEQ = 8
CHUNK = 128
D_A = D_MODEL // 2
N_GROUPS_A = 8
GW_A = D_A // N_GROUPS_A
D_B = D_MODEL // 2
CONV_W = 3
N_EXPERTS = 32
TOP_K = 4
D_FF = D_MODEL
SWIGLU_LIMIT = 7.0
SWIGLU_ALPHA = 1.702
PLE_DIM = 256
EPS = 1e-6

T_S = DEC_BATCH * DEC_SEQ
T_P = BATCH * SEQ
T = T_S + T_P

LANES = 128
V7X_VMEM_BYTES = 64 * 1024 * 1024
MIB = 1024 * 1024

NORM_TM = 512
IN_TM = 1536
IN_TN = 512
IN_SUB = 256
MIX_TM = CHUNK
MIX_NS = T_S // MIX_TM
MIX_SEQ_TILES = SEQ // MIX_TM
ROW_TILES = D_MODEL // LANES
MOE_RB = 128
MOE_BMAX = 12
MOE_RMAX = MOE_RB * MOE_BMAX
MOE_CHUNKS = (8, 4, 2, 1)
MOE_TF = 256
MOE_TN = 512
MOE_P1 = D_FF // MOE_TF
MOE_P2 = D_MODEL // MOE_TN
MOE_TILES = MOE_P1 + MOE_P2
MOE_WSLOTS = 3
MOE_WAHEAD = MOE_WSLOTS - 1
assert D_MODEL == D_FF and 2 * MOE_TF == MOE_TN and MOE_TILES % MOE_WSLOTS == 0
MOE_G = MOE_RB // MOE_P1
MOE_YSLOTS = 8
N_SLOTS = T * TOP_K + N_EXPERTS * MOE_RB
N_BLOCKS = N_SLOTS // MOE_RB
MOE_NI = (N_BLOCKS + N_EXPERTS * (MOE_BMAX - 1)) // MOE_BMAX
GATHER_UNROLL = 8
CMB_TM = 256
CMB_NS = T_S // CMB_TM
CMB_CHUNKS = 8
NEG_BIG = -1e30


def _rms(x, g):
    return x * lax.rsqrt(jnp.mean(x * x, axis=-1, keepdims=True) + EPS) * g


def _cparams(sem, vmem_mib):
    return pltpu.CompilerParams(dimension_semantics=sem, vmem_limit_bytes=vmem_mib * MIB)


def _resident(shape):
    zeros = (0,) * len(shape)
    return pl.BlockSpec(shape, lambda *_: zeros, pipeline_mode=pl.Buffered(1))


def _norm_kernel(xs_ref, xp_ref, g_ref, h_ref, *, ns):
    m = pl.program_id(0)
    x = jnp.where(m < ns, xs_ref[...], xp_ref[...])
    h_ref[...] = _rms(x, g_ref[...]).astype(BF16)


def _norm_call(xs, xp, g):
    ns = T_S // NORM_TM
    return pl.pallas_call(
        functools.partial(_norm_kernel, ns=ns),
        grid=(T // NORM_TM,),
        in_specs=[
            pl.BlockSpec((NORM_TM, D_MODEL), lambda m: (jnp.minimum(m, ns - 1), 0)),
            pl.BlockSpec((NORM_TM, D_MODEL), lambda m: (jnp.maximum(m - ns, 0), 0)),
            pl.BlockSpec((1, D_MODEL), lambda m: (0, 0)),
        ],
        out_specs=pl.BlockSpec((NORM_TM, D_MODEL), lambda m: (m, 0)),
        out_shape=jax.ShapeDtypeStruct((T, D_MODEL), BF16),
        compiler_params=_cparams(("arbitrary",), 32),
        name="norm",
    )(xs, xp, g)


IN_N_GELU = 2 * D_A // IN_TN
IN_N_V0 = D_A // IN_TN
IN_N_LIN = (2 * D_A + 3 * D_B) // IN_TN


def _in_proj_kernel(h_ref, w_ref, z_ref, vpre_ref, wb_ref):
    n = pl.program_id(1)
    wb_ref[...] = w_ref[...].astype(BF16)

    def blocks(epilogue):
        for b in range(IN_TM // IN_SUB):
            rows = pl.ds(b * IN_SUB, IN_SUB)
            epilogue(rows, jnp.dot(h_ref[rows, :], wb_ref[...], preferred_element_type=F32))

    @pl.when(n < IN_N_V0)
    def _():
        def ep(rows, acc):
            z_ref[rows, :] = jax.nn.gelu(acc, approximate=True).astype(BF16)

        blocks(ep)

    @pl.when((n >= IN_N_V0) & (n < IN_N_GELU))
    def _():
        def ep(rows, acc):
            g = jax.nn.gelu(acc, approximate=True)
            z_ref[rows, :] = g.astype(BF16)
            vpre_ref[rows, :] = g

        blocks(ep)

    @pl.when((n >= IN_N_GELU) & (n < IN_N_LIN))
    def _():
        def ep(rows, acc):
            z_ref[rows, :] = acc.astype(BF16)

        blocks(ep)

    @pl.when(n >= IN_N_LIN)
    def _():
        def ep(rows, acc):
            z_ref[rows, :] = jax.nn.sigmoid(acc).astype(BF16)

        blocks(ep)


def _in_proj_call(h, w_in):
    d_in = w_in.shape[1]
    return pl.pallas_call(
        _in_proj_kernel,
        grid=(T // IN_TM, d_in // IN_TN),
        in_specs=[
            pl.BlockSpec((IN_TM, D_MODEL), lambda m, n: (m, 0)),
            pl.BlockSpec((D_MODEL, IN_TN), lambda m, n: (0, n)),
        ],
        out_specs=[
            pl.BlockSpec((IN_TM, IN_TN), lambda m, n: (m, n)),
            pl.BlockSpec((IN_TM, IN_TN), lambda m, n: (m, jnp.clip(n - IN_N_V0, 0, IN_N_GELU - IN_N_V0 - 1))),
        ],
        out_shape=[
            jax.ShapeDtypeStruct((T, d_in), BF16),
            jax.ShapeDtypeStruct((T, D_A), F32),
        ],
        scratch_shapes=[pltpu.VMEM((D_MODEL, IN_TN), BF16)],
        compiler_params=_cparams(("arbitrary", "arbitrary"), 44),
        name="in_proj",
    )(h, w_in)


def _mixer_kernel(xs_ref, xp_ref, z_ref, vpre_ref, ext_ref, wsg_ref, bsg_ref, lng_ref, lnb_ref,
                  cw_ref, cb_ref, wa_ref, wb_ref, wo_ref, gmoe_ref, wrh_ref, wrl_ref, br_ref,
                  x1_ref, xn_ref, ri_ref, rg_ref, cnt_ref, vln_ref, cxs_ref, tail_ref,
                  prev_ref, carry_ref):
    m = pl.program_id(0)
    is_s = m < MIX_NS
    tm = MIX_TM

    @pl.when(m == 0)
    def _():
        prev_ref[...] = jnp.zeros_like(prev_ref)
        carry_ref[...] = jnp.zeros_like(carry_ref)

    vg = vpre_ref[...]
    mu = jnp.mean(vg, axis=-1, keepdims=True)
    vc = vg - mu
    v = vc * lax.rsqrt(jnp.mean(vc * vc, axis=-1, keepdims=True) + EPS) * lng_ref[...] + lnb_ref[...]

    @pl.when(is_s)
    def _():
        vln_ref[...] = v

    vb = v.astype(BF16)
    s_parts = []
    for g in range(N_GROUPS_A):
        s_parts.append(jnp.dot(wsg_ref[0, g], vb[:, g * GW_A:(g + 1) * GW_A], preferred_element_type=F32))
    s = jnp.concatenate(s_parts, axis=1) + bsg_ref[0]
    u = z_ref[:, 0:D_A].astype(F32)
    a_in = (u * s).astype(BF16)

    o_b = 2 * D_A
    bg = z_ref[:, o_b:o_b + D_B].astype(F32)
    cg = z_ref[:, o_b + D_B:o_b + 2 * D_B].astype(F32)
    xin = z_ref[:, o_b + 2 * D_B:o_b + 3 * D_B].astype(F32)
    cx = cg * xin

    @pl.when(is_s)
    def _():
        cxs_ref[...] = cx

    @pl.when(jnp.logical_not(is_s))
    def _():
        tail_ref[0] = cx[tm - 8:tm]

    row = lax.broadcasted_iota(I32, (tm, D_B), 0)
    seq_start = ((m - MIX_NS) % MIX_SEQ_TILES) == 0
    prev = jnp.where(seq_start, 0.0, prev_ref[...])
    row8 = lax.broadcasted_iota(I32, (8, D_B), 0)
    top = jnp.where(row8 < CONV_W - 1, pltpu.roll(prev, CONV_W - 1, 0), 0.0)
    ext_p = jnp.concatenate([top, jnp.zeros((tm - 8, D_B), F32)], axis=0)
    ext = jnp.where(is_s, ext_ref[...], ext_p)
    t_in = jnp.where(is_s, row & (DEC_SEQ - 1), row)
    s1 = jnp.where(t_in < 1, pltpu.roll(ext, tm - 1, 0), pltpu.roll(cx, 1, 0))
    s2 = jnp.where(t_in < 2, ext, pltpu.roll(cx, 2, 0))
    prev_ref[...] = cx[tm - 8:tm]
    conv = cb_ref[...] + s2 * cw_ref[0:1, :] + s1 * cw_ref[1:2, :] + cx * cw_ref[2:3, :]
    b_in = (bg * conv).astype(BF16)

    y_a = jnp.dot(a_in, wa_ref[...], preferred_element_type=F32)
    y_b = jnp.dot(b_in, wb_ref[...], preferred_element_type=F32)
    o_g = 2 * D_A + 3 * D_B
    ga = z_ref[:, o_g:o_g + D_MODEL].astype(F32)
    gb = z_ref[:, o_g + D_MODEL:o_g + 2 * D_MODEL].astype(F32)
    mix = (ga * y_a + gb * y_b).astype(BF16)
    x = jnp.where(is_s, xs_ref[...], xp_ref[...])
    x1 = x + jnp.dot(mix, wo_ref[...], preferred_element_type=F32)
    x1_ref[...] = x1

    xn = _rms(x1, gmoe_ref[...])
    for j in range(ROW_TILES):
        xn_ref[pl.ds(j, tm, stride=ROW_TILES), :] = xn[:, j * LANES:(j + 1) * LANES]
    hi = xn.astype(BF16)
    lo = (xn - hi.astype(F32)).astype(BF16)
    logits = (jnp.dot(hi, wrh_ref[...], preferred_element_type=F32)
              + jnp.dot(lo, wrh_ref[...], preferred_element_type=F32)
              + jnp.dot(hi, wrl_ref[...], preferred_element_type=F32)) + br_ref[...]
    lane = lax.broadcasted_iota(I32, (tm, LANES), 1)
    lane_f = lane.astype(F32)
    vals, idxs, hots = [], [], []
    work = logits
    for _ in range(TOP_K):
        mx = jnp.max(work, axis=-1, keepdims=True)
        idx = jnp.min(jnp.where(work == mx, lane_f, float(LANES)), axis=-1, keepdims=True)
        hot = lane_f == idx
        work = jnp.where(hot, -jnp.inf, work)
        vals.append(mx)
        idxs.append(idx)
        hots.append(hot)
    exps = [jnp.exp(vk - vals[0]) for vk in vals]
    den = exps[0] + exps[1] + exps[2] + exps[3]

    chosen = jnp.zeros((tm, LANES), F32)
    for hot in hots:
        chosen = chosen + jnp.where(hot, 1.0, 0.0)
    r_i = lax.broadcasted_iota(I32, (tm, tm), 0)
    c_i = lax.broadcasted_iota(I32, (tm, tm), 1)
    tri = jnp.where(c_i < r_i, 1.0, 0.0).astype(BF16)
    before = jnp.dot(tri, chosen.astype(BF16), preferred_element_type=F32) + carry_ref[0:1, :]
    total = carry_ref[0:1, :] + jnp.sum(chosen, axis=0, keepdims=True)
    carry_ref[...] = jnp.broadcast_to(total, carry_ref.shape)
    cnt_ref[...] = jnp.broadcast_to(total, cnt_ref.shape)

    ri = jnp.zeros((tm, LANES), F32)
    rg = jnp.zeros((tm, LANES), F32)
    for k in range(TOP_K):
        rank = jnp.sum(jnp.where(hots[k], before, 0.0), axis=-1, keepdims=True)
        ri = ri + jnp.where(lane == k, idxs[k], 0.0) + jnp.where(lane == TOP_K + k, rank, 0.0)
        rg = rg + jnp.where(lane == k, exps[k] / den, 0.0)
    ri_ref[...] = ri.astype(I32)
    rg_ref[...] = rg


def _mixer_call(xs, xp, z, vpre, ext, wsg, bsg, ln_g, ln_b, conv_w, conv_b, wa, wb, wo, g_moe,
                wr_hi, wr_lo, b_r):
    tm = MIX_TM
    ns = MIX_NS
    d_in = z.shape[1]
    s_idx = lambda m: (jnp.minimum(m, ns - 1), 0)
    p_idx = lambda m: (jnp.maximum(m - ns, 0), 0)
    row = lambda m: (m, 0)
    return pl.pallas_call(
        _mixer_kernel,
        grid=(T // tm,),
        in_specs=[
            pl.BlockSpec((tm, D_MODEL), s_idx),
            pl.BlockSpec((tm, D_MODEL), p_idx),
            pl.BlockSpec((tm, d_in), row),
            pl.BlockSpec((tm, D_A), row),
            pl.BlockSpec((tm, D_B), s_idx),
            pl.BlockSpec((1, N_GROUPS_A, CHUNK, CHUNK), lambda m: (jnp.minimum(m // ns, 1), 0, 0, 0)),
            pl.BlockSpec((1, CHUNK, D_A), lambda m: (jnp.minimum(m // ns, 1), 0, 0)),
            _resident((1, D_A)),
            _resident((1, D_A)),
            _resident((CONV_W, D_B)),
            _resident((1, D_B)),
            _resident((D_A, D_MODEL)),
            _resident((D_B, D_MODEL)),
            _resident((D_MODEL, D_MODEL)),
            _resident((1, D_MODEL)),
            _resident((D_MODEL, LANES)),
            _resident((D_MODEL, LANES)),
            _resident((1, LANES)),
        ],
        out_specs=[
            pl.BlockSpec((tm, D_MODEL), row),
            pl.BlockSpec((tm * ROW_TILES, LANES), row),
            pl.BlockSpec((tm, LANES), row),
            pl.BlockSpec((tm, LANES), row),
            pl.BlockSpec((8, LANES), lambda m: (0, 0)),
            pl.BlockSpec((tm, D_A), s_idx),
            pl.BlockSpec((tm, D_B), s_idx),
            pl.BlockSpec((1, 8, D_B), lambda m: (jnp.maximum(m - ns, 0), 0, 0)),
        ],
        out_shape=[
            jax.ShapeDtypeStruct((T, D_MODEL), F32),
            jax.ShapeDtypeStruct((T * ROW_TILES, LANES), F32),
            jax.ShapeDtypeStruct((T, LANES), I32),
            jax.ShapeDtypeStruct((T, LANES), F32),
            jax.ShapeDtypeStruct((8, LANES), F32),
            jax.ShapeDtypeStruct((T_S, D_A), F32),
            jax.ShapeDtypeStruct((T_S, D_B), F32),
            jax.ShapeDtypeStruct((T_P // tm, 8, D_B), F32),
        ],
        scratch_shapes=[pltpu.VMEM((8, D_B), F32), pltpu.VMEM((8, LANES), F32)],
        compiler_params=_cparams(("arbitrary",), 52),
        name="mixer",
    )(xs, xp, z, vpre, ext, wsg, bsg, ln_g, ln_b, conv_w, conv_b, wa, wb, wo, g_moe, wr_hi, wr_lo, b_r)


MOE_CHUNK_SLOT0 = {8: 0, 4: 0, 2: 4, 1: 6}


def _for_units(n, unit_fn, chunk_begin=None):
    big = MOE_CHUNKS[0]

    def chunk(u0, count):
        slot0 = MOE_CHUNK_SLOT0[count]
        if chunk_begin is not None:
            chunk_begin(range(slot0, slot0 + count))
        for j in range(count):
            unit_fn(u0 + j, slot0 + j)

    def body(c, carry):
        chunk(c * big, big)
        return carry

    n_big = lax.shift_right_logical(n, big.bit_length() - 1)
    lax.fori_loop(0, n_big, body, 0)
    base = n_big * big
    for count in MOE_CHUNKS[1:]:
        @pl.when((n & count) != 0)
        def _():
            chunk(base, count)

        base = base + (n & count)


def _moe_kernel(tok_ref, ite_ref, row0_ref, nblk_ref, wait_ref, used_ref,
                xn_hbm, wgu_hbm, wd_hbm, bgu_hbm, bd_hbm,
                y_hbm,
                xraw_ref, xb_ref, act_ref, wst_ref, wbf_ref, bgu_ref, bd_ref,
                ystage_ref, zbuf_ref, gsem, ysem, zsem, wsem, bsem):
    i = pl.program_id(0)
    nblk = nblk_ref[i]
    row0 = row0_ref[i]
    valid = nblk > 0
    par = i & 1

    def gu_copies(e, tile, slot):
        col = pl.multiple_of(tile * MOE_TF, MOE_TF)
        return (pltpu.make_async_copy(wgu_hbm.at[e, :, pl.ds(col, MOE_TF)],
                                      wst_ref.at[slot, :, pl.ds(0, MOE_TF)], wsem.at[slot]),
                pltpu.make_async_copy(wgu_hbm.at[e, :, pl.ds(D_FF + col, MOE_TF)],
                                      wst_ref.at[slot, :, pl.ds(MOE_TF, MOE_TF)], wsem.at[slot]))

    def d_copy(e, tile, slot):
        col = pl.multiple_of(tile * MOE_TN, MOE_TN)
        return pltpu.make_async_copy(wd_hbm.at[e, :, pl.ds(col, MOE_TN)], wst_ref.at[slot], wsem.at[slot])

    def start_tile(e, t):
        slot = lax.rem(t, MOE_WSLOTS)

        @pl.when(t < MOE_P1)
        def _():
            for c in gu_copies(e, t, slot):
                c.start()

        @pl.when(t >= MOE_P1)
        def _():
            d_copy(e, t - MOE_P1, slot).start()

    def bias_copies(e, slot):
        return (pltpu.make_async_copy(bgu_hbm.at[e], bgu_ref.at[slot], bsem.at[slot]),
                pltpu.make_async_copy(bd_hbm.at[e], bd_ref.at[slot], bsem.at[slot]))

    def tail_copy(b):
        r = pl.multiple_of(b * MOE_RB, MOE_RB)
        return pltpu.make_async_copy(zbuf_ref, y_hbm.at[pl.ds(r, MOE_RB), :], zsem)

    def row_copy(tok, r):
        return pltpu.make_async_copy(
            xn_hbm.at[pl.ds(pl.multiple_of(tok * ROW_TILES, ROW_TILES), ROW_TILES), :],
            xraw_ref.at[pl.ds(pl.multiple_of(r * ROW_TILES, ROW_TILES), ROW_TILES), :],
            gsem)

    def unit_wait():
        n = MOE_RB * ROW_TILES
        return pltpu.make_async_copy(xn_hbm.at[pl.ds(0, n), :], xraw_ref.at[pl.ds(0, n), :], gsem)

    def gather_rows(item, lo_unit, hi_unit):
        base = row0_ref[item]

        def body(c, carry):
            for j in range(GATHER_UNROLL):
                r = c * GATHER_UNROLL + j
                row_copy(tok_ref[base + r], r).start()
            return carry

        per_unit = MOE_RB // GATHER_UNROLL
        lax.fori_loop(lo_unit * per_unit, hi_unit * per_unit, body, 0)

    def y_copy(slot, u, col):
        r = pl.multiple_of(row0 + u * MOE_RB, MOE_RB)
        return pltpu.make_async_copy(ystage_ref.at[slot], y_hbm.at[pl.ds(r, MOE_RB), pl.ds(col, MOE_TN)],
                                     ysem.at[slot])

    def dump_copy(slot):
        r = N_SLOTS + (slot // MOE_P2) * MOE_RB
        c = (slot % MOE_P2) * MOE_TN
        return pltpu.make_async_copy(ystage_ref.at[slot], y_hbm.at[pl.ds(r, MOE_RB), pl.ds(c, MOE_TN)],
                                     ysem.at[slot])

    @pl.when(i == 0)
    def _():
        for t in range(MOE_WAHEAD):
            for c in gu_copies(ite_ref[0], t, t):
                c.start()
        for c in bias_copies(ite_ref[0], 0):
            c.start()
        gather_rows(0, 0, nblk_ref[0])
        ystage_ref[...] = jnp.zeros_like(ystage_ref)
        for slot in range(MOE_YSLOTS):
            dump_copy(slot).start()
        zbuf_ref[...] = jnp.zeros_like(zbuf_ref)

        def fill(b, carry):
            tail_copy(b).start()
            return carry

        lax.fori_loop(used_ref[0], N_BLOCKS, fill, 0)

    def wait_rows(b, carry):
        unit_wait().wait()
        return carry

    lax.fori_loop(0, wait_ref[i], wait_rows, 0)

    def conv(u, carry):
        r = pl.multiple_of(u * MOE_RB, MOE_RB)
        parts = [xraw_ref[pl.ds(r * ROW_TILES + j, MOE_RB, stride=ROW_TILES), :] for j in range(ROW_TILES)]
        xb_ref[pl.ds(r, MOE_RB), :] = jnp.concatenate(parts, axis=1).astype(BF16)
        return carry

    lax.fori_loop(0, nblk, conv, 0)

    e = ite_ref[i]
    e_next = ite_ref[i + 1]
    next_valid = nblk_ref[i + 1] > 0
    next_base = row0_ref[i + 1]

    @pl.when(valid)
    def _():
        for c in bias_copies(e, par):
            c.wait()

    def gate_up_step(s, carry):
        wslot = lax.rem(s, MOE_WSLOTS)
        for c in gu_copies(e, s, wslot):
            c.wait()
        start_tile(e, s + MOE_WAHEAD)
        wbf_ref[...] = wst_ref[wslot].astype(BF16)
        b_g = bgu_ref[par, pl.ds(s, 1), :]
        b_u = bgu_ref[par, pl.ds(MOE_P1 + s, 1), :]

        def unit(u, slot):
            r = pl.multiple_of(u * MOE_RB, MOE_RB)
            gu = jnp.dot(xb_ref[pl.ds(r, MOE_RB), :], wbf_ref[...], preferred_element_type=F32)
            gate = jnp.minimum(gu[:, 0:MOE_TF] + b_g, SWIGLU_LIMIT)
            up = jnp.clip(gu[:, MOE_TF:2 * MOE_TF] + b_u, -SWIGLU_LIMIT, SWIGLU_LIMIT)
            act = (up + 1) * (gate * jax.nn.sigmoid(gate * SWIGLU_ALPHA))
            act_ref[s, pl.ds(r, MOE_RB), :] = act.astype(BF16)
            for j in range(MOE_G):
                rr = r + s * MOE_G + j
                row_copy(tok_ref[next_base + rr], rr).start()

        _for_units(nblk, unit)
        return carry

    def down_step(s, carry):
        t = MOE_P1 + s
        wslot = lax.rem(t, MOE_WSLOTS)
        d_copy(e, s, wslot).wait()

        @pl.when(t + MOE_WAHEAD < MOE_TILES)
        def _():
            start_tile(e, t + MOE_WAHEAD)

        @pl.when((t + MOE_WAHEAD >= MOE_TILES) & next_valid)
        def _():
            start_tile(e_next, t + MOE_WAHEAD - MOE_TILES)

        @pl.when((t + MOE_WAHEAD == MOE_TILES) & next_valid)
        def _():
            for c in bias_copies(e_next, 1 - par):
                c.start()

        wbf_ref[...] = wst_ref[wslot].astype(BF16)
        b_d = bd_ref[par, pl.ds(s, 1), :]
        col = pl.multiple_of(s * MOE_TN, MOE_TN)

        def free_slots(slots):
            for slot in slots:
                y_copy(slot, 0, col).wait()

        def unit(u, slot):
            r = pl.multiple_of(u * MOE_RB, MOE_RB)
            a = jnp.concatenate([act_ref[j, pl.ds(r, MOE_RB), :] for j in range(MOE_P1)], axis=1)
            ystage_ref[slot] = jnp.dot(a, wbf_ref[...], preferred_element_type=F32) + b_d
            y_copy(slot, u, col).start()

        _for_units(nblk, unit, free_slots)
        return carry

    @pl.when(valid)
    def _():
        lax.fori_loop(0, MOE_P1, gate_up_step, 0)
        gather_rows(i + 1, nblk, nblk_ref[i + 1])
        lax.fori_loop(0, MOE_P2, down_step, 0)

    @pl.when(i == MOE_NI - 1)
    def _():
        for slot in range(MOE_YSLOTS):
            dump_copy(slot).wait()

        lax.fori_loop(0, wait_ref[MOE_NI], wait_rows, 0)

        def drain(b, carry):
            tail_copy(b).wait()
            return carry

        lax.fori_loop(used_ref[0], N_BLOCKS, drain, 0)


def _moe_call(slot_tok, it_e, it_row0, it_nblk, it_wait, n_used, xn, w_gate_up, w_down, b_gate_up, b_down):
    any_spec = pl.BlockSpec(memory_space=pl.ANY)
    grid_spec = pltpu.PrefetchScalarGridSpec(
        num_scalar_prefetch=6,
        grid=(MOE_NI,),
        in_specs=[any_spec] * 5,
        out_specs=any_spec,
        scratch_shapes=[
            pltpu.VMEM((MOE_RMAX * ROW_TILES, LANES), F32),
            pltpu.VMEM((MOE_RMAX, D_MODEL), BF16),
            pltpu.VMEM((MOE_P1, MOE_RMAX, MOE_TF), BF16),
            pltpu.VMEM((MOE_WSLOTS, D_MODEL, MOE_TN), F32),
            pltpu.VMEM((D_MODEL, MOE_TN), BF16),
            pltpu.VMEM((2, 2 * MOE_P1, MOE_TF), F32),
            pltpu.VMEM((2, MOE_P2, MOE_TN), F32),
            pltpu.VMEM((MOE_YSLOTS, MOE_RB, MOE_TN), F32),
            pltpu.VMEM((MOE_RB, D_MODEL), F32),
            pltpu.SemaphoreType.DMA(()),
            pltpu.SemaphoreType.DMA((MOE_YSLOTS,)),
            pltpu.SemaphoreType.DMA(()),
            pltpu.SemaphoreType.DMA((MOE_WSLOTS,)),
            pltpu.SemaphoreType.DMA((2,)),
        ],
    )
    spare_blocks = MOE_YSLOTS // MOE_P2
    return pl.pallas_call(
        _moe_kernel,
        grid_spec=grid_spec,
        out_shape=jax.ShapeDtypeStruct((N_SLOTS + spare_blocks * MOE_RB, D_MODEL), F32),
        compiler_params=_cparams(("arbitrary",), 56),
        name="moe",
    )(slot_tok, it_e, it_row0, it_nblk, it_wait, n_used, xn, w_gate_up, w_down, b_gate_up, b_down)


def _combine_kernel(dest_ref, y_hbm, x1_ref, rg_ref, ps_ref, pp_ref, wple_ref, wpg_ref, gple_ref,
                    gfin_ref, ys_ref, yp_ref, gbuf_ref, gsem):
    m = pl.program_id(0)
    nm = pl.num_programs(0)
    tm = CMB_TM
    slot = m % 2

    def row_copy(tile, slot_, r, k):
        d = dest_ref[(tile * tm + r) * TOP_K + k]
        return pltpu.make_async_copy(y_hbm.at[pl.ds(d, 1), :], gbuf_ref.at[slot_, k, pl.ds(r, 1), :],
                                     gsem.at[slot_])

    def wait_tile(slot_):
        for k in range(TOP_K):
            pltpu.make_async_copy(y_hbm.at[pl.ds(0, tm), :], gbuf_ref.at[slot_, k], gsem.at[slot_]).wait()

    @pl.when(m == 0)
    def _():
        def body(c, carry):
            for j in range(GATHER_UNROLL // TOP_K):
                for k in range(TOP_K):
                    row_copy(0, 0, c * (GATHER_UNROLL // TOP_K) + j, k).start()
            return carry

        lax.fori_loop(0, tm // (GATHER_UNROLL // TOP_K), body, 0)

    wait_tile(slot)
    gates = rg_ref[...]
    moe = gates[:, 0:1] * gbuf_ref[slot, 0]
    for k in range(1, TOP_K):
        moe = moe + gates[:, k:k + 1] * gbuf_ref[slot, k]
    x2 = x1_ref[...] + moe
    is_s = m < CMB_NS
    p = jnp.where(is_s, ps_ref[...], pp_ref[...]).astype(BF16)
    hn = _rms(x2, gple_ref[...]).astype(BF16)
    nxt = jnp.minimum(m + 1, nm - 1)
    rows_per_chunk = tm // CMB_CHUNKS
    cw = D_MODEL // CMB_CHUNKS
    x3_parts = []
    for c in range(CMB_CHUNKS):
        cols = slice(c * cw, (c + 1) * cw)
        pe = jnp.dot(p, wple_ref[:, cols], preferred_element_type=F32)
        gate = jax.nn.sigmoid(jnp.dot(hn, wpg_ref[:, cols], preferred_element_type=F32))
        x3_parts.append(x2[:, cols] + pe * gate)
        for r in range(c * rows_per_chunk, (c + 1) * rows_per_chunk):
            for k in range(TOP_K):
                row_copy(nxt, 1 - slot, r, k).start()
    x3 = jnp.concatenate(x3_parts, axis=1)
    y = _rms(x3, gfin_ref[...])

    @pl.when(is_s)
    def _():
        ys_ref[...] = y

    @pl.when(jnp.logical_not(is_s))
    def _():
        yp_ref[...] = y

    @pl.when(m == nm - 1)
    def _():
        wait_tile(1 - slot)


def _combine_call(dest, y_sorted, x1, rg, ps, pp, wple, wpg, g_ple, g_final):
    tm = CMB_TM
    ns = CMB_NS
    s_idx = lambda m, d: (jnp.minimum(m, ns - 1), 0)
    p_idx = lambda m, d: (jnp.maximum(m - ns, 0), 0)
    row = lambda m, d: (m, 0)
    const2 = lambda m, d: (0, 0)
    grid_spec = pltpu.PrefetchScalarGridSpec(
        num_scalar_prefetch=1,
        grid=(T // tm,),
        in_specs=[
            pl.BlockSpec(memory_space=pl.ANY),
            pl.BlockSpec((tm, D_MODEL), row),
            pl.BlockSpec((tm, LANES), row),
            pl.BlockSpec((tm, PLE_DIM), s_idx),
            pl.BlockSpec((tm, PLE_DIM), p_idx),
            pl.BlockSpec((PLE_DIM, D_MODEL), const2),
            pl.BlockSpec((D_MODEL, D_MODEL), const2),
            pl.BlockSpec((1, D_MODEL), const2),
            pl.BlockSpec((1, D_MODEL), const2),
        ],
        out_specs=[
            pl.BlockSpec((tm, D_MODEL), s_idx),
            pl.BlockSpec((tm, D_MODEL), p_idx),
        ],
        scratch_shapes=[
            pltpu.VMEM((2, TOP_K, tm, D_MODEL), F32),
            pltpu.SemaphoreType.DMA((2,)),
        ],
    )
    return pl.pallas_call(
        _combine_kernel,
        grid_spec=grid_spec,
        out_shape=[
            jax.ShapeDtypeStruct((T_S, D_MODEL), F32),
            jax.ShapeDtypeStruct((T_P, D_MODEL), F32),
        ],
        compiler_params=_cparams(("arbitrary",), 56),
        name="combine",
    )(dest, y_sorted, x1, rg, ps, pp, wple, wpg, g_ple, g_final)


def _routing_tables(route_i, counts):
    e_idx = route_i[:, 0:TOP_K]
    rank = route_i[:, TOP_K:2 * TOP_K]
    cnt = counts[0, :N_EXPERTS].astype(I32)
    padded = (cnt + MOE_RB - 1) // MOE_RB * MOE_RB
    gend = jnp.cumsum(padded)
    gstart = gend - padded
    dest = (gstart[e_idx] + rank).astype(I32)
    tok_of = jnp.repeat(jnp.arange(T, dtype=I32), TOP_K)
    slot_tok = jnp.zeros((N_SLOTS + MOE_RMAX,), I32).at[dest.reshape(-1)].set(tok_of)
    nblk_e = padded // MOE_RB
    items_e = (nblk_e + MOE_BMAX - 1) // MOE_BMAX
    iend = jnp.cumsum(items_e)
    istart = iend - items_e
    n_items = iend[-1]
    ids = jnp.arange(MOE_NI + 1, dtype=I32)
    valid = ids < n_items
    e_of = jnp.minimum(jnp.searchsorted(iend, ids, side="right"), N_EXPERTS - 1).astype(I32)
    e_last = jnp.minimum(jnp.searchsorted(iend, n_items - 1, side="right"), N_EXPERTS - 1).astype(I32)
    it_e = jnp.where(valid, e_of, e_last)
    local = ids - istart[it_e]
    it_nblk = jnp.where(valid, jnp.minimum(MOE_BMAX, nblk_e[it_e] - local * MOE_BMAX), 0).astype(I32)
    it_row0 = jnp.where(valid, gstart[it_e] + local * MOE_RMAX, 0).astype(I32)
    prev_nblk = jnp.concatenate([jnp.zeros((1,), I32), it_nblk[:-1]])
    it_wait = jnp.maximum(it_nblk, prev_nblk)
    n_used = jnp.sum(nblk_e).astype(I32).reshape(1)
    return dest.reshape(-1), slot_tok, it_e.astype(I32), it_row0, it_nblk, it_wait, n_used


def kernel(x_prompt, x_sample, state_conv, p_prompt, p_sample, g_mix, w_in, ln_v_g, ln_v_b, w_s, b_s,
           conv_w, conv_b, w_proj_a, w_proj_b, w_o, g_moe, w_router, b_router, w_gate_up, b_gate_up,
           w_down, b_down, g_ple, w_ple, w_ple_gate, g_final):
    assert g_mix.shape[0] == 1, "one layer"
    xs = x_sample.reshape(T_S, D_MODEL)
    xp = x_prompt.reshape(T_P, D_MODEL)

    tril = jnp.tril(jnp.ones((CHUNK, CHUNK), bool))
    w_prompt = jnp.where(tril[None], w_s[0], 0.0)
    small = jnp.where(tril[None, :DEC_SEQ, :DEC_SEQ], w_s[0, :, :DEC_SEQ, :DEC_SEQ], 0.0)
    reps = CHUNK // DEC_SEQ
    blockdiag = jnp.kron(jnp.eye(reps, dtype=F32), jnp.ones((DEC_SEQ, DEC_SEQ), F32))
    w_sample = jnp.tile(small, (1, reps, reps)) * blockdiag[None]
    wsg = jnp.stack([w_sample, w_prompt]).astype(BF16)
    bias_p = jnp.repeat(b_s[0].T, GW_A, axis=1)
    bias_s = jnp.tile(jnp.repeat(b_s[0, :, :DEC_SEQ].T, GW_A, axis=1), (reps, 1))
    bsg = jnp.stack([bias_s, bias_p])
    ext = jnp.pad(state_conv[0], ((0, 0), (0, DEC_SEQ - (CONV_W - 1)), (0, 0))).reshape(T_S, D_B)

    wr = jnp.pad(w_router[0], ((0, 0), (0, LANES - N_EXPERTS)))
    wr_hi = wr.astype(BF16)
    wr_lo = (wr - wr_hi.astype(F32)).astype(BF16)
    b_r = jnp.pad(b_router[0], (0, LANES - N_EXPERTS), constant_values=NEG_BIG).reshape(1, LANES)

    h = _norm_call(xs, xp, g_mix)
    z, vpre = _in_proj_call(h, w_in[0])
    x1, xn, route_i, route_g, counts, vln, cxs, tail = _mixer_call(
        xs, xp, z, vpre, ext, wsg, bsg, ln_v_g, ln_v_b, conv_w[0], conv_b,
        w_proj_a[0].astype(BF16), w_proj_b[0].astype(BF16), w_o[0].astype(BF16), g_moe,
        wr_hi, wr_lo, b_r)

    dest, slot_tok, it_e, it_row0, it_nblk, it_wait, n_used = _routing_tables(route_i, counts)
    y_sorted = _moe_call(slot_tok, it_e, it_row0, it_nblk, it_wait, n_used, xn, w_gate_up[0], w_down[0],
                         b_gate_up[0].reshape(N_EXPERTS, 2 * MOE_P1, MOE_TF),
                         b_down[0].reshape(N_EXPERTS, MOE_P2, MOE_TN))
    ys, yp = _combine_call(dest, y_sorted, x1, route_g,
                           p_sample[0].reshape(T_S, PLE_DIM), p_prompt[0].reshape(T_P, PLE_DIM),
                           w_ple[0].astype(BF16), w_ple_gate[0].astype(BF16), g_ple, g_final.reshape(1, D_MODEL))

    y_prompt = yp.reshape(BATCH, SEQ, D_MODEL)
    y_sample = ys.reshape(DEC_BATCH, DEC_SEQ, D_MODEL)
    last = tail.reshape(BATCH, MIX_SEQ_TILES, 8, D_B)[:, -1, 8 - (CONV_W - 1):, :]
    state_conv_prompt = last[None]
    state_conv_sample = cxs.reshape(DEC_BATCH, DEC_SEQ, D_B)[:, DEC_SEQ - (CONV_W - 1):, :][None]
    state_chunk_v_sample = vln.reshape(DEC_BATCH, DEC_SEQ, D_A)[None]
    return (y_prompt, y_sample, state_conv_prompt, state_conv_sample, state_chunk_v_sample)
```

```python
import functools

import jax
import jax.numpy as jnp
from jax import lax
from jax.experimental import pallas as pl
from jax.experimental.pallas import tpu as pltpu

F32 = jnp.float32
BF16 = jnp.bfloat16
I32 = jnp.int32

D_MODEL = 2048
BATCH = 4
SEQ = 2048
DEC_BATCH = 128
DEC_SEQ = 8
CHUNK = 128
D_A = D_MODEL // 2
N_GROUPS_A = 8
GW_A = D_A // N_GROUPS_A
D_B = D_MODEL // 2
CONV_W = 3
N_EXPERTS = 32
TOP_K = 4
D_FF = D_MODEL
SWIGLU_LIMIT = 7.0
SWIGLU_ALPHA = 1.702
PLE_DIM = 256
EPS = 1e-6

T_S = DEC_BATCH * DEC_SEQ
T_P = BATCH * SEQ
T = T_S + T_P

LANES = 128
V7X_VMEM_BYTES = 64 * 1024 * 1024
MIB = 1024 * 1024

NORM_TM = 512
IN_TM = 1536
IN_TN = 512
IN_SUB = 256
MIX_TM = CHUNK
MIX_NS = T_S // MIX_TM
MIX_SEQ_TILES = SEQ // MIX_TM
MOE_RB = 128
MOE_BMAX = 12
MOE_RMAX = MOE_RB * MOE_BMAX
MOE_CHUNKS = (8, 4, 2, 1)
MOE_TF = 256
MOE_TN = 512
MOE_P1 = D_FF // MOE_TF
MOE_P2 = D_MODEL // MOE_TN
MOE_TILES = MOE_P1 + MOE_P2
MOE_WSLOTS = 3
MOE_WAHEAD = MOE_WSLOTS - 1
assert D_MODEL == D_FF and 2 * MOE_TF == MOE_TN and MOE_TILES % MOE_WSLOTS == 0
MOE_G = MOE_RB // MOE_P1
MOE_YSLOTS = 8
N_SLOTS = T * TOP_K + N_EXPERTS * MOE_RB
N_BLOCKS = N_SLOTS // MOE_RB
MOE_NI = (N_BLOCKS + N_EXPERTS * (MOE_BMAX - 1)) // MOE_BMAX
GATHER_UNROLL = 8
CMB_TM = 256
CMB_NS = T_S // CMB_TM
CMB_CHUNKS = 8
NEG_BIG = -1e30


def _rms(x, g):
    return x * lax.rsqrt(jnp.mean(x * x, axis=-1, keepdims=True) + EPS) * g


def _cparams(sem, vmem_mib):
    return pltpu.CompilerParams(dimension_semantics=sem, vmem_limit_bytes=vmem_mib * MIB)


def _resident(shape):
    zeros = (0,) * len(shape)
    return pl.BlockSpec(shape, lambda *_: zeros, pipeline_mode=pl.Buffered(1))


def _norm_kernel(xs_ref, xp_ref, g_ref, h_ref, *, ns):
    m = pl.program_id(0)
    x = jnp.where(m < ns, xs_ref[...], xp_ref[...])
    h_ref[...] = _rms(x, g_ref[...]).astype(BF16)


def _norm_call(xs, xp, g):
    ns = T_S // NORM_TM
    return pl.pallas_call(
        functools.partial(_norm_kernel, ns=ns),
        grid=(T // NORM_TM,),
        in_specs=[
            pl.BlockSpec((NORM_TM, D_MODEL), lambda m: (jnp.minimum(m, ns - 1), 0)),
            pl.BlockSpec((NORM_TM, D_MODEL), lambda m: (jnp.maximum(m - ns, 0), 0)),
            pl.BlockSpec((1, D_MODEL), lambda m: (0, 0)),
        ],
        out_specs=pl.BlockSpec((NORM_TM, D_MODEL), lambda m: (m, 0)),
        out_shape=jax.ShapeDtypeStruct((T, D_MODEL), BF16),
        compiler_params=_cparams(("arbitrary",), 32),
        name="norm",
    )(xs, xp, g)


IN_N_GELU = 2 * D_A // IN_TN
IN_N_V0 = D_A // IN_TN
IN_N_LIN = (2 * D_A + 3 * D_B) // IN_TN


def _in_proj_kernel(h_ref, w_ref, z_ref, vpre_ref, wb_ref):
    n = pl.program_id(1)
    wb_ref[...] = w_ref[...].astype(BF16)

    def blocks(epilogue):
        for b in range(IN_TM // IN_SUB):
            rows = pl.ds(b * IN_SUB, IN_SUB)
            epilogue(rows, jnp.dot(h_ref[rows, :], wb_ref[...], preferred_element_type=F32))

    @pl.when(n < IN_N_V0)
    def _():
        def ep(rows, acc):
            z_ref[rows, :] = jax.nn.gelu(acc, approximate=True).astype(BF16)

        blocks(ep)

    @pl.when((n >= IN_N_V0) & (n < IN_N_GELU))
    def _():
        def ep(rows, acc):
            g = jax.nn.gelu(acc, approximate=True)
            z_ref[rows, :] = g.astype(BF16)
            vpre_ref[rows, :] = g

        blocks(ep)

    @pl.when((n >= IN_N_GELU) & (n < IN_N_LIN))
    def _():
        def ep(rows, acc):
            z_ref[rows, :] = acc.astype(BF16)

        blocks(ep)

    @pl.when(n >= IN_N_LIN)
    def _():
        def ep(rows, acc):
            z_ref[rows, :] = jax.nn.sigmoid(acc).astype(BF16)

        blocks(ep)


def _in_proj_call(h, w_in):
    d_in = w_in.shape[1]
    return pl.pallas_call(
        _in_proj_kernel,
        grid=(T // IN_TM, d_in // IN_TN),
        in_specs=[
            pl.BlockSpec((IN_TM, D_MODEL), lambda m, n: (m, 0)),
            pl.BlockSpec((D_MODEL, IN_TN), lambda m, n: (0, n)),
        ],
        out_specs=[
            pl.BlockSpec((IN_TM, IN_TN), lambda m, n: (m, n)),
            pl.BlockSpec((IN_TM, IN_TN), lambda m, n: (m, jnp.clip(n - IN_N_V0, 0, IN_N_GELU - IN_N_V0 - 1))),
        ],
        out_shape=[
            jax.ShapeDtypeStruct((T, d_in), BF16),
            jax.ShapeDtypeStruct((T, D_A), F32),
        ],
        scratch_shapes=[pltpu.VMEM((D_MODEL, IN_TN), BF16)],
        compiler_params=_cparams(("arbitrary", "arbitrary"), 44),
        name="in_proj",
    )(h, w_in)


def _mixer_kernel(xs_ref, xp_ref, z_ref, vpre_ref, ext_ref, wsg_ref, bsg_ref, lng_ref, lnb_ref,
                  cw_ref, cb_ref, wa_ref, wb_ref, wo_ref, gmoe_ref, wrh_ref, wrl_ref, br_ref,
                  x1_ref, xn_ref, ri_ref, rg_ref, cnt_ref, vln_ref, cxs_ref, tail_ref,
                  prev_ref, carry_ref):
    m = pl.program_id(0)
    is_s = m < MIX_NS
    tm = MIX_TM

    @pl.when(m == 0)
    def _():
        prev_ref[...] = jnp.zeros_like(prev_ref)
        carry_ref[...] = jnp.zeros_like(carry_ref)

    vg = vpre_ref[...]
    mu = jnp.mean(vg, axis=-1, keepdims=True)
    vc = vg - mu
    v = vc * lax.rsqrt(jnp.mean(vc * vc, axis=-1, keepdims=True) + EPS) * lng_ref[...] + lnb_ref[...]

    @pl.when(is_s)
    def _():
        vln_ref[...] = v

    vb = v.astype(BF16)
    s_parts = []
    for g in range(N_GROUPS_A):
        s_parts.append(jnp.dot(wsg_ref[0, g], vb[:, g * GW_A:(g + 1) * GW_A], preferred_element_type=F32))
    s = jnp.concatenate(s_parts, axis=1) + bsg_ref[0]
    u = z_ref[:, 0:D_A].astype(F32)
    a_in = (u * s).astype(BF16)

    o_b = 2 * D_A
    bg = z_ref[:, o_b:o_b + D_B].astype(F32)
    cg = z_ref[:, o_b + D_B:o_b + 2 * D_B].astype(F32)
    xin = z_ref[:, o_b + 2 * D_B:o_b + 3 * D_B].astype(F32)
    cx = cg * xin

    @pl.when(is_s)
    def _():
        cxs_ref[...] = cx

    @pl.when(jnp.logical_not(is_s))
    def _():
        tail_ref[0] = cx[tm - 8:tm]

    row = lax.broadcasted_iota(I32, (tm, D_B), 0)
    seq_start = ((m - MIX_NS) % MIX_SEQ_TILES) == 0
    prev = jnp.where(seq_start, 0.0, prev_ref[...])
    row8 = lax.broadcasted_iota(I32, (8, D_B), 0)
    top = jnp.where(row8 < CONV_W - 1, pltpu.roll(prev, CONV_W - 1, 0), 0.0)
    ext_p = jnp.concatenate([top, jnp.zeros((tm - 8, D_B), F32)], axis=0)
    ext = jnp.where(is_s, ext_ref[...], ext_p)
    t_in = jnp.where(is_s, row & (DEC_SEQ - 1), row)
    s1 = jnp.where(t_in < 1, pltpu.roll(ext, tm - 1, 0), pltpu.roll(cx, 1, 0))
    s2 = jnp.where(t_in < 2, ext, pltpu.roll(cx, 2, 0))
    prev_ref[...] = cx[tm - 8:tm]
    conv = cb_ref[...] + s2 * cw_ref[0:1, :] + s1 * cw_ref[1:2, :] + cx * cw_ref[2:3, :]
    b_in = (bg * conv).astype(BF16)

    y_a = jnp.dot(a_in, wa_ref[...], preferred_element_type=F32)
    y_b = jnp.dot(b_in, wb_ref[...], preferred_element_type=F32)
    o_g = 2 * D_A + 3 * D_B
    ga = z_ref[:, o_g:o_g + D_MODEL].astype(F32)
    gb = z_ref[:, o_g + D_MODEL:o_g + 2 * D_MODEL].astype(F32)
    mix = (ga * y_a + gb * y_b).astype(BF16)
    x = jnp.where(is_s, xs_ref[...], xp_ref[...])
    x1 = x + jnp.dot(mix, wo_ref[...], preferred_element_type=F32)
    x1_ref[...] = x1

    xn = _rms(x1, gmoe_ref[...])
    xn_ref[...] = xn
    hi = xn.astype(BF16)
    lo = (xn - hi.astype(F32)).astype(BF16)
    logits = (jnp.dot(hi, wrh_ref[...], preferred_element_type=F32)
              + jnp.dot(lo, wrh_ref[...], preferred_element_type=F32)
              + jnp.dot(hi, wrl_ref[...], preferred_element_type=F32)) + br_ref[...]
    lane = lax.broadcasted_iota(I32, (tm, LANES), 1)
    lane_f = lane.astype(F32)
    vals, idxs, hots = [], [], []
    work = logits
    for _ in range(TOP_K):
        mx = jnp.max(work, axis=-1, keepdims=True)
        idx = jnp.min(jnp.where(work == mx, lane_f, float(LANES)), axis=-1, keepdims=True)
        hot = lane_f == idx
        work = jnp.where(hot, -jnp.inf, work)
        vals.append(mx)
        idxs.append(idx)
        hots.append(hot)
    exps = [jnp.exp(vk - vals[0]) for vk in vals]
    den = exps[0] + exps[1] + exps[2] + exps[3]

    chosen = jnp.zeros((tm, LANES), F32)
    for hot in hots:
        chosen = chosen + jnp.where(hot, 1.0, 0.0)
    r_i = lax.broadcasted_iota(I32, (tm, tm), 0)
    c_i = lax.broadcasted_iota(I32, (tm, tm), 1)
    tri = jnp.where(c_i < r_i, 1.0, 0.0).astype(BF16)
    before = jnp.dot(tri, chosen.astype(BF16), preferred_element_type=F32) + carry_ref[0:1, :]
    total = carry_ref[0:1, :] + jnp.sum(chosen, axis=0, keepdims=True)
    carry_ref[...] = jnp.broadcast_to(total, carry_ref.shape)
    cnt_ref[...] = jnp.broadcast_to(total, cnt_ref.shape)

    ri = jnp.zeros((tm, LANES), F32)
    rg = jnp.zeros((tm, LANES), F32)
    for k in range(TOP_K):
        rank = jnp.sum(jnp.where(hots[k], before, 0.0), axis=-1, keepdims=True)
        ri = ri + jnp.where(lane == k, idxs[k], 0.0) + jnp.where(lane == TOP_K + k, rank, 0.0)
        rg = rg + jnp.where(lane == k, exps[k] / den, 0.0)
    ri_ref[...] = ri.astype(I32)
    rg_ref[...] = rg


def _mixer_call(xs, xp, z, vpre, ext, wsg, bsg, ln_g, ln_b, conv_w, conv_b, wa, wb, wo, g_moe,
                wr_hi, wr_lo, b_r):
    tm = MIX_TM
    ns = MIX_NS
    d_in = z.shape[1]
    s_idx = lambda m: (jnp.minimum(m, ns - 1), 0)
    p_idx = lambda m: (jnp.maximum(m - ns, 0), 0)
    row = lambda m: (m, 0)
    return pl.pallas_call(
        _mixer_kernel,
        grid=(T // tm,),
        in_specs=[
            pl.BlockSpec((tm, D_MODEL), s_idx),
            pl.BlockSpec((tm, D_MODEL), p_idx),
            pl.BlockSpec((tm, d_in), row),
            pl.BlockSpec((tm, D_A), row),
            pl.BlockSpec((tm, D_B), s_idx),
            pl.BlockSpec((1, N_GROUPS_A, CHUNK, CHUNK), lambda m: (jnp.minimum(m // ns, 1), 0, 0, 0)),
            pl.BlockSpec((1, CHUNK, D_A), lambda m: (jnp.minimum(m // ns, 1), 0, 0)),
            _resident((1, D_A)),
            _resident((1, D_A)),
            _resident((CONV_W, D_B)),
            _resident((1, D_B)),
            _resident((D_A, D_MODEL)),
            _resident((D_B, D_MODEL)),
            _resident((D_MODEL, D_MODEL)),
            _resident((1, D_MODEL)),
            _resident((D_MODEL, LANES)),
            _resident((D_MODEL, LANES)),
            _resident((1, LANES)),
        ],
        out_specs=[
            pl.BlockSpec((tm, D_MODEL), row),
            pl.BlockSpec((tm, D_MODEL), row),
            pl.BlockSpec((tm, LANES), row),
            pl.BlockSpec((tm, LANES), row),
            pl.BlockSpec((8, LANES), lambda m: (0, 0)),
            pl.BlockSpec((tm, D_A), s_idx),
            pl.BlockSpec((tm, D_B), s_idx),
            pl.BlockSpec((1, 8, D_B), lambda m: (jnp.maximum(m - ns, 0), 0, 0)),
        ],
        out_shape=[
            jax.ShapeDtypeStruct((T, D_MODEL), F32),
            jax.ShapeDtypeStruct((T, D_MODEL), F32),
            jax.ShapeDtypeStruct((T, LANES), I32),
            jax.ShapeDtypeStruct((T, LANES), F32),
            jax.ShapeDtypeStruct((8, LANES), F32),
            jax.ShapeDtypeStruct((T_S, D_A), F32),
            jax.ShapeDtypeStruct((T_S, D_B), F32),
            jax.ShapeDtypeStruct((T_P // tm, 8, D_B), F32),
        ],
        scratch_shapes=[pltpu.VMEM((8, D_B), F32), pltpu.VMEM((8, LANES), F32)],
        compiler_params=_cparams(("arbitrary",), 52),
        name="mixer",
    )(xs, xp, z, vpre, ext, wsg, bsg, ln_g, ln_b, conv_w, conv_b, wa, wb, wo, g_moe, wr_hi, wr_lo, b_r)


MOE_CHUNK_SLOT0 = {8: 0, 4: 0, 2: 4, 1: 6}


def _for_units(n, unit_fn, chunk_begin=None):
    big = MOE_CHUNKS[0]

    def chunk(u0, count):
        slot0 = MOE_CHUNK_SLOT0[count]
        if chunk_begin is not None:
            chunk_begin(range(slot0, slot0 + count))
        for j in range(count):
            unit_fn(u0 + j, slot0 + j)

    def body(c, carry):
        chunk(c * big, big)
        return carry

    n_big = lax.shift_right_logical(n, big.bit_length() - 1)
    lax.fori_loop(0, n_big, body, 0)
    base = n_big * big
    for count in MOE_CHUNKS[1:]:
        @pl.when((n & count) != 0)
        def _():
            chunk(base, count)

        base = base + (n & count)


def _moe_kernel(tok_ref, ite_ref, row0_ref, nblk_ref, wait_ref, used_ref,
                xn_hbm, wgu_hbm, wd_hbm, bgu_hbm, bd_hbm,
                y_hbm,
                xraw_ref, xb_ref, act_ref, wst_ref, wbf_ref, bgu_ref, bd_ref,
                ystage_ref, zbuf_ref, gsem, ysem, zsem, wsem, bsem):
    i = pl.program_id(0)
    nblk = nblk_ref[i]
    row0 = row0_ref[i]
    valid = nblk > 0
    par = i & 1

    def gu_copies(e, tile, slot):
        col = pl.multiple_of(tile * MOE_TF, MOE_TF)
        return (pltpu.make_async_copy(wgu_hbm.at[e, :, pl.ds(col, MOE_TF)],
                                      wst_ref.at[slot, :, pl.ds(0, MOE_TF)], wsem.at[slot]),
                pltpu.make_async_copy(wgu_hbm.at[e, :, pl.ds(D_FF + col, MOE_TF)],
                                      wst_ref.at[slot, :, pl.ds(MOE_TF, MOE_TF)], wsem.at[slot]))

    def d_copy(e, tile, slot):
        col = pl.multiple_of(tile * MOE_TN, MOE_TN)
        return pltpu.make_async_copy(wd_hbm.at[e, :, pl.ds(col, MOE_TN)], wst_ref.at[slot], wsem.at[slot])

    def start_tile(e, t):
        slot = lax.rem(t, MOE_WSLOTS)

        @pl.when(t < MOE_P1)
        def _():
            for c in gu_copies(e, t, slot):
                c.start()

        @pl.when(t >= MOE_P1)
        def _():
            d_copy(e, t - MOE_P1, slot).start()

    def bias_copies(e, slot):
        return (pltpu.make_async_copy(bgu_hbm.at[e], bgu_ref.at[slot], bsem.at[slot]),
                pltpu.make_async_copy(bd_hbm.at[e], bd_ref.at[slot], bsem.at[slot]))

    def tail_copy(b):
        r = pl.multiple_of(b * MOE_RB, MOE_RB)
        return pltpu.make_async_copy(zbuf_ref, y_hbm.at[pl.ds(r, MOE_RB), :], zsem)

    def row_copy(tok, r):
        return pltpu.make_async_copy(xn_hbm.at[pl.ds(tok, 1), :], xraw_ref.at[pl.ds(r, 1), :], gsem)

    def unit_wait():
        return pltpu.make_async_copy(xn_hbm.at[pl.ds(0, MOE_RB), :], xraw_ref.at[pl.ds(0, MOE_RB), :], gsem)

    def gather_rows(item, lo_unit, hi_unit):
        base = row0_ref[item]

        def body(c, carry):
            for j in range(GATHER_UNROLL):
                r = c * GATHER_UNROLL + j
                row_copy(tok_ref[base + r], r).start()
            return carry

        per_unit = MOE_RB // GATHER_UNROLL
        lax.fori_loop(lo_unit * per_unit, hi_unit * per_unit, body, 0)

    def y_copy(slot, u, col):
        r = pl.multiple_of(row0 + u * MOE_RB, MOE_RB)
        return pltpu.make_async_copy(ystage_ref.at[slot], y_hbm.at[pl.ds(r, MOE_RB), pl.ds(col, MOE_TN)],
                                     ysem.at[slot])

    def dump_copy(slot):
        r = N_SLOTS + (slot // MOE_P2) * MOE_RB
        c = (slot % MOE_P2) * MOE_TN
        return pltpu.make_async_copy(ystage_ref.at[slot], y_hbm.at[pl.ds(r, MOE_RB), pl.ds(c, MOE_TN)],
                                     ysem.at[slot])

    @pl.when(i == 0)
    def _():
        for t in range(MOE_WAHEAD):
            for c in gu_copies(ite_ref[0], t, t):
                c.start()
        for c in bias_copies(ite_ref[0], 0):
            c.start()
        gather_rows(0, 0, nblk_ref[0])
        ystage_ref[...] = jnp.zeros_like(ystage_ref)
        for slot in range(MOE_YSLOTS):
            dump_copy(slot).start()
        zbuf_ref[...] = jnp.zeros_like(zbuf_ref)

        def fill(b, carry):
            tail_copy(b).start()
            return carry

        lax.fori_loop(used_ref[0], N_BLOCKS, fill, 0)

    def wait_rows(b, carry):
        unit_wait().wait()
        return carry

    lax.fori_loop(0, wait_ref[i], wait_rows, 0)

    def conv(u, carry):
        rows = pl.ds(pl.multiple_of(u * MOE_RB, MOE_RB), MOE_RB)
        xb_ref[rows, :] = xraw_ref[rows, :].astype(BF16)
        return carry

    lax.fori_loop(0, nblk, conv, 0)

    e = ite_ref[i]
    e_next = ite_ref[i + 1]
    next_valid = nblk_ref[i + 1] > 0
    next_base = row0_ref[i + 1]

    @pl.when(valid)
    def _():
        for c in bias_copies(e, par):
            c.wait()

    def gate_up_step(s, carry):
        wslot = lax.rem(s, MOE_WSLOTS)
        for c in gu_copies(e, s, wslot):
            c.wait()
        start_tile(e, s + MOE_WAHEAD)
        wbf_ref[...] = wst_ref[wslot].astype(BF16)
        b_g = bgu_ref[par, pl.ds(s, 1), :]
        b_u = bgu_ref[par, pl.ds(MOE_P1 + s, 1), :]

        def unit(u, slot):
            r = pl.multiple_of(u * MOE_RB, MOE_RB)
            gu = jnp.dot(xb_ref[pl.ds(r, MOE_RB), :], wbf_ref[...], preferred_element_type=F32)
            gate = jnp.minimum(gu[:, 0:MOE_TF] + b_g, SWIGLU_LIMIT)
            up = jnp.clip(gu[:, MOE_TF:2 * MOE_TF] + b_u, -SWIGLU_LIMIT, SWIGLU_LIMIT)
            act = (up + 1) * (gate * jax.nn.sigmoid(gate * SWIGLU_ALPHA))
            act_ref[s, pl.ds(r, MOE_RB), :] = act.astype(BF16)
            for j in range(MOE_G):
                rr = r + s * MOE_G + j
                row_copy(tok_ref[next_base + rr], rr).start()

        _for_units(nblk, unit)
        return carry

    def down_step(s, carry):
        t = MOE_P1 + s
        wslot = lax.rem(t, MOE_WSLOTS)
        d_copy(e, s, wslot).wait()

        @pl.when(t + MOE_WAHEAD < MOE_TILES)
        def _():
            start_tile(e, t + MOE_WAHEAD)

        @pl.when((t + MOE_WAHEAD >= MOE_TILES) & next_valid)
        def _():
            start_tile(e_next, t + MOE_WAHEAD - MOE_TILES)

        @pl.when((t + MOE_WAHEAD == MOE_TILES) & next_valid)
        def _():
            for c in bias_copies(e_next, 1 - par):
                c.start()

        wbf_ref[...] = wst_ref[wslot].astype(BF16)
        b_d = bd_ref[par, pl.ds(s, 1), :]
        col = pl.multiple_of(s * MOE_TN, MOE_TN)

        def free_slots(slots):
            for slot in slots:
                y_copy(slot, 0, col).wait()

        def unit(u, slot):
            r = pl.multiple_of(u * MOE_RB, MOE_RB)
            a = jnp.concatenate([act_ref[j, pl.ds(r, MOE_RB), :] for j in range(MOE_P1)], axis=1)
            ystage_ref[slot] = jnp.dot(a, wbf_ref[...], preferred_element_type=F32) + b_d
            y_copy(slot, u, col).start()

        _for_units(nblk, unit, free_slots)
        return carry

    @pl.when(valid)
    def _():
        lax.fori_loop(0, MOE_P1, gate_up_step, 0)
        gather_rows(i + 1, nblk, nblk_ref[i + 1])
        lax.fori_loop(0, MOE_P2, down_step, 0)

    @pl.when(i == MOE_NI - 1)
    def _():
        for slot in range(MOE_YSLOTS):
            dump_copy(slot).wait()

        lax.fori_loop(0, wait_ref[MOE_NI], wait_rows, 0)

        def drain(b, carry):
            tail_copy(b).wait()
            return carry

        lax.fori_loop(used_ref[0], N_BLOCKS, drain, 0)


def _moe_call(slot_tok, it_e, it_row0, it_nblk, it_wait, n_used, xn, w_gate_up, w_down, b_gate_up, b_down):
    any_spec = pl.BlockSpec(memory_space=pl.ANY)
    grid_spec = pltpu.PrefetchScalarGridSpec(
        num_scalar_prefetch=6,
        grid=(MOE_NI,),
        in_specs=[any_spec] * 5,
        out_specs=any_spec,
        scratch_shapes=[
            pltpu.VMEM((MOE_RMAX, D_MODEL), F32),
            pltpu.VMEM((MOE_RMAX, D_MODEL), BF16),
            pltpu.VMEM((MOE_P1, MOE_RMAX, MOE_TF), BF16),
            pltpu.VMEM((MOE_WSLOTS, D_MODEL, MOE_TN), F32),
            pltpu.VMEM((D_MODEL, MOE_TN), BF16),
            pltpu.VMEM((2, 2 * MOE_P1, MOE_TF), F32),
            pltpu.VMEM((2, MOE_P2, MOE_TN), F32),
            pltpu.VMEM((MOE_YSLOTS, MOE_RB, MOE_TN), F32),
            pltpu.VMEM((MOE_RB, D_MODEL), F32),
            pltpu.SemaphoreType.DMA(()),
            pltpu.SemaphoreType.DMA((MOE_YSLOTS,)),
            pltpu.SemaphoreType.DMA(()),
            pltpu.SemaphoreType.DMA((MOE_WSLOTS,)),
            pltpu.SemaphoreType.DMA((2,)),
        ],
    )
    spare_blocks = MOE_YSLOTS // MOE_P2
    return pl.pallas_call(
        _moe_kernel,
        grid_spec=grid_spec,
        out_shape=jax.ShapeDtypeStruct((N_SLOTS + spare_blocks * MOE_RB, D_MODEL), F32),
        compiler_params=_cparams(("arbitrary",), 56),
        name="moe",
    )(slot_tok, it_e, it_row0, it_nblk, it_wait, n_used, xn, w_gate_up, w_down, b_gate_up, b_down)


def _combine_kernel(dest_ref, y_hbm, x1_ref, rg_ref, ps_ref, pp_ref, wple_ref, wpg_ref, gple_ref,
                    gfin_ref, ys_ref, yp_ref, gbuf_ref, gsem):
    m = pl.program_id(0)
    nm = pl.num_programs(0)
    tm = CMB_TM
    slot = m % 2

    def row_copy(tile, slot_, r, k):
        d = dest_ref[(tile * tm + r) * TOP_K + k]
        return pltpu.make_async_copy(y_hbm.at[pl.ds(d, 1), :], gbuf_ref.at[slot_, k, pl.ds(r, 1), :],
                                     gsem.at[slot_])

    def wait_tile(slot_):
        for k in range(TOP_K):
            pltpu.make_async_copy(y_hbm.at[pl.ds(0, tm), :], gbuf_ref.at[slot_, k], gsem.at[slot_]).wait()

    @pl.when(m == 0)
    def _():
        def body(c, carry):
            for j in range(GATHER_UNROLL // TOP_K):
                for k in range(TOP_K):
                    row_copy(0, 0, c * (GATHER_UNROLL // TOP_K) + j, k).start()
            return carry

        lax.fori_loop(0, tm // (GATHER_UNROLL // TOP_K), body, 0)

    wait_tile(slot)
    gates = rg_ref[...]
    moe = gates[:, 0:1] * gbuf_ref[slot, 0]
    for k in range(1, TOP_K):
        moe = moe + gates[:, k:k + 1] * gbuf_ref[slot, k]
    x2 = x1_ref[...] + moe
    is_s = m < CMB_NS
    p = jnp.where(is_s, ps_ref[...], pp_ref[...]).astype(BF16)
    hn = _rms(x2, gple_ref[...]).astype(BF16)
    nxt = jnp.minimum(m + 1, nm - 1)
    rows_per_chunk = tm // CMB_CHUNKS
    cw = D_MODEL // CMB_CHUNKS
    x3_parts = []
    for c in range(CMB_CHUNKS):
        cols = slice(c * cw, (c + 1) * cw)
        pe = jnp.dot(p, wple_ref[:, cols], preferred_element_type=F32)
        gate = jax.nn.sigmoid(jnp.dot(hn, wpg_ref[:, cols], preferred_element_type=F32))
        x3_parts.append(x2[:, cols] + pe * gate)
        for r in range(c * rows_per_chunk, (c + 1) * rows_per_chunk):
            for k in range(TOP_K):
                row_copy(nxt, 1 - slot, r, k).start()
    x3 = jnp.concatenate(x3_parts, axis=1)
    y = _rms(x3, gfin_ref[...])

    @pl.when(is_s)
    def _():
        ys_ref[...] = y

    @pl.when(jnp.logical_not(is_s))
    def _():
        yp_ref[...] = y

    @pl.when(m == nm - 1)
    def _():
        wait_tile(1 - slot)


def _combine_call(dest, y_sorted, x1, rg, ps, pp, wple, wpg, g_ple, g_final):
    tm = CMB_TM
    ns = CMB_NS
    s_idx = lambda m, d: (jnp.minimum(m, ns - 1), 0)
    p_idx = lambda m, d: (jnp.maximum(m - ns, 0), 0)
    row = lambda m, d: (m, 0)
    const2 = lambda m, d: (0, 0)
    grid_spec = pltpu.PrefetchScalarGridSpec(
        num_scalar_prefetch=1,
        grid=(T // tm,),
        in_specs=[
            pl.BlockSpec(memory_space=pl.ANY),
            pl.BlockSpec((tm, D_MODEL), row),
            pl.BlockSpec((tm, LANES), row),
            pl.BlockSpec((tm, PLE_DIM), s_idx),
            pl.BlockSpec((tm, PLE_DIM), p_idx),
            pl.BlockSpec((PLE_DIM, D_MODEL), const2),
            pl.BlockSpec((D_MODEL, D_MODEL), const2),
            pl.BlockSpec((1, D_MODEL), const2),
            pl.BlockSpec((1, D_MODEL), const2),
        ],
        out_specs=[
            pl.BlockSpec((tm, D_MODEL), s_idx),
            pl.BlockSpec((tm, D_MODEL), p_idx),
        ],
        scratch_shapes=[
            pltpu.VMEM((2, TOP_K, tm, D_MODEL), F32),
            pltpu.SemaphoreType.DMA((2,)),
        ],
    )
    return pl.pallas_call(
        _combine_kernel,
        grid_spec=grid_spec,
        out_shape=[
            jax.ShapeDtypeStruct((T_S, D_MODEL), F32),
            jax.ShapeDtypeStruct((T_P, D_MODEL), F32),
        ],
        compiler_params=_cparams(("arbitrary",), 56),
        name="combine",
    )(dest, y_sorted, x1, rg, ps, pp, wple, wpg, g_ple, g_final)


def _routing_tables(route_i, counts):
    e_idx = route_i[:, 0:TOP_K]
    rank = route_i[:, TOP_K:2 * TOP_K]
    cnt = counts[0, :N_EXPERTS].astype(I32)
    padded = (cnt + MOE_RB - 1) // MOE_RB * MOE_RB
    gend = jnp.cumsum(padded)
    gstart = gend - padded
    dest = (gstart[e_idx] + rank).astype(I32)
    tok_of = jnp.repeat(jnp.arange(T, dtype=I32), TOP_K)
    slot_tok = jnp.zeros((N_SLOTS + MOE_RMAX,), I32).at[dest.reshape(-1)].set(tok_of)
    nblk_e = padded // MOE_RB
    items_e = (nblk_e + MOE_BMAX - 1) // MOE_BMAX
    iend = jnp.cumsum(items_e)
    istart = iend - items_e
    n_items = iend[-1]
    ids = jnp.arange(MOE_NI + 1, dtype=I32)
    valid = ids < n_items
    e_of = jnp.minimum(jnp.searchsorted(iend, ids, side="right"), N_EXPERTS - 1).astype(I32)
    e_last = jnp.minimum(jnp.searchsorted(iend, n_items - 1, side="right"), N_EXPERTS - 1).astype(I32)
    it_e = jnp.where(valid, e_of, e_last)
    local = ids - istart[it_e]
    it_nblk = jnp.where(valid, jnp.minimum(MOE_BMAX, nblk_e[it_e] - local * MOE_BMAX), 0).astype(I32)
    it_row0 = jnp.where(valid, gstart[it_e] + local * MOE_RMAX, 0).astype(I32)
    prev_nblk = jnp.concatenate([jnp.zeros((1,), I32), it_nblk[:-1]])
    it_wait = jnp.maximum(it_nblk, prev_nblk)
    n_used = jnp.sum(nblk_e).astype(I32).reshape(1)
    return dest.reshape(-1), slot_tok, it_e.astype(I32), it_row0, it_nblk, it_wait, n_used


def kernel(x_prompt, x_sample, state_conv, p_prompt, p_sample, g_mix, w_in, ln_v_g, ln_v_b, w_s, b_s,
           conv_w, conv_b, w_proj_a, w_proj_b, w_o, g_moe, w_router, b_router, w_gate_up, b_gate_up,
           w_down, b_down, g_ple, w_ple, w_ple_gate, g_final):
    assert g_mix.shape[0] == 1, "one layer"
    xs = x_sample.reshape(T_S, D_MODEL)
    xp = x_prompt.reshape(T_P, D_MODEL)

    tril = jnp.tril(jnp.ones((CHUNK, CHUNK), bool))
    w_prompt = jnp.where(tril[None], w_s[0], 0.0)
    small = jnp.where(tril[None, :DEC_SEQ, :DEC_SEQ], w_s[0, :, :DEC_SEQ, :DEC_SEQ], 0.0)
    reps = CHUNK // DEC_SEQ
    blockdiag = jnp.kron(jnp.eye(reps, dtype=F32), jnp.ones((DEC_SEQ, DEC_SEQ), F32))
    w_sample = jnp.tile(small, (1, reps, reps)) * blockdiag[None]
    wsg = jnp.stack([w_sample, w_prompt]).astype(BF16)
    bias_p = jnp.repeat(b_s[0].T, GW_A, axis=1)
    bias_s = jnp.tile(jnp.repeat(b_s[0, :, :DEC_SEQ].T, GW_A, axis=1), (reps, 1))
    bsg = jnp.stack([bias_s, bias_p])
    ext = jnp.pad(state_conv[0], ((0, 0), (0, DEC_SEQ - (CONV_W - 1)), (0, 0))).reshape(T_S, D_B)

    wr = jnp.pad(w_router[0], ((0, 0), (0, LANES - N_EXPERTS)))
    wr_hi = wr.astype(BF16)
    wr_lo = (wr - wr_hi.astype(F32)).astype(BF16)
    b_r = jnp.pad(b_router[0], (0, LANES - N_EXPERTS), constant_values=NEG_BIG).reshape(1, LANES)

    h = _norm_call(xs, xp, g_mix)
    z, vpre = _in_proj_call(h, w_in[0])
    x1, xn, route_i, route_g, counts, vln, cxs, tail = _mixer_call(
        xs, xp, z, vpre, ext, wsg, bsg, ln_v_g, ln_v_b, conv_w[0], conv_b,
        w_proj_a[0].astype(BF16), w_proj_b[0].astype(BF16), w_o[0].astype(BF16), g_moe,
        wr_hi, wr_lo, b_r)

    dest, slot_tok, it_e, it_row0, it_nblk, it_wait, n_used = _routing_tables(route_i, counts)
    y_sorted = _moe_call(slot_tok, it_e, it_row0, it_nblk, it_wait, n_used, xn, w_gate_up[0], w_down[0],
                         b_gate_up[0].reshape(N_EXPERTS, 2 * MOE_P1, MOE_TF),
                         b_down[0].reshape(N_EXPERTS, MOE_P2, MOE_TN))
    ys, yp = _combine_call(dest, y_sorted, x1, route_g,
                           p_sample[0].reshape(T_S, PLE_DIM), p_prompt[0].reshape(T_P, PLE_DIM),
                           w_ple[0].astype(BF16), w_ple_gate[0].astype(BF16), g_ple, g_final.reshape(1, D_MODEL))

    y_prompt = yp.reshape(BATCH, SEQ, D_MODEL)
    y_sample = ys.reshape(DEC_BATCH, DEC_SEQ, D_MODEL)
    last = tail.reshape(BATCH, MIX_SEQ_TILES, 8, D_B)[:, -1, 8 - (CONV_W - 1):, :]
    state_conv_prompt = last[None]
    state_conv_sample = cxs.reshape(DEC_BATCH, DEC_SEQ, D_B)[:, DEC_SEQ - (CONV_W - 1):, :][None]
    state_chunk_v_sample = vln.reshape(DEC_BATCH, DEC_SEQ, D_A)[None]
    return (y_prompt, y_sample, state_conv_prompt, state_conv_sample, state_chunk_v_sample)
```

```python
import functools

import jax
import jax.numpy as jnp
from jax import lax
from jax.experimental import pallas as pl
from jax.experimental.pallas import tpu as pltpu

F32 = jnp.float32
BF16 = jnp.bfloat16
I32 = jnp.int32

D_MODEL = 2048
BATCH = 4
SEQ = 2048
DEC_BATCH = 128
DEC_SEQ = 8
CHUNK = 128
D_A = D_MODEL // 2
N_GROUPS_A = 8
GW_A = D_A // N_GROUPS_A
D_B = D_MODEL // 2
CONV_W = 3
N_EXPERTS = 32
TOP_K = 4
D_FF = D_MODEL
SWIGLU_LIMIT = 7.0
SWIGLU_ALPHA = 1.702
PLE_DIM = 256
EPS = 1e-6

T_S = DEC_BATCH * DEC_SEQ
T_P = BATCH * SEQ
T = T_S + T_P

LANES = 128
V7X_VMEM_BYTES = 64 * 1024 * 1024
MIB = 1024 * 1024

NORM_TM = 512
IN_TM = 1536
IN_TN = 512
IN_SUB = 256
MIX_TM = CHUNK
MIX_NS = T_S // MIX_TM
MIX_SEQ_TILES = SEQ // MIX_TM
MOE_RB = 256
MOE_BMAX = 6
MOE_RMAX = MOE_RB * MOE_BMAX
MOE_CHUNKS = (4, 2, 1)
MOE_TF = 256
MOE_TN = 512
MOE_P1 = D_FF // MOE_TF
MOE_P2 = D_MODEL // MOE_TN
MOE_TILES = MOE_P1 + MOE_P2
MOE_WSLOTS = 3
MOE_WAHEAD = MOE_WSLOTS - 1
assert D_MODEL == D_FF and 2 * MOE_TF == MOE_TN and MOE_TILES % MOE_WSLOTS == 0
MOE_G = MOE_RB // MOE_P1
MOE_YSLOTS = 8
N_SLOTS = T * TOP_K + N_EXPERTS * MOE_RB
N_BLOCKS = N_SLOTS // MOE_RB
MOE_NI = (N_BLOCKS + N_EXPERTS * (MOE_BMAX - 1)) // MOE_BMAX
GATHER_UNROLL = 8
CMB_TM = 256
CMB_NS = T_S // CMB_TM
CMB_CHUNKS = 8
NEG_BIG = -1e30


def _rms(x, g):
    return x * lax.rsqrt(jnp.mean(x * x, axis=-1, keepdims=True) + EPS) * g


def _cparams(sem, vmem_mib):
    return pltpu.CompilerParams(dimension_semantics=sem, vmem_limit_bytes=vmem_mib * MIB)


def _resident(shape):
    zeros = (0,) * len(shape)
    return pl.BlockSpec(shape, lambda *_: zeros, pipeline_mode=pl.Buffered(1))


def _norm_kernel(xs_ref, xp_ref, g_ref, h_ref, *, ns):
    m = pl.program_id(0)
    x = jnp.where(m < ns, xs_ref[...], xp_ref[...])
    h_ref[...] = _rms(x, g_ref[...]).astype(BF16)


def _norm_call(xs, xp, g):
    ns = T_S // NORM_TM
    return pl.pallas_call(
        functools.partial(_norm_kernel, ns=ns),
        grid=(T // NORM_TM,),
        in_specs=[
            pl.BlockSpec((NORM_TM, D_MODEL), lambda m: (jnp.minimum(m, ns - 1), 0)),
            pl.BlockSpec((NORM_TM, D_MODEL), lambda m: (jnp.maximum(m - ns, 0), 0)),
            pl.BlockSpec((1, D_MODEL), lambda m: (0, 0)),
        ],
        out_specs=pl.BlockSpec((NORM_TM, D_MODEL), lambda m: (m, 0)),
        out_shape=jax.ShapeDtypeStruct((T, D_MODEL), BF16),
        compiler_params=_cparams(("arbitrary",), 32),
        name="norm",
    )(xs, xp, g)


IN_N_GELU = 2 * D_A // IN_TN
IN_N_V0 = D_A // IN_TN
IN_N_LIN = (2 * D_A + 3 * D_B) // IN_TN


def _in_proj_kernel(h_ref, w_ref, z_ref, vpre_ref, wb_ref):
    n = pl.program_id(1)
    wb_ref[...] = w_ref[...].astype(BF16)

    def blocks(epilogue):
        for b in range(IN_TM // IN_SUB):
            rows = pl.ds(b * IN_SUB, IN_SUB)
            epilogue(rows, jnp.dot(h_ref[rows, :], wb_ref[...], preferred_element_type=F32))

    @pl.when(n < IN_N_V0)
    def _():
        def ep(rows, acc):
            z_ref[rows, :] = jax.nn.gelu(acc, approximate=True).astype(BF16)

        blocks(ep)

    @pl.when((n >= IN_N_V0) & (n < IN_N_GELU))
    def _():
        def ep(rows, acc):
            g = jax.nn.gelu(acc, approximate=True)
            z_ref[rows, :] = g.astype(BF16)
            vpre_ref[rows, :] = g

        blocks(ep)

    @pl.when((n >= IN_N_GELU) & (n < IN_N_LIN))
    def _():
        def ep(rows, acc):
            z_ref[rows, :] = acc.astype(BF16)

        blocks(ep)

    @pl.when(n >= IN_N_LIN)
    def _():
        def ep(rows, acc):
            z_ref[rows, :] = jax.nn.sigmoid(acc).astype(BF16)

        blocks(ep)


def _in_proj_call(h, w_in):
    d_in = w_in.shape[1]
    return pl.pallas_call(
        _in_proj_kernel,
        grid=(T // IN_TM, d_in // IN_TN),
        in_specs=[
            pl.BlockSpec((IN_TM, D_MODEL), lambda m, n: (m, 0)),
            pl.BlockSpec((D_MODEL, IN_TN), lambda m, n: (0, n)),
        ],
        out_specs=[
            pl.BlockSpec((IN_TM, IN_TN), lambda m, n: (m, n)),
            pl.BlockSpec((IN_TM, IN_TN), lambda m, n: (m, jnp.clip(n - IN_N_V0, 0, IN_N_GELU - IN_N_V0 - 1))),
        ],
        out_shape=[
            jax.ShapeDtypeStruct((T, d_in), BF16),
            jax.ShapeDtypeStruct((T, D_A), F32),
        ],
        scratch_shapes=[pltpu.VMEM((D_MODEL, IN_TN), BF16)],
        compiler_params=_cparams(("arbitrary", "arbitrary"), 44),
        name="in_proj",
    )(h, w_in)


def _mixer_kernel(xs_ref, xp_ref, z_ref, vpre_ref, ext_ref, wsg_ref, bsg_ref, lng_ref, lnb_ref,
                  cw_ref, cb_ref, wa_ref, wb_ref, wo_ref, gmoe_ref, wrh_ref, wrl_ref, br_ref,
                  x1_ref, xn_ref, ri_ref, rg_ref, cnt_ref, vln_ref, cxs_ref, tail_ref,
                  prev_ref, carry_ref):
    m = pl.program_id(0)
    is_s = m < MIX_NS
    tm = MIX_TM

    @pl.when(m == 0)
    def _():
        prev_ref[...] = jnp.zeros_like(prev_ref)
        carry_ref[...] = jnp.zeros_like(carry_ref)

    vg = vpre_ref[...]
    mu = jnp.mean(vg, axis=-1, keepdims=True)
    vc = vg - mu
    v = vc * lax.rsqrt(jnp.mean(vc * vc, axis=-1, keepdims=True) + EPS) * lng_ref[...] + lnb_ref[...]

    @pl.when(is_s)
    def _():
        vln_ref[...] = v

    vb = v.astype(BF16)
    s_parts = []
    for g in range(N_GROUPS_A):
        s_parts.append(jnp.dot(wsg_ref[0, g], vb[:, g * GW_A:(g + 1) * GW_A], preferred_element_type=F32))
    s = jnp.concatenate(s_parts, axis=1) + bsg_ref[0]
    u = z_ref[:, 0:D_A].astype(F32)
    a_in = (u * s).astype(BF16)

    o_b = 2 * D_A
    bg = z_ref[:, o_b:o_b + D_B].astype(F32)
    cg = z_ref[:, o_b + D_B:o_b + 2 * D_B].astype(F32)
    xin = z_ref[:, o_b + 2 * D_B:o_b + 3 * D_B].astype(F32)
    cx = cg * xin

    @pl.when(is_s)
    def _():
        cxs_ref[...] = cx

    @pl.when(jnp.logical_not(is_s))
    def _():
        tail_ref[0] = cx[tm - 8:tm]

    row = lax.broadcasted_iota(I32, (tm, D_B), 0)
    seq_start = ((m - MIX_NS) % MIX_SEQ_TILES) == 0
    prev = jnp.where(seq_start, 0.0, prev_ref[...])
    row8 = lax.broadcasted_iota(I32, (8, D_B), 0)
    top = jnp.where(row8 < CONV_W - 1, pltpu.roll(prev, CONV_W - 1, 0), 0.0)
    ext_p = jnp.concatenate([top, jnp.zeros((tm - 8, D_B), F32)], axis=0)
    ext = jnp.where(is_s, ext_ref[...], ext_p)
    t_in = jnp.where(is_s, row & (DEC_SEQ - 1), row)
    s1 = jnp.where(t_in < 1, pltpu.roll(ext, tm - 1, 0), pltpu.roll(cx, 1, 0))
    s2 = jnp.where(t_in < 2, ext, pltpu.roll(cx, 2, 0))
    prev_ref[...] = cx[tm - 8:tm]
    conv = cb_ref[...] + s2 * cw_ref[0:1, :] + s1 * cw_ref[1:2, :] + cx * cw_ref[2:3, :]
    b_in = (bg * conv).astype(BF16)

    y_a = jnp.dot(a_in, wa_ref[...], preferred_element_type=F32)
    y_b = jnp.dot(b_in, wb_ref[...], preferred_element_type=F32)
    o_g = 2 * D_A + 3 * D_B
    ga = z_ref[:, o_g:o_g + D_MODEL].astype(F32)
    gb = z_ref[:, o_g + D_MODEL:o_g + 2 * D_MODEL].astype(F32)
    mix = (ga * y_a + gb * y_b).astype(BF16)
    x = jnp.where(is_s, xs_ref[...], xp_ref[...])
    x1 = x + jnp.dot(mix, wo_ref[...], preferred_element_type=F32)
    x1_ref[...] = x1

    xn = _rms(x1, gmoe_ref[...])
    xn_ref[...] = xn
    hi = xn.astype(BF16)
    lo = (xn - hi.astype(F32)).astype(BF16)
    logits = (jnp.dot(hi, wrh_ref[...], preferred_element_type=F32)
              + jnp.dot(lo, wrh_ref[...], preferred_element_type=F32)
              + jnp.dot(hi, wrl_ref[...], preferred_element_type=F32)) + br_ref[...]
    lane = lax.broadcasted_iota(I32, (tm, LANES), 1)
    lane_f = lane.astype(F32)
    vals, idxs, hots = [], [], []
    work = logits
    for _ in range(TOP_K):
        mx = jnp.max(work, axis=-1, keepdims=True)
        idx = jnp.min(jnp.where(work == mx, lane_f, float(LANES)), axis=-1, keepdims=True)
        hot = lane_f == idx
        work = jnp.where(hot, -jnp.inf, work)
        vals.append(mx)
        idxs.append(idx)
        hots.append(hot)
    exps = [jnp.exp(vk - vals[0]) for vk in vals]
    den = exps[0] + exps[1] + exps[2] + exps[3]

    chosen = jnp.zeros((tm, LANES), F32)
    for hot in hots:
        chosen = chosen + jnp.where(hot, 1.0, 0.0)
    r_i = lax.broadcasted_iota(I32, (tm, tm), 0)
    c_i = lax.broadcasted_iota(I32, (tm, tm), 1)
    tri = jnp.where(c_i < r_i, 1.0, 0.0).astype(BF16)
    before = jnp.dot(tri, chosen.astype(BF16), preferred_element_type=F32) + carry_ref[0:1, :]
    total = carry_ref[0:1, :] + jnp.sum(chosen, axis=0, keepdims=True)
    carry_ref[...] = jnp.broadcast_to(total, carry_ref.shape)
    cnt_ref[...] = jnp.broadcast_to(total, cnt_ref.shape)

    ri = jnp.zeros((tm, LANES), F32)
    rg = jnp.zeros((tm, LANES), F32)
    for k in range(TOP_K):
        rank = jnp.sum(jnp.where(hots[k], before, 0.0), axis=-1, keepdims=True)
        ri = ri + jnp.where(lane == k, idxs[k], 0.0) + jnp.where(lane == TOP_K + k, rank, 0.0)
        rg = rg + jnp.where(lane == k, exps[k] / den, 0.0)
    ri_ref[...] = ri.astype(I32)
    rg_ref[...] = rg


def _mixer_call(xs, xp, z, vpre, ext, wsg, bsg, ln_g, ln_b, conv_w, conv_b, wa, wb, wo, g_moe,
                wr_hi, wr_lo, b_r):
    tm = MIX_TM
    ns = MIX_NS
    d_in = z.shape[1]
    s_idx = lambda m: (jnp.minimum(m, ns - 1), 0)
    p_idx = lambda m: (jnp.maximum(m - ns, 0), 0)
    row = lambda m: (m, 0)
    return pl.pallas_call(
        _mixer_kernel,
        grid=(T // tm,),
        in_specs=[
            pl.BlockSpec((tm, D_MODEL), s_idx),
            pl.BlockSpec((tm, D_MODEL), p_idx),
            pl.BlockSpec((tm, d_in), row),
            pl.BlockSpec((tm, D_A), row),
            pl.BlockSpec((tm, D_B), s_idx),
            pl.BlockSpec((1, N_GROUPS_A, CHUNK, CHUNK), lambda m: (jnp.minimum(m // ns, 1), 0, 0, 0)),
            pl.BlockSpec((1, CHUNK, D_A), lambda m: (jnp.minimum(m // ns, 1), 0, 0)),
            _resident((1, D_A)),
            _resident((1, D_A)),
            _resident((CONV_W, D_B)),
            _resident((1, D_B)),
            _resident((D_A, D_MODEL)),
            _resident((D_B, D_MODEL)),
            _resident((D_MODEL, D_MODEL)),
            _resident((1, D_MODEL)),
            _resident((D_MODEL, LANES)),
            _resident((D_MODEL, LANES)),
            _resident((1, LANES)),
        ],
        out_specs=[
            pl.BlockSpec((tm, D_MODEL), row),
            pl.BlockSpec((tm, D_MODEL), row),
            pl.BlockSpec((tm, LANES), row),
            pl.BlockSpec((tm, LANES), row),
            pl.BlockSpec((8, LANES), lambda m: (0, 0)),
            pl.BlockSpec((tm, D_A), s_idx),
            pl.BlockSpec((tm, D_B), s_idx),
            pl.BlockSpec((1, 8, D_B), lambda m: (jnp.maximum(m - ns, 0), 0, 0)),
        ],
        out_shape=[
            jax.ShapeDtypeStruct((T, D_MODEL), F32),
            jax.ShapeDtypeStruct((T, D_MODEL), F32),
            jax.ShapeDtypeStruct((T, LANES), I32),
            jax.ShapeDtypeStruct((T, LANES), F32),
            jax.ShapeDtypeStruct((8, LANES), F32),
            jax.ShapeDtypeStruct((T_S, D_A), F32),
            jax.ShapeDtypeStruct((T_S, D_B), F32),
            jax.ShapeDtypeStruct((T_P // tm, 8, D_B), F32),
        ],
        scratch_shapes=[pltpu.VMEM((8, D_B), F32), pltpu.VMEM((8, LANES), F32)],
        compiler_params=_cparams(("arbitrary",), 52),
        name="mixer",
    )(xs, xp, z, vpre, ext, wsg, bsg, ln_g, ln_b, conv_w, conv_b, wa, wb, wo, g_moe, wr_hi, wr_lo, b_r)


MOE_CHUNK_SLOT0 = {4: 0, 2: 4, 1: 6}


def _for_units(n, unit_fn, chunk_begin=None):
    big = MOE_CHUNKS[0]

    def chunk(u0, count):
        slot0 = MOE_CHUNK_SLOT0[count]
        if chunk_begin is not None:
            chunk_begin(range(slot0, slot0 + count))
        for j in range(count):
            unit_fn(u0 + j, slot0 + j)

    def body(c, carry):
        chunk(c * big, big)
        return carry

    n_big = lax.shift_right_logical(n, big.bit_length() - 1)
    lax.fori_loop(0, n_big, body, 0)
    base = n_big * big
    for count in MOE_CHUNKS[1:]:
        @pl.when((n & count) != 0)
        def _():
            chunk(base, count)

        base = base + (n & count)


def _moe_kernel(tok_ref, ite_ref, row0_ref, nblk_ref, wait_ref, used_ref,
                xn_hbm, wgu_hbm, wd_hbm, bgu_hbm, bd_hbm,
                y_hbm,
                xraw_ref, xb_ref, act_ref, wst_ref, wbf_ref, bgu_ref, bd_ref,
                ystage_ref, zbuf_ref, gsem, ysem, zsem, wsem, bsem):
    i = pl.program_id(0)
    nblk = nblk_ref[i]
    row0 = row0_ref[i]
    valid = nblk > 0
    par = i & 1

    def gu_copies(e, tile, slot):
        col = pl.multiple_of(tile * MOE_TF, MOE_TF)
        return (pltpu.make_async_copy(wgu_hbm.at[e, :, pl.ds(col, MOE_TF)],
                                      wst_ref.at[slot, :, pl.ds(0, MOE_TF)], wsem.at[slot]),
                pltpu.make_async_copy(wgu_hbm.at[e, :, pl.ds(D_FF + col, MOE_TF)],
                                      wst_ref.at[slot, :, pl.ds(MOE_TF, MOE_TF)], wsem.at[slot]))

    def d_copy(e, tile, slot):
        col = pl.multiple_of(tile * MOE_TN, MOE_TN)
        return pltpu.make_async_copy(wd_hbm.at[e, :, pl.ds(col, MOE_TN)], wst_ref.at[slot], wsem.at[slot])

    def start_tile(e, t):
        slot = lax.rem(t, MOE_WSLOTS)

        @pl.when(t < MOE_P1)
        def _():
            for c in gu_copies(e, t, slot):
                c.start()

        @pl.when(t >= MOE_P1)
        def _():
            d_copy(e, t - MOE_P1, slot).start()

    def bias_copies(e, slot):
        return (pltpu.make_async_copy(bgu_hbm.at[e], bgu_ref.at[slot], bsem.at[slot]),
                pltpu.make_async_copy(bd_hbm.at[e], bd_ref.at[slot], bsem.at[slot]))

    def tail_copy(b):
        r = pl.multiple_of(b * MOE_RB, MOE_RB)
        return pltpu.make_async_copy(zbuf_ref, y_hbm.at[pl.ds(r, MOE_RB), :], zsem)

    def row_copy(tok, r):
        return pltpu.make_async_copy(xn_hbm.at[pl.ds(tok, 1), :], xraw_ref.at[pl.ds(r, 1), :], gsem)

    def unit_wait():
        return pltpu.make_async_copy(xn_hbm.at[pl.ds(0, MOE_RB), :], xraw_ref.at[pl.ds(0, MOE_RB), :], gsem)

    def gather_rows(item, lo_unit, hi_unit):
        base = row0_ref[item]

        def body(c, carry):
            for j in range(GATHER_UNROLL):
                r = c * GATHER_UNROLL + j
                row_copy(tok_ref[base + r], r).start()
            return carry

        per_unit = MOE_RB // GATHER_UNROLL
        lax.fori_loop(lo_unit * per_unit, hi_unit * per_unit, body, 0)

    def y_copy(slot, u, col):
        r = pl.multiple_of(row0 + u * MOE_RB, MOE_RB)
        return pltpu.make_async_copy(ystage_ref.at[slot], y_hbm.at[pl.ds(r, MOE_RB), pl.ds(col, MOE_TN)],
                                     ysem.at[slot])

    def dump_copy(slot):
        r = N_SLOTS + (slot // MOE_P2) * MOE_RB
        c = (slot % MOE_P2) * MOE_TN
        return pltpu.make_async_copy(ystage_ref.at[slot], y_hbm.at[pl.ds(r, MOE_RB), pl.ds(c, MOE_TN)],
                                     ysem.at[slot])

    @pl.when(i == 0)
    def _():
        for t in range(MOE_WAHEAD):
            for c in gu_copies(ite_ref[0], t, t):
                c.start()
        for c in bias_copies(ite_ref[0], 0):
            c.start()
        gather_rows(0, 0, nblk_ref[0])
        ystage_ref[...] = jnp.zeros_like(ystage_ref)
        for slot in range(MOE_YSLOTS):
            dump_copy(slot).start()
        zbuf_ref[...] = jnp.zeros_like(zbuf_ref)

        def fill(b, carry):
            tail_copy(b).start()
            return carry

        lax.fori_loop(used_ref[0], N_BLOCKS, fill, 0)

    def wait_rows(b, carry):
        unit_wait().wait()
        return carry

    lax.fori_loop(0, wait_ref[i], wait_rows, 0)

    def conv(u, carry):
        rows = pl.ds(pl.multiple_of(u * MOE_RB, MOE_RB), MOE_RB)
        xb_ref[rows, :] = xraw_ref[rows, :].astype(BF16)
        return carry

    lax.fori_loop(0, nblk, conv, 0)

    e = ite_ref[i]
    e_next = ite_ref[i + 1]
    next_valid = nblk_ref[i + 1] > 0
    next_base = row0_ref[i + 1]

    @pl.when(valid)
    def _():
        for c in bias_copies(e, par):
            c.wait()

    def gate_up_step(s, carry):
        wslot = lax.rem(s, MOE_WSLOTS)
        for c in gu_copies(e, s, wslot):
            c.wait()
        start_tile(e, s + MOE_WAHEAD)
        wbf_ref[...] = wst_ref[wslot].astype(BF16)
        b_g = bgu_ref[par, pl.ds(s, 1), :]
        b_u = bgu_ref[par, pl.ds(MOE_P1 + s, 1), :]

        def unit(u, slot):
            r = pl.multiple_of(u * MOE_RB, MOE_RB)
            gu = jnp.dot(xb_ref[pl.ds(r, MOE_RB), :], wbf_ref[...], preferred_element_type=F32)
            gate = jnp.minimum(gu[:, 0:MOE_TF] + b_g, SWIGLU_LIMIT)
            up = jnp.clip(gu[:, MOE_TF:2 * MOE_TF] + b_u, -SWIGLU_LIMIT, SWIGLU_LIMIT)
            act = (up + 1) * (gate * jax.nn.sigmoid(gate * SWIGLU_ALPHA))
            act_ref[s, pl.ds(r, MOE_RB), :] = act.astype(BF16)
            for j in range(MOE_G):
                rr = r + s * MOE_G + j
                row_copy(tok_ref[next_base + rr], rr).start()

        _for_units(nblk, unit)
        return carry

    def down_step(s, carry):
        t = MOE_P1 + s
        wslot = lax.rem(t, MOE_WSLOTS)
        d_copy(e, s, wslot).wait()

        @pl.when(t + MOE_WAHEAD < MOE_TILES)
        def _():
            start_tile(e, t + MOE_WAHEAD)

        @pl.when((t + MOE_WAHEAD >= MOE_TILES) & next_valid)
        def _():
            start_tile(e_next, t + MOE_WAHEAD - MOE_TILES)

        @pl.when((t + MOE_WAHEAD == MOE_TILES) & next_valid)
        def _():
            for c in bias_copies(e_next, 1 - par):
                c.start()

        wbf_ref[...] = wst_ref[wslot].astype(BF16)
        b_d = bd_ref[par, pl.ds(s, 1), :]
        col = pl.multiple_of(s * MOE_TN, MOE_TN)

        def free_slots(slots):
            for slot in slots:
                y_copy(slot, 0, col).wait()

        def unit(u, slot):
            r = pl.multiple_of(u * MOE_RB, MOE_RB)
            a = jnp.concatenate([act_ref[j, pl.ds(r, MOE_RB), :] for j in range(MOE_P1)], axis=1)
            ystage_ref[slot] = jnp.dot(a, wbf_ref[...], preferred_element_type=F32) + b_d
            y_copy(slot, u, col).start()

        _for_units(nblk, unit, free_slots)
        return carry

    @pl.when(valid)
    def _():
        lax.fori_loop(0, MOE_P1, gate_up_step, 0)
        gather_rows(i + 1, nblk, nblk_ref[i + 1])
        lax.fori_loop(0, MOE_P2, down_step, 0)

    @pl.when(i == MOE_NI - 1)
    def _():
        for slot in range(MOE_YSLOTS):
            dump_copy(slot).wait()

        lax.fori_loop(0, wait_ref[MOE_NI], wait_rows, 0)

        def drain(b, carry):
            tail_copy(b).wait()
            return carry

        lax.fori_loop(used_ref[0], N_BLOCKS, drain, 0)


def _moe_call(slot_tok, it_e, it_row0, it_nblk, it_wait, n_used, xn, w_gate_up, w_down, b_gate_up, b_down):
    any_spec = pl.BlockSpec(memory_space=pl.ANY)
    grid_spec = pltpu.PrefetchScalarGridSpec(
        num_scalar_prefetch=6,
        grid=(MOE_NI,),
        in_specs=[any_spec] * 5,
        out_specs=any_spec,
        scratch_shapes=[
            pltpu.VMEM((MOE_RMAX, D_MODEL), F32),
            pltpu.VMEM((MOE_RMAX, D_MODEL), BF16),
            pltpu.VMEM((MOE_P1, MOE_RMAX, MOE_TF), BF16),
            pltpu.VMEM((MOE_WSLOTS, D_MODEL, MOE_TN), F32),
            pltpu.VMEM((D_MODEL, MOE_TN), BF16),
            pltpu.VMEM((2, 2 * MOE_P1, MOE_TF), F32),
            pltpu.VMEM((2, MOE_P2, MOE_TN), F32),
            pltpu.VMEM((MOE_YSLOTS, MOE_RB, MOE_TN), F32),
            pltpu.VMEM((MOE_RB, D_MODEL), F32),
            pltpu.SemaphoreType.DMA(()),
            pltpu.SemaphoreType.DMA((MOE_YSLOTS,)),
            pltpu.SemaphoreType.DMA(()),
            pltpu.SemaphoreType.DMA((MOE_WSLOTS,)),
            pltpu.SemaphoreType.DMA((2,)),
        ],
    )
    spare_blocks = MOE_YSLOTS // MOE_P2
    return pl.pallas_call(
        _moe_kernel,
        grid_spec=grid_spec,
        out_shape=jax.ShapeDtypeStruct((N_SLOTS + spare_blocks * MOE_RB, D_MODEL), F32),
        compiler_params=_cparams(("arbitrary",), 56),
        name="moe",
    )(slot_tok, it_e, it_row0, it_nblk, it_wait, n_used, xn, w_gate_up, w_down, b_gate_up, b_down)


def _combine_kernel(dest_ref, y_hbm, x1_ref, rg_ref, ps_ref, pp_ref, wple_ref, wpg_ref, gple_ref,
                    gfin_ref, ys_ref, yp_ref, gbuf_ref, gsem):
    m = pl.program_id(0)
    nm = pl.num_programs(0)
    tm = CMB_TM
    slot = m % 2

    def row_copy(tile, slot_, r, k):
        d = dest_ref[(tile * tm + r) * TOP_K + k]
        return pltpu.make_async_copy(y_hbm.at[pl.ds(d, 1), :], gbuf_ref.at[slot_, k, pl.ds(r, 1), :],
                                     gsem.at[slot_])

    def wait_tile(slot_):
        for k in range(TOP_K):
            pltpu.make_async_copy(y_hbm.at[pl.ds(0, tm), :], gbuf_ref.at[slot_, k], gsem.at[slot_]).wait()

    @pl.when(m == 0)
    def _():
        def body(c, carry):
            for j in range(GATHER_UNROLL // TOP_K):
                for k in range(TOP_K):
                    row_copy(0, 0, c * (GATHER_UNROLL // TOP_K) + j, k).start()
            return carry

        lax.fori_loop(0, tm // (GATHER_UNROLL // TOP_K), body, 0)

    wait_tile(slot)
    gates = rg_ref[...]
    moe = gates[:, 0:1] * gbuf_ref[slot, 0]
    for k in range(1, TOP_K):
        moe = moe + gates[:, k:k + 1] * gbuf_ref[slot, k]
    x2 = x1_ref[...] + moe
    is_s = m < CMB_NS
    p = jnp.where(is_s, ps_ref[...], pp_ref[...]).astype(BF16)
    hn = _rms(x2, gple_ref[...]).astype(BF16)
    nxt = jnp.minimum(m + 1, nm - 1)
    rows_per_chunk = tm // CMB_CHUNKS
    cw = D_MODEL // CMB_CHUNKS
    x3_parts = []
    for c in range(CMB_CHUNKS):
        cols = slice(c * cw, (c + 1) * cw)
        pe = jnp.dot(p, wple_ref[:, cols], preferred_element_type=F32)
        gate = jax.nn.sigmoid(jnp.dot(hn, wpg_ref[:, cols], preferred_element_type=F32))
        x3_parts.append(x2[:, cols] + pe * gate)
        for r in range(c * rows_per_chunk, (c + 1) * rows_per_chunk):
            for k in range(TOP_K):
                row_copy(nxt, 1 - slot, r, k).start()
    x3 = jnp.concatenate(x3_parts, axis=1)
    y = _rms(x3, gfin_ref[...])

    @pl.when(is_s)
    def _():
        ys_ref[...] = y

    @pl.when(jnp.logical_not(is_s))
    def _():
        yp_ref[...] = y

    @pl.when(m == nm - 1)
    def _():
        wait_tile(1 - slot)


def _combine_call(dest, y_sorted, x1, rg, ps, pp, wple, wpg, g_ple, g_final):
    tm = CMB_TM
    ns = CMB_NS
    s_idx = lambda m, d: (jnp.minimum(m, ns - 1), 0)
    p_idx = lambda m, d: (jnp.maximum(m - ns, 0), 0)
    row = lambda m, d: (m, 0)
    const2 = lambda m, d: (0, 0)
    grid_spec = pltpu.PrefetchScalarGridSpec(
        num_scalar_prefetch=1,
        grid=(T // tm,),
        in_specs=[
            pl.BlockSpec(memory_space=pl.ANY),
            pl.BlockSpec((tm, D_MODEL), row),
            pl.BlockSpec((tm, LANES), row),
            pl.BlockSpec((tm, PLE_DIM), s_idx),
            pl.BlockSpec((tm, PLE_DIM), p_idx),
            pl.BlockSpec((PLE_DIM, D_MODEL), const2),
            pl.BlockSpec((D_MODEL, D_MODEL), const2),
            pl.BlockSpec((1, D_MODEL), const2),
            pl.BlockSpec((1, D_MODEL), const2),
        ],
        out_specs=[
            pl.BlockSpec((tm, D_MODEL), s_idx),
            pl.BlockSpec((tm, D_MODEL), p_idx),
        ],
        scratch_shapes=[
            pltpu.VMEM((2, TOP_K, tm, D_MODEL), F32),
            pltpu.SemaphoreType.DMA((2,)),
        ],
    )
    return pl.pallas_call(
        _combine_kernel,
        grid_spec=grid_spec,
        out_shape=[
            jax.ShapeDtypeStruct((T_S, D_MODEL), F32),
            jax.ShapeDtypeStruct((T_P, D_MODEL), F32),
        ],
        compiler_params=_cparams(("arbitrary",), 56),
        name="combine",
    )(dest, y_sorted, x1, rg, ps, pp, wple, wpg, g_ple, g_final)


def _routing_tables(route_i, counts):
    e_idx = route_i[:, 0:TOP_K]
    rank = route_i[:, TOP_K:2 * TOP_K]
    cnt = counts[0, :N_EXPERTS].astype(I32)
    padded = (cnt + MOE_RB - 1) // MOE_RB * MOE_RB
    gend = jnp.cumsum(padded)
    gstart = gend - padded
    dest = (gstart[e_idx] + rank).astype(I32)
    tok_of = jnp.repeat(jnp.arange(T, dtype=I32), TOP_K)
    slot_tok = jnp.zeros((N_SLOTS + MOE_RMAX,), I32).at[dest.reshape(-1)].set(tok_of)
    nblk_e = padded // MOE_RB
    items_e = (nblk_e + MOE_BMAX - 1) // MOE_BMAX
    iend = jnp.cumsum(items_e)
    istart = iend - items_e
    n_items = iend[-1]
    ids = jnp.arange(MOE_NI + 1, dtype=I32)
    valid = ids < n_items
    e_of = jnp.minimum(jnp.searchsorted(iend, ids, side="right"), N_EXPERTS - 1).astype(I32)
    e_last = jnp.minimum(jnp.searchsorted(iend, n_items - 1, side="right"), N_EXPERTS - 1).astype(I32)
    it_e = jnp.where(valid, e_of, e_last)
    local = ids - istart[it_e]
    it_nblk = jnp.where(valid, jnp.minimum(MOE_BMAX, nblk_e[it_e] - local * MOE_BMAX), 0).astype(I32)
    it_row0 = jnp.where(valid, gstart[it_e] + local * MOE_RMAX, 0).astype(I32)
    prev_nblk = jnp.concatenate([jnp.zeros((1,), I32), it_nblk[:-1]])
    it_wait = jnp.maximum(it_nblk, prev_nblk)
    n_used = jnp.sum(nblk_e).astype(I32).reshape(1)
    return dest.reshape(-1), slot_tok, it_e.astype(I32), it_row0, it_nblk, it_wait, n_used


def kernel(x_prompt, x_sample, state_conv, p_prompt, p_sample, g_mix, w_in, ln_v_g, ln_v_b, w_s, b_s,
           conv_w, conv_b, w_proj_a, w_proj_b, w_o, g_moe, w_router, b_router, w_gate_up, b_gate_up,
           w_down, b_down, g_ple, w_ple, w_ple_gate, g_final):
    assert g_mix.shape[0] == 1, "one layer"
    xs = x_sample.reshape(T_S, D_MODEL)
    xp = x_prompt.reshape(T_P, D_MODEL)

    tril = jnp.tril(jnp.ones((CHUNK, CHUNK), bool))
    w_prompt = jnp.where(tril[None], w_s[0], 0.0)
    small = jnp.where(tril[None, :DEC_SEQ, :DEC_SEQ], w_s[0, :, :DEC_SEQ, :DEC_SEQ], 0.0)
    reps = CHUNK // DEC_SEQ
    blockdiag = jnp.kron(jnp.eye(reps, dtype=F32), jnp.ones((DEC_SEQ, DEC_SEQ), F32))
    w_sample = jnp.tile(small, (1, reps, reps)) * blockdiag[None]
    wsg = jnp.stack([w_sample, w_prompt]).astype(BF16)
    bias_p = jnp.repeat(b_s[0].T, GW_A, axis=1)
    bias_s = jnp.tile(jnp.repeat(b_s[0, :, :DEC_SEQ].T, GW_A, axis=1), (reps, 1))
    bsg = jnp.stack([bias_s, bias_p])
    ext = jnp.pad(state_conv[0], ((0, 0), (0, DEC_SEQ - (CONV_W - 1)), (0, 0))).reshape(T_S, D_B)

    wr = jnp.pad(w_router[0], ((0, 0), (0, LANES - N_EXPERTS)))
    wr_hi = wr.astype(BF16)
    wr_lo = (wr - wr_hi.astype(F32)).astype(BF16)
    b_r = jnp.pad(b_router[0], (0, LANES - N_EXPERTS), constant_values=NEG_BIG).reshape(1, LANES)

    h = _norm_call(xs, xp, g_mix)
    z, vpre = _in_proj_call(h, w_in[0])
    x1, xn, route_i, route_g, counts, vln, cxs, tail = _mixer_call(
        xs, xp, z, vpre, ext, wsg, bsg, ln_v_g, ln_v_b, conv_w[0], conv_b,
        w_proj_a[0].astype(BF16), w_proj_b[0].astype(BF16), w_o[0].astype(BF16), g_moe,
        wr_hi, wr_lo, b_r)

    dest, slot_tok, it_e, it_row0, it_nblk, it_wait, n_used = _routing_tables(route_i, counts)
    y_sorted = _moe_call(slot_tok, it_e, it_row0, it_nblk, it_wait, n_used, xn, w_gate_up[0], w_down[0],
                         b_gate_up[0].reshape(N_EXPERTS, 2 * MOE_P1, MOE_TF),
                         b_down[0].reshape(N_EXPERTS, MOE_P2, MOE_TN))
    ys, yp = _combine_call(dest, y_sorted, x1, route_g,
                           p_sample[0].reshape(T_S, PLE_DIM), p_prompt[0].reshape(T_P, PLE_DIM),
                           w_ple[0].astype(BF16), w_ple_gate[0].astype(BF16), g_ple, g_final.reshape(1, D_MODEL))

    y_prompt = yp.reshape(BATCH, SEQ, D_MODEL)
    y_sample = ys.reshape(DEC_BATCH, DEC_SEQ, D_MODEL)
    last = tail.reshape(BATCH, MIX_SEQ_TILES, 8, D_B)[:, -1, 8 - (CONV_W - 1):, :]
    state_conv_prompt = last[None]
    state_conv_sample = cxs.reshape(DEC_BATCH, DEC_SEQ, D_B)[:, DEC_SEQ - (CONV_W - 1):, :][None]
    state_chunk_v_sample = vln.reshape(DEC_BATCH, DEC_SEQ, D_A)[None]
    return (y_prompt, y_sample, state_conv_prompt, state_conv_sample, state_chunk_v_sample)
```

```python
import functools

import jax
import jax.numpy as jnp
from jax import lax
from jax.experimental import pallas as pl
from jax.experimental.pallas import tpu as pltpu

F32 = jnp.float32
BF16 = jnp.bfloat16
I32 = jnp.int32

D_MODEL = 2048
BATCH = 4
SEQ = 2048
DEC_BATCH = 128
DEC_SEQ = 8
CHUNK = 128
D_A = D_MODEL // 2
N_GROUPS_A = 8
GW_A = D_A // N_GROUPS_A
D_B = D_MODEL // 2
CONV_W = 3
N_EXPERTS = 32
TOP_K = 4
D_FF = D_MODEL
SWIGLU_LIMIT = 7.0
SWIGLU_ALPHA = 1.702
PLE_DIM = 256
EPS = 1e-6

T_S = DEC_BATCH * DEC_SEQ
T_P = BATCH * SEQ
T = T_S + T_P

LANES = 128
V7X_VMEM_BYTES = 64 * 1024 * 1024
MIB = 1024 * 1024

NORM_TM = 512
IN_TM = 1536
IN_TN = 512
IN_SUB = 256
MIX_TM = CHUNK
MIX_NS = T_S // MIX_TM
MIX_SEQ_TILES = SEQ // MIX_TM
MOE_RB = 128
MOE_BMAX = 12
MOE_RMAX = MOE_RB * MOE_BMAX
MOE_CHUNKS = (8, 4, 2, 1)
MOE_TF = 256
MOE_TN = 512
MOE_P1 = D_FF // MOE_TF
MOE_P2 = D_MODEL // MOE_TN
MOE_TILES = MOE_P1 + MOE_P2
MOE_WSLOTS = 3
MOE_WAHEAD = MOE_WSLOTS - 1
assert D_MODEL == D_FF and 2 * MOE_TF == MOE_TN and MOE_TILES % MOE_WSLOTS == 0
MOE_G = MOE_RB // MOE_P1
MOE_YSLOTS = 8
N_SLOTS = T * TOP_K + N_EXPERTS * MOE_RB
N_BLOCKS = N_SLOTS // MOE_RB
MOE_NI = (N_BLOCKS + N_EXPERTS * (MOE_BMAX - 1)) // MOE_BMAX
GATHER_UNROLL = 8
TAB_STRIDE = 64
TAB_E, TAB_ROW0, TAB_NBLK, TAB_WAIT, TAB_USED = 0, TAB_STRIDE, 2 * TAB_STRIDE, 3 * TAB_STRIDE, 4 * TAB_STRIDE
TAB_SIZE = 5 * TAB_STRIDE
assert MOE_NI + 1 <= TAB_STRIDE
N_TOK_TAB = -(-(N_SLOTS + MOE_RMAX) // 1024) * 1024
CMB_TM = 256
CMB_NS = T_S // CMB_TM
CMB_CHUNKS = 8
NEG_BIG = -1e30


def _rms(x, g):
    return x * lax.rsqrt(jnp.mean(x * x, axis=-1, keepdims=True) + EPS) * g


def _cparams(sem, vmem_mib):
    return pltpu.CompilerParams(dimension_semantics=sem, vmem_limit_bytes=vmem_mib * MIB)


def _resident(shape):
    zeros = (0,) * len(shape)
    return pl.BlockSpec(shape, lambda *_: zeros, pipeline_mode=pl.Buffered(1))


def _norm_kernel(xs_ref, xp_ref, g_ref, h_ref, *, ns):
    m = pl.program_id(0)
    x = jnp.where(m < ns, xs_ref[...], xp_ref[...])
    h_ref[...] = _rms(x, g_ref[...]).astype(BF16)


def _norm_call(xs, xp, g):
    ns = T_S // NORM_TM
    return pl.pallas_call(
        functools.partial(_norm_kernel, ns=ns),
        grid=(T // NORM_TM,),
        in_specs=[
            pl.BlockSpec((NORM_TM, D_MODEL), lambda m: (jnp.minimum(m, ns - 1), 0)),
            pl.BlockSpec((NORM_TM, D_MODEL), lambda m: (jnp.maximum(m - ns, 0), 0)),
            pl.BlockSpec((1, D_MODEL), lambda m: (0, 0)),
        ],
        out_specs=pl.BlockSpec((NORM_TM, D_MODEL), lambda m: (m, 0)),
        out_shape=jax.ShapeDtypeStruct((T, D_MODEL), BF16),
        compiler_params=_cparams(("arbitrary",), 32),
        name="norm",
    )(xs, xp, g)


IN_N_GELU = 2 * D_A // IN_TN
IN_N_V0 = D_A // IN_TN
IN_N_LIN = (2 * D_A + 3 * D_B) // IN_TN


def _in_proj_kernel(h_ref, w_ref, z_ref, vpre_ref, wb_ref):
    n = pl.program_id(1)
    wb_ref[...] = w_ref[...].astype(BF16)

    def blocks(epilogue):
        for b in range(IN_TM // IN_SUB):
            rows = pl.ds(b * IN_SUB, IN_SUB)
            epilogue(rows, jnp.dot(h_ref[rows, :], wb_ref[...], preferred_element_type=F32))

    @pl.when(n < IN_N_V0)
    def _():
        def ep(rows, acc):
            z_ref[rows, :] = jax.nn.gelu(acc, approximate=True).astype(BF16)

        blocks(ep)

    @pl.when((n >= IN_N_V0) & (n < IN_N_GELU))
    def _():
        def ep(rows, acc):
            g = jax.nn.gelu(acc, approximate=True)
            z_ref[rows, :] = g.astype(BF16)
            vpre_ref[rows, :] = g

        blocks(ep)

    @pl.when((n >= IN_N_GELU) & (n < IN_N_LIN))
    def _():
        def ep(rows, acc):
            z_ref[rows, :] = acc.astype(BF16)

        blocks(ep)

    @pl.when(n >= IN_N_LIN)
    def _():
        def ep(rows, acc):
            z_ref[rows, :] = jax.nn.sigmoid(acc).astype(BF16)

        blocks(ep)


def _in_proj_call(h, w_in):
    d_in = w_in.shape[1]
    return pl.pallas_call(
        _in_proj_kernel,
        grid=(T // IN_TM, d_in // IN_TN),
        in_specs=[
            pl.BlockSpec((IN_TM, D_MODEL), lambda m, n: (m, 0)),
            pl.BlockSpec((D_MODEL, IN_TN), lambda m, n: (0, n)),
        ],
        out_specs=[
            pl.BlockSpec((IN_TM, IN_TN), lambda m, n: (m, n)),
            pl.BlockSpec((IN_TM, IN_TN), lambda m, n: (m, jnp.clip(n - IN_N_V0, 0, IN_N_GELU - IN_N_V0 - 1))),
        ],
        out_shape=[
            jax.ShapeDtypeStruct((T, d_in), BF16),
            jax.ShapeDtypeStruct((T, D_A), F32),
        ],
        scratch_shapes=[pltpu.VMEM((D_MODEL, IN_TN), BF16)],
        compiler_params=_cparams(("arbitrary", "arbitrary"), 44),
        name="in_proj",
    )(h, w_in)


def _mixer_kernel(xs_ref, xp_ref, z_ref, vpre_ref, ext_ref, wsg_ref, bsg_ref, lng_ref, lnb_ref,
                  cw_ref, cb_ref, wa_ref, wb_ref, wo_ref, gmoe_ref, wrh_ref, wrl_ref, br_ref,
                  x1_ref, xn_ref, ri_ref, rg_ref, cnt_ref, vln_ref, cxs_ref, tail_ref,
                  prev_ref, carry_ref):
    m = pl.program_id(0)
    is_s = m < MIX_NS
    tm = MIX_TM

    @pl.when(m == 0)
    def _():
        prev_ref[...] = jnp.zeros_like(prev_ref)
        carry_ref[...] = jnp.zeros_like(carry_ref)

    vg = vpre_ref[...]
    mu = jnp.mean(vg, axis=-1, keepdims=True)
    vc = vg - mu
    v = vc * lax.rsqrt(jnp.mean(vc * vc, axis=-1, keepdims=True) + EPS) * lng_ref[...] + lnb_ref[...]

    @pl.when(is_s)
    def _():
        vln_ref[...] = v

    vb = v.astype(BF16)
    s_parts = []
    for g in range(N_GROUPS_A):
        s_parts.append(jnp.dot(wsg_ref[0, g], vb[:, g * GW_A:(g + 1) * GW_A], preferred_element_type=F32))
    s = jnp.concatenate(s_parts, axis=1) + bsg_ref[0]
    u = z_ref[:, 0:D_A].astype(F32)
    a_in = (u * s).astype(BF16)

    o_b = 2 * D_A
    bg = z_ref[:, o_b:o_b + D_B].astype(F32)
    cg = z_ref[:, o_b + D_B:o_b + 2 * D_B].astype(F32)
    xin = z_ref[:, o_b + 2 * D_B:o_b + 3 * D_B].astype(F32)
    cx = cg * xin

    @pl.when(is_s)
    def _():
        cxs_ref[...] = cx

    @pl.when(jnp.logical_not(is_s))
    def _():
        tail_ref[0] = cx[tm - 8:tm]

    row = lax.broadcasted_iota(I32, (tm, D_B), 0)
    seq_start = ((m - MIX_NS) % MIX_SEQ_TILES) == 0
    prev = jnp.where(seq_start, 0.0, prev_ref[...])
    row8 = lax.broadcasted_iota(I32, (8, D_B), 0)
    top = jnp.where(row8 < CONV_W - 1, pltpu.roll(prev, CONV_W - 1, 0), 0.0)
    ext_p = jnp.concatenate([top, jnp.zeros((tm - 8, D_B), F32)], axis=0)
    ext = jnp.where(is_s, ext_ref[...], ext_p)
    t_in = jnp.where(is_s, row & (DEC_SEQ - 1), row)
    s1 = jnp.where(t_in < 1, pltpu.roll(ext, tm - 1, 0), pltpu.roll(cx, 1, 0))
    s2 = jnp.where(t_in < 2, ext, pltpu.roll(cx, 2, 0))
    prev_ref[...] = cx[tm - 8:tm]
    conv = cb_ref[...] + s2 * cw_ref[0:1, :] + s1 * cw_ref[1:2, :] + cx * cw_ref[2:3, :]
    b_in = (bg * conv).astype(BF16)

    y_a = jnp.dot(a_in, wa_ref[...], preferred_element_type=F32)
    y_b = jnp.dot(b_in, wb_ref[...], preferred_element_type=F32)
    o_g = 2 * D_A + 3 * D_B
    ga = z_ref[:, o_g:o_g + D_MODEL].astype(F32)
    gb = z_ref[:, o_g + D_MODEL:o_g + 2 * D_MODEL].astype(F32)
    mix = (ga * y_a + gb * y_b).astype(BF16)
    x = jnp.where(is_s, xs_ref[...], xp_ref[...])
    x1 = x + jnp.dot(mix, wo_ref[...], preferred_element_type=F32)
    x1_ref[...] = x1

    xn = _rms(x1, gmoe_ref[...])
    xn_ref[...] = xn
    hi = xn.astype(BF16)
    lo = (xn - hi.astype(F32)).astype(BF16)
    logits = (jnp.dot(hi, wrh_ref[...], preferred_element_type=F32)
              + jnp.dot(lo, wrh_ref[...], preferred_element_type=F32)
              + jnp.dot(hi, wrl_ref[...], preferred_element_type=F32)) + br_ref[...]
    lane = lax.broadcasted_iota(I32, (tm, LANES), 1)
    lane_f = lane.astype(F32)
    vals, idxs, hots = [], [], []
    work = logits
    for _ in range(TOP_K):
        mx = jnp.max(work, axis=-1, keepdims=True)
        idx = jnp.min(jnp.where(work == mx, lane_f, float(LANES)), axis=-1, keepdims=True)
        hot = lane_f == idx
        work = jnp.where(hot, -jnp.inf, work)
        vals.append(mx)
        idxs.append(idx)
        hots.append(hot)
    exps = [jnp.exp(vk - vals[0]) for vk in vals]
    den = exps[0] + exps[1] + exps[2] + exps[3]

    chosen = jnp.zeros((tm, LANES), F32)
    for hot in hots:
        chosen = chosen + jnp.where(hot, 1.0, 0.0)
    r_i = lax.broadcasted_iota(I32, (tm, tm), 0)
    c_i = lax.broadcasted_iota(I32, (tm, tm), 1)
    tri = jnp.where(c_i < r_i, 1.0, 0.0).astype(BF16)
    before = jnp.dot(tri, chosen.astype(BF16), preferred_element_type=F32) + carry_ref[0:1, :]
    total = carry_ref[0:1, :] + jnp.sum(chosen, axis=0, keepdims=True)
    carry_ref[...] = jnp.broadcast_to(total, carry_ref.shape)
    cnt_ref[...] = jnp.broadcast_to(total, cnt_ref.shape)

    ri = jnp.zeros((tm, LANES), F32)
    rg = jnp.zeros((tm, LANES), F32)
    for k in range(TOP_K):
        rank = jnp.sum(jnp.where(hots[k], before, 0.0), axis=-1, keepdims=True)
        ri = ri + jnp.where(lane == k, idxs[k], 0.0) + jnp.where(lane == TOP_K + k, rank, 0.0)
        rg = rg + jnp.where(lane == k, exps[k] / den, 0.0)
    ri_ref[...] = jnp.transpose(ri)[0:2 * TOP_K, :].astype(I32)
    rg_ref[...] = rg


def _mixer_call(xs, xp, z, vpre, ext, wsg, bsg, ln_g, ln_b, conv_w, conv_b, wa, wb, wo, g_moe,
                wr_hi, wr_lo, b_r):
    tm = MIX_TM
    ns = MIX_NS
    d_in = z.shape[1]
    s_idx = lambda m: (jnp.minimum(m, ns - 1), 0)
    p_idx = lambda m: (jnp.maximum(m - ns, 0), 0)
    row = lambda m: (m, 0)
    return pl.pallas_call(
        _mixer_kernel,
        grid=(T // tm,),
        in_specs=[
            pl.BlockSpec((tm, D_MODEL), s_idx),
            pl.BlockSpec((tm, D_MODEL), p_idx),
            pl.BlockSpec((tm, d_in), row),
            pl.BlockSpec((tm, D_A), row),
            pl.BlockSpec((tm, D_B), s_idx),
            pl.BlockSpec((1, N_GROUPS_A, CHUNK, CHUNK), lambda m: (jnp.minimum(m // ns, 1), 0, 0, 0)),
            pl.BlockSpec((1, CHUNK, D_A), lambda m: (jnp.minimum(m // ns, 1), 0, 0)),
            _resident((1, D_A)),
            _resident((1, D_A)),
            _resident((CONV_W, D_B)),
            _resident((1, D_B)),
            _resident((D_A, D_MODEL)),
            _resident((D_B, D_MODEL)),
            _resident((D_MODEL, D_MODEL)),
            _resident((1, D_MODEL)),
            _resident((D_MODEL, LANES)),
            _resident((D_MODEL, LANES)),
            _resident((1, LANES)),
        ],
        out_specs=[
            pl.BlockSpec((tm, D_MODEL), row),
            pl.BlockSpec((tm, D_MODEL), row),
            pl.BlockSpec((2 * TOP_K, tm), lambda m: (0, m)),
            pl.BlockSpec((tm, LANES), row),
            pl.BlockSpec((8, LANES), lambda m: (0, 0)),
            pl.BlockSpec((tm, D_A), s_idx),
            pl.BlockSpec((tm, D_B), s_idx),
            pl.BlockSpec((1, 8, D_B), lambda m: (jnp.maximum(m - ns, 0), 0, 0)),
        ],
        out_shape=[
            jax.ShapeDtypeStruct((T, D_MODEL), F32),
            jax.ShapeDtypeStruct((T, D_MODEL), F32),
            jax.ShapeDtypeStruct((2 * TOP_K, T), I32),
            jax.ShapeDtypeStruct((T, LANES), F32),
            jax.ShapeDtypeStruct((8, LANES), F32),
            jax.ShapeDtypeStruct((T_S, D_A), F32),
            jax.ShapeDtypeStruct((T_S, D_B), F32),
            jax.ShapeDtypeStruct((T_P // tm, 8, D_B), F32),
        ],
        scratch_shapes=[pltpu.VMEM((8, D_B), F32), pltpu.VMEM((8, LANES), F32)],
        compiler_params=_cparams(("arbitrary",), 52),
        name="mixer",
    )(xs, xp, z, vpre, ext, wsg, bsg, ln_g, ln_b, conv_w, conv_b, wa, wb, wo, g_moe, wr_hi, wr_lo, b_r)


MOE_CHUNK_SLOT0 = {8: 0, 4: 0, 2: 4, 1: 6}


def _for_units(n, unit_fn, chunk_begin=None):
    big = MOE_CHUNKS[0]

    def chunk(u0, count):
        slot0 = MOE_CHUNK_SLOT0[count]
        if chunk_begin is not None:
            chunk_begin(range(slot0, slot0 + count))
        for j in range(count):
            unit_fn(u0 + j, slot0 + j)

    def body(c, carry):
        chunk(c * big, big)
        return carry

    n_big = lax.shift_right_logical(n, big.bit_length() - 1)
    lax.fori_loop(0, n_big, body, 0)
    base = n_big * big
    for count in MOE_CHUNKS[1:]:
        @pl.when((n & count) != 0)
        def _():
            chunk(base, count)

        base = base + (n & count)


def _moe_kernel(tok_ref, tab_ref,
                xn_hbm, wgu_hbm, wd_hbm, bgu_hbm, bd_hbm,
                y_hbm,
                xraw_ref, xb_ref, act_ref, wst_ref, wbf_ref, bgu_ref, bd_ref,
                ystage_ref, zbuf_ref, gsem, ysem, zsem, wsem, bsem):
    i = pl.program_id(0)
    nblk = tab_ref[TAB_NBLK +i]
    row0 = tab_ref[TAB_ROW0 +i]
    valid = nblk > 0
    par = i & 1

    def gu_copies(e, tile, slot):
        col = pl.multiple_of(tile * MOE_TF, MOE_TF)
        return (pltpu.make_async_copy(wgu_hbm.at[e, :, pl.ds(col, MOE_TF)],
                                      wst_ref.at[slot, :, pl.ds(0, MOE_TF)], wsem.at[slot]),
                pltpu.make_async_copy(wgu_hbm.at[e, :, pl.ds(D_FF + col, MOE_TF)],
                                      wst_ref.at[slot, :, pl.ds(MOE_TF, MOE_TF)], wsem.at[slot]))

    def d_copy(e, tile, slot):
        col = pl.multiple_of(tile * MOE_TN, MOE_TN)
        return pltpu.make_async_copy(wd_hbm.at[e, :, pl.ds(col, MOE_TN)], wst_ref.at[slot], wsem.at[slot])

    def start_tile(e, t):
        slot = lax.rem(t, MOE_WSLOTS)

        @pl.when(t < MOE_P1)
        def _():
            for c in gu_copies(e, t, slot):
                c.start()

        @pl.when(t >= MOE_P1)
        def _():
            d_copy(e, t - MOE_P1, slot).start()

    def bias_copies(e, slot):
        return (pltpu.make_async_copy(bgu_hbm.at[e], bgu_ref.at[slot], bsem.at[slot]),
                pltpu.make_async_copy(bd_hbm.at[e], bd_ref.at[slot], bsem.at[slot]))

    def tail_copy(b):
        r = pl.multiple_of(b * MOE_RB, MOE_RB)
        return pltpu.make_async_copy(zbuf_ref, y_hbm.at[pl.ds(r, MOE_RB), :], zsem)

    def row_copy(tok, r):
        return pltpu.make_async_copy(xn_hbm.at[pl.ds(tok, 1), :], xraw_ref.at[pl.ds(r, 1), :], gsem)

    def unit_wait():
        return pltpu.make_async_copy(xn_hbm.at[pl.ds(0, MOE_RB), :], xraw_ref.at[pl.ds(0, MOE_RB), :], gsem)

    def gather_rows(item, lo_unit, hi_unit):
        base = tab_ref[TAB_ROW0 +item]

        def body(c, carry):
            for j in range(GATHER_UNROLL):
                r = c * GATHER_UNROLL + j
                row_copy(tok_ref[base + r], r).start()
            return carry

        per_unit = MOE_RB // GATHER_UNROLL
        lax.fori_loop(lo_unit * per_unit, hi_unit * per_unit, body, 0)

    def y_copy(slot, u, col):
        r = pl.multiple_of(row0 + u * MOE_RB, MOE_RB)
        return pltpu.make_async_copy(ystage_ref.at[slot], y_hbm.at[pl.ds(r, MOE_RB), pl.ds(col, MOE_TN)],
                                     ysem.at[slot])

    def dump_copy(slot):
        r = N_SLOTS + (slot // MOE_P2) * MOE_RB
        c = (slot % MOE_P2) * MOE_TN
        return pltpu.make_async_copy(ystage_ref.at[slot], y_hbm.at[pl.ds(r, MOE_RB), pl.ds(c, MOE_TN)],
                                     ysem.at[slot])

    @pl.when(i == 0)
    def _():
        for t in range(MOE_WAHEAD):
            for c in gu_copies(tab_ref[TAB_E +0], t, t):
                c.start()
        for c in bias_copies(tab_ref[TAB_E +0], 0):
            c.start()
        gather_rows(0, 0, tab_ref[TAB_NBLK +0])
        ystage_ref[...] = jnp.zeros_like(ystage_ref)
        for slot in range(MOE_YSLOTS):
            dump_copy(slot).start()
        zbuf_ref[...] = jnp.zeros_like(zbuf_ref)

        def fill(b, carry):
            tail_copy(b).start()
            return carry

        lax.fori_loop(tab_ref[TAB_USED], N_BLOCKS, fill, 0)

    def wait_rows(b, carry):
        unit_wait().wait()
        return carry

    lax.fori_loop(0, tab_ref[TAB_WAIT +i], wait_rows, 0)

    def conv(u, carry):
        rows = pl.ds(pl.multiple_of(u * MOE_RB, MOE_RB), MOE_RB)
        xb_ref[rows, :] = xraw_ref[rows, :].astype(BF16)
        return carry

    lax.fori_loop(0, nblk, conv, 0)

    e = tab_ref[TAB_E +i]
    e_next = tab_ref[TAB_E +i + 1]
    next_valid = tab_ref[TAB_NBLK +i + 1] > 0
    next_base = tab_ref[TAB_ROW0 +i + 1]

    @pl.when(valid)
    def _():
        for c in bias_copies(e, par):
            c.wait()

    def gate_up_step(s, carry):
        wslot = lax.rem(s, MOE_WSLOTS)
        for c in gu_copies(e, s, wslot):
            c.wait()
        start_tile(e, s + MOE_WAHEAD)
        wbf_ref[...] = wst_ref[wslot].astype(BF16)
        b_g = bgu_ref[par, pl.ds(s, 1), :]
        b_u = bgu_ref[par, pl.ds(MOE_P1 + s, 1), :]

        def unit(u, slot):
            r = pl.multiple_of(u * MOE_RB, MOE_RB)
            gu = jnp.dot(xb_ref[pl.ds(r, MOE_RB), :], wbf_ref[...], preferred_element_type=F32)
            gate = jnp.minimum(gu[:, 0:MOE_TF] + b_g, SWIGLU_LIMIT)
            up = jnp.clip(gu[:, MOE_TF:2 * MOE_TF] + b_u, -SWIGLU_LIMIT, SWIGLU_LIMIT)
            act = (up + 1) * (gate * jax.nn.sigmoid(gate * SWIGLU_ALPHA))
            act_ref[s, pl.ds(r, MOE_RB), :] = act.astype(BF16)
            for j in range(MOE_G):
                rr = r + s * MOE_G + j
                row_copy(tok_ref[next_base + rr], rr).start()

        _for_units(nblk, unit)
        return carry

    def down_step(s, carry):
        t = MOE_P1 + s
        wslot = lax.rem(t, MOE_WSLOTS)
        d_copy(e, s, wslot).wait()

        @pl.when(t + MOE_WAHEAD < MOE_TILES)
        def _():
            start_tile(e, t + MOE_WAHEAD)

        @pl.when((t + MOE_WAHEAD >= MOE_TILES) & next_valid)
        def _():
            start_tile(e_next, t + MOE_WAHEAD - MOE_TILES)

        @pl.when((t + MOE_WAHEAD == MOE_TILES) & next_valid)
        def _():
            for c in bias_copies(e_next, 1 - par):
                c.start()

        wbf_ref[...] = wst_ref[wslot].astype(BF16)
        b_d = bd_ref[par, pl.ds(s, 1), :]
        col = pl.multiple_of(s * MOE_TN, MOE_TN)

        def free_slots(slots):
            for slot in slots:
                y_copy(slot, 0, col).wait()

        def unit(u, slot):
            r = pl.multiple_of(u * MOE_RB, MOE_RB)
            a = jnp.concatenate([act_ref[j, pl.ds(r, MOE_RB), :] for j in range(MOE_P1)], axis=1)
            ystage_ref[slot] = jnp.dot(a, wbf_ref[...], preferred_element_type=F32) + b_d
            y_copy(slot, u, col).start()

        _for_units(nblk, unit, free_slots)
        return carry

    @pl.when(valid)
    def _():
        lax.fori_loop(0, MOE_P1, gate_up_step, 0)
        gather_rows(i + 1, nblk, tab_ref[TAB_NBLK +i + 1])
        lax.fori_loop(0, MOE_P2, down_step, 0)

    @pl.when(i == MOE_NI - 1)
    def _():
        for slot in range(MOE_YSLOTS):
            dump_copy(slot).wait()

        lax.fori_loop(0, tab_ref[TAB_WAIT +MOE_NI], wait_rows, 0)

        def drain(b, carry):
            tail_copy(b).wait()
            return carry

        lax.fori_loop(tab_ref[TAB_USED], N_BLOCKS, drain, 0)


def _moe_call(slot_tok, tables, xn, w_gate_up, w_down, b_gate_up, b_down):
    any_spec = pl.BlockSpec(memory_space=pl.ANY)
    grid_spec = pltpu.PrefetchScalarGridSpec(
        num_scalar_prefetch=2,
        grid=(MOE_NI,),
        in_specs=[any_spec] * 5,
        out_specs=any_spec,
        scratch_shapes=[
            pltpu.VMEM((MOE_RMAX, D_MODEL), F32),
            pltpu.VMEM((MOE_RMAX, D_MODEL), BF16),
            pltpu.VMEM((MOE_P1, MOE_RMAX, MOE_TF), BF16),
            pltpu.VMEM((MOE_WSLOTS, D_MODEL, MOE_TN), F32),
            pltpu.VMEM((D_MODEL, MOE_TN), BF16),
            pltpu.VMEM((2, 2 * MOE_P1, MOE_TF), F32),
            pltpu.VMEM((2, MOE_P2, MOE_TN), F32),
            pltpu.VMEM((MOE_YSLOTS, MOE_RB, MOE_TN), F32),
            pltpu.VMEM((MOE_RB, D_MODEL), F32),
            pltpu.SemaphoreType.DMA(()),
            pltpu.SemaphoreType.DMA((MOE_YSLOTS,)),
            pltpu.SemaphoreType.DMA(()),
            pltpu.SemaphoreType.DMA((MOE_WSLOTS,)),
            pltpu.SemaphoreType.DMA((2,)),
        ],
    )
    spare_blocks = MOE_YSLOTS // MOE_P2
    return pl.pallas_call(
        _moe_kernel,
        grid_spec=grid_spec,
        out_shape=jax.ShapeDtypeStruct((N_SLOTS + spare_blocks * MOE_RB, D_MODEL), F32),
        compiler_params=_cparams(("arbitrary",), 56),
        name="moe",
    )(slot_tok, tables, xn, w_gate_up, w_down, b_gate_up, b_down)


def _combine_kernel(dest_ref, y_hbm, x1_ref, rg_ref, ps_ref, pp_ref, wple_ref, wpg_ref, gple_ref,
                    gfin_ref, ys_ref, yp_ref, gbuf_ref, gsem):
    m = pl.program_id(0)
    nm = pl.num_programs(0)
    tm = CMB_TM
    slot = m % 2

    def row_copy(tile, slot_, r, k):
        d = dest_ref[k * T + tile * tm + r]
        return pltpu.make_async_copy(y_hbm.at[pl.ds(d, 1), :], gbuf_ref.at[slot_, k, pl.ds(r, 1), :],
                                     gsem.at[slot_])

    def wait_tile(slot_):
        for k in range(TOP_K):
            pltpu.make_async_copy(y_hbm.at[pl.ds(0, tm), :], gbuf_ref.at[slot_, k], gsem.at[slot_]).wait()

    @pl.when(m == 0)
    def _():
        def body(c, carry):
            for j in range(GATHER_UNROLL // TOP_K):
                for k in range(TOP_K):
                    row_copy(0, 0, c * (GATHER_UNROLL // TOP_K) + j, k).start()
            return carry

        lax.fori_loop(0, tm // (GATHER_UNROLL // TOP_K), body, 0)

    wait_tile(slot)
    gates = rg_ref[...]
    moe = gates[:, 0:1] * gbuf_ref[slot, 0]
    for k in range(1, TOP_K):
        moe = moe + gates[:, k:k + 1] * gbuf_ref[slot, k]
    x2 = x1_ref[...] + moe
    is_s = m < CMB_NS
    p = jnp.where(is_s, ps_ref[...], pp_ref[...]).astype(BF16)
    hn = _rms(x2, gple_ref[...]).astype(BF16)
    nxt = jnp.minimum(m + 1, nm - 1)
    rows_per_chunk = tm // CMB_CHUNKS
    cw = D_MODEL // CMB_CHUNKS
    x3_parts = []
    for c in range(CMB_CHUNKS):
        cols = slice(c * cw, (c + 1) * cw)
        pe = jnp.dot(p, wple_ref[:, cols], preferred_element_type=F32)
        gate = jax.nn.sigmoid(jnp.dot(hn, wpg_ref[:, cols], preferred_element_type=F32))
        x3_parts.append(x2[:, cols] + pe * gate)
        for r in range(c * rows_per_chunk, (c + 1) * rows_per_chunk):
            for k in range(TOP_K):
                row_copy(nxt, 1 - slot, r, k).start()
    x3 = jnp.concatenate(x3_parts, axis=1)
    y = _rms(x3, gfin_ref[...])

    @pl.when(is_s)
    def _():
        ys_ref[...] = y

    @pl.when(jnp.logical_not(is_s))
    def _():
        yp_ref[...] = y

    @pl.when(m == nm - 1)
    def _():
        wait_tile(1 - slot)


def _combine_call(dest, y_sorted, x1, rg, ps, pp, wple, wpg, g_ple, g_final):
    tm = CMB_TM
    ns = CMB_NS
    s_idx = lambda m, d: (jnp.minimum(m, ns - 1), 0)
    p_idx = lambda m, d: (jnp.maximum(m - ns, 0), 0)
    row = lambda m, d: (m, 0)
    const2 = lambda m, d: (0, 0)
    grid_spec = pltpu.PrefetchScalarGridSpec(
        num_scalar_prefetch=1,
        grid=(T // tm,),
        in_specs=[
            pl.BlockSpec(memory_space=pl.ANY),
            pl.BlockSpec((tm, D_MODEL), row),
            pl.BlockSpec((tm, LANES), row),
            pl.BlockSpec((tm, PLE_DIM), s_idx),
            pl.BlockSpec((tm, PLE_DIM), p_idx),
            pl.BlockSpec((PLE_DIM, D_MODEL), const2),
            pl.BlockSpec((D_MODEL, D_MODEL), const2),
            pl.BlockSpec((1, D_MODEL), const2),
            pl.BlockSpec((1, D_MODEL), const2),
        ],
        out_specs=[
            pl.BlockSpec((tm, D_MODEL), s_idx),
            pl.BlockSpec((tm, D_MODEL), p_idx),
        ],
        scratch_shapes=[
            pltpu.VMEM((2, TOP_K, tm, D_MODEL), F32),
            pltpu.SemaphoreType.DMA((2,)),
        ],
    )
    return pl.pallas_call(
        _combine_kernel,
        grid_spec=grid_spec,
        out_shape=[
            jax.ShapeDtypeStruct((T_S, D_MODEL), F32),
            jax.ShapeDtypeStruct((T_P, D_MODEL), F32),
        ],
        compiler_params=_cparams(("arbitrary",), 56),
        name="combine",
    )(dest, y_sorted, x1, rg, ps, pp, wple, wpg, g_ple, g_final)


def _route_kernel(cnt_ref, rit_ref, dest_ref, tok_ref, tab_ref, dvm_ref, zvm_ref, gs_ref, sem):
    rb_shift = MOE_RB.bit_length() - 1

    def clear(j, carry):
        tab_ref[j] = 0
        return carry

    lax.fori_loop(0, TAB_SIZE, clear, 0)

    def expert(e, carry):
        acc, item, used = carry
        n = lax.shift_right_logical(cnt_ref[0, e].astype(I32) + (MOE_RB - 1), rb_shift)
        gs_ref[e] = acc

        def add_item(local, it):
            tab_ref[TAB_E + it] = e
            tab_ref[TAB_ROW0 + it] = acc + local * MOE_RMAX
            tab_ref[TAB_NBLK + it] = jnp.minimum(MOE_BMAX, n - local * MOE_BMAX)
            return it + 1

        item = lax.fori_loop(0, lax.div(n + (MOE_BMAX - 1), MOE_BMAX), add_item, item)
        return acc + n * MOE_RB, item, used + n

    _, n_items, used = lax.fori_loop(0, N_EXPERTS, expert, (jnp.int32(0), jnp.int32(0), jnp.int32(0)))
    tab_ref[TAB_USED] = used
    e_last = tab_ref[TAB_E + n_items - 1]

    def pad_item(it, carry):
        tab_ref[TAB_E + it] = e_last
        tab_ref[TAB_ROW0 + it] = 0
        tab_ref[TAB_NBLK + it] = 0
        return carry

    lax.fori_loop(n_items, MOE_NI + 1, pad_item, 0)

    def wait_units(it, prev):
        nb = tab_ref[TAB_NBLK + it]
        tab_ref[TAB_WAIT + it] = jnp.maximum(nb, prev)
        return nb

    lax.fori_loop(0, MOE_NI + 1, wait_units, jnp.int32(0))

    e_idx = rit_ref[0:TOP_K, :]
    d = rit_ref[TOP_K:2 * TOP_K, :]
    for e in range(N_EXPERTS):
        d = d + jnp.where(e_idx == e, gs_ref[e], 0)
    dvm_ref[0:TOP_K, :] = d
    zvm_ref[...] = jnp.zeros_like(zvm_ref)
    copies = [pltpu.make_async_copy(dvm_ref.at[k], dest_ref.at[pl.ds(k * T, T)], sem) for k in range(TOP_K)]
    copies.append(pltpu.make_async_copy(zvm_ref, tok_ref, sem))
    for c in copies:
        c.start()
    for c in copies:
        c.wait()

    for k in range(TOP_K):
        def scatter(c, carry):
            for j in range(GATHER_UNROLL):
                t = c * GATHER_UNROLL + j
                tok_ref[dest_ref[k * T + t]] = t
            return carry

        lax.fori_loop(0, T // GATHER_UNROLL, scatter, 0)


def _route_call(counts, rit):
    smem = pl.BlockSpec(memory_space=pltpu.SMEM)
    return pl.pallas_call(
        _route_kernel,
        in_specs=[smem, pl.BlockSpec(memory_space=pltpu.VMEM)],
        out_specs=[smem, smem, smem],
        out_shape=[
            jax.ShapeDtypeStruct((TOP_K * T,), I32),
            jax.ShapeDtypeStruct((N_TOK_TAB,), I32),
            jax.ShapeDtypeStruct((TAB_SIZE,), I32),
        ],
        scratch_shapes=[
            pltpu.VMEM((2 * TOP_K, T), I32),
            pltpu.VMEM((N_TOK_TAB,), I32),
            pltpu.SMEM((N_EXPERTS,), I32),
            pltpu.SemaphoreType.DMA(()),
        ],
        name="route",
    )(counts, rit)


def kernel(x_prompt, x_sample, state_conv, p_prompt, p_sample, g_mix, w_in, ln_v_g, ln_v_b, w_s, b_s,
           conv_w, conv_b, w_proj_a, w_proj_b, w_o, g_moe, w_router, b_router, w_gate_up, b_gate_up,
           w_down, b_down, g_ple, w_ple, w_ple_gate, g_final):
    assert g_mix.shape[0] == 1, "one layer"
    xs = x_sample.reshape(T_S, D_MODEL)
    xp = x_prompt.reshape(T_P, D_MODEL)

    tril = jnp.tril(jnp.ones((CHUNK, CHUNK), bool))
    w_prompt = jnp.where(tril[None], w_s[0], 0.0)
    small = jnp.where(tril[None, :DEC_SEQ, :DEC_SEQ], w_s[0, :, :DEC_SEQ, :DEC_SEQ], 0.0)
    reps = CHUNK // DEC_SEQ
    blockdiag = jnp.kron(jnp.eye(reps, dtype=F32), jnp.ones((DEC_SEQ, DEC_SEQ), F32))
    w_sample = jnp.tile(small, (1, reps, reps)) * blockdiag[None]
    wsg = jnp.stack([w_sample, w_prompt]).astype(BF16)
    bias_p = jnp.repeat(b_s[0].T, GW_A, axis=1)
    bias_s = jnp.tile(jnp.repeat(b_s[0, :, :DEC_SEQ].T, GW_A, axis=1), (reps, 1))
    bsg = jnp.stack([bias_s, bias_p])
    ext = jnp.pad(state_conv[0], ((0, 0), (0, DEC_SEQ - (CONV_W - 1)), (0, 0))).reshape(T_S, D_B)

    wr = jnp.pad(w_router[0], ((0, 0), (0, LANES - N_EXPERTS)))
    wr_hi = wr.astype(BF16)
    wr_lo = (wr - wr_hi.astype(F32)).astype(BF16)
    b_r = jnp.pad(b_router[0], (0, LANES - N_EXPERTS), constant_values=NEG_BIG).reshape(1, LANES)

    h = _norm_call(xs, xp, g_mix)
    z, vpre = _in_proj_call(h, w_in[0])
    x1, xn, route_i, route_g, counts, vln, cxs, tail = _mixer_call(
        xs, xp, z, vpre, ext, wsg, bsg, ln_v_g, ln_v_b, conv_w[0], conv_b,
        w_proj_a[0].astype(BF16), w_proj_b[0].astype(BF16), w_o[0].astype(BF16), g_moe,
        wr_hi, wr_lo, b_r)

    dest, slot_tok, tables = _route_call(counts, route_i)
    y_sorted = _moe_call(slot_tok, tables, xn, w_gate_up[0], w_down[0],
                         b_gate_up[0].reshape(N_EXPERTS, 2 * MOE_P1, MOE_TF),
                         b_down[0].reshape(N_EXPERTS, MOE_P2, MOE_TN))
    ys, yp = _combine_call(dest, y_sorted, x1, route_g,
                           p_sample[0].reshape(T_S, PLE_DIM), p_prompt[0].reshape(T_P, PLE_DIM),
                           w_ple[0].astype(BF16), w_ple_gate[0].astype(BF16), g_ple, g_final.reshape(1, D_MODEL))

    y_prompt = yp.reshape(BATCH, SEQ, D_MODEL)
    y_sample = ys.reshape(DEC_BATCH, DEC_SEQ, D_MODEL)
    last = tail.reshape(BATCH, MIX_SEQ_TILES, 8, D_B)[:, -1, 8 - (CONV_W - 1):, :]
    state_conv_prompt = last[None]
    state_conv_sample = cxs.reshape(DEC_BATCH, DEC_SEQ, D_B)[:, DEC_SEQ - (CONV_W - 1):, :][None]
    state_chunk_v_sample = vln.reshape(DEC_BATCH, DEC_SEQ, D_A)[None]
    return (y_prompt, y_sample, state_conv_prompt, state_conv_sample, state_chunk_v_sample)
```

```python
import functools

import jax
import jax.numpy as jnp
from jax import lax
from jax.experimental import pallas as pl
from jax.experimental.pallas import tpu as pltpu

F32 = jnp.float32
BF16 = jnp.bfloat16
I32 = jnp.int32

D_MODEL = 2048
BATCH = 4
SEQ = 2048
DEC_BATCH = 128
DEC_SEQ = 8
CHUNK = 128
D_A = D_MODEL // 2
N_GROUPS_A = 8
GW_A = D_A // N_GROUPS_A
D_B = D_MODEL // 2
CONV_W = 3
N_EXPERTS = 32
TOP_K = 4
D_FF = D_MODEL
SWIGLU_LIMIT = 7.0
SWIGLU_ALPHA = 1.702
PLE_DIM = 256
EPS = 1e-6

T_S = DEC_BATCH * DEC_SEQ
T_P = BATCH * SEQ
T = T_S + T_P

LANES = 128
V7X_VMEM_BYTES = 64 * 1024 * 1024
MIB = 1024 * 1024

NORM_TM = 512
IN_TM = 1536
IN_TN = 512
IN_SUB = 256
MIX_TM = CHUNK
MIX_NS = T_S // MIX_TM
MIX_SEQ_TILES = SEQ // MIX_TM
MIX_NT = T // MIX_TM
MOE_RB = 128
MOE_BMAX = 12
MOE_RMAX = MOE_RB * MOE_BMAX
MOE_CHUNKS = (8, 4, 2, 1)
MOE_TF = 256
MOE_TN = 512
MOE_P1 = D_FF // MOE_TF
MOE_P2 = D_MODEL // MOE_TN
MOE_TILES = MOE_P1 + MOE_P2
MOE_WSLOTS = 3
MOE_WAHEAD = MOE_WSLOTS - 1
assert D_MODEL == D_FF and 2 * MOE_TF == MOE_TN and MOE_TILES % MOE_WSLOTS == 0
MOE_G = MOE_RB // MOE_P1
MOE_YSLOTS = 8
N_SLOTS = T * TOP_K + N_EXPERTS * MOE_RB
N_BLOCKS = N_SLOTS // MOE_RB
MOE_NI = (N_BLOCKS + N_EXPERTS * (MOE_BMAX - 1)) // MOE_BMAX
GATHER_UNROLL = 8
TAB_STRIDE = 64
TAB_E, TAB_ROW0, TAB_NBLK, TAB_WAIT, TAB_USED = 0, TAB_STRIDE, 2 * TAB_STRIDE, 3 * TAB_STRIDE, 4 * TAB_STRIDE
TAB_SIZE = 5 * TAB_STRIDE
assert MOE_NI + 1 <= TAB_STRIDE
N_TOK_TAB = -(-(N_SLOTS + MOE_RMAX) // 1024) * 1024
CMB_TM = 256
CMB_NS = T_S // CMB_TM
CMB_CHUNKS = 8
NEG_BIG = -1e30


def _rms(x, g):
    return x * lax.rsqrt(jnp.mean(x * x, axis=-1, keepdims=True) + EPS) * g


def _cparams(sem, vmem_mib):
    return pltpu.CompilerParams(dimension_semantics=sem, vmem_limit_bytes=vmem_mib * MIB)


def _resident(shape):
    zeros = (0,) * len(shape)
    return pl.BlockSpec(shape, lambda *_: zeros, pipeline_mode=pl.Buffered(1))


def _norm_kernel(xs_ref, xp_ref, g_ref, h_ref, *, ns):
    m = pl.program_id(0)
    x = jnp.where(m < ns, xs_ref[...], xp_ref[...])
    h_ref[...] = _rms(x, g_ref[...]).astype(BF16)


def _norm_call(xs, xp, g):
    ns = T_S // NORM_TM
    return pl.pallas_call(
        functools.partial(_norm_kernel, ns=ns),
        grid=(T // NORM_TM,),
        in_specs=[
            pl.BlockSpec((NORM_TM, D_MODEL), lambda m: (jnp.minimum(m, ns - 1), 0)),
            pl.BlockSpec((NORM_TM, D_MODEL), lambda m: (jnp.maximum(m - ns, 0), 0)),
            pl.BlockSpec((1, D_MODEL), lambda m: (0, 0)),
        ],
        out_specs=pl.BlockSpec((NORM_TM, D_MODEL), lambda m: (m, 0)),
        out_shape=jax.ShapeDtypeStruct((T, D_MODEL), BF16),
        compiler_params=_cparams(("arbitrary",), 32),
        name="norm",
    )(xs, xp, g)


IN_N_GELU = 2 * D_A // IN_TN
IN_N_V0 = D_A // IN_TN
IN_N_LIN = (2 * D_A + 3 * D_B) // IN_TN


def _in_proj_kernel(h_ref, w_ref, z_ref, vpre_ref, wb_ref):
    n = pl.program_id(1)
    wb_ref[...] = w_ref[...].astype(BF16)

    def blocks(epilogue):
        for b in range(IN_TM // IN_SUB):
            rows = pl.ds(b * IN_SUB, IN_SUB)
            epilogue(rows, jnp.dot(h_ref[rows, :], wb_ref[...], preferred_element_type=F32))

    @pl.when(n < IN_N_V0)
    def _():
        def ep(rows, acc):
            z_ref[rows, :] = jax.nn.gelu(acc, approximate=True).astype(BF16)

        blocks(ep)

    @pl.when((n >= IN_N_V0) & (n < IN_N_GELU))
    def _():
        def ep(rows, acc):
            g = jax.nn.gelu(acc, approximate=True)
            z_ref[rows, :] = g.astype(BF16)
            vpre_ref[rows, :] = g

        blocks(ep)

    @pl.when((n >= IN_N_GELU) & (n < IN_N_LIN))
    def _():
        def ep(rows, acc):
            z_ref[rows, :] = acc.astype(BF16)

        blocks(ep)

    @pl.when(n >= IN_N_LIN)
    def _():
        def ep(rows, acc):
            z_ref[rows, :] = jax.nn.sigmoid(acc).astype(BF16)

        blocks(ep)


def _in_proj_call(h, w_in):
    d_in = w_in.shape[1]
    return pl.pallas_call(
        _in_proj_kernel,
        grid=(T // IN_TM, d_in // IN_TN),
        in_specs=[
            pl.BlockSpec((IN_TM, D_MODEL), lambda m, n: (m, 0)),
            pl.BlockSpec((D_MODEL, IN_TN), lambda m, n: (0, n)),
        ],
        out_specs=[
            pl.BlockSpec((IN_TM, IN_TN), lambda m, n: (m, n)),
            pl.BlockSpec((IN_TM, IN_TN), lambda m, n: (m, jnp.clip(n - IN_N_V0, 0, IN_N_GELU - IN_N_V0 - 1))),
        ],
        out_shape=[
            jax.ShapeDtypeStruct((T, d_in), BF16),
            jax.ShapeDtypeStruct((T, D_A), F32),
        ],
        scratch_shapes=[pltpu.VMEM((D_MODEL, IN_TN), BF16)],
        compiler_params=_cparams(("arbitrary", "arbitrary"), 44),
        name="in_proj",
    )(h, w_in)


def _mixer_kernel(xs_ref, xp_ref, zu_ref, zb_ref, zc_ref, zx_ref, ga0_ref, ga1_ref, gb0_ref, gb1_ref,
                  vpre_ref, ext_ref, wsg_ref, bsg_ref, lng_ref, lnb_ref,
                  cw_ref, cb_ref, wa_ref, wb_ref, wo_ref, gmoe_ref, wrh_ref, wrl_ref, br_ref,
                  x1_ref, xn_ref, ri_ref, rg_ref, cnt_ref, vln_ref, cxs_ref, tail_ref,
                  prev_ref, carry_ref, ab_ref, lg_ref):
    m = pl.program_id(0)
    tm = MIX_TM
    a_tile = jnp.minimum(m, MIX_NT - 1)
    is_s = a_tile < MIX_NS
    b_is_s = (m - 1) < MIX_NS
    slot = m & 1

    @pl.when(m == 0)
    def _():
        prev_ref[...] = jnp.zeros_like(prev_ref)
        carry_ref[...] = jnp.zeros_like(carry_ref)
        ab_ref[...] = jnp.zeros_like(ab_ref)
        lg_ref[...] = jnp.zeros_like(lg_ref)

    y_a = jnp.dot(ab_ref[1 - slot, :, 0:D_A], wa_ref[...], preferred_element_type=F32)
    y_b = jnp.dot(ab_ref[1 - slot, :, D_A:D_A + D_B], wb_ref[...], preferred_element_type=F32)

    vg = vpre_ref[...]
    mu = jnp.mean(vg, axis=-1, keepdims=True)
    vc = vg - mu
    v = vc * lax.rsqrt(jnp.mean(vc * vc, axis=-1, keepdims=True) + EPS) * lng_ref[...] + lnb_ref[...]
    vln_ref[...] = v

    vb = v.astype(BF16)
    s_parts = []
    for g in range(N_GROUPS_A):
        s_parts.append(jnp.dot(wsg_ref[0, g], vb[:, g * GW_A:(g + 1) * GW_A], preferred_element_type=F32))
    s = jnp.concatenate(s_parts, axis=1) + bsg_ref[0]
    u = zu_ref[...].astype(F32)
    a_in = (u * s).astype(BF16)

    ga = jnp.concatenate([ga0_ref[...], ga1_ref[...]], axis=1).astype(F32)
    gb = jnp.concatenate([gb0_ref[...], gb1_ref[...]], axis=1).astype(F32)
    mix = (ga * y_a + gb * y_b).astype(BF16)
    x = jnp.where(b_is_s, xs_ref[...], xp_ref[...])
    x1 = x + jnp.dot(mix, wo_ref[...], preferred_element_type=F32)
    x1_ref[...] = x1

    bg = zb_ref[...].astype(F32)
    cg = zc_ref[...].astype(F32)
    xin = zx_ref[...].astype(F32)
    cx = cg * xin
    cxs_ref[...] = cx
    tail_ref[0] = cx[tm - 8:tm]

    row = lax.broadcasted_iota(I32, (tm, D_B), 0)
    seq_start = ((a_tile - MIX_NS) % MIX_SEQ_TILES) == 0
    prev = jnp.where(seq_start, 0.0, prev_ref[...])
    row8 = lax.broadcasted_iota(I32, (8, D_B), 0)
    top = jnp.where(row8 < CONV_W - 1, pltpu.roll(prev, CONV_W - 1, 0), 0.0)
    ext_p = jnp.concatenate([top, jnp.zeros((tm - 8, D_B), F32)], axis=0)
    ext = jnp.where(is_s, ext_ref[...], ext_p)
    t_in = jnp.where(is_s, row & (DEC_SEQ - 1), row)
    s1 = jnp.where(t_in < 1, pltpu.roll(ext, tm - 1, 0), pltpu.roll(cx, 1, 0))
    s2 = jnp.where(t_in < 2, ext, pltpu.roll(cx, 2, 0))
    prev_ref[...] = cx[tm - 8:tm]
    conv = cb_ref[...] + s2 * cw_ref[0:1, :] + s1 * cw_ref[1:2, :] + cx * cw_ref[2:3, :]
    b_in = (bg * conv).astype(BF16)

    xn = _rms(x1, gmoe_ref[...])
    xn_ref[...] = xn
    hi = xn.astype(BF16)
    lo = (xn - hi.astype(F32)).astype(BF16)
    logits = (jnp.dot(hi, wrh_ref[...], preferred_element_type=F32)
              + jnp.dot(lo, wrh_ref[...], preferred_element_type=F32)
              + jnp.dot(hi, wrl_ref[...], preferred_element_type=F32)) + br_ref[...]

    lane = lax.broadcasted_iota(I32, (tm, LANES), 1)
    lane_f = lane.astype(F32)
    vals, idxs, hots = [], [], []
    work = lg_ref[1 - slot]
    for _ in range(TOP_K):
        mx = jnp.max(work, axis=-1, keepdims=True)
        idx = jnp.min(jnp.where(work == mx, lane_f, float(LANES)), axis=-1, keepdims=True)
        hot = lane_f == idx
        work = jnp.where(hot, -jnp.inf, work)
        vals.append(mx)
        idxs.append(idx)
        hots.append(hot)
    exps = [jnp.exp(vk - vals[0]) for vk in vals]
    den = exps[0] + exps[1] + exps[2] + exps[3]

    chosen = jnp.zeros((tm, LANES), F32)
    real = jnp.where(m > 1, 1.0, 0.0)
    for hot in hots:
        chosen = chosen + jnp.where(hot, real, 0.0)
    r_i = lax.broadcasted_iota(I32, (tm, tm), 0)
    c_i = lax.broadcasted_iota(I32, (tm, tm), 1)
    tri = jnp.where(c_i < r_i, 1.0, 0.0).astype(BF16)
    before = jnp.dot(tri, chosen.astype(BF16), preferred_element_type=F32) + carry_ref[0:1, :]
    total = carry_ref[0:1, :] + jnp.sum(chosen, axis=0, keepdims=True)
    carry_ref[...] = jnp.broadcast_to(total, carry_ref.shape)
    cnt_ref[...] = jnp.broadcast_to(total, cnt_ref.shape)

    ri = jnp.zeros((tm, LANES), F32)
    rg = jnp.zeros((tm, LANES), F32)
    for k in range(TOP_K):
        rank = jnp.sum(jnp.where(hots[k], before, 0.0), axis=-1, keepdims=True)
        ri = ri + jnp.where(lane == k, idxs[k], 0.0) + jnp.where(lane == TOP_K + k, rank, 0.0)
        rg = rg + jnp.where(lane == k, exps[k] / den, 0.0)
    ri_ref[...] = jnp.transpose(ri)[0:2 * TOP_K, :].astype(I32)
    rg_ref[...] = rg

    ab_ref[slot, :, 0:D_A] = a_in
    ab_ref[slot, :, D_A:D_A + D_B] = b_in
    lg_ref[slot] = logits


def _mixer_call(xs, xp, z, vpre, ext, wsg, bsg, ln_g, ln_b, conv_w, conv_b, wa, wb, wo, g_moe,
                wr_hi, wr_lo, b_r):
    tm = MIX_TM
    ns = MIX_NS
    nt = MIX_NT
    assert D_A == D_B and D_MODEL == 2 * D_A and z.shape[1] == 9 * D_A
    front = lambda m: jnp.minimum(m, nt - 1)
    back = lambda m: jnp.clip(m - 1, 0, nt - 1)
    tail = lambda m: jnp.clip(m - 2, 0, nt - 1)
    zcol = lambda tile_of, c: (lambda m: (tile_of(m), c))
    row = lambda m: (jnp.clip(m - 1, 0, nt), 0)
    trow = lambda m: (tail(m), 0)
    s_out = lambda m: (jnp.minimum(front(m), ns), 0)
    p_out = lambda m: (jnp.where(front(m) < ns, nt - ns, front(m) - ns), 0, 0)
    sel = lambda m: jnp.minimum(front(m) // ns, 1)
    return pl.pallas_call(
        _mixer_kernel,
        grid=(nt + 2,),
        in_specs=[
            pl.BlockSpec((tm, D_MODEL), lambda m: (jnp.minimum(back(m), ns - 1), 0)),
            pl.BlockSpec((tm, D_MODEL), lambda m: (jnp.maximum(back(m) - ns, 0), 0)),
            pl.BlockSpec((tm, D_A), zcol(front, 0)),
            pl.BlockSpec((tm, D_A), zcol(front, 2)),
            pl.BlockSpec((tm, D_A), zcol(front, 3)),
            pl.BlockSpec((tm, D_A), zcol(front, 4)),
            pl.BlockSpec((tm, D_A), zcol(back, 5)),
            pl.BlockSpec((tm, D_A), zcol(back, 6)),
            pl.BlockSpec((tm, D_A), zcol(back, 7)),
            pl.BlockSpec((tm, D_A), zcol(back, 8)),
            pl.BlockSpec((tm, D_A), lambda m: (front(m), 0)),
            pl.BlockSpec((tm, D_B), lambda m: (jnp.minimum(front(m), ns - 1), 0)),
            pl.BlockSpec((1, N_GROUPS_A, CHUNK, CHUNK), lambda m: (sel(m), 0, 0, 0)),
            pl.BlockSpec((1, CHUNK, D_A), lambda m: (sel(m), 0, 0)),
            _resident((1, D_A)),
            _resident((1, D_A)),
            _resident((CONV_W, D_B)),
            _resident((1, D_B)),
            _resident((D_A, D_MODEL)),
            _resident((D_B, D_MODEL)),
            _resident((D_MODEL, D_MODEL)),
            _resident((1, D_MODEL)),
            _resident((D_MODEL, LANES)),
            _resident((D_MODEL, LANES)),
            _resident((1, LANES)),
        ],
        out_specs=[
            pl.BlockSpec((tm, D_MODEL), row),
            pl.BlockSpec((tm, D_MODEL), row),
            pl.BlockSpec((2 * TOP_K, tm), lambda m: (0, tail(m))),
            pl.BlockSpec((tm, LANES), trow),
            pl.BlockSpec((8, LANES), lambda m: (0, 0)),
            pl.BlockSpec((tm, D_A), s_out),
            pl.BlockSpec((tm, D_B), s_out),
            pl.BlockSpec((1, 8, D_B), p_out),
        ],
        out_shape=[
            jax.ShapeDtypeStruct((T + tm, D_MODEL), F32),
            jax.ShapeDtypeStruct((T + tm, D_MODEL), F32),
            jax.ShapeDtypeStruct((2 * TOP_K, T), I32),
            jax.ShapeDtypeStruct((T, LANES), F32),
            jax.ShapeDtypeStruct((8, LANES), F32),
            jax.ShapeDtypeStruct((T_S + tm, D_A), F32),
            jax.ShapeDtypeStruct((T_S + tm, D_B), F32),
            jax.ShapeDtypeStruct((nt - ns + 1, 8, D_B), F32),
        ],
        scratch_shapes=[pltpu.VMEM((8, D_B), F32), pltpu.VMEM((8, LANES), F32),
                        pltpu.VMEM((2, tm, D_A + D_B), BF16), pltpu.VMEM((2, tm, LANES), F32)],
        compiler_params=_cparams(("arbitrary",), 52),
        name="mixer",
    )(xs, xp, z, z, z, z, z, z, z, z, vpre, ext, wsg, bsg, ln_g, ln_b, conv_w, conv_b, wa, wb, wo, g_moe,
      wr_hi, wr_lo, b_r)


MOE_CHUNK_SLOT0 = {8: 0, 4: 0, 2: 4, 1: 6}


def _for_units(n, unit_fn, chunk_begin=None):
    big = MOE_CHUNKS[0]

    def chunk(u0, count):
        slot0 = MOE_CHUNK_SLOT0[count]
        if chunk_begin is not None:
            chunk_begin(range(slot0, slot0 + count))
        for j in range(count):
            unit_fn(u0 + j, slot0 + j)

    def body(c, carry):
        chunk(c * big, big)
        return carry

    n_big = lax.shift_right_logical(n, big.bit_length() - 1)
    lax.fori_loop(0, n_big, body, 0)
    base = n_big * big
    for count in MOE_CHUNKS[1:]:
        @pl.when((n & count) != 0)
        def _():
            chunk(base, count)

        base = base + (n & count)


def _moe_kernel(tok_ref, tab_ref,
                xn_hbm, wgu_hbm, wd_hbm, bgu_hbm, bd_hbm,
                y_hbm,
                xraw_ref, xb_ref, act_ref, wst_ref, wbf_ref, bgu_ref, bd_ref,
                ystage_ref, zbuf_ref, gsem, ysem, zsem, wsem, bsem):
    i = pl.program_id(0)
    nblk = tab_ref[TAB_NBLK +i]
    row0 = tab_ref[TAB_ROW0 +i]
    valid = nblk > 0
    par = i & 1

    def gu_copies(e, tile, slot):
        col = pl.multiple_of(tile * MOE_TF, MOE_TF)
        return (pltpu.make_async_copy(wgu_hbm.at[e, :, pl.ds(col, MOE_TF)],
                                      wst_ref.at[slot, :, pl.ds(0, MOE_TF)], wsem.at[slot]),
                pltpu.make_async_copy(wgu_hbm.at[e, :, pl.ds(D_FF + col, MOE_TF)],
                                      wst_ref.at[slot, :, pl.ds(MOE_TF, MOE_TF)], wsem.at[slot]))

    def d_copy(e, tile, slot):
        col = pl.multiple_of(tile * MOE_TN, MOE_TN)
        return pltpu.make_async_copy(wd_hbm.at[e, :, pl.ds(col, MOE_TN)], wst_ref.at[slot], wsem.at[slot])

    def start_tile(e, t):
        slot = lax.rem(t, MOE_WSLOTS)

        @pl.when(t < MOE_P1)
        def _():
            for c in gu_copies(e, t, slot):
                c.start()

        @pl.when(t >= MOE_P1)
        def _():
            d_copy(e, t - MOE_P1, slot).start()

    def bias_copies(e, slot):
        return (pltpu.make_async_copy(bgu_hbm.at[e], bgu_ref.at[slot], bsem.at[slot]),
                pltpu.make_async_copy(bd_hbm.at[e], bd_ref.at[slot], bsem.at[slot]))

    def tail_copy(b):
        r = pl.multiple_of(b * MOE_RB, MOE_RB)
        return pltpu.make_async_copy(zbuf_ref, y_hbm.at[pl.ds(r, MOE_RB), :], zsem)

    def row_copy(tok, r):
        return pltpu.make_async_copy(xn_hbm.at[pl.ds(tok, 1), :], xraw_ref.at[pl.ds(r, 1), :], gsem)

    def unit_wait():
        return pltpu.make_async_copy(xn_hbm.at[pl.ds(0, MOE_RB), :], xraw_ref.at[pl.ds(0, MOE_RB), :], gsem)

    def gather_rows(item, lo_unit, hi_unit):
        base = tab_ref[TAB_ROW0 +item]

        def body(c, carry):
            for j in range(GATHER_UNROLL):
                r = c * GATHER_UNROLL + j
                row_copy(tok_ref[base + r], r).start()
            return carry

        per_unit = MOE_RB // GATHER_UNROLL
        lax.fori_loop(lo_unit * per_unit, hi_unit * per_unit, body, 0)

    def y_copy(slot, u, col):
        r = pl.multiple_of(row0 + u * MOE_RB, MOE_RB)
        return pltpu.make_async_copy(ystage_ref.at[slot], y_hbm.at[pl.ds(r, MOE_RB), pl.ds(col, MOE_TN)],
                                     ysem.at[slot])

    def dump_copy(slot):
        r = N_SLOTS + (slot // MOE_P2) * MOE_RB
        c = (slot % MOE_P2) * MOE_TN
        return pltpu.make_async_copy(ystage_ref.at[slot], y_hbm.at[pl.ds(r, MOE_RB), pl.ds(c, MOE_TN)],
                                     ysem.at[slot])

    @pl.when(i == 0)
    def _():
        for t in range(MOE_WAHEAD):
            for c in gu_copies(tab_ref[TAB_E +0], t, t):
                c.start()
        for c in bias_copies(tab_ref[TAB_E +0], 0):
            c.start()
        gather_rows(0, 0, tab_ref[TAB_NBLK +0])
        ystage_ref[...] = jnp.zeros_like(ystage_ref)
        for slot in range(MOE_YSLOTS):
            dump_copy(slot).start()
        zbuf_ref[...] = jnp.zeros_like(zbuf_ref)

        def fill(b, carry):
            tail_copy(b).start()
            return carry

        lax.fori_loop(tab_ref[TAB_USED], N_BLOCKS, fill, 0)

    def wait_rows(b, carry):
        unit_wait().wait()
        return carry

    lax.fori_loop(0, tab_ref[TAB_WAIT +i], wait_rows, 0)

    def conv(u, carry):
        rows = pl.ds(pl.multiple_of(u * MOE_RB, MOE_RB), MOE_RB)
        xb_ref[rows, :] = xraw_ref[rows, :].astype(BF16)
        return carry

    lax.fori_loop(0, nblk, conv, 0)

    e = tab_ref[TAB_E +i]
    e_next = tab_ref[TAB_E +i + 1]
    next_valid = tab_ref[TAB_NBLK +i + 1] > 0
    next_base = tab_ref[TAB_ROW0 +i + 1]

    @pl.when(valid)
    def _():
        for c in bias_copies(e, par):
            c.wait()

    def gate_up_step(s, carry):
        wslot = lax.rem(s, MOE_WSLOTS)
        for c in gu_copies(e, s, wslot):
            c.wait()
        start_tile(e, s + MOE_WAHEAD)
        wbf_ref[...] = wst_ref[wslot].astype(BF16)
        b_g = bgu_ref[par, pl.ds(s, 1), :]
        b_u = bgu_ref[par, pl.ds(MOE_P1 + s, 1), :]

        def unit(u, slot):
            r = pl.multiple_of(u * MOE_RB, MOE_RB)
            first = pl.multiple_of(r + s * MOE_G, MOE_G)
            for j in range(MOE_G):
                row_copy(tok_ref[next_base + first + j], first + j).start()
            gu = jnp.dot(xb_ref[pl.ds(r, MOE_RB), :], wbf_ref[...], preferred_element_type=F32)
            gate = jnp.minimum(gu[:, 0:MOE_TF] + b_g, SWIGLU_LIMIT)
            up = jnp.clip(gu[:, MOE_TF:2 * MOE_TF] + b_u, -SWIGLU_LIMIT, SWIGLU_LIMIT)
            act = (up + 1) * (gate * jax.nn.sigmoid(gate * SWIGLU_ALPHA))
            act_ref[s, pl.ds(r, MOE_RB), :] = act.astype(BF16)

        _for_units(nblk, unit)
        return carry

    def down_step(s, carry):
        t = MOE_P1 + s
        wslot = lax.rem(t, MOE_WSLOTS)
        d_copy(e, s, wslot).wait()

        @pl.when(t + MOE_WAHEAD < MOE_TILES)
        def _():
            start_tile(e, t + MOE_WAHEAD)

        @pl.when((t + MOE_WAHEAD >= MOE_TILES) & next_valid)
        def _():
            start_tile(e_next, t + MOE_WAHEAD - MOE_TILES)

        @pl.when((t + MOE_WAHEAD == MOE_TILES) & next_valid)
        def _():
            for c in bias_copies(e_next, 1 - par):
                c.start()

        wbf_ref[...] = wst_ref[wslot].astype(BF16)
        b_d = bd_ref[par, pl.ds(s, 1), :]
        col = pl.multiple_of(s * MOE_TN, MOE_TN)

        def free_slots(slots):
            for slot in slots:
                y_copy(slot, 0, col).wait()

        def unit(u, slot):
            r = pl.multiple_of(u * MOE_RB, MOE_RB)
            a = jnp.concatenate([act_ref[j, pl.ds(r, MOE_RB), :] for j in range(MOE_P1)], axis=1)
            ystage_ref[slot] = jnp.dot(a, wbf_ref[...], preferred_element_type=F32) + b_d
            y_copy(slot, u, col).start()

        _for_units(nblk, unit, free_slots)
        return carry

    @pl.when(valid)
    def _():
        lax.fori_loop(0, MOE_P1, gate_up_step, 0)
        gather_rows(i + 1, nblk, tab_ref[TAB_NBLK +i + 1])
        lax.fori_loop(0, MOE_P2, down_step, 0)

    @pl.when(i == MOE_NI - 1)
    def _():
        for slot in range(MOE_YSLOTS):
            dump_copy(slot).wait()

        lax.fori_loop(0, tab_ref[TAB_WAIT +MOE_NI], wait_rows, 0)

        def drain(b, carry):
            tail_copy(b).wait()
            return carry

        lax.fori_loop(tab_ref[TAB_USED], N_BLOCKS, drain, 0)


def _moe_call(slot_tok, tables, xn, w_gate_up, w_down, b_gate_up, b_down):
    any_spec = pl.BlockSpec(memory_space=pl.ANY)
    grid_spec = pltpu.PrefetchScalarGridSpec(
        num_scalar_prefetch=2,
        grid=(MOE_NI,),
        in_specs=[any_spec] * 5,
        out_specs=any_spec,
        scratch_shapes=[
            pltpu.VMEM((MOE_RMAX, D_MODEL), F32),
            pltpu.VMEM((MOE_RMAX, D_MODEL), BF16),
            pltpu.VMEM((MOE_P1, MOE_RMAX, MOE_TF), BF16),
            pltpu.VMEM((MOE_WSLOTS, D_MODEL, MOE_TN), F32),
            pltpu.VMEM((D_MODEL, MOE_TN), BF16),
            pltpu.VMEM((2, 2 * MOE_P1, MOE_TF), F32),
            pltpu.VMEM((2, MOE_P2, MOE_TN), F32),
            pltpu.VMEM((MOE_YSLOTS, MOE_RB, MOE_TN), F32),
            pltpu.VMEM((MOE_RB, D_MODEL), F32),
            pltpu.SemaphoreType.DMA(()),
            pltpu.SemaphoreType.DMA((MOE_YSLOTS,)),
            pltpu.SemaphoreType.DMA(()),
            pltpu.SemaphoreType.DMA((MOE_WSLOTS,)),
            pltpu.SemaphoreType.DMA((2,)),
        ],
    )
    spare_blocks = MOE_YSLOTS // MOE_P2
    return pl.pallas_call(
        _moe_kernel,
        grid_spec=grid_spec,
        out_shape=jax.ShapeDtypeStruct((N_SLOTS + spare_blocks * MOE_RB, D_MODEL), F32),
        compiler_params=_cparams(("arbitrary",), 56),
        name="moe",
    )(slot_tok, tables, xn, w_gate_up, w_down, b_gate_up, b_down)


def _combine_kernel(dest_ref, y_hbm, x1_ref, rg_ref, ps_ref, pp_ref, wple_ref, wpg_ref, gple_ref,
                    gfin_ref, ys_ref, yp_ref, gbuf_ref, gsem):
    m = pl.program_id(0)
    nm = pl.num_programs(0)
    tm = CMB_TM
    slot = m % 2

    def row_copy(tile, slot_, r, k):
        d = dest_ref[k * T + tile * tm + r]
        return pltpu.make_async_copy(y_hbm.at[pl.ds(d, 1), :], gbuf_ref.at[slot_, k, pl.ds(r, 1), :],
                                     gsem.at[slot_])

    def wait_tile(slot_):
        for k in range(TOP_K):
            pltpu.make_async_copy(y_hbm.at[pl.ds(0, tm), :], gbuf_ref.at[slot_, k], gsem.at[slot_]).wait()

    @pl.when(m == 0)
    def _():
        def body(c, carry):
            for j in range(GATHER_UNROLL // TOP_K):
                for k in range(TOP_K):
                    row_copy(0, 0, c * (GATHER_UNROLL // TOP_K) + j, k).start()
            return carry

        lax.fori_loop(0, tm // (GATHER_UNROLL // TOP_K), body, 0)

    wait_tile(slot)
    gates = rg_ref[...]
    moe = gates[:, 0:1] * gbuf_ref[slot, 0]
    for k in range(1, TOP_K):
        moe = moe + gates[:, k:k + 1] * gbuf_ref[slot, k]
    x2 = x1_ref[...] + moe
    is_s = m < CMB_NS
    p = jnp.where(is_s, ps_ref[...], pp_ref[...]).astype(BF16)
    hn = _rms(x2, gple_ref[...]).astype(BF16)
    nxt = jnp.minimum(m + 1, nm - 1)
    rows_per_chunk = tm // CMB_CHUNKS
    cw = D_MODEL // CMB_CHUNKS
    x3_parts = []
    for c in range(CMB_CHUNKS):
        cols = slice(c * cw, (c + 1) * cw)
        pe = jnp.dot(p, wple_ref[:, cols], preferred_element_type=F32)
        gate = jax.nn.sigmoid(jnp.dot(hn, wpg_ref[:, cols], preferred_element_type=F32))
        x3_parts.append(x2[:, cols] + pe * gate)
        for r in range(c * rows_per_chunk, (c + 1) * rows_per_chunk):
            for k in range(TOP_K):
                row_copy(nxt, 1 - slot, r, k).start()
    x3 = jnp.concatenate(x3_parts, axis=1)
    y = _rms(x3, gfin_ref[...])

    @pl.when(is_s)
    def _():
        ys_ref[...] = y

    @pl.when(jnp.logical_not(is_s))
    def _():
        yp_ref[...] = y

    @pl.when(m == nm - 1)
    def _():
        wait_tile(1 - slot)


def _combine_call(dest, y_sorted, x1, rg, ps, pp, wple, wpg, g_ple, g_final):
    tm = CMB_TM
    ns = CMB_NS
    s_idx = lambda m, d: (jnp.minimum(m, ns - 1), 0)
    p_idx = lambda m, d: (jnp.maximum(m - ns, 0), 0)
    row = lambda m, d: (m, 0)
    const2 = lambda m, d: (0, 0)
    grid_spec = pltpu.PrefetchScalarGridSpec(
        num_scalar_prefetch=1,
        grid=(T // tm,),
        in_specs=[
            pl.BlockSpec(memory_space=pl.ANY),
            pl.BlockSpec((tm, D_MODEL), row),
            pl.BlockSpec((tm, LANES), row),
            pl.BlockSpec((tm, PLE_DIM), s_idx),
            pl.BlockSpec((tm, PLE_DIM), p_idx),
            pl.BlockSpec((PLE_DIM, D_MODEL), const2),
            pl.BlockSpec((D_MODEL, D_MODEL), const2),
            pl.BlockSpec((1, D_MODEL), const2),
            pl.BlockSpec((1, D_MODEL), const2),
        ],
        out_specs=[
            pl.BlockSpec((tm, D_MODEL), s_idx),
            pl.BlockSpec((tm, D_MODEL), p_idx),
        ],
        scratch_shapes=[
            pltpu.VMEM((2, TOP_K, tm, D_MODEL), F32),
            pltpu.SemaphoreType.DMA((2,)),
        ],
    )
    return pl.pallas_call(
        _combine_kernel,
        grid_spec=grid_spec,
        out_shape=[
            jax.ShapeDtypeStruct((T_S, D_MODEL), F32),
            jax.ShapeDtypeStruct((T_P, D_MODEL), F32),
        ],
        compiler_params=_cparams(("arbitrary",), 56),
        name="combine",
    )(dest, y_sorted, x1, rg, ps, pp, wple, wpg, g_ple, g_final)


def _route_kernel(cnt_ref, rit_ref, dest_ref, tok_ref, tab_ref, dvm_ref, zvm_ref, gs_ref, sem):
    rb_shift = MOE_RB.bit_length() - 1

    def clear(j, carry):
        tab_ref[j] = 0
        return carry

    lax.fori_loop(0, TAB_SIZE, clear, 0)

    def expert(e, carry):
        acc, item, used = carry
        n = lax.shift_right_logical(cnt_ref[0, e].astype(I32) + (MOE_RB - 1), rb_shift)
        gs_ref[e] = acc

        def add_item(local, it):
            tab_ref[TAB_E + it] = e
            tab_ref[TAB_ROW0 + it] = acc + local * MOE_RMAX
            tab_ref[TAB_NBLK + it] = jnp.minimum(MOE_BMAX, n - local * MOE_BMAX)
            return it + 1

        item = lax.fori_loop(0, lax.div(n + (MOE_BMAX - 1), MOE_BMAX), add_item, item)
        return acc + n * MOE_RB, item, used + n

    _, n_items, used = lax.fori_loop(0, N_EXPERTS, expert, (jnp.int32(0), jnp.int32(0), jnp.int32(0)))
    tab_ref[TAB_USED] = used
    e_last = tab_ref[TAB_E + n_items - 1]

    def pad_item(it, carry):
        tab_ref[TAB_E + it] = e_last
        tab_ref[TAB_ROW0 + it] = 0
        tab_ref[TAB_NBLK + it] = 0
        return carry

    lax.fori_loop(n_items, MOE_NI + 1, pad_item, 0)

    def wait_units(it, prev):
        nb = tab_ref[TAB_NBLK + it]
        tab_ref[TAB_WAIT + it] = jnp.maximum(nb, prev)
        return nb

    lax.fori_loop(0, MOE_NI + 1, wait_units, jnp.int32(0))

    e_idx = rit_ref[0:TOP_K, :]
    d = rit_ref[TOP_K:2 * TOP_K, :]
    for e in range(N_EXPERTS):
        d = d + jnp.where(e_idx == e, gs_ref[e], 0)
    dvm_ref[0:TOP_K, :] = d
    zvm_ref[...] = jnp.zeros_like(zvm_ref)
    copies = [pltpu.make_async_copy(dvm_ref.at[k], dest_ref.at[pl.ds(k * T, T)], sem) for k in range(TOP_K)]
    copies.append(pltpu.make_async_copy(zvm_ref, tok_ref, sem))
    for c in copies:
        c.start()
    for c in copies:
        c.wait()

    for k in range(TOP_K):
        def scatter(c, carry):
            for j in range(GATHER_UNROLL):
                t = c * GATHER_UNROLL + j
                tok_ref[dest_ref[k * T + t]] = t
            return carry

        lax.fori_loop(0, T // GATHER_UNROLL, scatter, 0)


def _route_call(counts, rit):
    smem = pl.BlockSpec(memory_space=pltpu.SMEM)
    return pl.pallas_call(
        _route_kernel,
        in_specs=[smem, pl.BlockSpec(memory_space=pltpu.VMEM)],
        out_specs=[smem, smem, smem],
        out_shape=[
            jax.ShapeDtypeStruct((TOP_K * T,), I32),
            jax.ShapeDtypeStruct((N_TOK_TAB,), I32),
            jax.ShapeDtypeStruct((TAB_SIZE,), I32),
        ],
        scratch_shapes=[
            pltpu.VMEM((2 * TOP_K, T), I32),
            pltpu.VMEM((N_TOK_TAB,), I32),
            pltpu.SMEM((N_EXPERTS,), I32),
            pltpu.SemaphoreType.DMA(()),
        ],
        name="route",
    )(counts, rit)


def kernel(x_prompt, x_sample, state_conv, p_prompt, p_sample, g_mix, w_in, ln_v_g, ln_v_b, w_s, b_s,
           conv_w, conv_b, w_proj_a, w_proj_b, w_o, g_moe, w_router, b_router, w_gate_up, b_gate_up,
           w_down, b_down, g_ple, w_ple, w_ple_gate, g_final):
    assert g_mix.shape[0] == 1, "one layer"
    xs = x_sample.reshape(T_S, D_MODEL)
    xp = x_prompt.reshape(T_P, D_MODEL)

    tril = jnp.tril(jnp.ones((CHUNK, CHUNK), bool))
    w_prompt = jnp.where(tril[None], w_s[0], 0.0)
    small = jnp.where(tril[None, :DEC_SEQ, :DEC_SEQ], w_s[0, :, :DEC_SEQ, :DEC_SEQ], 0.0)
    reps = CHUNK // DEC_SEQ
    blockdiag = jnp.kron(jnp.eye(reps, dtype=F32), jnp.ones((DEC_SEQ, DEC_SEQ), F32))
    w_sample = jnp.tile(small, (1, reps, reps)) * blockdiag[None]
    wsg = jnp.stack([w_sample, w_prompt]).astype(BF16)
    bias_p = jnp.repeat(b_s[0].T, GW_A, axis=1)
    bias_s = jnp.tile(jnp.repeat(b_s[0, :, :DEC_SEQ].T, GW_A, axis=1), (reps, 1))
    bsg = jnp.stack([bias_s, bias_p])
    ext = jnp.pad(state_conv[0], ((0, 0), (0, DEC_SEQ - (CONV_W - 1)), (0, 0))).reshape(T_S, D_B)

    wr = jnp.pad(w_router[0], ((0, 0), (0, LANES - N_EXPERTS)))
    wr_hi = wr.astype(BF16)
    wr_lo = (wr - wr_hi.astype(F32)).astype(BF16)
    b_r = jnp.pad(b_router[0], (0, LANES - N_EXPERTS), constant_values=NEG_BIG).reshape(1, LANES)

    h = _norm_call(xs, xp, g_mix)
    z, vpre = _in_proj_call(h, w_in[0])
    x1, xn, route_i, route_g, counts, vln, cxs, tail = _mixer_call(
        xs, xp, z, vpre, ext, wsg, bsg, ln_v_g, ln_v_b, conv_w[0], conv_b,
        w_proj_a[0].astype(BF16), w_proj_b[0].astype(BF16), w_o[0].astype(BF16), g_moe,
        wr_hi, wr_lo, b_r)

    dest, slot_tok, tables = _route_call(counts, route_i)
    y_sorted = _moe_call(slot_tok, tables, xn, w_gate_up[0], w_down[0],
                         b_gate_up[0].reshape(N_EXPERTS, 2 * MOE_P1, MOE_TF),
                         b_down[0].reshape(N_EXPERTS, MOE_P2, MOE_TN))
    ys, yp = _combine_call(dest, y_sorted, x1, route_g,
                           p_sample[0].reshape(T_S, PLE_DIM), p_prompt[0].reshape(T_P, PLE_DIM),
                           w_ple[0].astype(BF16), w_ple_gate[0].astype(BF16), g_ple, g_final.reshape(1, D_MODEL))

    y_prompt = yp.reshape(BATCH, SEQ, D_MODEL)
    y_sample = ys.reshape(DEC_BATCH, DEC_SEQ, D_MODEL)
    last = tail[:MIX_NT - MIX_NS].reshape(BATCH, MIX_SEQ_TILES, 8, D_B)[:, -1, 8 - (CONV_W - 1):, :]
    state_conv_prompt = last[None]
    state_conv_sample = cxs[:T_S].reshape(DEC_BATCH, DEC_SEQ, D_B)[:, DEC_SEQ - (CONV_W - 1):, :][None]
    state_chunk_v_sample = vln[:T_S].reshape(DEC_BATCH, DEC_SEQ, D_A)[None]
    return (y_prompt, y_sample, state_conv_prompt, state_conv_sample, state_chunk_v_sample)
```

```python
import functools

import jax
import jax.numpy as jnp
from jax import lax
from jax.experimental import pallas as pl
from jax.experimental.pallas import tpu as pltpu

F32 = jnp.float32
BF16 = jnp.bfloat16
I32 = jnp.int32

D_MODEL = 2048
BATCH = 4
SEQ = 2048
DEC_BATCH = 128
DEC_SEQ = 8
CHUNK = 128
D_A = D_MODEL // 2
N_GROUPS_A = 8
GW_A = D_A // N_GROUPS_A
D_B = D_MODEL // 2
CONV_W = 3
N_EXPERTS = 32
TOP_K = 4
D_FF = D_MODEL
SWIGLU_LIMIT = 7.0
SWIGLU_ALPHA = 1.702
PLE_DIM = 256
EPS = 1e-6

T_S = DEC_BATCH * DEC_SEQ
T_P = BATCH * SEQ
T = T_S + T_P

LANES = 128
V7X_VMEM_BYTES = 64 * 1024 * 1024
MIB = 1024 * 1024

NORM_TM = 512
IN_TM = 1536
IN_TN = 512
IN_SUB = 256
MIX_TM = CHUNK
MIX_NS = T_S // MIX_TM
MIX_SEQ_TILES = SEQ // MIX_TM
MIX_NT = T // MIX_TM
MOE_RB = 128
MOE_BMAX = 12
MOE_RMAX = MOE_RB * MOE_BMAX
MOE_CHUNKS = (8, 4, 2, 1)
MOE_TF = 256
MOE_TN = 512
MOE_P1 = D_FF // MOE_TF
MOE_P2 = D_MODEL // MOE_TN
MOE_TILES = MOE_P1 + MOE_P2
MOE_WSLOTS = 3
MOE_WAHEAD = MOE_WSLOTS - 1
assert D_MODEL == D_FF and 2 * MOE_TF == MOE_TN and MOE_TILES % MOE_WSLOTS == 0
MOE_G = MOE_RB // MOE_P1
MOE_YSLOTS = 8
N_SLOTS = T * TOP_K + N_EXPERTS * MOE_RB
N_BLOCKS = N_SLOTS // MOE_RB
MOE_NI = (N_BLOCKS + N_EXPERTS * (MOE_BMAX - 1)) // MOE_BMAX
GATHER_UNROLL = 8
TAB_STRIDE = 64
TAB_E, TAB_ROW0, TAB_NBLK, TAB_WAIT, TAB_USED = 0, TAB_STRIDE, 2 * TAB_STRIDE, 3 * TAB_STRIDE, 4 * TAB_STRIDE
TAB_SIZE = 5 * TAB_STRIDE
assert MOE_NI + 1 <= TAB_STRIDE
N_TOK_TAB = -(-(N_SLOTS + MOE_RMAX) // 1024) * 1024
ROUTE_TB = 512
CMB_TM = 256
CMB_NS = T_S // CMB_TM
CMB_CHUNKS = 8
NEG_BIG = -1e30


def _rms(x, g):
    return x * lax.rsqrt(jnp.mean(x * x, axis=-1, keepdims=True) + EPS) * g


def _cparams(sem, vmem_mib):
    return pltpu.CompilerParams(dimension_semantics=sem, vmem_limit_bytes=vmem_mib * MIB)


def _resident(shape):
    zeros = (0,) * len(shape)
    return pl.BlockSpec(shape, lambda *_: zeros, pipeline_mode=pl.Buffered(1))


def _norm_kernel(xs_ref, xp_ref, g_ref, h_ref, *, ns):
    m = pl.program_id(0)
    x = jnp.where(m < ns, xs_ref[...], xp_ref[...])
    h_ref[...] = _rms(x, g_ref[...]).astype(BF16)


def _norm_call(xs, xp, g):
    ns = T_S // NORM_TM
    return pl.pallas_call(
        functools.partial(_norm_kernel, ns=ns),
        grid=(T // NORM_TM,),
        in_specs=[
            pl.BlockSpec((NORM_TM, D_MODEL), lambda m: (jnp.minimum(m, ns - 1), 0)),
            pl.BlockSpec((NORM_TM, D_MODEL), lambda m: (jnp.maximum(m - ns, 0), 0)),
            pl.BlockSpec((1, D_MODEL), lambda m: (0, 0)),
        ],
        out_specs=pl.BlockSpec((NORM_TM, D_MODEL), lambda m: (m, 0)),
        out_shape=jax.ShapeDtypeStruct((T, D_MODEL), BF16),
        compiler_params=_cparams(("arbitrary",), 32),
        name="norm",
    )(xs, xp, g)


IN_N_GELU = 2 * D_A // IN_TN
IN_N_V0 = D_A // IN_TN
IN_N_LIN = (2 * D_A + 3 * D_B) // IN_TN


def _in_proj_kernel(h_ref, w_ref, z_ref, vpre_ref, wb_ref):
    n = pl.program_id(1)
    wb_ref[...] = w_ref[...].astype(BF16)

    def blocks(epilogue):
        for b in range(IN_TM // IN_SUB):
            rows = pl.ds(b * IN_SUB, IN_SUB)
            epilogue(rows, jnp.dot(h_ref[rows, :], wb_ref[...], preferred_element_type=F32))

    @pl.when(n < IN_N_V0)
    def _():
        def ep(rows, acc):
            z_ref[rows, :] = jax.nn.gelu(acc, approximate=True).astype(BF16)

        blocks(ep)

    @pl.when((n >= IN_N_V0) & (n < IN_N_GELU))
    def _():
        def ep(rows, acc):
            g = jax.nn.gelu(acc, approximate=True)
            z_ref[rows, :] = g.astype(BF16)
            vpre_ref[rows, :] = g

        blocks(ep)

    @pl.when((n >= IN_N_GELU) & (n < IN_N_LIN))
    def _():
        def ep(rows, acc):
            z_ref[rows, :] = acc.astype(BF16)

        blocks(ep)

    @pl.when(n >= IN_N_LIN)
    def _():
        def ep(rows, acc):
            z_ref[rows, :] = jax.nn.sigmoid(acc).astype(BF16)

        blocks(ep)


def _in_proj_call(h, w_in):
    d_in = w_in.shape[1]
    return pl.pallas_call(
        _in_proj_kernel,
        grid=(T // IN_TM, d_in // IN_TN),
        in_specs=[
            pl.BlockSpec((IN_TM, D_MODEL), lambda m, n: (m, 0)),
            pl.BlockSpec((D_MODEL, IN_TN), lambda m, n: (0, n)),
        ],
        out_specs=[
            pl.BlockSpec((IN_TM, IN_TN), lambda m, n: (m, n)),
            pl.BlockSpec((IN_TM, IN_TN), lambda m, n: (m, jnp.clip(n - IN_N_V0, 0, IN_N_GELU - IN_N_V0 - 1))),
        ],
        out_shape=[
            jax.ShapeDtypeStruct((T, d_in), BF16),
            jax.ShapeDtypeStruct((T, D_A), F32),
        ],
        scratch_shapes=[pltpu.VMEM((D_MODEL, IN_TN), BF16)],
        compiler_params=_cparams(("arbitrary", "arbitrary"), 44),
        name="in_proj",
    )(h, w_in)


def _mixer_kernel(xs_ref, xp_ref, zu_ref, zb_ref, zc_ref, zx_ref, ga0_ref, ga1_ref, gb0_ref, gb1_ref,
                  vpre_ref, ext_ref, wsg_ref, bsg_ref, lng_ref, lnb_ref,
                  cw_ref, cb_ref, wa_ref, wb_ref, wo_ref, gmoe_ref, wrh_ref, wrl_ref, br_ref,
                  x1_ref, xn_ref, lgt_ref, vln_ref, cxs_ref, tail_ref,
                  prev_ref, ab_ref):
    m = pl.program_id(0)
    tm = MIX_TM
    a_tile = jnp.minimum(m, MIX_NT - 1)
    is_s = a_tile < MIX_NS
    b_is_s = (m - 1) < MIX_NS
    slot = m & 1

    @pl.when(m == 0)
    def _():
        prev_ref[...] = jnp.zeros_like(prev_ref)
        ab_ref[...] = jnp.zeros_like(ab_ref)

    y_a = jnp.dot(ab_ref[1 - slot, :, 0:D_A], wa_ref[...], preferred_element_type=F32)
    y_b = jnp.dot(ab_ref[1 - slot, :, D_A:D_A + D_B], wb_ref[...], preferred_element_type=F32)

    vg = vpre_ref[...]
    mu = jnp.mean(vg, axis=-1, keepdims=True)
    vc = vg - mu
    v = vc * lax.rsqrt(jnp.mean(vc * vc, axis=-1, keepdims=True) + EPS) * lng_ref[...] + lnb_ref[...]
    vln_ref[...] = v

    vb = v.astype(BF16)
    s_parts = []
    for g in range(N_GROUPS_A):
        s_parts.append(jnp.dot(wsg_ref[0, g], vb[:, g * GW_A:(g + 1) * GW_A], preferred_element_type=F32))
    s = jnp.concatenate(s_parts, axis=1) + bsg_ref[0]
    u = zu_ref[...].astype(F32)
    a_in = (u * s).astype(BF16)

    ga = jnp.concatenate([ga0_ref[...], ga1_ref[...]], axis=1).astype(F32)
    gb = jnp.concatenate([gb0_ref[...], gb1_ref[...]], axis=1).astype(F32)
    mix = (ga * y_a + gb * y_b).astype(BF16)
    x = jnp.where(b_is_s, xs_ref[...], xp_ref[...])
    x1 = x + jnp.dot(mix, wo_ref[...], preferred_element_type=F32)
    x1_ref[...] = x1

    bg = zb_ref[...].astype(F32)
    cg = zc_ref[...].astype(F32)
    xin = zx_ref[...].astype(F32)
    cx = cg * xin
    cxs_ref[...] = cx
    tail_ref[0] = cx[tm - 8:tm]

    row = lax.broadcasted_iota(I32, (tm, D_B), 0)
    seq_start = ((a_tile - MIX_NS) % MIX_SEQ_TILES) == 0
    prev = jnp.where(seq_start, 0.0, prev_ref[...])
    row8 = lax.broadcasted_iota(I32, (8, D_B), 0)
    top = jnp.where(row8 < CONV_W - 1, pltpu.roll(prev, CONV_W - 1, 0), 0.0)
    ext_p = jnp.concatenate([top, jnp.zeros((tm - 8, D_B), F32)], axis=0)
    ext = jnp.where(is_s, ext_ref[...], ext_p)
    t_in = jnp.where(is_s, row & (DEC_SEQ - 1), row)
    s1 = jnp.where(t_in < 1, pltpu.roll(ext, tm - 1, 0), pltpu.roll(cx, 1, 0))
    s2 = jnp.where(t_in < 2, ext, pltpu.roll(cx, 2, 0))
    prev_ref[...] = cx[tm - 8:tm]
    conv = cb_ref[...] + s2 * cw_ref[0:1, :] + s1 * cw_ref[1:2, :] + cx * cw_ref[2:3, :]
    b_in = (bg * conv).astype(BF16)

    xn = _rms(x1, gmoe_ref[...])
    xn_ref[...] = xn
    hi = xn.astype(BF16)
    lo = (xn - hi.astype(F32)).astype(BF16)
    logits = (jnp.dot(hi, wrh_ref[...], preferred_element_type=F32)
              + jnp.dot(lo, wrh_ref[...], preferred_element_type=F32)
              + jnp.dot(hi, wrl_ref[...], preferred_element_type=F32)) + br_ref[...]

    lgt_ref[...] = jnp.transpose(logits)[0:N_EXPERTS, :]

    ab_ref[slot, :, 0:D_A] = a_in
    ab_ref[slot, :, D_A:D_A + D_B] = b_in


def _mixer_call(xs, xp, z, vpre, ext, wsg, bsg, ln_g, ln_b, conv_w, conv_b, wa, wb, wo, g_moe,
                wr_hi, wr_lo, b_r):
    tm = MIX_TM
    ns = MIX_NS
    nt = MIX_NT
    assert D_A == D_B and D_MODEL == 2 * D_A and z.shape[1] == 9 * D_A
    front = lambda m: jnp.minimum(m, nt - 1)
    back = lambda m: jnp.maximum(m - 1, 0)
    zcol = lambda tile_of, c: (lambda m: (tile_of(m), c))
    row = lambda m: (back(m), 0)
    s_out = lambda m: (jnp.minimum(front(m), ns), 0)
    p_out = lambda m: (jnp.where(front(m) < ns, nt - ns, front(m) - ns), 0, 0)
    sel = lambda m: jnp.minimum(front(m) // ns, 1)
    return pl.pallas_call(
        _mixer_kernel,
        grid=(nt + 1,),
        in_specs=[
            pl.BlockSpec((tm, D_MODEL), lambda m: (jnp.minimum(back(m), ns - 1), 0)),
            pl.BlockSpec((tm, D_MODEL), lambda m: (jnp.maximum(back(m) - ns, 0), 0)),
            pl.BlockSpec((tm, D_A), zcol(front, 0)),
            pl.BlockSpec((tm, D_A), zcol(front, 2)),
            pl.BlockSpec((tm, D_A), zcol(front, 3)),
            pl.BlockSpec((tm, D_A), zcol(front, 4)),
            pl.BlockSpec((tm, D_A), zcol(back, 5)),
            pl.BlockSpec((tm, D_A), zcol(back, 6)),
            pl.BlockSpec((tm, D_A), zcol(back, 7)),
            pl.BlockSpec((tm, D_A), zcol(back, 8)),
            pl.BlockSpec((tm, D_A), lambda m: (front(m), 0)),
            pl.BlockSpec((tm, D_B), lambda m: (jnp.minimum(front(m), ns - 1), 0)),
            pl.BlockSpec((1, N_GROUPS_A, CHUNK, CHUNK), lambda m: (sel(m), 0, 0, 0)),
            pl.BlockSpec((1, CHUNK, D_A), lambda m: (sel(m), 0, 0)),
            _resident((1, D_A)),
            _resident((1, D_A)),
            _resident((CONV_W, D_B)),
            _resident((1, D_B)),
            _resident((D_A, D_MODEL)),
            _resident((D_B, D_MODEL)),
            _resident((D_MODEL, D_MODEL)),
            _resident((1, D_MODEL)),
            _resident((D_MODEL, LANES)),
            _resident((D_MODEL, LANES)),
            _resident((1, LANES)),
        ],
        out_specs=[
            pl.BlockSpec((tm, D_MODEL), row),
            pl.BlockSpec((tm, D_MODEL), row),
            pl.BlockSpec((N_EXPERTS, tm), lambda m: (0, back(m))),
            pl.BlockSpec((tm, D_A), s_out),
            pl.BlockSpec((tm, D_B), s_out),
            pl.BlockSpec((1, 8, D_B), p_out),
        ],
        out_shape=[
            jax.ShapeDtypeStruct((T, D_MODEL), F32),
            jax.ShapeDtypeStruct((T, D_MODEL), F32),
            jax.ShapeDtypeStruct((N_EXPERTS, T), F32),
            jax.ShapeDtypeStruct((T_S + tm, D_A), F32),
            jax.ShapeDtypeStruct((T_S + tm, D_B), F32),
            jax.ShapeDtypeStruct((nt - ns + 1, 8, D_B), F32),
        ],
        scratch_shapes=[pltpu.VMEM((8, D_B), F32), pltpu.VMEM((2, tm, D_A + D_B), BF16)],
        compiler_params=_cparams(("arbitrary",), 52),
        name="mixer",
    )(xs, xp, z, z, z, z, z, z, z, z, vpre, ext, wsg, bsg, ln_g, ln_b, conv_w, conv_b, wa, wb, wo, g_moe,
      wr_hi, wr_lo, b_r)


MOE_CHUNK_SLOT0 = {8: 0, 4: 0, 2: 4, 1: 6}


def _for_units(n, unit_fn, chunk_begin=None):
    big = MOE_CHUNKS[0]

    def chunk(u0, count):
        slot0 = MOE_CHUNK_SLOT0[count]
        if chunk_begin is not None:
            chunk_begin(range(slot0, slot0 + count))
        for j in range(count):
            unit_fn(u0 + j, slot0 + j)

    def body(c, carry):
        chunk(c * big, big)
        return carry

    n_big = lax.shift_right_logical(n, big.bit_length() - 1)
    lax.fori_loop(0, n_big, body, 0)
    base = n_big * big
    for count in MOE_CHUNKS[1:]:
        @pl.when((n & count) != 0)
        def _():
            chunk(base, count)

        base = base + (n & count)


def _moe_kernel(tok_ref, tab_ref,
                xn_hbm, wgu_hbm, wd_hbm, bgu_hbm, bd_hbm,
                y_hbm,
                xraw_ref, xb_ref, act_ref, wst_ref, wbf_ref, bgu_ref, bd_ref,
                ystage_ref, zbuf_ref, gsem, ysem, zsem, wsem, bsem):
    i = pl.program_id(0)
    nblk = tab_ref[TAB_NBLK +i]
    row0 = tab_ref[TAB_ROW0 +i]
    valid = nblk > 0
    par = i & 1

    def gu_copies(e, tile, slot):
        col = pl.multiple_of(tile * MOE_TF, MOE_TF)
        return (pltpu.make_async_copy(wgu_hbm.at[e, :, pl.ds(col, MOE_TF)],
                                      wst_ref.at[slot, :, pl.ds(0, MOE_TF)], wsem.at[slot]),
                pltpu.make_async_copy(wgu_hbm.at[e, :, pl.ds(D_FF + col, MOE_TF)],
                                      wst_ref.at[slot, :, pl.ds(MOE_TF, MOE_TF)], wsem.at[slot]))

    def d_copy(e, tile, slot):
        col = pl.multiple_of(tile * MOE_TN, MOE_TN)
        return pltpu.make_async_copy(wd_hbm.at[e, :, pl.ds(col, MOE_TN)], wst_ref.at[slot], wsem.at[slot])

    def start_tile(e, t):
        slot = lax.rem(t, MOE_WSLOTS)

        @pl.when(t < MOE_P1)
        def _():
            for c in gu_copies(e, t, slot):
                c.start()

        @pl.when(t >= MOE_P1)
        def _():
            d_copy(e, t - MOE_P1, slot).start()

    def bias_copies(e, slot):
        return (pltpu.make_async_copy(bgu_hbm.at[e], bgu_ref.at[slot], bsem.at[slot]),
                pltpu.make_async_copy(bd_hbm.at[e], bd_ref.at[slot], bsem.at[slot]))

    def tail_copy(b):
        r = pl.multiple_of(b * MOE_RB, MOE_RB)
        return pltpu.make_async_copy(zbuf_ref, y_hbm.at[pl.ds(r, MOE_RB), :], zsem)

    def row_copy(tok, r):
        return pltpu.make_async_copy(xn_hbm.at[pl.ds(tok, 1), :], xraw_ref.at[pl.ds(r, 1), :], gsem)

    def unit_wait():
        return pltpu.make_async_copy(xn_hbm.at[pl.ds(0, MOE_RB), :], xraw_ref.at[pl.ds(0, MOE_RB), :], gsem)

    def gather_rows(item, lo_unit, hi_unit):
        base = tab_ref[TAB_ROW0 +item]

        def body(c, carry):
            for j in range(GATHER_UNROLL):
                r = c * GATHER_UNROLL + j
                row_copy(tok_ref[base + r], r).start()
            return carry

        per_unit = MOE_RB // GATHER_UNROLL
        lax.fori_loop(lo_unit * per_unit, hi_unit * per_unit, body, 0)

    def y_copy(slot, u, col):
        r = pl.multiple_of(row0 + u * MOE_RB, MOE_RB)
        return pltpu.make_async_copy(ystage_ref.at[slot], y_hbm.at[pl.ds(r, MOE_RB), pl.ds(col, MOE_TN)],
                                     ysem.at[slot])

    def dump_copy(slot):
        r = N_SLOTS + (slot // MOE_P2) * MOE_RB
        c = (slot % MOE_P2) * MOE_TN
        return pltpu.make_async_copy(ystage_ref.at[slot], y_hbm.at[pl.ds(r, MOE_RB), pl.ds(c, MOE_TN)],
                                     ysem.at[slot])

    @pl.when(i == 0)
    def _():
        for t in range(MOE_WAHEAD):
            for c in gu_copies(tab_ref[TAB_E +0], t, t):
                c.start()
        for c in bias_copies(tab_ref[TAB_E +0], 0):
            c.start()
        gather_rows(0, 0, tab_ref[TAB_NBLK +0])
        ystage_ref[...] = jnp.zeros_like(ystage_ref)
        for slot in range(MOE_YSLOTS):
            dump_copy(slot).start()
        zbuf_ref[...] = jnp.zeros_like(zbuf_ref)

        def fill(b, carry):
            tail_copy(b).start()
            return carry

        lax.fori_loop(tab_ref[TAB_USED], N_BLOCKS, fill, 0)

    def wait_rows(b, carry):
        unit_wait().wait()
        return carry

    lax.fori_loop(0, tab_ref[TAB_WAIT +i], wait_rows, 0)

    def conv(u, carry):
        rows = pl.ds(pl.multiple_of(u * MOE_RB, MOE_RB), MOE_RB)
        xb_ref[rows, :] = xraw_ref[rows, :].astype(BF16)
        return carry

    lax.fori_loop(0, nblk, conv, 0)

    e = tab_ref[TAB_E +i]
    e_next = tab_ref[TAB_E +i + 1]
    next_valid = tab_ref[TAB_NBLK +i + 1] > 0
    next_base = tab_ref[TAB_ROW0 +i + 1]

    @pl.when(valid)
    def _():
        for c in bias_copies(e, par):
            c.wait()

    def gate_up_step(s, carry):
        wslot = lax.rem(s, MOE_WSLOTS)
        for c in gu_copies(e, s, wslot):
            c.wait()
        start_tile(e, s + MOE_WAHEAD)
        wbf_ref[...] = wst_ref[wslot].astype(BF16)
        b_g = bgu_ref[par, pl.ds(s, 1), :]
        b_u = bgu_ref[par, pl.ds(MOE_P1 + s, 1), :]

        def unit(u, slot):
            r = pl.multiple_of(u * MOE_RB, MOE_RB)
            first = pl.multiple_of(r + s * MOE_G, MOE_G)
            for j in range(MOE_G):
                row_copy(tok_ref[next_base + first + j], first + j).start()
            gu = jnp.dot(xb_ref[pl.ds(r, MOE_RB), :], wbf_ref[...], preferred_element_type=F32)
            gate = jnp.minimum(gu[:, 0:MOE_TF] + b_g, SWIGLU_LIMIT)
            up = jnp.clip(gu[:, MOE_TF:2 * MOE_TF] + b_u, -SWIGLU_LIMIT, SWIGLU_LIMIT)
            act = (up + 1) * (gate * jax.nn.sigmoid(gate * SWIGLU_ALPHA))
            act_ref[s, pl.ds(r, MOE_RB), :] = act.astype(BF16)

        _for_units(nblk, unit)
        return carry

    def down_step(s, carry):
        t = MOE_P1 + s
        wslot = lax.rem(t, MOE_WSLOTS)
        d_copy(e, s, wslot).wait()

        @pl.when(t + MOE_WAHEAD < MOE_TILES)
        def _():
            start_tile(e, t + MOE_WAHEAD)

        @pl.when((t + MOE_WAHEAD >= MOE_TILES) & next_valid)
        def _():
            start_tile(e_next, t + MOE_WAHEAD - MOE_TILES)

        @pl.when((t + MOE_WAHEAD == MOE_TILES) & next_valid)
        def _():
            for c in bias_copies(e_next, 1 - par):
                c.start()

        wbf_ref[...] = wst_ref[wslot].astype(BF16)
        b_d = bd_ref[par, pl.ds(s, 1), :]
        col = pl.multiple_of(s * MOE_TN, MOE_TN)

        def free_slots(slots):
            for slot in slots:
                y_copy(slot, 0, col).wait()

        def unit(u, slot):
            r = pl.multiple_of(u * MOE_RB, MOE_RB)
            a = jnp.concatenate([act_ref[j, pl.ds(r, MOE_RB), :] for j in range(MOE_P1)], axis=1)
            ystage_ref[slot] = jnp.dot(a, wbf_ref[...], preferred_element_type=F32) + b_d
            y_copy(slot, u, col).start()

        _for_units(nblk, unit, free_slots)
        return carry

    @pl.when(valid)
    def _():
        lax.fori_loop(0, MOE_P1, gate_up_step, 0)
        gather_rows(i + 1, nblk, tab_ref[TAB_NBLK +i + 1])
        lax.fori_loop(0, MOE_P2, down_step, 0)

    @pl.when(i == MOE_NI - 1)
    def _():
        for slot in range(MOE_YSLOTS):
            dump_copy(slot).wait()

        lax.fori_loop(0, tab_ref[TAB_WAIT +MOE_NI], wait_rows, 0)

        def drain(b, carry):
            tail_copy(b).wait()
            return carry

        lax.fori_loop(tab_ref[TAB_USED], N_BLOCKS, drain, 0)


def _moe_call(slot_tok, tables, xn, w_gate_up, w_down, b_gate_up, b_down):
    any_spec = pl.BlockSpec(memory_space=pl.ANY)
    grid_spec = pltpu.PrefetchScalarGridSpec(
        num_scalar_prefetch=2,
        grid=(MOE_NI,),
        in_specs=[any_spec] * 5,
        out_specs=any_spec,
        scratch_shapes=[
            pltpu.VMEM((MOE_RMAX, D_MODEL), F32),
            pltpu.VMEM((MOE_RMAX, D_MODEL), BF16),
            pltpu.VMEM((MOE_P1, MOE_RMAX, MOE_TF), BF16),
            pltpu.VMEM((MOE_WSLOTS, D_MODEL, MOE_TN), F32),
            pltpu.VMEM((D_MODEL, MOE_TN), BF16),
            pltpu.VMEM((2, 2 * MOE_P1, MOE_TF), F32),
            pltpu.VMEM((2, MOE_P2, MOE_TN), F32),
            pltpu.VMEM((MOE_YSLOTS, MOE_RB, MOE_TN), F32),
            pltpu.VMEM((MOE_RB, D_MODEL), F32),
            pltpu.SemaphoreType.DMA(()),
            pltpu.SemaphoreType.DMA((MOE_YSLOTS,)),
            pltpu.SemaphoreType.DMA(()),
            pltpu.SemaphoreType.DMA((MOE_WSLOTS,)),
            pltpu.SemaphoreType.DMA((2,)),
        ],
    )
    spare_blocks = MOE_YSLOTS // MOE_P2
    return pl.pallas_call(
        _moe_kernel,
        grid_spec=grid_spec,
        out_shape=jax.ShapeDtypeStruct((N_SLOTS + spare_blocks * MOE_RB, D_MODEL), F32),
        compiler_params=_cparams(("arbitrary",), 56),
        name="moe",
    )(slot_tok, tables, xn, w_gate_up, w_down, b_gate_up, b_down)


def _combine_kernel(dest_ref, y_hbm, x1_ref, rg_ref, ps_ref, pp_ref, wple_ref, wpg_ref, gple_ref,
                    gfin_ref, ys_ref, yp_ref, gbuf_ref, gsem):
    m = pl.program_id(0)
    nm = pl.num_programs(0)
    tm = CMB_TM
    slot = m % 2

    def row_copy(tile, slot_, r, k):
        d = dest_ref[k * T + tile * tm + r]
        return pltpu.make_async_copy(y_hbm.at[pl.ds(d, 1), :], gbuf_ref.at[slot_, k, pl.ds(r, 1), :],
                                     gsem.at[slot_])

    def wait_tile(slot_):
        for k in range(TOP_K):
            pltpu.make_async_copy(y_hbm.at[pl.ds(0, tm), :], gbuf_ref.at[slot_, k], gsem.at[slot_]).wait()

    @pl.when(m == 0)
    def _():
        def body(c, carry):
            for j in range(GATHER_UNROLL // TOP_K):
                for k in range(TOP_K):
                    row_copy(0, 0, c * (GATHER_UNROLL // TOP_K) + j, k).start()
            return carry

        lax.fori_loop(0, tm // (GATHER_UNROLL // TOP_K), body, 0)

    wait_tile(slot)
    gates = rg_ref[...]
    moe = gates[:, 0:1] * gbuf_ref[slot, 0]
    for k in range(1, TOP_K):
        moe = moe + gates[:, k:k + 1] * gbuf_ref[slot, k]
    x2 = x1_ref[...] + moe
    is_s = m < CMB_NS
    p = jnp.where(is_s, ps_ref[...], pp_ref[...]).astype(BF16)
    hn = _rms(x2, gple_ref[...]).astype(BF16)
    nxt = jnp.minimum(m + 1, nm - 1)
    rows_per_chunk = tm // CMB_CHUNKS
    cw = D_MODEL // CMB_CHUNKS
    x3_parts = []
    for c in range(CMB_CHUNKS):
        cols = slice(c * cw, (c + 1) * cw)
        pe = jnp.dot(p, wple_ref[:, cols], preferred_element_type=F32)
        gate = jax.nn.sigmoid(jnp.dot(hn, wpg_ref[:, cols], preferred_element_type=F32))
        x3_parts.append(x2[:, cols] + pe * gate)
        for r in range(c * rows_per_chunk, (c + 1) * rows_per_chunk):
            for k in range(TOP_K):
                row_copy(nxt, 1 - slot, r, k).start()
    x3 = jnp.concatenate(x3_parts, axis=1)
    y = _rms(x3, gfin_ref[...])

    @pl.when(is_s)
    def _():
        ys_ref[...] = y

    @pl.when(jnp.logical_not(is_s))
    def _():
        yp_ref[...] = y

    @pl.when(m == nm - 1)
    def _():
        wait_tile(1 - slot)


def _combine_call(dest, y_sorted, x1, rg, ps, pp, wple, wpg, g_ple, g_final):
    tm = CMB_TM
    ns = CMB_NS
    s_idx = lambda m, d: (jnp.minimum(m, ns - 1), 0)
    p_idx = lambda m, d: (jnp.maximum(m - ns, 0), 0)
    row = lambda m, d: (m, 0)
    const2 = lambda m, d: (0, 0)
    grid_spec = pltpu.PrefetchScalarGridSpec(
        num_scalar_prefetch=1,
        grid=(T // tm,),
        in_specs=[
            pl.BlockSpec(memory_space=pl.ANY),
            pl.BlockSpec((tm, D_MODEL), row),
            pl.BlockSpec((tm, LANES), row),
            pl.BlockSpec((tm, PLE_DIM), s_idx),
            pl.BlockSpec((tm, PLE_DIM), p_idx),
            pl.BlockSpec((PLE_DIM, D_MODEL), const2),
            pl.BlockSpec((D_MODEL, D_MODEL), const2),
            pl.BlockSpec((1, D_MODEL), const2),
            pl.BlockSpec((1, D_MODEL), const2),
        ],
        out_specs=[
            pl.BlockSpec((tm, D_MODEL), s_idx),
            pl.BlockSpec((tm, D_MODEL), p_idx),
        ],
        scratch_shapes=[
            pltpu.VMEM((2, TOP_K, tm, D_MODEL), F32),
            pltpu.SemaphoreType.DMA((2,)),
        ],
    )
    return pl.pallas_call(
        _combine_kernel,
        grid_spec=grid_spec,
        out_shape=[
            jax.ShapeDtypeStruct((T_S, D_MODEL), F32),
            jax.ShapeDtypeStruct((T_P, D_MODEL), F32),
        ],
        compiler_params=_cparams(("arbitrary",), 56),
        name="combine",
    )(dest, y_sorted, x1, rg, ps, pp, wple, wpg, g_ple, g_final)


def _route_kernel(lgt_ref, dest_ref, tok_ref, tab_ref, rg_ref,
                  rit_ref, dvm_ref, zvm_ref, carry_ref, cnt_ref, gs_ref, sem):
    rb_shift = MOE_RB.bit_length() - 1
    tb = ROUTE_TB
    sub = tb // LANES

    a_i = lax.broadcasted_iota(I32, (LANES, LANES), 0)
    b_i = lax.broadcasted_iota(I32, (LANES, LANES), 1)
    earlier = jnp.where(a_i < b_i, 1.0, 0.0).astype(BF16)
    carry_ref[...] = jnp.zeros_like(carry_ref)

    def tile(c, loop_carry):
        c0 = pl.multiple_of(c * tb, tb)
        work = lgt_ref[:, pl.ds(c0, tb)]
        row = lax.broadcasted_iota(I32, (N_EXPERTS, tb), 0).astype(F32)
        vals, idxs, hots = [], [], []
        for _ in range(TOP_K):
            mx = jnp.max(work, axis=0, keepdims=True)
            idx = jnp.min(jnp.where(work == mx, row, float(N_EXPERTS)), axis=0, keepdims=True)
            hot = row == idx
            work = jnp.where(hot, -jnp.inf, work)
            vals.append(mx)
            idxs.append(idx)
            hots.append(hot)
        exps = [jnp.exp(vk - vals[0]) for vk in vals]
        den = exps[0] + exps[1] + exps[2] + exps[3]
        chosen = jnp.zeros((N_EXPERTS, tb), F32)
        for hot in hots:
            chosen = chosen + jnp.where(hot, 1.0, 0.0)
        counts = carry_ref[...]
        before = []
        for j in range(sub):
            cj = chosen[:, j * LANES:(j + 1) * LANES]
            before.append(jnp.dot(cj.astype(BF16), earlier, preferred_element_type=F32) + counts)
            counts = counts + jnp.sum(cj, axis=1, keepdims=True)
        carry_ref[...] = counts
        before = jnp.concatenate(before, axis=1)
        for k in range(TOP_K):
            rank = jnp.sum(jnp.where(hots[k], before, 0.0), axis=0, keepdims=True)
            rit_ref[k:k + 1, pl.ds(c0, tb)] = idxs[k].astype(I32)
            rit_ref[TOP_K + k:TOP_K + k + 1, pl.ds(c0, tb)] = rank.astype(I32)
        gates = jnp.concatenate([ek / den for ek in exps], axis=0)
        for j in range(sub):
            g_tile = jnp.concatenate([gates[:, j * LANES:(j + 1) * LANES],
                                      jnp.zeros((LANES - TOP_K, LANES), F32)], axis=0)
            rg_ref[pl.ds(pl.multiple_of(c0 + j * LANES, LANES), LANES), :] = jnp.transpose(g_tile)
        return loop_carry

    lax.fori_loop(0, T // tb, tile, 0)
    cnt_copy = pltpu.make_async_copy(carry_ref, cnt_ref, sem)
    cnt_copy.start()
    cnt_copy.wait()

    def clear(j, carry):
        tab_ref[j] = 0
        return carry

    lax.fori_loop(0, TAB_SIZE, clear, 0)

    def expert(e, carry):
        acc, item, used = carry
        n = lax.shift_right_logical(cnt_ref[e, 0].astype(I32) + (MOE_RB - 1), rb_shift)
        gs_ref[e] = acc

        def add_item(local, it):
            tab_ref[TAB_E + it] = e
            tab_ref[TAB_ROW0 + it] = acc + local * MOE_RMAX
            tab_ref[TAB_NBLK + it] = jnp.minimum(MOE_BMAX, n - local * MOE_BMAX)
            return it + 1

        item = lax.fori_loop(0, lax.div(n + (MOE_BMAX - 1), MOE_BMAX), add_item, item)
        return acc + n * MOE_RB, item, used + n

    _, n_items, used = lax.fori_loop(0, N_EXPERTS, expert, (jnp.int32(0), jnp.int32(0), jnp.int32(0)))
    tab_ref[TAB_USED] = used
    e_last = tab_ref[TAB_E + n_items - 1]

    def pad_item(it, carry):
        tab_ref[TAB_E + it] = e_last
        tab_ref[TAB_ROW0 + it] = 0
        tab_ref[TAB_NBLK + it] = 0
        return carry

    lax.fori_loop(n_items, MOE_NI + 1, pad_item, 0)

    def wait_units(it, prev):
        nb = tab_ref[TAB_NBLK + it]
        tab_ref[TAB_WAIT + it] = jnp.maximum(nb, prev)
        return nb

    lax.fori_loop(0, MOE_NI + 1, wait_units, jnp.int32(0))

    e_idx = rit_ref[0:TOP_K, :]
    d = rit_ref[TOP_K:2 * TOP_K, :]
    for e in range(N_EXPERTS):
        d = d + jnp.where(e_idx == e, gs_ref[e], 0)
    dvm_ref[0:TOP_K, :] = d
    zvm_ref[...] = jnp.zeros_like(zvm_ref)
    copies = [pltpu.make_async_copy(dvm_ref.at[k], dest_ref.at[pl.ds(k * T, T)], sem) for k in range(TOP_K)]
    copies.append(pltpu.make_async_copy(zvm_ref, tok_ref, sem))
    for c in copies:
        c.start()
    for c in copies:
        c.wait()

    for k in range(TOP_K):
        def scatter(c, carry):
            for j in range(GATHER_UNROLL):
                t = c * GATHER_UNROLL + j
                tok_ref[dest_ref[k * T + t]] = t
            return carry

        lax.fori_loop(0, T // GATHER_UNROLL, scatter, 0)


def _route_call(logits_t):
    smem = pl.BlockSpec(memory_space=pltpu.SMEM)
    vmem = pl.BlockSpec(memory_space=pltpu.VMEM)
    return pl.pallas_call(
        _route_kernel,
        in_specs=[vmem],
        out_specs=[smem, smem, smem, vmem],
        out_shape=[
            jax.ShapeDtypeStruct((TOP_K * T,), I32),
            jax.ShapeDtypeStruct((N_TOK_TAB,), I32),
            jax.ShapeDtypeStruct((TAB_SIZE,), I32),
            jax.ShapeDtypeStruct((T, LANES), F32),
        ],
        scratch_shapes=[
            pltpu.VMEM((2 * TOP_K, T), I32),
            pltpu.VMEM((2 * TOP_K, T), I32),
            pltpu.VMEM((N_TOK_TAB,), I32),
            pltpu.VMEM((N_EXPERTS, LANES), F32),
            pltpu.SMEM((N_EXPERTS, LANES), F32),
            pltpu.SMEM((N_EXPERTS,), I32),
            pltpu.SemaphoreType.DMA(()),
        ],
        compiler_params=pltpu.CompilerParams(vmem_limit_bytes=32 * MIB),
        name="route",
    )(logits_t)


def kernel(x_prompt, x_sample, state_conv, p_prompt, p_sample, g_mix, w_in, ln_v_g, ln_v_b, w_s, b_s,
           conv_w, conv_b, w_proj_a, w_proj_b, w_o, g_moe, w_router, b_router, w_gate_up, b_gate_up,
           w_down, b_down, g_ple, w_ple, w_ple_gate, g_final):
    assert g_mix.shape[0] == 1, "one layer"
    xs = x_sample.reshape(T_S, D_MODEL)
    xp = x_prompt.reshape(T_P, D_MODEL)

    tril = jnp.tril(jnp.ones((CHUNK, CHUNK), bool))
    w_prompt = jnp.where(tril[None], w_s[0], 0.0)
    small = jnp.where(tril[None, :DEC_SEQ, :DEC_SEQ], w_s[0, :, :DEC_SEQ, :DEC_SEQ], 0.0)
    reps = CHUNK // DEC_SEQ
    blockdiag = jnp.kron(jnp.eye(reps, dtype=F32), jnp.ones((DEC_SEQ, DEC_SEQ), F32))
    w_sample = jnp.tile(small, (1, reps, reps)) * blockdiag[None]
    wsg = jnp.stack([w_sample, w_prompt]).astype(BF16)
    bias_p = jnp.repeat(b_s[0].T, GW_A, axis=1)
    bias_s = jnp.tile(jnp.repeat(b_s[0, :, :DEC_SEQ].T, GW_A, axis=1), (reps, 1))
    bsg = jnp.stack([bias_s, bias_p])
    ext = jnp.pad(state_conv[0], ((0, 0), (0, DEC_SEQ - (CONV_W - 1)), (0, 0))).reshape(T_S, D_B)

    wr = jnp.pad(w_router[0], ((0, 0), (0, LANES - N_EXPERTS)))
    wr_hi = wr.astype(BF16)
    wr_lo = (wr - wr_hi.astype(F32)).astype(BF16)
    b_r = jnp.pad(b_router[0], (0, LANES - N_EXPERTS), constant_values=NEG_BIG).reshape(1, LANES)

    h = _norm_call(xs, xp, g_mix)
    z, vpre = _in_proj_call(h, w_in[0])
    x1, xn, logits_t, vln, cxs, tail = _mixer_call(
        xs, xp, z, vpre, ext, wsg, bsg, ln_v_g, ln_v_b, conv_w[0], conv_b,
        w_proj_a[0].astype(BF16), w_proj_b[0].astype(BF16), w_o[0].astype(BF16), g_moe,
        wr_hi, wr_lo, b_r)

    dest, slot_tok, tables, route_g = _route_call(logits_t)
    y_sorted = _moe_call(slot_tok, tables, xn, w_gate_up[0], w_down[0],
                         b_gate_up[0].reshape(N_EXPERTS, 2 * MOE_P1, MOE_TF),
                         b_down[0].reshape(N_EXPERTS, MOE_P2, MOE_TN))
    ys, yp = _combine_call(dest, y_sorted, x1, route_g,
                           p_sample[0].reshape(T_S, PLE_DIM), p_prompt[0].reshape(T_P, PLE_DIM),
                           w_ple[0].astype(BF16), w_ple_gate[0].astype(BF16), g_ple, g_final.reshape(1, D_MODEL))

    y_prompt = yp.reshape(BATCH, SEQ, D_MODEL)
    y_sample = ys.reshape(DEC_BATCH, DEC_SEQ, D_MODEL)
    last = tail[:MIX_NT - MIX_NS].reshape(BATCH, MIX_SEQ_TILES, 8, D_B)[:, -1, 8 - (CONV_W - 1):, :]
    state_conv_prompt = last[None]
    state_conv_sample = cxs[:T_S].reshape(DEC_BATCH, DEC_SEQ, D_B)[:, DEC_SEQ - (CONV_W - 1):, :][None]
    state_chunk_v_sample = vln[:T_S].reshape(DEC_BATCH, DEC_SEQ, D_A)[None]
    return (y_prompt, y_sample, state_conv_prompt, state_conv_sample, state_chunk_v_sample)
```

```python
import functools

import jax
import jax.numpy as jnp
from jax import lax
from jax.experimental import pallas as pl
from jax.experimental.pallas import tpu as pltpu

F32 = jnp.float32
BF16 = jnp.bfloat16
I32 = jnp.int32

D_MODEL = 2048
BATCH = 4
SEQ = 2048
DEC_BATCH = 128
DEC_SEQ = 8
CHUNK = 128
D_A = D_MODEL // 2
N_GROUPS_A = 8
GW_A = D_A // N_GROUPS_A
D_B = D_MODEL // 2
CONV_W = 3
N_EXPERTS = 32
TOP_K = 4
D_FF = D_MODEL
SWIGLU_LIMIT = 7.0
SWIGLU_ALPHA = 1.702
PLE_DIM = 256
EPS = 1e-6

T_S = DEC_BATCH * DEC_SEQ
T_P = BATCH * SEQ
T = T_S + T_P

LANES = 128
V7X_VMEM_BYTES = 64 * 1024 * 1024
MIB = 1024 * 1024

NORM_TM = 512
IN_TM = 1024
IN_TN = 1024
IN_SUB = 256
MIX_TM = CHUNK
MIX_NS = T_S // MIX_TM
MIX_SEQ_TILES = SEQ // MIX_TM
MIX_NT = T // MIX_TM
MOE_RB = 128
MOE_BMAX = 12
MOE_RMAX = MOE_RB * MOE_BMAX
MOE_CHUNKS = (8, 4, 2, 1)
MOE_TF = 256
MOE_TN = 512
MOE_P1 = D_FF // MOE_TF
MOE_P2 = D_MODEL // MOE_TN
MOE_TILES = MOE_P1 + MOE_P2
MOE_WSLOTS = 3
MOE_WAHEAD = MOE_WSLOTS - 1
assert D_MODEL == D_FF and 2 * MOE_TF == MOE_TN and MOE_TILES % MOE_WSLOTS == 0
MOE_G = MOE_RB // MOE_P1
MOE_YSLOTS = 8
N_SLOTS = T * TOP_K + N_EXPERTS * MOE_RB
N_BLOCKS = N_SLOTS // MOE_RB
MOE_NI = (N_BLOCKS + N_EXPERTS * (MOE_BMAX - 1)) // MOE_BMAX
GATHER_UNROLL = 8
TAB_STRIDE = 64
TAB_E, TAB_ROW0, TAB_NBLK, TAB_WAIT, TAB_USED = 0, TAB_STRIDE, 2 * TAB_STRIDE, 3 * TAB_STRIDE, 4 * TAB_STRIDE
TAB_SIZE = 5 * TAB_STRIDE
assert MOE_NI + 1 <= TAB_STRIDE
N_TOK_TAB = -(-(N_SLOTS + MOE_RMAX) // 1024) * 1024
ROUTE_TB = 512
CMB_TM = 256
CMB_NS = T_S // CMB_TM
CMB_CHUNKS = 8
NEG_BIG = -1e30


def _rms(x, g):
    return x * lax.rsqrt(jnp.mean(x * x, axis=-1, keepdims=True) + EPS) * g


def _cparams(sem, vmem_mib):
    return pltpu.CompilerParams(dimension_semantics=sem, vmem_limit_bytes=vmem_mib * MIB)


def _resident(shape):
    zeros = (0,) * len(shape)
    return pl.BlockSpec(shape, lambda *_: zeros, pipeline_mode=pl.Buffered(1))


def _norm_kernel(xs_ref, xp_ref, g_ref, h_ref, *, ns):
    m = pl.program_id(0)
    x = jnp.where(m < ns, xs_ref[...], xp_ref[...])
    h_ref[...] = _rms(x, g_ref[...]).astype(BF16)


def _norm_call(xs, xp, g):
    ns = T_S // NORM_TM
    return pl.pallas_call(
        functools.partial(_norm_kernel, ns=ns),
        grid=(T // NORM_TM,),
        in_specs=[
            pl.BlockSpec((NORM_TM, D_MODEL), lambda m: (jnp.minimum(m, ns - 1), 0)),
            pl.BlockSpec((NORM_TM, D_MODEL), lambda m: (jnp.maximum(m - ns, 0), 0)),
            pl.BlockSpec((1, D_MODEL), lambda m: (0, 0)),
        ],
        out_specs=pl.BlockSpec((NORM_TM, D_MODEL), lambda m: (m, 0)),
        out_shape=jax.ShapeDtypeStruct((T, D_MODEL), BF16),
        compiler_params=_cparams(("arbitrary",), 32),
        name="norm",
    )(xs, xp, g)


IN_N_GELU = 2 * D_A // IN_TN
IN_N_V0 = D_A // IN_TN
IN_N_LIN = (2 * D_A + 3 * D_B) // IN_TN


def _in_proj_kernel(h_ref, w_ref, z_ref, vpre_ref, wb_ref):
    n = pl.program_id(0)

    @pl.when(pl.program_id(1) == 0)
    def _():
        wb_ref[...] = w_ref[...].astype(BF16)

    def blocks(epilogue):
        for b in range(IN_TM // IN_SUB):
            rows = pl.ds(b * IN_SUB, IN_SUB)
            epilogue(rows, jnp.dot(h_ref[rows, :], wb_ref[...], preferred_element_type=F32))

    @pl.when(n < IN_N_V0)
    def _():
        def ep(rows, acc):
            z_ref[rows, :] = jax.nn.gelu(acc, approximate=True).astype(BF16)

        blocks(ep)

    @pl.when((n >= IN_N_V0) & (n < IN_N_GELU))
    def _():
        def ep(rows, acc):
            g = jax.nn.gelu(acc, approximate=True)
            z_ref[rows, :] = g.astype(BF16)
            vpre_ref[rows, :] = g

        blocks(ep)

    @pl.when((n >= IN_N_GELU) & (n < IN_N_LIN))
    def _():
        def ep(rows, acc):
            z_ref[rows, :] = acc.astype(BF16)

        blocks(ep)

    @pl.when(n >= IN_N_LIN)
    def _():
        def ep(rows, acc):
            z_ref[rows, :] = jax.nn.sigmoid(acc).astype(BF16)

        blocks(ep)


def _in_proj_call(h, w_in):
    d_in = w_in.shape[1]
    n_m = T // IN_TM

    def vpre_map(n, m):
        row = jnp.where(n < IN_N_V0, 0, jnp.where(n < IN_N_GELU, m, n_m - 1))
        return (row, jnp.clip(n - IN_N_V0, 0, IN_N_GELU - IN_N_V0 - 1))

    return pl.pallas_call(
        _in_proj_kernel,
        grid=(d_in // IN_TN, n_m),
        in_specs=[
            pl.BlockSpec((IN_TM, D_MODEL), lambda n, m: (m, 0)),
            pl.BlockSpec((D_MODEL, IN_TN), lambda n, m: (0, n)),
        ],
        out_specs=[
            pl.BlockSpec((IN_TM, IN_TN), lambda n, m: (m, n)),
            pl.BlockSpec((IN_TM, IN_TN), vpre_map),
        ],
        out_shape=[
            jax.ShapeDtypeStruct((T, d_in), BF16),
            jax.ShapeDtypeStruct((T, D_A), F32),
        ],
        scratch_shapes=[pltpu.VMEM((D_MODEL, IN_TN), BF16)],
        compiler_params=_cparams(("arbitrary", "arbitrary"), 48),
        name="in_proj",
    )(h, w_in)


def _mixer_kernel(xs_ref, xp_ref, zu_ref, zb_ref, zc_ref, zx_ref, ga0_ref, ga1_ref, gb0_ref, gb1_ref,
                  vpre_ref, ext_ref, wsg_ref, bsg_ref, lng_ref, lnb_ref,
                  cw_ref, cb_ref, wa_ref, wb_ref, wo_ref, gmoe_ref, wrh_ref, wrl_ref, br_ref,
                  x1_ref, xn_ref, lgt_ref, vln_ref, cxs_ref, tail_ref,
                  prev_ref, ab_ref):
    m = pl.program_id(0)
    tm = MIX_TM
    a_tile = jnp.minimum(m, MIX_NT - 1)
    is_s = a_tile < MIX_NS
    b_is_s = (m - 1) < MIX_NS
    slot = m & 1

    @pl.when(m == 0)
    def _():
        prev_ref[...] = jnp.zeros_like(prev_ref)
        ab_ref[...] = jnp.zeros_like(ab_ref)

    y_a = jnp.dot(ab_ref[1 - slot, :, 0:D_A], wa_ref[...], preferred_element_type=F32)
    y_b = jnp.dot(ab_ref[1 - slot, :, D_A:D_A + D_B], wb_ref[...], preferred_element_type=F32)

    vg = vpre_ref[...]
    mu = jnp.mean(vg, axis=-1, keepdims=True)
    vc = vg - mu
    v = vc * lax.rsqrt(jnp.mean(vc * vc, axis=-1, keepdims=True) + EPS) * lng_ref[...] + lnb_ref[...]
    vln_ref[...] = v

    vb = v.astype(BF16)
    s_parts = []
    for g in range(N_GROUPS_A):
        s_parts.append(jnp.dot(wsg_ref[0, g], vb[:, g * GW_A:(g + 1) * GW_A], preferred_element_type=F32))
    s = jnp.concatenate(s_parts, axis=1) + bsg_ref[0]
    u = zu_ref[...].astype(F32)
    a_in = (u * s).astype(BF16)

    ga = jnp.concatenate([ga0_ref[...], ga1_ref[...]], axis=1).astype(F32)
    gb = jnp.concatenate([gb0_ref[...], gb1_ref[...]], axis=1).astype(F32)
    mix = (ga * y_a + gb * y_b).astype(BF16)
    x = jnp.where(b_is_s, xs_ref[...], xp_ref[...])
    x1 = x + jnp.dot(mix, wo_ref[...], preferred_element_type=F32)
    x1_ref[...] = x1

    bg = zb_ref[...].astype(F32)
    cg = zc_ref[...].astype(F32)
    xin = zx_ref[...].astype(F32)
    cx = cg * xin
    cxs_ref[...] = cx
    tail_ref[0] = cx[tm - 8:tm]

    row = lax.broadcasted_iota(I32, (tm, D_B), 0)
    seq_start = ((a_tile - MIX_NS) % MIX_SEQ_TILES) == 0
    prev = jnp.where(seq_start, 0.0, prev_ref[...])
    row8 = lax.broadcasted_iota(I32, (8, D_B), 0)
    top = jnp.where(row8 < CONV_W - 1, pltpu.roll(prev, CONV_W - 1, 0), 0.0)
    ext_p = jnp.concatenate([top, jnp.zeros((tm - 8, D_B), F32)], axis=0)
    ext = jnp.where(is_s, ext_ref[...], ext_p)
    t_in = jnp.where(is_s, row & (DEC_SEQ - 1), row)
    s1 = jnp.where(t_in < 1, pltpu.roll(ext, tm - 1, 0), pltpu.roll(cx, 1, 0))
    s2 = jnp.where(t_in < 2, ext, pltpu.roll(cx, 2, 0))
    prev_ref[...] = cx[tm - 8:tm]
    conv = cb_ref[...] + s2 * cw_ref[0:1, :] + s1 * cw_ref[1:2, :] + cx * cw_ref[2:3, :]
    b_in = (bg * conv).astype(BF16)

    xn = _rms(x1, gmoe_ref[...])
    xn_ref[...] = xn
    hi = xn.astype(BF16)
    lo = (xn - hi.astype(F32)).astype(BF16)
    logits = (jnp.dot(hi, wrh_ref[...], preferred_element_type=F32)
              + jnp.dot(lo, wrh_ref[...], preferred_element_type=F32)
              + jnp.dot(hi, wrl_ref[...], preferred_element_type=F32)) + br_ref[...]

    lgt_ref[...] = jnp.transpose(logits)[0:N_EXPERTS, :]

    ab_ref[slot, :, 0:D_A] = a_in
    ab_ref[slot, :, D_A:D_A + D_B] = b_in


def _mixer_call(xs, xp, z, vpre, ext, wsg, bsg, ln_g, ln_b, conv_w, conv_b, wa, wb, wo, g_moe,
                wr_hi, wr_lo, b_r):
    tm = MIX_TM
    ns = MIX_NS
    nt = MIX_NT
    assert D_A == D_B and D_MODEL == 2 * D_A and z.shape[1] == 9 * D_A
    front = lambda m: jnp.minimum(m, nt - 1)
    back = lambda m: jnp.maximum(m - 1, 0)
    zcol = lambda tile_of, c: (lambda m: (tile_of(m), c))
    row = lambda m: (back(m), 0)
    s_out = lambda m: (jnp.minimum(front(m), ns), 0)
    p_out = lambda m: (jnp.where(front(m) < ns, nt - ns, front(m) - ns), 0, 0)
    sel = lambda m: jnp.minimum(front(m) // ns, 1)
    return pl.pallas_call(
        _mixer_kernel,
        grid=(nt + 1,),
        in_specs=[
            pl.BlockSpec((tm, D_MODEL), lambda m: (jnp.minimum(back(m), ns - 1), 0)),
            pl.BlockSpec((tm, D_MODEL), lambda m: (jnp.maximum(back(m) - ns, 0), 0)),
            pl.BlockSpec((tm, D_A), zcol(front, 0)),
            pl.BlockSpec((tm, D_A), zcol(front, 2)),
            pl.BlockSpec((tm, D_A), zcol(front, 3)),
            pl.BlockSpec((tm, D_A), zcol(front, 4)),
            pl.BlockSpec((tm, D_A), zcol(back, 5)),
            pl.BlockSpec((tm, D_A), zcol(back, 6)),
            pl.BlockSpec((tm, D_A), zcol(back, 7)),
            pl.BlockSpec((tm, D_A), zcol(back, 8)),
            pl.BlockSpec((tm, D_A), lambda m: (front(m), 0)),
            pl.BlockSpec((tm, D_B), lambda m: (jnp.minimum(front(m), ns - 1), 0)),
            pl.BlockSpec((1, N_GROUPS_A, CHUNK, CHUNK), lambda m: (sel(m), 0, 0, 0)),
            pl.BlockSpec((1, CHUNK, D_A), lambda m: (sel(m), 0, 0)),
            _resident((1, D_A)),
            _resident((1, D_A)),
            _resident((CONV_W, D_B)),
            _resident((1, D_B)),
            _resident((D_A, D_MODEL)),
            _resident((D_B, D_MODEL)),
            _resident((D_MODEL, D_MODEL)),
            _resident((1, D_MODEL)),
            _resident((D_MODEL, LANES)),
            _resident((D_MODEL, LANES)),
            _resident((1, LANES)),
        ],
        out_specs=[
            pl.BlockSpec((tm, D_MODEL), row),
            pl.BlockSpec((tm, D_MODEL), row),
            pl.BlockSpec((N_EXPERTS, tm), lambda m: (0, back(m))),
            pl.BlockSpec((tm, D_A), s_out),
            pl.BlockSpec((tm, D_B), s_out),
            pl.BlockSpec((1, 8, D_B), p_out),
        ],
        out_shape=[
            jax.ShapeDtypeStruct((T, D_MODEL), F32),
            jax.ShapeDtypeStruct((T, D_MODEL), F32),
            jax.ShapeDtypeStruct((N_EXPERTS, T), F32),
            jax.ShapeDtypeStruct((T_S + tm, D_A), F32),
            jax.ShapeDtypeStruct((T_S + tm, D_B), F32),
            jax.ShapeDtypeStruct((nt - ns + 1, 8, D_B), F32),
        ],
        scratch_shapes=[pltpu.VMEM((8, D_B), F32), pltpu.VMEM((2, tm, D_A + D_B), BF16)],
        compiler_params=_cparams(("arbitrary",), 52),
        name="mixer",
    )(xs, xp, z, z, z, z, z, z, z, z, vpre, ext, wsg, bsg, ln_g, ln_b, conv_w, conv_b, wa, wb, wo, g_moe,
      wr_hi, wr_lo, b_r)


MOE_CHUNK_SLOT0 = {8: 0, 4: 0, 2: 4, 1: 6}


def _for_units(n, unit_fn, chunk_begin=None):
    big = MOE_CHUNKS[0]

    def chunk(u0, count):
        slot0 = MOE_CHUNK_SLOT0[count]
        if chunk_begin is not None:
            chunk_begin(range(slot0, slot0 + count))
        for j in range(count):
            unit_fn(u0 + j, slot0 + j)

    def body(c, carry):
        chunk(c * big, big)
        return carry

    n_big = lax.shift_right_logical(n, big.bit_length() - 1)
    lax.fori_loop(0, n_big, body, 0)
    base = n_big * big
    for count in MOE_CHUNKS[1:]:
        @pl.when((n & count) != 0)
        def _():
            chunk(base, count)

        base = base + (n & count)


def _moe_kernel(tok_ref, tab_ref,
                xn_hbm, wgu_hbm, wd_hbm, bgu_hbm, bd_hbm,
                y_hbm,
                xraw_ref, xb_ref, act_ref, wst_ref, wbf_ref, bgu_ref, bd_ref,
                ystage_ref, zbuf_ref, gsem, ysem, zsem, wsem, bsem):
    i = pl.program_id(0)
    nblk = tab_ref[TAB_NBLK +i]
    row0 = tab_ref[TAB_ROW0 +i]
    valid = nblk > 0
    par = i & 1

    def gu_copies(e, tile, slot):
        col = pl.multiple_of(tile * MOE_TF, MOE_TF)
        return (pltpu.make_async_copy(wgu_hbm.at[e, :, pl.ds(col, MOE_TF)],
                                      wst_ref.at[slot, :, pl.ds(0, MOE_TF)], wsem.at[slot]),
                pltpu.make_async_copy(wgu_hbm.at[e, :, pl.ds(D_FF + col, MOE_TF)],
                                      wst_ref.at[slot, :, pl.ds(MOE_TF, MOE_TF)], wsem.at[slot]))

    def d_copy(e, tile, slot):
        col = pl.multiple_of(tile * MOE_TN, MOE_TN)
        return pltpu.make_async_copy(wd_hbm.at[e, :, pl.ds(col, MOE_TN)], wst_ref.at[slot], wsem.at[slot])

    def start_tile(e, t):
        slot = lax.rem(t, MOE_WSLOTS)

        @pl.when(t < MOE_P1)
        def _():
            for c in gu_copies(e, t, slot):
                c.start()

        @pl.when(t >= MOE_P1)
        def _():
            d_copy(e, t - MOE_P1, slot).start()

    def bias_copies(e, slot):
        return (pltpu.make_async_copy(bgu_hbm.at[e], bgu_ref.at[slot], bsem.at[slot]),
                pltpu.make_async_copy(bd_hbm.at[e], bd_ref.at[slot], bsem.at[slot]))

    def tail_copy(b):
        r = pl.multiple_of(b * MOE_RB, MOE_RB)
        return pltpu.make_async_copy(zbuf_ref, y_hbm.at[pl.ds(r, MOE_RB), :], zsem)

    def row_copy(tok, r):
        return pltpu.make_async_copy(xn_hbm.at[pl.ds(tok, 1), :], xraw_ref.at[pl.ds(r, 1), :], gsem)

    def unit_wait():
        return pltpu.make_async_copy(xn_hbm.at[pl.ds(0, MOE_RB), :], xraw_ref.at[pl.ds(0, MOE_RB), :], gsem)

    def gather_rows(item, lo_unit, hi_unit):
        base = tab_ref[TAB_ROW0 +item]

        def body(c, carry):
            for j in range(GATHER_UNROLL):
                r = c * GATHER_UNROLL + j
                row_copy(tok_ref[base + r], r).start()
            return carry

        per_unit = MOE_RB // GATHER_UNROLL
        lax.fori_loop(lo_unit * per_unit, hi_unit * per_unit, body, 0)

    def y_copy(slot, u, col):
        r = pl.multiple_of(row0 + u * MOE_RB, MOE_RB)
        return pltpu.make_async_copy(ystage_ref.at[slot], y_hbm.at[pl.ds(r, MOE_RB), pl.ds(col, MOE_TN)],
                                     ysem.at[slot])

    def dump_copy(slot):
        r = N_SLOTS + (slot // MOE_P2) * MOE_RB
        c = (slot % MOE_P2) * MOE_TN
        return pltpu.make_async_copy(ystage_ref.at[slot], y_hbm.at[pl.ds(r, MOE_RB), pl.ds(c, MOE_TN)],
                                     ysem.at[slot])

    @pl.when(i == 0)
    def _():
        for t in range(MOE_WAHEAD):
            for c in gu_copies(tab_ref[TAB_E +0], t, t):
                c.start()
        for c in bias_copies(tab_ref[TAB_E +0], 0):
            c.start()
        gather_rows(0, 0, tab_ref[TAB_NBLK +0])
        ystage_ref[...] = jnp.zeros_like(ystage_ref)
        for slot in range(MOE_YSLOTS):
            dump_copy(slot).start()
        zbuf_ref[...] = jnp.zeros_like(zbuf_ref)

        def fill(b, carry):
            tail_copy(b).start()
            return carry

        lax.fori_loop(tab_ref[TAB_USED], N_BLOCKS, fill, 0)

    def wait_rows(b, carry):
        unit_wait().wait()
        return carry

    lax.fori_loop(0, tab_ref[TAB_WAIT +i], wait_rows, 0)

    def conv(u, carry):
        rows = pl.ds(pl.multiple_of(u * MOE_RB, MOE_RB), MOE_RB)
        xb_ref[rows, :] = xraw_ref[rows, :].astype(BF16)
        return carry

    lax.fori_loop(0, nblk, conv, 0)

    e = tab_ref[TAB_E +i]
    e_next = tab_ref[TAB_E +i + 1]
    next_valid = tab_ref[TAB_NBLK +i + 1] > 0
    next_base = tab_ref[TAB_ROW0 +i + 1]

    @pl.when(valid)
    def _():
        for c in bias_copies(e, par):
            c.wait()

    def gate_up_step(s, carry):
        wslot = lax.rem(s, MOE_WSLOTS)
        for c in gu_copies(e, s, wslot):
            c.wait()
        start_tile(e, s + MOE_WAHEAD)
        wbf_ref[...] = wst_ref[wslot].astype(BF16)
        b_g = bgu_ref[par, pl.ds(s, 1), :]
        b_u = bgu_ref[par, pl.ds(MOE_P1 + s, 1), :]

        def unit(u, slot):
            r = pl.multiple_of(u * MOE_RB, MOE_RB)
            first = pl.multiple_of(r + s * MOE_G, MOE_G)
            for j in range(MOE_G):
                row_copy(tok_ref[next_base + first + j], first + j).start()
            gu = jnp.dot(xb_ref[pl.ds(r, MOE_RB), :], wbf_ref[...], preferred_element_type=F32)
            gate = jnp.minimum(gu[:, 0:MOE_TF] + b_g, SWIGLU_LIMIT)
            up = jnp.clip(gu[:, MOE_TF:2 * MOE_TF] + b_u, -SWIGLU_LIMIT, SWIGLU_LIMIT)
            act = (up + 1) * (gate * jax.nn.sigmoid(gate * SWIGLU_ALPHA))
            act_ref[s, pl.ds(r, MOE_RB), :] = act.astype(BF16)

        _for_units(nblk, unit)
        return carry

    def down_step(s, carry):
        t = MOE_P1 + s
        wslot = lax.rem(t, MOE_WSLOTS)
        d_copy(e, s, wslot).wait()

        @pl.when(t + MOE_WAHEAD < MOE_TILES)
        def _():
            start_tile(e, t + MOE_WAHEAD)

        @pl.when((t + MOE_WAHEAD >= MOE_TILES) & next_valid)
        def _():
            start_tile(e_next, t + MOE_WAHEAD - MOE_TILES)

        @pl.when((t + MOE_WAHEAD == MOE_TILES) & next_valid)
        def _():
            for c in bias_copies(e_next, 1 - par):
                c.start()

        wbf_ref[...] = wst_ref[wslot].astype(BF16)
        b_d = bd_ref[par, pl.ds(s, 1), :]
        col = pl.multiple_of(s * MOE_TN, MOE_TN)

        def free_slots(slots):
            for slot in slots:
                y_copy(slot, 0, col).wait()

        def unit(u, slot):
            r = pl.multiple_of(u * MOE_RB, MOE_RB)
            a = jnp.concatenate([act_ref[j, pl.ds(r, MOE_RB), :] for j in range(MOE_P1)], axis=1)
            ystage_ref[slot] = jnp.dot(a, wbf_ref[...], preferred_element_type=F32) + b_d
            y_copy(slot, u, col).start()

        _for_units(nblk, unit, free_slots)
        return carry

    @pl.when(valid)
    def _():
        lax.fori_loop(0, MOE_P1, gate_up_step, 0)
        gather_rows(i + 1, nblk, tab_ref[TAB_NBLK +i + 1])
        lax.fori_loop(0, MOE_P2, down_step, 0)

    @pl.when(i == MOE_NI - 1)
    def _():
        for slot in range(MOE_YSLOTS):
            dump_copy(slot).wait()

        lax.fori_loop(0, tab_ref[TAB_WAIT +MOE_NI], wait_rows, 0)

        def drain(b, carry):
            tail_copy(b).wait()
            return carry

        lax.fori_loop(tab_ref[TAB_USED], N_BLOCKS, drain, 0)


def _moe_call(slot_tok, tables, xn, w_gate_up, w_down, b_gate_up, b_down):
    any_spec = pl.BlockSpec(memory_space=pl.ANY)
    grid_spec = pltpu.PrefetchScalarGridSpec(
        num_scalar_prefetch=2,
        grid=(MOE_NI,),
        in_specs=[any_spec] * 5,
        out_specs=any_spec,
        scratch_shapes=[
            pltpu.VMEM((MOE_RMAX, D_MODEL), F32),
            pltpu.VMEM((MOE_RMAX, D_MODEL), BF16),
            pltpu.VMEM((MOE_P1, MOE_RMAX, MOE_TF), BF16),
            pltpu.VMEM((MOE_WSLOTS, D_MODEL, MOE_TN), F32),
            pltpu.VMEM((D_MODEL, MOE_TN), BF16),
            pltpu.VMEM((2, 2 * MOE_P1, MOE_TF), F32),
            pltpu.VMEM((2, MOE_P2, MOE_TN), F32),
            pltpu.VMEM((MOE_YSLOTS, MOE_RB, MOE_TN), F32),
            pltpu.VMEM((MOE_RB, D_MODEL), F32),
            pltpu.SemaphoreType.DMA(()),
            pltpu.SemaphoreType.DMA((MOE_YSLOTS,)),
            pltpu.SemaphoreType.DMA(()),
            pltpu.SemaphoreType.DMA((MOE_WSLOTS,)),
            pltpu.SemaphoreType.DMA((2,)),
        ],
    )
    spare_blocks = MOE_YSLOTS // MOE_P2
    return pl.pallas_call(
        _moe_kernel,
        grid_spec=grid_spec,
        out_shape=jax.ShapeDtypeStruct((N_SLOTS + spare_blocks * MOE_RB, D_MODEL), F32),
        compiler_params=_cparams(("arbitrary",), 56),
        name="moe",
    )(slot_tok, tables, xn, w_gate_up, w_down, b_gate_up, b_down)


def _combine_kernel(dest_ref, y_hbm, x1_ref, rg_ref, ps_ref, pp_ref, wple_ref, wpg_ref, gple_ref,
                    gfin_ref, ys_ref, yp_ref, gbuf_ref, gsem):
    m = pl.program_id(0)
    nm = pl.num_programs(0)
    tm = CMB_TM
    slot = m % 2

    def row_copy(tile, slot_, r, k):
        d = dest_ref[k * T + tile * tm + r]
        return pltpu.make_async_copy(y_hbm.at[pl.ds(d, 1), :], gbuf_ref.at[slot_, k, pl.ds(r, 1), :],
                                     gsem.at[slot_])

    def wait_tile(slot_):
        for k in range(TOP_K):
            pltpu.make_async_copy(y_hbm.at[pl.ds(0, tm), :], gbuf_ref.at[slot_, k], gsem.at[slot_]).wait()

    @pl.when(m == 0)
    def _():
        def body(c, carry):
            for j in range(GATHER_UNROLL // TOP_K):
                for k in range(TOP_K):
                    row_copy(0, 0, c * (GATHER_UNROLL // TOP_K) + j, k).start()
            return carry

        lax.fori_loop(0, tm // (GATHER_UNROLL // TOP_K), body, 0)

    wait_tile(slot)
    gates = rg_ref[...]
    moe = gates[:, 0:1] * gbuf_ref[slot, 0]
    for k in range(1, TOP_K):
        moe = moe + gates[:, k:k + 1] * gbuf_ref[slot, k]
    x2 = x1_ref[...] + moe
    is_s = m < CMB_NS
    p = jnp.where(is_s, ps_ref[...], pp_ref[...]).astype(BF16)
    hn = _rms(x2, gple_ref[...]).astype(BF16)
    nxt = jnp.minimum(m + 1, nm - 1)
    rows_per_chunk = tm // CMB_CHUNKS
    cw = D_MODEL // CMB_CHUNKS
    x3_parts = []
    for c in range(CMB_CHUNKS):
        cols = slice(c * cw, (c + 1) * cw)
        pe = jnp.dot(p, wple_ref[:, cols], preferred_element_type=F32)
        gate = jax.nn.sigmoid(jnp.dot(hn, wpg_ref[:, cols], preferred_element_type=F32))
        x3_parts.append(x2[:, cols] + pe * gate)
        for r in range(c * rows_per_chunk, (c + 1) * rows_per_chunk):
            for k in range(TOP_K):
                row_copy(nxt, 1 - slot, r, k).start()
    x3 = jnp.concatenate(x3_parts, axis=1)
    y = _rms(x3, gfin_ref[...])

    @pl.when(is_s)
    def _():
        ys_ref[...] = y

    @pl.when(jnp.logical_not(is_s))
    def _():
        yp_ref[...] = y

    @pl.when(m == nm - 1)
    def _():
        wait_tile(1 - slot)


def _combine_call(dest, y_sorted, x1, rg, ps, pp, wple, wpg, g_ple, g_final):
    tm = CMB_TM
    ns = CMB_NS
    s_idx = lambda m, d: (jnp.minimum(m, ns - 1), 0)
    p_idx = lambda m, d: (jnp.maximum(m - ns, 0), 0)
    row = lambda m, d: (m, 0)
    const2 = lambda m, d: (0, 0)
    grid_spec = pltpu.PrefetchScalarGridSpec(
        num_scalar_prefetch=1,
        grid=(T // tm,),
        in_specs=[
            pl.BlockSpec(memory_space=pl.ANY),
            pl.BlockSpec((tm, D_MODEL), row),
            pl.BlockSpec((tm, LANES), row),
            pl.BlockSpec((tm, PLE_DIM), s_idx),
            pl.BlockSpec((tm, PLE_DIM), p_idx),
            pl.BlockSpec((PLE_DIM, D_MODEL), const2),
            pl.BlockSpec((D_MODEL, D_MODEL), const2),
            pl.BlockSpec((1, D_MODEL), const2),
            pl.BlockSpec((1, D_MODEL), const2),
        ],
        out_specs=[
            pl.BlockSpec((tm, D_MODEL), s_idx),
            pl.BlockSpec((tm, D_MODEL), p_idx),
        ],
        scratch_shapes=[
            pltpu.VMEM((2, TOP_K, tm, D_MODEL), F32),
            pltpu.SemaphoreType.DMA((2,)),
        ],
    )
    return pl.pallas_call(
        _combine_kernel,
        grid_spec=grid_spec,
        out_shape=[
            jax.ShapeDtypeStruct((T_S, D_MODEL), F32),
            jax.ShapeDtypeStruct((T_P, D_MODEL), F32),
        ],
        compiler_params=_cparams(("arbitrary",), 56),
        name="combine",
    )(dest, y_sorted, x1, rg, ps, pp, wple, wpg, g_ple, g_final)


def _route_kernel(lgt_ref, dest_ref, tok_ref, tab_ref, rg_ref,
                  rit_ref, dvm_ref, zvm_ref, carry_ref, cnt_ref, gs_ref, sem):
    rb_shift = MOE_RB.bit_length() - 1
    tb = ROUTE_TB
    sub = tb // LANES

    a_i = lax.broadcasted_iota(I32, (LANES, LANES), 0)
    b_i = lax.broadcasted_iota(I32, (LANES, LANES), 1)
    earlier = jnp.where(a_i < b_i, 1.0, 0.0).astype(BF16)
    carry_ref[...] = jnp.zeros_like(carry_ref)

    def tile(c, loop_carry):
        c0 = pl.multiple_of(c * tb, tb)
        work = lgt_ref[:, pl.ds(c0, tb)]
        row = lax.broadcasted_iota(I32, (N_EXPERTS, tb), 0).astype(F32)
        vals, idxs, hots = [], [], []
        for _ in range(TOP_K):
            mx = jnp.max(work, axis=0, keepdims=True)
            idx = jnp.min(jnp.where(work == mx, row, float(N_EXPERTS)), axis=0, keepdims=True)
            hot = row == idx
            work = jnp.where(hot, -jnp.inf, work)
            vals.append(mx)
            idxs.append(idx)
            hots.append(hot)
        exps = [jnp.exp(vk - vals[0]) for vk in vals]
        den = exps[0] + exps[1] + exps[2] + exps[3]
        chosen = jnp.zeros((N_EXPERTS, tb), F32)
        for hot in hots:
            chosen = chosen + jnp.where(hot, 1.0, 0.0)
        counts = carry_ref[...]
        before = []
        for j in range(sub):
            cj = chosen[:, j * LANES:(j + 1) * LANES]
            before.append(jnp.dot(cj.astype(BF16), earlier, preferred_element_type=F32) + counts)
            counts = counts + jnp.sum(cj, axis=1, keepdims=True)
        carry_ref[...] = counts
        before = jnp.concatenate(before, axis=1)
        for k in range(TOP_K):
            rank = jnp.sum(jnp.where(hots[k], before, 0.0), axis=0, keepdims=True)
            rit_ref[k:k + 1, pl.ds(c0, tb)] = idxs[k].astype(I32)
            rit_ref[TOP_K + k:TOP_K + k + 1, pl.ds(c0, tb)] = rank.astype(I32)
        gates = jnp.concatenate([ek / den for ek in exps], axis=0)
        for j in range(sub):
            g_tile = jnp.concatenate([gates[:, j * LANES:(j + 1) * LANES],
                                      jnp.zeros((LANES - TOP_K, LANES), F32)], axis=0)
            rg_ref[pl.ds(pl.multiple_of(c0 + j * LANES, LANES), LANES), :] = jnp.transpose(g_tile)
        return loop_carry

    lax.fori_loop(0, T // tb, tile, 0)
    cnt_copy = pltpu.make_async_copy(carry_ref, cnt_ref, sem)
    cnt_copy.start()
    cnt_copy.wait()

    def clear(j, carry):
        tab_ref[j] = 0
        return carry

    lax.fori_loop(0, TAB_SIZE, clear, 0)

    def expert(e, carry):
        acc, item, used = carry
        n = lax.shift_right_logical(cnt_ref[e, 0].astype(I32) + (MOE_RB - 1), rb_shift)
        gs_ref[e] = acc

        def add_item(local, it):
            tab_ref[TAB_E + it] = e
            tab_ref[TAB_ROW0 + it] = acc + local * MOE_RMAX
            tab_ref[TAB_NBLK + it] = jnp.minimum(MOE_BMAX, n - local * MOE_BMAX)
            return it + 1

        item = lax.fori_loop(0, lax.div(n + (MOE_BMAX - 1), MOE_BMAX), add_item, item)
        return acc + n * MOE_RB, item, used + n

    _, n_items, used = lax.fori_loop(0, N_EXPERTS, expert, (jnp.int32(0), jnp.int32(0), jnp.int32(0)))
    tab_ref[TAB_USED] = used
    e_last = tab_ref[TAB_E + n_items - 1]

    def pad_item(it, carry):
        tab_ref[TAB_E + it] = e_last
        tab_ref[TAB_ROW0 + it] = 0
        tab_ref[TAB_NBLK + it] = 0
        return carry

    lax.fori_loop(n_items, MOE_NI + 1, pad_item, 0)

    def wait_units(it, prev):
        nb = tab_ref[TAB_NBLK + it]
        tab_ref[TAB_WAIT + it] = jnp.maximum(nb, prev)
        return nb

    lax.fori_loop(0, MOE_NI + 1, wait_units, jnp.int32(0))

    e_idx = rit_ref[0:TOP_K, :]
    d = rit_ref[TOP_K:2 * TOP_K, :]
    for e in range(N_EXPERTS):
        d = d + jnp.where(e_idx == e, gs_ref[e], 0)
    dvm_ref[0:TOP_K, :] = d
    zvm_ref[...] = jnp.zeros_like(zvm_ref)
    copies = [pltpu.make_async_copy(dvm_ref.at[k], dest_ref.at[pl.ds(k * T, T)], sem) for k in range(TOP_K)]
    copies.append(pltpu.make_async_copy(zvm_ref, tok_ref, sem))
    for c in copies:
        c.start()
    for c in copies:
        c.wait()

    for k in range(TOP_K):
        def scatter(c, carry):
            for j in range(GATHER_UNROLL):
                t = c * GATHER_UNROLL + j
                tok_ref[dest_ref[k * T + t]] = t
            return carry

        lax.fori_loop(0, T // GATHER_UNROLL, scatter, 0)


def _route_call(logits_t):
    smem = pl.BlockSpec(memory_space=pltpu.SMEM)
    vmem = pl.BlockSpec(memory_space=pltpu.VMEM)
    return pl.pallas_call(
        _route_kernel,
        in_specs=[vmem],
        out_specs=[smem, smem, smem, vmem],
        out_shape=[
            jax.ShapeDtypeStruct((TOP_K * T,), I32),
            jax.ShapeDtypeStruct((N_TOK_TAB,), I32),
            jax.ShapeDtypeStruct((TAB_SIZE,), I32),
            jax.ShapeDtypeStruct((T, LANES), F32),
        ],
        scratch_shapes=[
            pltpu.VMEM((2 * TOP_K, T), I32),
            pltpu.VMEM((2 * TOP_K, T), I32),
            pltpu.VMEM((N_TOK_TAB,), I32),
            pltpu.VMEM((N_EXPERTS, LANES), F32),
            pltpu.SMEM((N_EXPERTS, LANES), F32),
            pltpu.SMEM((N_EXPERTS,), I32),
            pltpu.SemaphoreType.DMA(()),
        ],
        compiler_params=pltpu.CompilerParams(vmem_limit_bytes=32 * MIB),
        name="route",
    )(logits_t)


def kernel(x_prompt, x_sample, state_conv, p_prompt, p_sample, g_mix, w_in, ln_v_g, ln_v_b, w_s, b_s,
           conv_w, conv_b, w_proj_a, w_proj_b, w_o, g_moe, w_router, b_router, w_gate_up, b_gate_up,
           w_down, b_down, g_ple, w_ple, w_ple_gate, g_final):
    assert g_mix.shape[0] == 1, "one layer"
    xs = x_sample.reshape(T_S, D_MODEL)
    xp = x_prompt.reshape(T_P, D_MODEL)

    tril = jnp.tril(jnp.ones((CHUNK, CHUNK), bool))
    w_prompt = jnp.where(tril[None], w_s[0], 0.0)
    small = jnp.where(tril[None, :DEC_SEQ, :DEC_SEQ], w_s[0, :, :DEC_SEQ, :DEC_SEQ], 0.0)
    reps = CHUNK // DEC_SEQ
    blockdiag = jnp.kron(jnp.eye(reps, dtype=F32), jnp.ones((DEC_SEQ, DEC_SEQ), F32))
    w_sample = jnp.tile(small, (1, reps, reps)) * blockdiag[None]
    wsg = jnp.stack([w_sample, w_prompt]).astype(BF16)
    bias_p = jnp.repeat(b_s[0].T, GW_A, axis=1)
    bias_s = jnp.tile(jnp.repeat(b_s[0, :, :DEC_SEQ].T, GW_A, axis=1), (reps, 1))
    bsg = jnp.stack([bias_s, bias_p])
    ext = jnp.pad(state_conv[0], ((0, 0), (0, DEC_SEQ - (CONV_W - 1)), (0, 0))).reshape(T_S, D_B)

    wr = jnp.pad(w_router[0], ((0, 0), (0, LANES - N_EXPERTS)))
    wr_hi = wr.astype(BF16)
    wr_lo = (wr - wr_hi.astype(F32)).astype(BF16)
    b_r = jnp.pad(b_router[0], (0, LANES - N_EXPERTS), constant_values=NEG_BIG).reshape(1, LANES)

    h = _norm_call(xs, xp, g_mix)
    z, vpre = _in_proj_call(h, w_in[0])
    x1, xn, logits_t, vln, cxs, tail = _mixer_call(
        xs, xp, z, vpre, ext, wsg, bsg, ln_v_g, ln_v_b, conv_w[0], conv_b,
        w_proj_a[0].astype(BF16), w_proj_b[0].astype(BF16), w_o[0].astype(BF16), g_moe,
        wr_hi, wr_lo, b_r)

    dest, slot_tok, tables, route_g = _route_call(logits_t)
    y_sorted = _moe_call(slot_tok, tables, xn, w_gate_up[0], w_down[0],
                         b_gate_up[0].reshape(N_EXPERTS, 2 * MOE_P1, MOE_TF),
                         b_down[0].reshape(N_EXPERTS, MOE_P2, MOE_TN))
    ys, yp = _combine_call(dest, y_sorted, x1, route_g,
                           p_sample[0].reshape(T_S, PLE_DIM), p_prompt[0].reshape(T_P, PLE_DIM),
                           w_ple[0].astype(BF16), w_ple_gate[0].astype(BF16), g_ple, g_final.reshape(1, D_MODEL))

    y_prompt = yp.reshape(BATCH, SEQ, D_MODEL)
    y_sample = ys.reshape(DEC_BATCH, DEC_SEQ, D_MODEL)
    last = tail[:MIX_NT - MIX_NS].reshape(BATCH, MIX_SEQ_TILES, 8, D_B)[:, -1, 8 - (CONV_W - 1):, :]
    state_conv_prompt = last[None]
    state_conv_sample = cxs[:T_S].reshape(DEC_BATCH, DEC_SEQ, D_B)[:, DEC_SEQ - (CONV_W - 1):, :][None]
    state_chunk_v_sample = vln[:T_S].reshape(DEC_BATCH, DEC_SEQ, D_A)[None]
    return (y_prompt, y_sample, state_conv_prompt, state_conv_sample, state_chunk_v_sample)
```

```python
import functools

import jax
import jax.numpy as jnp
from jax import lax
from jax.experimental import pallas as pl
from jax.experimental.pallas import tpu as pltpu

F32 = jnp.float32
BF16 = jnp.bfloat16
I32 = jnp.int32

D_MODEL = 2048
BATCH = 4
SEQ = 2048
DEC_BATCH = 128
DEC_SEQ = 8
CHUNK = 128
D_A = D_MODEL // 2
N_GROUPS_A = 8
GW_A = D_A // N_GROUPS_A
D_B = D_MODEL // 2
CONV_W = 3
N_EXPERTS = 32
TOP_K = 4
D_FF = D_MODEL
SWIGLU_LIMIT = 7.0
SWIGLU_ALPHA = 1.702
PLE_DIM = 256
EPS = 1e-6

T_S = DEC_BATCH * DEC_SEQ
T_P = BATCH * SEQ
T = T_S + T_P

LANES = 128
V7X_VMEM_BYTES = 64 * 1024 * 1024
MIB = 1024 * 1024

NORM_TM = 512
IN_TM = 1024
IN_TN = 1024
IN_SUB = 256
MIX_TM = CHUNK
MIX_NS = T_S // MIX_TM
MIX_SEQ_TILES = SEQ // MIX_TM
MIX_NT = T // MIX_TM
MOE_RB = 128
MOE_BMAX = 12
MOE_RMAX = MOE_RB * MOE_BMAX
MOE_CHUNKS = (8, 4, 2, 1)
MOE_TF = 256
MOE_TN = 512
MOE_P1 = D_FF // MOE_TF
MOE_P2 = D_MODEL // MOE_TN
MOE_TILES = MOE_P1 + MOE_P2
MOE_W_PRIORITY = 1
MOE_WSLOTS = 3
MOE_WAHEAD = MOE_WSLOTS - 1
assert D_MODEL == D_FF and 2 * MOE_TF == MOE_TN and MOE_TILES % MOE_WSLOTS == 0
MOE_G = MOE_RB // MOE_P1
MOE_YSLOTS = 8
N_SLOTS = T * TOP_K + N_EXPERTS * MOE_RB
N_BLOCKS = N_SLOTS // MOE_RB
MOE_NI = (N_BLOCKS + N_EXPERTS * (MOE_BMAX - 1)) // MOE_BMAX
GATHER_UNROLL = 8
TAB_STRIDE = 64
TAB_E, TAB_ROW0, TAB_NBLK, TAB_WAIT, TAB_USED = 0, TAB_STRIDE, 2 * TAB_STRIDE, 3 * TAB_STRIDE, 4 * TAB_STRIDE
TAB_SIZE = 5 * TAB_STRIDE
assert MOE_NI + 1 <= TAB_STRIDE
N_TOK_TAB = -(-(N_SLOTS + MOE_RMAX) // 1024) * 1024
ROUTE_TB = 512
CMB_TM = 256
CMB_NS = T_S // CMB_TM
CMB_CHUNKS = 8
NEG_BIG = -1e30


def _rms(x, g):
    return x * lax.rsqrt(jnp.mean(x * x, axis=-1, keepdims=True) + EPS) * g


def _cparams(sem, vmem_mib):
    return pltpu.CompilerParams(dimension_semantics=sem, vmem_limit_bytes=vmem_mib * MIB)


def _resident(shape):
    zeros = (0,) * len(shape)
    return pl.BlockSpec(shape, lambda *_: zeros, pipeline_mode=pl.Buffered(1))


def _norm_kernel(xs_ref, xp_ref, g_ref, h_ref, *, ns):
    m = pl.program_id(0)
    x = jnp.where(m < ns, xs_ref[...], xp_ref[...])
    h_ref[...] = _rms(x, g_ref[...]).astype(BF16)


def _norm_call(xs, xp, g):
    ns = T_S // NORM_TM
    return pl.pallas_call(
        functools.partial(_norm_kernel, ns=ns),
        grid=(T // NORM_TM,),
        in_specs=[
            pl.BlockSpec((NORM_TM, D_MODEL), lambda m: (jnp.minimum(m, ns - 1), 0)),
            pl.BlockSpec((NORM_TM, D_MODEL), lambda m: (jnp.maximum(m - ns, 0), 0)),
            pl.BlockSpec((1, D_MODEL), lambda m: (0, 0)),
        ],
        out_specs=pl.BlockSpec((NORM_TM, D_MODEL), lambda m: (m, 0)),
        out_shape=jax.ShapeDtypeStruct((T, D_MODEL), BF16),
        compiler_params=_cparams(("arbitrary",), 32),
        name="norm",
    )(xs, xp, g)


IN_N_GELU = 2 * D_A // IN_TN
IN_N_V0 = D_A // IN_TN
IN_N_LIN = (2 * D_A + 3 * D_B) // IN_TN


def _in_proj_kernel(h_ref, w_ref, z_ref, vpre_ref, wb_ref):
    n = pl.program_id(0)

    @pl.when(pl.program_id(1) == 0)
    def _():
        wb_ref[...] = w_ref[...].astype(BF16)

    def blocks(epilogue):
        for b in range(IN_TM // IN_SUB):
            rows = pl.ds(b * IN_SUB, IN_SUB)
            epilogue(rows, jnp.dot(h_ref[rows, :], wb_ref[...], preferred_element_type=F32))

    @pl.when(n < IN_N_V0)
    def _():
        def ep(rows, acc):
            z_ref[rows, :] = jax.nn.gelu(acc, approximate=True).astype(BF16)

        blocks(ep)

    @pl.when((n >= IN_N_V0) & (n < IN_N_GELU))
    def _():
        def ep(rows, acc):
            g = jax.nn.gelu(acc, approximate=True)
            z_ref[rows, :] = g.astype(BF16)
            vpre_ref[rows, :] = g

        blocks(ep)

    @pl.when((n >= IN_N_GELU) & (n < IN_N_LIN))
    def _():
        def ep(rows, acc):
            z_ref[rows, :] = acc.astype(BF16)

        blocks(ep)

    @pl.when(n >= IN_N_LIN)
    def _():
        def ep(rows, acc):
            z_ref[rows, :] = jax.nn.sigmoid(acc).astype(BF16)

        blocks(ep)


def _in_proj_call(h, w_in):
    d_in = w_in.shape[1]
    n_m = T // IN_TM

    def vpre_map(n, m):
        row = jnp.where(n < IN_N_V0, 0, jnp.where(n < IN_N_GELU, m, n_m - 1))
        return (row, jnp.clip(n - IN_N_V0, 0, IN_N_GELU - IN_N_V0 - 1))

    return pl.pallas_call(
        _in_proj_kernel,
        grid=(d_in // IN_TN, n_m),
        in_specs=[
            pl.BlockSpec((IN_TM, D_MODEL), lambda n, m: (m, 0)),
            pl.BlockSpec((D_MODEL, IN_TN), lambda n, m: (0, n)),
        ],
        out_specs=[
            pl.BlockSpec((IN_TM, IN_TN), lambda n, m: (m, n)),
            pl.BlockSpec((IN_TM, IN_TN), vpre_map),
        ],
        out_shape=[
            jax.ShapeDtypeStruct((T, d_in), BF16),
            jax.ShapeDtypeStruct((T, D_A), F32),
        ],
        scratch_shapes=[pltpu.VMEM((D_MODEL, IN_TN), BF16)],
        compiler_params=_cparams(("arbitrary", "arbitrary"), 48),
        name="in_proj",
    )(h, w_in)


def _mixer_kernel(xs_ref, xp_ref, zu_ref, zb_ref, zc_ref, zx_ref, ga0_ref, ga1_ref, gb0_ref, gb1_ref,
                  vpre_ref, ext_ref, wsg_ref, bsg_ref, lng_ref, lnb_ref,
                  cw_ref, cb_ref, wa_ref, wb_ref, wo_ref, gmoe_ref, wrh_ref, wrl_ref, br_ref,
                  x1_ref, xn_ref, lgt_ref, vln_ref, cxs_ref, tail_ref,
                  prev_ref, ab_ref):
    m = pl.program_id(0)
    tm = MIX_TM
    a_tile = jnp.minimum(m, MIX_NT - 1)
    is_s = a_tile < MIX_NS
    b_is_s = (m - 1) < MIX_NS
    slot = m & 1

    @pl.when(m == 0)
    def _():
        prev_ref[...] = jnp.zeros_like(prev_ref)
        ab_ref[...] = jnp.zeros_like(ab_ref)

    y_a = jnp.dot(ab_ref[1 - slot, :, 0:D_A], wa_ref[...], preferred_element_type=F32)
    y_b = jnp.dot(ab_ref[1 - slot, :, D_A:D_A + D_B], wb_ref[...], preferred_element_type=F32)

    vg = vpre_ref[...]
    mu = jnp.mean(vg, axis=-1, keepdims=True)
    vc = vg - mu
    v = vc * lax.rsqrt(jnp.mean(vc * vc, axis=-1, keepdims=True) + EPS) * lng_ref[...] + lnb_ref[...]
    vln_ref[...] = v

    vb = v.astype(BF16)
    s_parts = []
    for g in range(N_GROUPS_A):
        s_parts.append(jnp.dot(wsg_ref[0, g], vb[:, g * GW_A:(g + 1) * GW_A], preferred_element_type=F32))
    s = jnp.concatenate(s_parts, axis=1) + bsg_ref[0]
    u = zu_ref[...].astype(F32)
    a_in = (u * s).astype(BF16)

    ga = jnp.concatenate([ga0_ref[...], ga1_ref[...]], axis=1).astype(F32)
    gb = jnp.concatenate([gb0_ref[...], gb1_ref[...]], axis=1).astype(F32)
    mix = (ga * y_a + gb * y_b).astype(BF16)
    x = jnp.where(b_is_s, xs_ref[...], xp_ref[...])
    x1 = x + jnp.dot(mix, wo_ref[...], preferred_element_type=F32)
    x1_ref[...] = x1

    bg = zb_ref[...].astype(F32)
    cg = zc_ref[...].astype(F32)
    xin = zx_ref[...].astype(F32)
    cx = cg * xin
    cxs_ref[...] = cx
    tail_ref[0] = cx[tm - 8:tm]

    row = lax.broadcasted_iota(I32, (tm, D_B), 0)
    seq_start = ((a_tile - MIX_NS) % MIX_SEQ_TILES) == 0
    prev = jnp.where(seq_start, 0.0, prev_ref[...])
    row8 = lax.broadcasted_iota(I32, (8, D_B), 0)
    top = jnp.where(row8 < CONV_W - 1, pltpu.roll(prev, CONV_W - 1, 0), 0.0)
    ext_p = jnp.concatenate([top, jnp.zeros((tm - 8, D_B), F32)], axis=0)
    ext = jnp.where(is_s, ext_ref[...], ext_p)
    t_in = jnp.where(is_s, row & (DEC_SEQ - 1), row)
    s1 = jnp.where(t_in < 1, pltpu.roll(ext, tm - 1, 0), pltpu.roll(cx, 1, 0))
    s2 = jnp.where(t_in < 2, ext, pltpu.roll(cx, 2, 0))
    prev_ref[...] = cx[tm - 8:tm]
    conv = cb_ref[...] + s2 * cw_ref[0:1, :] + s1 * cw_ref[1:2, :] + cx * cw_ref[2:3, :]
    b_in = (bg * conv).astype(BF16)

    xn = _rms(x1, gmoe_ref[...])
    xn_ref[...] = xn
    hi = xn.astype(BF16)
    lo = (xn - hi.astype(F32)).astype(BF16)
    logits = (jnp.dot(hi, wrh_ref[...], preferred_element_type=F32)
              + jnp.dot(lo, wrh_ref[...], preferred_element_type=F32)
              + jnp.dot(hi, wrl_ref[...], preferred_element_type=F32)) + br_ref[...]

    lgt_ref[...] = jnp.transpose(logits)[0:N_EXPERTS, :]

    ab_ref[slot, :, 0:D_A] = a_in
    ab_ref[slot, :, D_A:D_A + D_B] = b_in


def _mixer_call(xs, xp, z, vpre, ext, wsg, bsg, ln_g, ln_b, conv_w, conv_b, wa, wb, wo, g_moe,
                wr_hi, wr_lo, b_r):
    tm = MIX_TM
    ns = MIX_NS
    nt = MIX_NT
    assert D_A == D_B and D_MODEL == 2 * D_A and z.shape[1] == 9 * D_A
    front = lambda m: jnp.minimum(m, nt - 1)
    back = lambda m: jnp.maximum(m - 1, 0)
    zcol = lambda tile_of, c: (lambda m: (tile_of(m), c))
    row = lambda m: (back(m), 0)
    s_out = lambda m: (jnp.minimum(front(m), ns), 0)
    p_out = lambda m: (jnp.where(front(m) < ns, nt - ns, front(m) - ns), 0, 0)
    sel = lambda m: jnp.minimum(front(m) // ns, 1)
    return pl.pallas_call(
        _mixer_kernel,
        grid=(nt + 1,),
        in_specs=[
            pl.BlockSpec((tm, D_MODEL), lambda m: (jnp.minimum(back(m), ns - 1), 0)),
            pl.BlockSpec((tm, D_MODEL), lambda m: (jnp.maximum(back(m) - ns, 0), 0)),
            pl.BlockSpec((tm, D_A), zcol(front, 0)),
            pl.BlockSpec((tm, D_A), zcol(front, 2)),
            pl.BlockSpec((tm, D_A), zcol(front, 3)),
            pl.BlockSpec((tm, D_A), zcol(front, 4)),
            pl.BlockSpec((tm, D_A), zcol(back, 5)),
            pl.BlockSpec((tm, D_A), zcol(back, 6)),
            pl.BlockSpec((tm, D_A), zcol(back, 7)),
            pl.BlockSpec((tm, D_A), zcol(back, 8)),
            pl.BlockSpec((tm, D_A), lambda m: (front(m), 0)),
            pl.BlockSpec((tm, D_B), lambda m: (jnp.minimum(front(m), ns - 1), 0)),
            pl.BlockSpec((1, N_GROUPS_A, CHUNK, CHUNK), lambda m: (sel(m), 0, 0, 0)),
            pl.BlockSpec((1, CHUNK, D_A), lambda m: (sel(m), 0, 0)),
            _resident((1, D_A)),
            _resident((1, D_A)),
            _resident((CONV_W, D_B)),
            _resident((1, D_B)),
            _resident((D_A, D_MODEL)),
            _resident((D_B, D_MODEL)),
            _resident((D_MODEL, D_MODEL)),
            _resident((1, D_MODEL)),
            _resident((D_MODEL, LANES)),
            _resident((D_MODEL, LANES)),
            _resident((1, LANES)),
        ],
        out_specs=[
            pl.BlockSpec((tm, D_MODEL), row),
            pl.BlockSpec((tm, D_MODEL), row),
            pl.BlockSpec((N_EXPERTS, tm), lambda m: (0, back(m))),
            pl.BlockSpec((tm, D_A), s_out),
            pl.BlockSpec((tm, D_B), s_out),
            pl.BlockSpec((1, 8, D_B), p_out),
        ],
        out_shape=[
            jax.ShapeDtypeStruct((T, D_MODEL), F32),
            jax.ShapeDtypeStruct((T, D_MODEL), F32),
            jax.ShapeDtypeStruct((N_EXPERTS, T), F32),
            jax.ShapeDtypeStruct((T_S + tm, D_A), F32),
            jax.ShapeDtypeStruct((T_S + tm, D_B), F32),
            jax.ShapeDtypeStruct((nt - ns + 1, 8, D_B), F32),
        ],
        scratch_shapes=[pltpu.VMEM((8, D_B), F32), pltpu.VMEM((2, tm, D_A + D_B), BF16)],
        compiler_params=_cparams(("arbitrary",), 52),
        name="mixer",
    )(xs, xp, z, z, z, z, z, z, z, z, vpre, ext, wsg, bsg, ln_g, ln_b, conv_w, conv_b, wa, wb, wo, g_moe,
      wr_hi, wr_lo, b_r)


MOE_CHUNK_SLOT0 = {8: 0, 4: 0, 2: 4, 1: 6}


def _for_units(n, unit_fn, chunk_begin=None):
    big = MOE_CHUNKS[0]

    def chunk(u0, count):
        slot0 = MOE_CHUNK_SLOT0[count]
        if chunk_begin is not None:
            chunk_begin(range(slot0, slot0 + count))
        for j in range(count):
            unit_fn(u0 + j, slot0 + j)

    def body(c, carry):
        chunk(c * big, big)
        return carry

    n_big = lax.shift_right_logical(n, big.bit_length() - 1)
    lax.fori_loop(0, n_big, body, 0)
    base = n_big * big
    for count in MOE_CHUNKS[1:]:
        @pl.when((n & count) != 0)
        def _():
            chunk(base, count)

        base = base + (n & count)


def _moe_kernel(tok_ref, tab_ref,
                xn_hbm, wgu_hbm, wd_hbm, bgu_hbm, bd_hbm,
                y_hbm,
                xraw_ref, xb_ref, act_ref, wst_ref, wbf_ref, bgu_ref, bd_ref,
                ystage_ref, zbuf_ref, gsem, ysem, zsem, wsem, bsem):
    i = pl.program_id(0)
    nblk = tab_ref[TAB_NBLK +i]
    row0 = tab_ref[TAB_ROW0 +i]
    valid = nblk > 0
    par = i & 1

    def gu_copies(e, tile, slot):
        col = pl.multiple_of(tile * MOE_TF, MOE_TF)
        return (pltpu.make_async_copy(wgu_hbm.at[e, :, pl.ds(col, MOE_TF)],
                                      wst_ref.at[slot, :, pl.ds(0, MOE_TF)], wsem.at[slot]),
                pltpu.make_async_copy(wgu_hbm.at[e, :, pl.ds(D_FF + col, MOE_TF)],
                                      wst_ref.at[slot, :, pl.ds(MOE_TF, MOE_TF)], wsem.at[slot]))

    def d_copy(e, tile, slot):
        col = pl.multiple_of(tile * MOE_TN, MOE_TN)
        return pltpu.make_async_copy(wd_hbm.at[e, :, pl.ds(col, MOE_TN)], wst_ref.at[slot], wsem.at[slot])

    def start_tile(e, t):
        slot = lax.rem(t, MOE_WSLOTS)

        @pl.when(t < MOE_P1)
        def _():
            for c in gu_copies(e, t, slot):
                c.start(priority=MOE_W_PRIORITY)

        @pl.when(t >= MOE_P1)
        def _():
            d_copy(e, t - MOE_P1, slot).start(priority=MOE_W_PRIORITY)

    def bias_copies(e, slot):
        return (pltpu.make_async_copy(bgu_hbm.at[e], bgu_ref.at[slot], bsem.at[slot]),
                pltpu.make_async_copy(bd_hbm.at[e], bd_ref.at[slot], bsem.at[slot]))

    def tail_copy(b):
        r = pl.multiple_of(b * MOE_RB, MOE_RB)
        return pltpu.make_async_copy(zbuf_ref, y_hbm.at[pl.ds(r, MOE_RB), :], zsem)

    def row_copy(tok, r):
        return pltpu.make_async_copy(xn_hbm.at[pl.ds(tok, 1), :], xraw_ref.at[pl.ds(r, 1), :], gsem)

    def unit_wait():
        return pltpu.make_async_copy(xn_hbm.at[pl.ds(0, MOE_RB), :], xraw_ref.at[pl.ds(0, MOE_RB), :], gsem)

    def gather_rows(item, lo_unit, hi_unit):
        base = tab_ref[TAB_ROW0 +item]

        def body(c, carry):
            for j in range(GATHER_UNROLL):
                r = c * GATHER_UNROLL + j
                row_copy(tok_ref[base + r], r).start()
            return carry

        per_unit = MOE_RB // GATHER_UNROLL
        lax.fori_loop(lo_unit * per_unit, hi_unit * per_unit, body, 0)

    def y_copy(slot, u, col):
        r = pl.multiple_of(row0 + u * MOE_RB, MOE_RB)
        return pltpu.make_async_copy(ystage_ref.at[slot], y_hbm.at[pl.ds(r, MOE_RB), pl.ds(col, MOE_TN)],
                                     ysem.at[slot])

    def dump_copy(slot):
        r = N_SLOTS + (slot // MOE_P2) * MOE_RB
        c = (slot % MOE_P2) * MOE_TN
        return pltpu.make_async_copy(ystage_ref.at[slot], y_hbm.at[pl.ds(r, MOE_RB), pl.ds(c, MOE_TN)],
                                     ysem.at[slot])

    @pl.when(i == 0)
    def _():
        for t in range(MOE_WAHEAD):
            for c in gu_copies(tab_ref[TAB_E +0], t, t):
                c.start(priority=MOE_W_PRIORITY)
        for c in bias_copies(tab_ref[TAB_E +0], 0):
            c.start()
        gather_rows(0, 0, tab_ref[TAB_NBLK +0])
        ystage_ref[...] = jnp.zeros_like(ystage_ref)
        for slot in range(MOE_YSLOTS):
            dump_copy(slot).start()
        zbuf_ref[...] = jnp.zeros_like(zbuf_ref)

        def fill(b, carry):
            tail_copy(b).start()
            return carry

        lax.fori_loop(tab_ref[TAB_USED], N_BLOCKS, fill, 0)

    def wait_rows(b, carry):
        unit_wait().wait()
        return carry

    lax.fori_loop(0, tab_ref[TAB_WAIT +i], wait_rows, 0)

    def conv(u, carry):
        rows = pl.ds(pl.multiple_of(u * MOE_RB, MOE_RB), MOE_RB)
        xb_ref[rows, :] = xraw_ref[rows, :].astype(BF16)
        return carry

    lax.fori_loop(0, nblk, conv, 0)

    e = tab_ref[TAB_E +i]
    e_next = tab_ref[TAB_E +i + 1]
    next_valid = tab_ref[TAB_NBLK +i + 1] > 0
    next_base = tab_ref[TAB_ROW0 +i + 1]

    @pl.when(valid)
    def _():
        for c in bias_copies(e, par):
            c.wait()

    def gate_up_step(s, carry):
        wslot = lax.rem(s, MOE_WSLOTS)
        for c in gu_copies(e, s, wslot):
            c.wait()
        start_tile(e, s + MOE_WAHEAD)
        wbf_ref[...] = wst_ref[wslot].astype(BF16)
        b_g = bgu_ref[par, pl.ds(s, 1), :]
        b_u = bgu_ref[par, pl.ds(MOE_P1 + s, 1), :]

        def unit(u, slot):
            r = pl.multiple_of(u * MOE_RB, MOE_RB)
            first = pl.multiple_of(r + s * MOE_G, MOE_G)
            for j in range(MOE_G):
                row_copy(tok_ref[next_base + first + j], first + j).start()
            gu = jnp.dot(xb_ref[pl.ds(r, MOE_RB), :], wbf_ref[...], preferred_element_type=F32)
            gate = jnp.minimum(gu[:, 0:MOE_TF] + b_g, SWIGLU_LIMIT)
            up = jnp.clip(gu[:, MOE_TF:2 * MOE_TF] + b_u, -SWIGLU_LIMIT, SWIGLU_LIMIT)
            act = (up + 1) * (gate * jax.nn.sigmoid(gate * SWIGLU_ALPHA))
            act_ref[s, pl.ds(r, MOE_RB), :] = act.astype(BF16)

        _for_units(nblk, unit)
        return carry

    def down_step(s, carry):
        t = MOE_P1 + s
        wslot = lax.rem(t, MOE_WSLOTS)
        d_copy(e, s, wslot).wait()

        @pl.when(t + MOE_WAHEAD < MOE_TILES)
        def _():
            start_tile(e, t + MOE_WAHEAD)

        @pl.when((t + MOE_WAHEAD >= MOE_TILES) & next_valid)
        def _():
            start_tile(e_next, t + MOE_WAHEAD - MOE_TILES)

        @pl.when((t + MOE_WAHEAD == MOE_TILES) & next_valid)
        def _():
            for c in bias_copies(e_next, 1 - par):
                c.start()

        wbf_ref[...] = wst_ref[wslot].astype(BF16)
        b_d = bd_ref[par, pl.ds(s, 1), :]
        col = pl.multiple_of(s * MOE_TN, MOE_TN)

        def free_slots(slots):
            for slot in slots:
                y_copy(slot, 0, col).wait()

        def unit(u, slot):
            r = pl.multiple_of(u * MOE_RB, MOE_RB)
            a = jnp.concatenate([act_ref[j, pl.ds(r, MOE_RB), :] for j in range(MOE_P1)], axis=1)
            ystage_ref[slot] = jnp.dot(a, wbf_ref[...], preferred_element_type=F32) + b_d
            y_copy(slot, u, col).start()

        _for_units(nblk, unit, free_slots)
        return carry

    @pl.when(valid)
    def _():
        lax.fori_loop(0, MOE_P1, gate_up_step, 0)
        gather_rows(i + 1, nblk, tab_ref[TAB_NBLK +i + 1])
        lax.fori_loop(0, MOE_P2, down_step, 0)

    @pl.when(i == MOE_NI - 1)
    def _():
        for slot in range(MOE_YSLOTS):
            dump_copy(slot).wait()

        lax.fori_loop(0, tab_ref[TAB_WAIT +MOE_NI], wait_rows, 0)

        def drain(b, carry):
            tail_copy(b).wait()
            return carry

        lax.fori_loop(tab_ref[TAB_USED], N_BLOCKS, drain, 0)


def _moe_call(slot_tok, tables, xn, w_gate_up, w_down, b_gate_up, b_down):
    any_spec = pl.BlockSpec(memory_space=pl.ANY)
    grid_spec = pltpu.PrefetchScalarGridSpec(
        num_scalar_prefetch=2,
        grid=(MOE_NI,),
        in_specs=[any_spec] * 5,
        out_specs=any_spec,
        scratch_shapes=[
            pltpu.VMEM((MOE_RMAX, D_MODEL), F32),
            pltpu.VMEM((MOE_RMAX, D_MODEL), BF16),
            pltpu.VMEM((MOE_P1, MOE_RMAX, MOE_TF), BF16),
            pltpu.VMEM((MOE_WSLOTS, D_MODEL, MOE_TN), F32),
            pltpu.VMEM((D_MODEL, MOE_TN), BF16),
            pltpu.VMEM((2, 2 * MOE_P1, MOE_TF), F32),
            pltpu.VMEM((2, MOE_P2, MOE_TN), F32),
            pltpu.VMEM((MOE_YSLOTS, MOE_RB, MOE_TN), F32),
            pltpu.VMEM((MOE_RB, D_MODEL), F32),
            pltpu.SemaphoreType.DMA(()),
            pltpu.SemaphoreType.DMA((MOE_YSLOTS,)),
            pltpu.SemaphoreType.DMA(()),
            pltpu.SemaphoreType.DMA((MOE_WSLOTS,)),
            pltpu.SemaphoreType.DMA((2,)),
        ],
    )
    spare_blocks = MOE_YSLOTS // MOE_P2
    return pl.pallas_call(
        _moe_kernel,
        grid_spec=grid_spec,
        out_shape=jax.ShapeDtypeStruct((N_SLOTS + spare_blocks * MOE_RB, D_MODEL), F32),
        compiler_params=_cparams(("arbitrary",), 56),
        name="moe",
    )(slot_tok, tables, xn, w_gate_up, w_down, b_gate_up, b_down)


def _combine_kernel(dest_ref, y_hbm, x1_ref, rg_ref, ps_ref, pp_ref, wple_ref, wpg_ref, gple_ref,
                    gfin_ref, ys_ref, yp_ref, gbuf_ref, gsem):
    m = pl.program_id(0)
    nm = pl.num_programs(0)
    tm = CMB_TM
    slot = m % 2

    def row_copy(tile, slot_, r, k):
        d = dest_ref[k * T + tile * tm + r]
        return pltpu.make_async_copy(y_hbm.at[pl.ds(d, 1), :], gbuf_ref.at[slot_, k, pl.ds(r, 1), :],
                                     gsem.at[slot_])

    def wait_tile(slot_):
        for k in range(TOP_K):
            pltpu.make_async_copy(y_hbm.at[pl.ds(0, tm), :], gbuf_ref.at[slot_, k], gsem.at[slot_]).wait()

    @pl.when(m == 0)
    def _():
        def body(c, carry):
            for j in range(GATHER_UNROLL // TOP_K):
                for k in range(TOP_K):
                    row_copy(0, 0, c * (GATHER_UNROLL // TOP_K) + j, k).start()
            return carry

        lax.fori_loop(0, tm // (GATHER_UNROLL // TOP_K), body, 0)

    wait_tile(slot)
    gates = rg_ref[...]
    moe = gates[:, 0:1] * gbuf_ref[slot, 0]
    for k in range(1, TOP_K):
        moe = moe + gates[:, k:k + 1] * gbuf_ref[slot, k]
    x2 = x1_ref[...] + moe
    is_s = m < CMB_NS
    p = jnp.where(is_s, ps_ref[...], pp_ref[...]).astype(BF16)
    hn = _rms(x2, gple_ref[...]).astype(BF16)
    nxt = jnp.minimum(m + 1, nm - 1)
    rows_per_chunk = tm // CMB_CHUNKS
    cw = D_MODEL // CMB_CHUNKS
    x3_parts = []
    for c in range(CMB_CHUNKS):
        cols = slice(c * cw, (c + 1) * cw)
        pe = jnp.dot(p, wple_ref[:, cols], preferred_element_type=F32)
        gate = jax.nn.sigmoid(jnp.dot(hn, wpg_ref[:, cols], preferred_element_type=F32))
        x3_parts.append(x2[:, cols] + pe * gate)
        for r in range(c * rows_per_chunk, (c + 1) * rows_per_chunk):
            for k in range(TOP_K):
                row_copy(nxt, 1 - slot, r, k).start()
    x3 = jnp.concatenate(x3_parts, axis=1)
    y = _rms(x3, gfin_ref[...])

    @pl.when(is_s)
    def _():
        ys_ref[...] = y

    @pl.when(jnp.logical_not(is_s))
    def _():
        yp_ref[...] = y

    @pl.when(m == nm - 1)
    def _():
        wait_tile(1 - slot)


def _combine_call(dest, y_sorted, x1, rg, ps, pp, wple, wpg, g_ple, g_final):
    tm = CMB_TM
    ns = CMB_NS
    s_idx = lambda m, d: (jnp.minimum(m, ns - 1), 0)
    p_idx = lambda m, d: (jnp.maximum(m - ns, 0), 0)
    row = lambda m, d: (m, 0)
    const2 = lambda m, d: (0, 0)
    grid_spec = pltpu.PrefetchScalarGridSpec(
        num_scalar_prefetch=1,
        grid=(T // tm,),
        in_specs=[
            pl.BlockSpec(memory_space=pl.ANY),
            pl.BlockSpec((tm, D_MODEL), row),
            pl.BlockSpec((tm, LANES), row),
            pl.BlockSpec((tm, PLE_DIM), s_idx),
            pl.BlockSpec((tm, PLE_DIM), p_idx),
            pl.BlockSpec((PLE_DIM, D_MODEL), const2),
            pl.BlockSpec((D_MODEL, D_MODEL), const2),
            pl.BlockSpec((1, D_MODEL), const2),
            pl.BlockSpec((1, D_MODEL), const2),
        ],
        out_specs=[
            pl.BlockSpec((tm, D_MODEL), s_idx),
            pl.BlockSpec((tm, D_MODEL), p_idx),
        ],
        scratch_shapes=[
            pltpu.VMEM((2, TOP_K, tm, D_MODEL), F32),
            pltpu.SemaphoreType.DMA((2,)),
        ],
    )
    return pl.pallas_call(
        _combine_kernel,
        grid_spec=grid_spec,
        out_shape=[
            jax.ShapeDtypeStruct((T_S, D_MODEL), F32),
            jax.ShapeDtypeStruct((T_P, D_MODEL), F32),
        ],
        compiler_params=_cparams(("arbitrary",), 56),
        name="combine",
    )(dest, y_sorted, x1, rg, ps, pp, wple, wpg, g_ple, g_final)


def _route_kernel(lgt_ref, dest_ref, tok_ref, tab_ref, rg_ref,
                  rit_ref, dvm_ref, zvm_ref, carry_ref, cnt_ref, gs_ref, sem):
    rb_shift = MOE_RB.bit_length() - 1
    tb = ROUTE_TB
    sub = tb // LANES

    a_i = lax.broadcasted_iota(I32, (LANES, LANES), 0)
    b_i = lax.broadcasted_iota(I32, (LANES, LANES), 1)
    earlier = jnp.where(a_i < b_i, 1.0, 0.0).astype(BF16)
    carry_ref[...] = jnp.zeros_like(carry_ref)

    def tile(c, loop_carry):
        c0 = pl.multiple_of(c * tb, tb)
        work = lgt_ref[:, pl.ds(c0, tb)]
        row = lax.broadcasted_iota(I32, (N_EXPERTS, tb), 0).astype(F32)
        vals, idxs, hots = [], [], []
        for _ in range(TOP_K):
            mx = jnp.max(work, axis=0, keepdims=True)
            idx = jnp.min(jnp.where(work == mx, row, float(N_EXPERTS)), axis=0, keepdims=True)
            hot = row == idx
            work = jnp.where(hot, -jnp.inf, work)
            vals.append(mx)
            idxs.append(idx)
            hots.append(hot)
        exps = [jnp.exp(vk - vals[0]) for vk in vals]
        den = exps[0] + exps[1] + exps[2] + exps[3]
        chosen = jnp.zeros((N_EXPERTS, tb), F32)
        for hot in hots:
            chosen = chosen + jnp.where(hot, 1.0, 0.0)
        counts = carry_ref[...]
        before = []
        for j in range(sub):
            cj = chosen[:, j * LANES:(j + 1) * LANES]
            before.append(jnp.dot(cj.astype(BF16), earlier, preferred_element_type=F32) + counts)
            counts = counts + jnp.sum(cj, axis=1, keepdims=True)
        carry_ref[...] = counts
        before = jnp.concatenate(before, axis=1)
        for k in range(TOP_K):
            rank = jnp.sum(jnp.where(hots[k], before, 0.0), axis=0, keepdims=True)
            rit_ref[k:k + 1, pl.ds(c0, tb)] = idxs[k].astype(I32)
            rit_ref[TOP_K + k:TOP_K + k + 1, pl.ds(c0, tb)] = rank.astype(I32)
        gates = jnp.concatenate([ek / den for ek in exps], axis=0)
        for j in range(sub):
            g_tile = jnp.concatenate([gates[:, j * LANES:(j + 1) * LANES],
                                      jnp.zeros((LANES - TOP_K, LANES), F32)], axis=0)
            rg_ref[pl.ds(pl.multiple_of(c0 + j * LANES, LANES), LANES), :] = jnp.transpose(g_tile)
        return loop_carry

    lax.fori_loop(0, T // tb, tile, 0)
    cnt_copy = pltpu.make_async_copy(carry_ref, cnt_ref, sem)
    cnt_copy.start()
    cnt_copy.wait()

    def clear(j, carry):
        tab_ref[j] = 0
        return carry

    lax.fori_loop(0, TAB_SIZE, clear, 0)

    def expert(e, carry):
        acc, item, used = carry
        n = lax.shift_right_logical(cnt_ref[e, 0].astype(I32) + (MOE_RB - 1), rb_shift)
        gs_ref[e] = acc

        def add_item(local, it):
            tab_ref[TAB_E + it] = e
            tab_ref[TAB_ROW0 + it] = acc + local * MOE_RMAX
            tab_ref[TAB_NBLK + it] = jnp.minimum(MOE_BMAX, n - local * MOE_BMAX)
            return it + 1

        item = lax.fori_loop(0, lax.div(n + (MOE_BMAX - 1), MOE_BMAX), add_item, item)
        return acc + n * MOE_RB, item, used + n

    _, n_items, used = lax.fori_loop(0, N_EXPERTS, expert, (jnp.int32(0), jnp.int32(0), jnp.int32(0)))
    tab_ref[TAB_USED] = used
    e_last = tab_ref[TAB_E + n_items - 1]

    def pad_item(it, carry):
        tab_ref[TAB_E + it] = e_last
        tab_ref[TAB_ROW0 + it] = 0
        tab_ref[TAB_NBLK + it] = 0
        return carry

    lax.fori_loop(n_items, MOE_NI + 1, pad_item, 0)

    def wait_units(it, prev):
        nb = tab_ref[TAB_NBLK + it]
        tab_ref[TAB_WAIT + it] = jnp.maximum(nb, prev)
        return nb

    lax.fori_loop(0, MOE_NI + 1, wait_units, jnp.int32(0))

    e_idx = rit_ref[0:TOP_K, :]
    d = rit_ref[TOP_K:2 * TOP_K, :]
    for e in range(N_EXPERTS):
        d = d + jnp.where(e_idx == e, gs_ref[e], 0)
    dvm_ref[0:TOP_K, :] = d
    zvm_ref[...] = jnp.zeros_like(zvm_ref)
    copies = [pltpu.make_async_copy(dvm_ref.at[k], dest_ref.at[pl.ds(k * T, T)], sem) for k in range(TOP_K)]
    copies.append(pltpu.make_async_copy(zvm_ref, tok_ref, sem))
    for c in copies:
        c.start()
    for c in copies:
        c.wait()

    for k in range(TOP_K):
        def scatter(c, carry):
            for j in range(GATHER_UNROLL):
                t = c * GATHER_UNROLL + j
                tok_ref[dest_ref[k * T + t]] = t
            return carry

        lax.fori_loop(0, T // GATHER_UNROLL, scatter, 0)


def _route_call(logits_t):
    smem = pl.BlockSpec(memory_space=pltpu.SMEM)
    vmem = pl.BlockSpec(memory_space=pltpu.VMEM)
    return pl.pallas_call(
        _route_kernel,
        in_specs=[vmem],
        out_specs=[smem, smem, smem, vmem],
        out_shape=[
            jax.ShapeDtypeStruct((TOP_K * T,), I32),
            jax.ShapeDtypeStruct((N_TOK_TAB,), I32),
            jax.ShapeDtypeStruct((TAB_SIZE,), I32),
            jax.ShapeDtypeStruct((T, LANES), F32),
        ],
        scratch_shapes=[
            pltpu.VMEM((2 * TOP_K, T), I32),
            pltpu.VMEM((2 * TOP_K, T), I32),
            pltpu.VMEM((N_TOK_TAB,), I32),
            pltpu.VMEM((N_EXPERTS, LANES), F32),
            pltpu.SMEM((N_EXPERTS, LANES), F32),
            pltpu.SMEM((N_EXPERTS,), I32),
            pltpu.SemaphoreType.DMA(()),
        ],
        compiler_params=pltpu.CompilerParams(vmem_limit_bytes=32 * MIB),
        name="route",
    )(logits_t)


def kernel(x_prompt, x_sample, state_conv, p_prompt, p_sample, g_mix, w_in, ln_v_g, ln_v_b, w_s, b_s,
           conv_w, conv_b, w_proj_a, w_proj_b, w_o, g_moe, w_router, b_router, w_gate_up, b_gate_up,
           w_down, b_down, g_ple, w_ple, w_ple_gate, g_final):
    assert g_mix.shape[0] == 1, "one layer"
    xs = x_sample.reshape(T_S, D_MODEL)
    xp = x_prompt.reshape(T_P, D_MODEL)

    tril = jnp.tril(jnp.ones((CHUNK, CHUNK), bool))
    w_prompt = jnp.where(tril[None], w_s[0], 0.0)
    small = jnp.where(tril[None, :DEC_SEQ, :DEC_SEQ], w_s[0, :, :DEC_SEQ, :DEC_SEQ], 0.0)
    reps = CHUNK // DEC_SEQ
    blockdiag = jnp.kron(jnp.eye(reps, dtype=F32), jnp.ones((DEC_SEQ, DEC_SEQ), F32))
    w_sample = jnp.tile(small, (1, reps, reps)) * blockdiag[None]
    wsg = jnp.stack([w_sample, w_prompt]).astype(BF16)
    bias_p = jnp.repeat(b_s[0].T, GW_A, axis=1)
    bias_s = jnp.tile(jnp.repeat(b_s[0, :, :DEC_SEQ].T, GW_A, axis=1), (reps, 1))
    bsg = jnp.stack([bias_s, bias_p])
    ext = jnp.pad(state_conv[0], ((0, 0), (0, DEC_SEQ - (CONV_W - 1)), (0, 0))).reshape(T_S, D_B)

    wr = jnp.pad(w_router[0], ((0, 0), (0, LANES - N_EXPERTS)))
    wr_hi = wr.astype(BF16)
    wr_lo = (wr - wr_hi.astype(F32)).astype(BF16)
    b_r = jnp.pad(b_router[0], (0, LANES - N_EXPERTS), constant_values=NEG_BIG).reshape(1, LANES)

    h = _norm_call(xs, xp, g_mix)
    z, vpre = _in_proj_call(h, w_in[0])
    x1, xn, logits_t, vln, cxs, tail = _mixer_call(
        xs, xp, z, vpre, ext, wsg, bsg, ln_v_g, ln_v_b, conv_w[0], conv_b,
        w_proj_a[0].astype(BF16), w_proj_b[0].astype(BF16), w_o[0].astype(BF16), g_moe,
        wr_hi, wr_lo, b_r)

    dest, slot_tok, tables, route_g = _route_call(logits_t)
    y_sorted = _moe_call(slot_tok, tables, xn, w_gate_up[0], w_down[0],
                         b_gate_up[0].reshape(N_EXPERTS, 2 * MOE_P1, MOE_TF),
                         b_down[0].reshape(N_EXPERTS, MOE_P2, MOE_TN))
    ys, yp = _combine_call(dest, y_sorted, x1, route_g,
                           p_sample[0].reshape(T_S, PLE_DIM), p_prompt[0].reshape(T_P, PLE_DIM),
                           w_ple[0].astype(BF16), w_ple_gate[0].astype(BF16), g_ple, g_final.reshape(1, D_MODEL))

    y_prompt = yp.reshape(BATCH, SEQ, D_MODEL)
    y_sample = ys.reshape(DEC_BATCH, DEC_SEQ, D_MODEL)
    last = tail[:MIX_NT - MIX_NS].reshape(BATCH, MIX_SEQ_TILES, 8, D_B)[:, -1, 8 - (CONV_W - 1):, :]
    state_conv_prompt = last[None]
    state_conv_sample = cxs[:T_S].reshape(DEC_BATCH, DEC_SEQ, D_B)[:, DEC_SEQ - (CONV_W - 1):, :][None]
    state_chunk_v_sample = vln[:T_S].reshape(DEC_BATCH, DEC_SEQ, D_A)[None]
    return (y_prompt, y_sample, state_conv_prompt, state_conv_sample, state_chunk_v_sample)
```

```python
import functools

import jax
import jax.numpy as jnp
from jax import lax
from jax.experimental import pallas as pl
from jax.experimental.pallas import tpu as pltpu

F32 = jnp.float32
BF16 = jnp.bfloat16
I32 = jnp.int32

D_MODEL = 2048
BATCH = 4
SEQ = 2048
DEC_BATCH = 128
DEC_SEQ = 8
CHUNK = 128
D_A = D_MODEL // 2
N_GROUPS_A = 8
GW_A = D_A // N_GROUPS_A
D_B = D_MODEL // 2
CONV_W = 3
N_EXPERTS = 32
TOP_K = 4
D_FF = D_MODEL
SWIGLU_LIMIT = 7.0
SWIGLU_ALPHA = 1.702
PLE_DIM = 256
EPS = 1e-6

T_S = DEC_BATCH * DEC_SEQ
T_P = BATCH * SEQ
T = T_S + T_P

LANES = 128
SUBLANES = 8
V7X_VMEM_BYTES = 64 * 1024 * 1024
MIB = 1024 * 1024

NORM_TM = 512
IN_TM = 1024
IN_TN = 1024
IN_SUB = 256
MIX_TM = CHUNK
MIX_NS = T_S // MIX_TM
MIX_SEQ_TILES = SEQ // MIX_TM
MIX_NT = T // MIX_TM
MOE_RB = 128
MOE_BMAX = 14
MOE_RMAX = MOE_RB * MOE_BMAX
MOE_CHUNKS = (8, 4, 2, 1)
MOE_TF = 256
MOE_TN = 512
MOE_P1 = D_FF // MOE_TF
MOE_P2 = D_MODEL // MOE_TN
MOE_TILES = MOE_P1 + MOE_P2
MOE_W_PRIORITY = 1
MOE_WSLOTS = 3
MOE_WAHEAD = MOE_WSLOTS - 1
assert D_MODEL == D_FF and 2 * MOE_TF == MOE_TN and MOE_TILES % MOE_WSLOTS == 0
MOE_G = MOE_RB // MOE_P1
MOE_YSLOTS = 8
N_SLOTS = T * TOP_K + N_EXPERTS * MOE_RB
N_BLOCKS = N_SLOTS // MOE_RB
MOE_NI = (N_BLOCKS + N_EXPERTS * (MOE_BMAX - 1)) // MOE_BMAX
GATHER_UNROLL = 8
TAB_STRIDE = 64
TAB_E, TAB_ROW0, TAB_NBLK, TAB_WAIT, TAB_USED = 0, TAB_STRIDE, 2 * TAB_STRIDE, 3 * TAB_STRIDE, 4 * TAB_STRIDE
TAB_SIZE = 5 * TAB_STRIDE
assert MOE_NI + 1 <= TAB_STRIDE
N_TOK_TAB = -(-(N_SLOTS + MOE_RMAX) // 1024) * 1024
ROUTE_TB = 512
CMB_TM = 256
CMB_NS = T_S // CMB_TM
CMB_CHUNKS = 8
NEG_BIG = -1e30


def _rms(x, g):
    return x * lax.rsqrt(jnp.mean(x * x, axis=-1, keepdims=True) + EPS) * g


def _cparams(sem, vmem_mib):
    return pltpu.CompilerParams(dimension_semantics=sem, vmem_limit_bytes=vmem_mib * MIB)


def _resident(shape):
    zeros = (0,) * len(shape)
    return pl.BlockSpec(shape, lambda *_: zeros, pipeline_mode=pl.Buffered(1))


def _norm_kernel(xs_ref, xp_ref, g_ref, h_ref, *, ns):
    m = pl.program_id(0)
    x = jnp.where(m < ns, xs_ref[...], xp_ref[...])
    h_ref[...] = _rms(x, g_ref[...]).astype(BF16)


def _norm_call(xs, xp, g):
    ns = T_S // NORM_TM
    return pl.pallas_call(
        functools.partial(_norm_kernel, ns=ns),
        grid=(T // NORM_TM,),
        in_specs=[
            pl.BlockSpec((NORM_TM, D_MODEL), lambda m: (jnp.minimum(m, ns - 1), 0)),
            pl.BlockSpec((NORM_TM, D_MODEL), lambda m: (jnp.maximum(m - ns, 0), 0)),
            pl.BlockSpec((1, D_MODEL), lambda m: (0, 0)),
        ],
        out_specs=pl.BlockSpec((NORM_TM, D_MODEL), lambda m: (m, 0)),
        out_shape=jax.ShapeDtypeStruct((T, D_MODEL), BF16),
        compiler_params=_cparams(("arbitrary",), 32),
        name="norm",
    )(xs, xp, g)


IN_N_GELU = 2 * D_A // IN_TN
IN_N_V0 = D_A // IN_TN
IN_N_LIN = (2 * D_A + 3 * D_B) // IN_TN


def _in_proj_kernel(h_ref, w_ref, z_ref, vpre_ref, wb_ref):
    n = pl.program_id(0)

    @pl.when(pl.program_id(1) == 0)
    def _():
        wb_ref[...] = w_ref[...].astype(BF16)

    def blocks(epilogue):
        for b in range(IN_TM // IN_SUB):
            rows = pl.ds(b * IN_SUB, IN_SUB)
            epilogue(rows, jnp.dot(h_ref[rows, :], wb_ref[...], preferred_element_type=F32))

    @pl.when(n < IN_N_V0)
    def _():
        def ep(rows, acc):
            z_ref[rows, :] = jax.nn.gelu(acc, approximate=True).astype(BF16)

        blocks(ep)

    @pl.when((n >= IN_N_V0) & (n < IN_N_GELU))
    def _():
        def ep(rows, acc):
            g = jax.nn.gelu(acc, approximate=True)
            z_ref[rows, :] = g.astype(BF16)
            vpre_ref[rows, :] = g

        blocks(ep)

    @pl.when((n >= IN_N_GELU) & (n < IN_N_LIN))
    def _():
        def ep(rows, acc):
            z_ref[rows, :] = acc.astype(BF16)

        blocks(ep)

    @pl.when(n >= IN_N_LIN)
    def _():
        def ep(rows, acc):
            z_ref[rows, :] = jax.nn.sigmoid(acc).astype(BF16)

        blocks(ep)


def _in_proj_call(h, w_in):
    d_in = w_in.shape[1]
    n_m = T // IN_TM

    def vpre_map(n, m):
        row = jnp.where(n < IN_N_V0, 0, jnp.where(n < IN_N_GELU, m, n_m - 1))
        return (row, jnp.clip(n - IN_N_V0, 0, IN_N_GELU - IN_N_V0 - 1))

    return pl.pallas_call(
        _in_proj_kernel,
        grid=(d_in // IN_TN, n_m),
        in_specs=[
            pl.BlockSpec((IN_TM, D_MODEL), lambda n, m: (m, 0)),
            pl.BlockSpec((D_MODEL, IN_TN), lambda n, m: (0, n)),
        ],
        out_specs=[
            pl.BlockSpec((IN_TM, IN_TN), lambda n, m: (m, n)),
            pl.BlockSpec((IN_TM, IN_TN), vpre_map),
        ],
        out_shape=[
            jax.ShapeDtypeStruct((T, d_in), BF16),
            jax.ShapeDtypeStruct((T, D_A), F32),
        ],
        scratch_shapes=[pltpu.VMEM((D_MODEL, IN_TN), BF16)],
        compiler_params=_cparams(("arbitrary", "arbitrary"), 48),
        name="in_proj",
    )(h, w_in)


def _mixer_kernel(xs_ref, xp_ref, zu_ref, zb_ref, zc_ref, zx_ref, ga0_ref, ga1_ref, gb0_ref, gb1_ref,
                  vpre_ref, ext_ref, wsg_ref, bsg_ref, lng_ref, lnb_ref,
                  cw_ref, cb_ref, wa_ref, wb_ref, wo_ref, gmoe_ref, wrh_ref, wrl_ref, br_ref,
                  x1_ref, xn_ref, lgt_ref, vln_ref, cxs_ref, tail_ref,
                  prev_ref, ab_ref):
    m = pl.program_id(0)
    tm = MIX_TM
    a_tile = jnp.minimum(m, MIX_NT - 1)
    is_s = a_tile < MIX_NS
    b_is_s = (m - 1) < MIX_NS
    slot = m & 1

    @pl.when(m == 0)
    def _():
        prev_ref[...] = jnp.zeros_like(prev_ref)
        ab_ref[...] = jnp.zeros_like(ab_ref)

    y_a = jnp.dot(ab_ref[1 - slot, :, 0:D_A], wa_ref[...], preferred_element_type=F32)
    y_b = jnp.dot(ab_ref[1 - slot, :, D_A:D_A + D_B], wb_ref[...], preferred_element_type=F32)

    vg = vpre_ref[...]
    mu = jnp.mean(vg, axis=-1, keepdims=True)
    vc = vg - mu
    v = vc * lax.rsqrt(jnp.mean(vc * vc, axis=-1, keepdims=True) + EPS) * lng_ref[...] + lnb_ref[...]
    vln_ref[...] = v

    vb = v.astype(BF16)
    s_parts = []
    for g in range(N_GROUPS_A):
        s_parts.append(jnp.dot(wsg_ref[0, g], vb[:, g * GW_A:(g + 1) * GW_A], preferred_element_type=F32))
    s = jnp.concatenate(s_parts, axis=1) + bsg_ref[0]
    u = zu_ref[...].astype(F32)
    a_in = (u * s).astype(BF16)

    ga = jnp.concatenate([ga0_ref[...], ga1_ref[...]], axis=1).astype(F32)
    gb = jnp.concatenate([gb0_ref[...], gb1_ref[...]], axis=1).astype(F32)
    mix = (ga * y_a + gb * y_b).astype(BF16)
    x = jnp.where(b_is_s, xs_ref[...], xp_ref[...])
    x1 = x + jnp.dot(mix, wo_ref[...], preferred_element_type=F32)
    x1_ref[...] = x1

    bg = zb_ref[...].astype(F32)
    cg = zc_ref[...].astype(F32)
    xin = zx_ref[...].astype(F32)
    cx = cg * xin
    cxs_ref[...] = cx
    tail_ref[0] = cx[tm - 8:tm]

    row = lax.broadcasted_iota(I32, (tm, D_B), 0)
    seq_start = ((a_tile - MIX_NS) % MIX_SEQ_TILES) == 0
    prev = jnp.where(seq_start, 0.0, prev_ref[...])
    row8 = lax.broadcasted_iota(I32, (8, D_B), 0)
    top = jnp.where(row8 < CONV_W - 1, pltpu.roll(prev, CONV_W - 1, 0), 0.0)
    ext_p = jnp.concatenate([top, jnp.zeros((tm - 8, D_B), F32)], axis=0)
    ext = jnp.where(is_s, ext_ref[...], ext_p)
    t_in = jnp.where(is_s, row & (DEC_SEQ - 1), row)
    s1 = jnp.where(t_in < 1, pltpu.roll(ext, tm - 1, 0), pltpu.roll(cx, 1, 0))
    s2 = jnp.where(t_in < 2, ext, pltpu.roll(cx, 2, 0))
    prev_ref[...] = cx[tm - 8:tm]
    conv = cb_ref[...] + s2 * cw_ref[0:1, :] + s1 * cw_ref[1:2, :] + cx * cw_ref[2:3, :]
    b_in = (bg * conv).astype(BF16)

    xn = _rms(x1, gmoe_ref[...])
    xn_ref[...] = xn
    hi = xn.astype(BF16)
    lo = (xn - hi.astype(F32)).astype(BF16)
    logits = (jnp.dot(hi, wrh_ref[...], preferred_element_type=F32)
              + jnp.dot(lo, wrh_ref[...], preferred_element_type=F32)
              + jnp.dot(hi, wrl_ref[...], preferred_element_type=F32)) + br_ref[...]

    lgt_ref[...] = jnp.transpose(logits)[0:N_EXPERTS, :]

    ab_ref[slot, :, 0:D_A] = a_in
    ab_ref[slot, :, D_A:D_A + D_B] = b_in


def _mixer_call(xs, xp, z, vpre, ext, wsg, bsg, ln_g, ln_b, conv_w, conv_b, wa, wb, wo, g_moe,
                wr_hi, wr_lo, b_r):
    tm = MIX_TM
    ns = MIX_NS
    nt = MIX_NT
    assert D_A == D_B and D_MODEL == 2 * D_A and z.shape[1] == 9 * D_A
    front = lambda m: jnp.minimum(m, nt - 1)
    back = lambda m: jnp.maximum(m - 1, 0)
    zcol = lambda tile_of, c: (lambda m: (tile_of(m), c))
    row = lambda m: (back(m), 0)
    s_out = lambda m: (jnp.minimum(front(m), ns), 0)
    p_out = lambda m: (jnp.where(front(m) < ns, nt - ns, front(m) - ns), 0, 0)
    sel = lambda m: jnp.minimum(front(m) // ns, 1)
    return pl.pallas_call(
        _mixer_kernel,
        grid=(nt + 1,),
        in_specs=[
            pl.BlockSpec((tm, D_MODEL), lambda m: (jnp.minimum(back(m), ns - 1), 0)),
            pl.BlockSpec((tm, D_MODEL), lambda m: (jnp.maximum(back(m) - ns, 0), 0)),
            pl.BlockSpec((tm, D_A), zcol(front, 0)),
            pl.BlockSpec((tm, D_A), zcol(front, 2)),
            pl.BlockSpec((tm, D_A), zcol(front, 3)),
            pl.BlockSpec((tm, D_A), zcol(front, 4)),
            pl.BlockSpec((tm, D_A), zcol(back, 5)),
            pl.BlockSpec((tm, D_A), zcol(back, 6)),
            pl.BlockSpec((tm, D_A), zcol(back, 7)),
            pl.BlockSpec((tm, D_A), zcol(back, 8)),
            pl.BlockSpec((tm, D_A), lambda m: (front(m), 0)),
            pl.BlockSpec((tm, D_B), lambda m: (jnp.minimum(front(m), ns - 1), 0)),
            pl.BlockSpec((1, N_GROUPS_A, CHUNK, CHUNK), lambda m: (sel(m), 0, 0, 0)),
            pl.BlockSpec((1, CHUNK, D_A), lambda m: (sel(m), 0, 0)),
            _resident((1, D_A)),
            _resident((1, D_A)),
            _resident((CONV_W, D_B)),
            _resident((1, D_B)),
            _resident((D_A, D_MODEL)),
            _resident((D_B, D_MODEL)),
            _resident((D_MODEL, D_MODEL)),
            _resident((1, D_MODEL)),
            _resident((D_MODEL, LANES)),
            _resident((D_MODEL, LANES)),
            _resident((1, LANES)),
        ],
        out_specs=[
            pl.BlockSpec((tm, D_MODEL), row),
            pl.BlockSpec((tm, D_MODEL), row),
            pl.BlockSpec((N_EXPERTS, tm), lambda m: (0, back(m))),
            pl.BlockSpec((tm, D_A), s_out),
            pl.BlockSpec((tm, D_B), s_out),
            pl.BlockSpec((1, 8, D_B), p_out),
        ],
        out_shape=[
            jax.ShapeDtypeStruct((T, D_MODEL), F32),
            jax.ShapeDtypeStruct((T, D_MODEL), F32),
            jax.ShapeDtypeStruct((N_EXPERTS, T), F32),
            jax.ShapeDtypeStruct((T_S + tm, D_A), F32),
            jax.ShapeDtypeStruct((T_S + tm, D_B), F32),
            jax.ShapeDtypeStruct((nt - ns + 1, 8, D_B), F32),
        ],
        scratch_shapes=[pltpu.VMEM((8, D_B), F32), pltpu.VMEM((2, tm, D_A + D_B), BF16)],
        compiler_params=_cparams(("arbitrary",), 52),
        name="mixer",
    )(xs, xp, z, z, z, z, z, z, z, z, vpre, ext, wsg, bsg, ln_g, ln_b, conv_w, conv_b, wa, wb, wo, g_moe,
      wr_hi, wr_lo, b_r)


MOE_CHUNK_SLOT0 = {8: 0, 4: 0, 2: 4, 1: 6}


def _for_units(n, unit_fn, chunk_begin=None):
    big = MOE_CHUNKS[0]

    def chunk(u0, count):
        slot0 = MOE_CHUNK_SLOT0[count]
        if chunk_begin is not None:
            chunk_begin(range(slot0, slot0 + count))
        for j in range(count):
            unit_fn(u0 + j, slot0 + j)

    def body(c, carry):
        chunk(c * big, big)
        return carry

    n_big = lax.shift_right_logical(n, big.bit_length() - 1)
    lax.fori_loop(0, n_big, body, 0)
    base = n_big * big
    for count in MOE_CHUNKS[1:]:
        @pl.when((n & count) != 0)
        def _():
            chunk(base, count)

        base = base + (n & count)


def _moe_kernel(tok_ref, tab_ref,
                xn_hbm, wgu_hbm, wd_hbm, bgu_hbm, bd_hbm,
                y_hbm,
                xraw_ref, xb_ref, act_ref, wst_ref, wbf_ref, bgu_ref, bd_ref,
                ystage_ref, zbuf_ref, gsem, ysem, zsem, wsem, bsem):
    i = pl.program_id(0)
    nblk = tab_ref[TAB_NBLK +i]
    row0 = tab_ref[TAB_ROW0 +i]
    valid = nblk > 0
    par = i & 1

    def gu_copies(e, tile, slot):
        col = pl.multiple_of(tile * MOE_TF, MOE_TF)
        return (pltpu.make_async_copy(wgu_hbm.at[e, :, pl.ds(col, MOE_TF)],
                                      wst_ref.at[slot, :, pl.ds(0, MOE_TF)], wsem.at[slot]),
                pltpu.make_async_copy(wgu_hbm.at[e, :, pl.ds(D_FF + col, MOE_TF)],
                                      wst_ref.at[slot, :, pl.ds(MOE_TF, MOE_TF)], wsem.at[slot]))

    def d_copy(e, tile, slot):
        col = pl.multiple_of(tile * MOE_TN, MOE_TN)
        return pltpu.make_async_copy(wd_hbm.at[e, :, pl.ds(col, MOE_TN)], wst_ref.at[slot], wsem.at[slot])

    def start_tile(e, t):
        slot = lax.rem(t, MOE_WSLOTS)

        @pl.when(t < MOE_P1)
        def _():
            for c in gu_copies(e, t, slot):
                c.start(priority=MOE_W_PRIORITY)

        @pl.when(t >= MOE_P1)
        def _():
            d_copy(e, t - MOE_P1, slot).start(priority=MOE_W_PRIORITY)

    def bias_copies(e, slot):
        return (pltpu.make_async_copy(bgu_hbm.at[e], bgu_ref.at[slot], bsem.at[slot]),
                pltpu.make_async_copy(bd_hbm.at[e], bd_ref.at[slot], bsem.at[slot]))

    def tail_copy(b):
        r = pl.multiple_of(b * MOE_RB, MOE_RB)
        return pltpu.make_async_copy(zbuf_ref, y_hbm.at[pl.ds(r, MOE_RB), :], zsem)

    def row_copy(tok, group, sub):
        return pltpu.make_async_copy(xn_hbm.at[pl.ds(tok, 1), :], xraw_ref.at[group, pl.ds(sub, 1), :], gsem)

    def unit_wait():
        blk = xraw_ref.at[pl.ds(0, MOE_RB // SUBLANES)]
        return pltpu.make_async_copy(blk, blk, gsem)

    def gather_rows(item, lo_unit, hi_unit):
        base = tab_ref[TAB_ROW0 +item]

        def body(c, carry):
            for j in range(SUBLANES):
                row_copy(tok_ref[base + c * SUBLANES + j], c, j).start()
            return carry

        per_unit = MOE_RB // SUBLANES
        lax.fori_loop(lo_unit * per_unit, hi_unit * per_unit, body, 0)

    def y_copy(slot, u, col):
        r = pl.multiple_of(row0 + u * MOE_RB, MOE_RB)
        return pltpu.make_async_copy(ystage_ref.at[slot], y_hbm.at[pl.ds(r, MOE_RB), pl.ds(col, MOE_TN)],
                                     ysem.at[slot])

    def dump_copy(slot):
        r = N_SLOTS + (slot // MOE_P2) * MOE_RB
        c = (slot % MOE_P2) * MOE_TN
        return pltpu.make_async_copy(ystage_ref.at[slot], y_hbm.at[pl.ds(r, MOE_RB), pl.ds(c, MOE_TN)],
                                     ysem.at[slot])

    @pl.when(i == 0)
    def _():
        for t in range(MOE_WAHEAD):
            for c in gu_copies(tab_ref[TAB_E +0], t, t):
                c.start(priority=MOE_W_PRIORITY)
        for c in bias_copies(tab_ref[TAB_E +0], 0):
            c.start()
        gather_rows(0, 0, tab_ref[TAB_NBLK +0])
        ystage_ref[...] = jnp.zeros_like(ystage_ref)
        for slot in range(MOE_YSLOTS):
            dump_copy(slot).start()
        zbuf_ref[...] = jnp.zeros_like(zbuf_ref)

        def fill(b, carry):
            tail_copy(b).start()
            return carry

        lax.fori_loop(tab_ref[TAB_USED], N_BLOCKS, fill, 0)

    def wait_rows(b, carry):
        unit_wait().wait()
        return carry

    lax.fori_loop(0, tab_ref[TAB_WAIT +i], wait_rows, 0)

    def conv(u, carry):
        rows = pl.ds(pl.multiple_of(u * MOE_RB, MOE_RB), MOE_RB)
        groups = pl.ds(u * (MOE_RB // SUBLANES), MOE_RB // SUBLANES)
        xb_ref[rows, :] = xraw_ref[groups].reshape(MOE_RB, D_MODEL).astype(BF16)
        return carry

    lax.fori_loop(0, nblk, conv, 0)

    e = tab_ref[TAB_E +i]
    e_next = tab_ref[TAB_E +i + 1]
    next_valid = tab_ref[TAB_NBLK +i + 1] > 0
    next_base = tab_ref[TAB_ROW0 +i + 1]

    @pl.when(valid)
    def _():
        for c in bias_copies(e, par):
            c.wait()

    def gate_up_step(s, carry):
        wslot = lax.rem(s, MOE_WSLOTS)
        for c in gu_copies(e, s, wslot):
            c.wait()
        start_tile(e, s + MOE_WAHEAD)
        wbf_ref[...] = wst_ref[wslot].astype(BF16)
        b_g = bgu_ref[par, pl.ds(s, 1), :]
        b_u = bgu_ref[par, pl.ds(MOE_P1 + s, 1), :]

        def unit(u, slot):
            r = pl.multiple_of(u * MOE_RB, MOE_RB)
            first = pl.multiple_of(r + s * MOE_G, MOE_G)
            group = lax.shift_right_logical(first, SUBLANES.bit_length() - 1)
            for j in range(MOE_G):
                row_copy(tok_ref[next_base + first + j], group + j // SUBLANES, j % SUBLANES).start()
            gu = jnp.dot(xb_ref[pl.ds(r, MOE_RB), :], wbf_ref[...], preferred_element_type=F32)
            gate = jnp.minimum(gu[:, 0:MOE_TF] + b_g, SWIGLU_LIMIT)
            up = jnp.clip(gu[:, MOE_TF:2 * MOE_TF] + b_u, -SWIGLU_LIMIT, SWIGLU_LIMIT)
            act = (up + 1) * (gate * jax.nn.sigmoid(gate * SWIGLU_ALPHA))
            act_ref[s, pl.ds(r, MOE_RB), :] = act.astype(BF16)

        _for_units(nblk, unit)
        return carry

    def down_step(s, carry):
        t = MOE_P1 + s
        wslot = lax.rem(t, MOE_WSLOTS)
        d_copy(e, s, wslot).wait()

        @pl.when(t + MOE_WAHEAD < MOE_TILES)
        def _():
            start_tile(e, t + MOE_WAHEAD)

        @pl.when((t + MOE_WAHEAD >= MOE_TILES) & next_valid)
        def _():
            start_tile(e_next, t + MOE_WAHEAD - MOE_TILES)

        @pl.when((t + MOE_WAHEAD == MOE_TILES) & next_valid)
        def _():
            for c in bias_copies(e_next, 1 - par):
                c.start()

        wbf_ref[...] = wst_ref[wslot].astype(BF16)
        b_d = bd_ref[par, pl.ds(s, 1), :]
        col = pl.multiple_of(s * MOE_TN, MOE_TN)

        def free_slots(slots):
            for slot in slots:
                y_copy(slot, 0, col).wait()

        def unit(u, slot):
            r = pl.multiple_of(u * MOE_RB, MOE_RB)
            a = jnp.concatenate([act_ref[j, pl.ds(r, MOE_RB), :] for j in range(MOE_P1)], axis=1)
            ystage_ref[slot] = jnp.dot(a, wbf_ref[...], preferred_element_type=F32) + b_d
            y_copy(slot, u, col).start()

        _for_units(nblk, unit, free_slots)
        return carry

    @pl.when(valid)
    def _():
        lax.fori_loop(0, MOE_P1, gate_up_step, 0)
        gather_rows(i + 1, nblk, tab_ref[TAB_NBLK +i + 1])
        lax.fori_loop(0, MOE_P2, down_step, 0)

    @pl.when(i == MOE_NI - 1)
    def _():
        for slot in range(MOE_YSLOTS):
            dump_copy(slot).wait()

        lax.fori_loop(0, tab_ref[TAB_WAIT +MOE_NI], wait_rows, 0)

        def drain(b, carry):
            tail_copy(b).wait()
            return carry

        lax.fori_loop(tab_ref[TAB_USED], N_BLOCKS, drain, 0)


def _moe_call(slot_tok, tables, xn, w_gate_up, w_down, b_gate_up, b_down):
    any_spec = pl.BlockSpec(memory_space=pl.ANY)
    grid_spec = pltpu.PrefetchScalarGridSpec(
        num_scalar_prefetch=2,
        grid=(MOE_NI,),
        in_specs=[any_spec] * 5,
        out_specs=any_spec,
        scratch_shapes=[
            pltpu.VMEM((MOE_RMAX // SUBLANES, SUBLANES, D_MODEL), F32),
            pltpu.VMEM((MOE_RMAX, D_MODEL), BF16),
            pltpu.VMEM((MOE_P1, MOE_RMAX, MOE_TF), BF16),
            pltpu.VMEM((MOE_WSLOTS, D_MODEL, MOE_TN), F32),
            pltpu.VMEM((D_MODEL, MOE_TN), BF16),
            pltpu.VMEM((2, 2 * MOE_P1, MOE_TF), F32),
            pltpu.VMEM((2, MOE_P2, MOE_TN), F32),
            pltpu.VMEM((MOE_YSLOTS, MOE_RB, MOE_TN), F32),
            pltpu.VMEM((MOE_RB, D_MODEL), F32),
            pltpu.SemaphoreType.DMA(()),
            pltpu.SemaphoreType.DMA((MOE_YSLOTS,)),
            pltpu.SemaphoreType.DMA(()),
            pltpu.SemaphoreType.DMA((MOE_WSLOTS,)),
            pltpu.SemaphoreType.DMA((2,)),
        ],
    )
    spare_blocks = MOE_YSLOTS // MOE_P2
    return pl.pallas_call(
        _moe_kernel,
        grid_spec=grid_spec,
        out_shape=jax.ShapeDtypeStruct((N_SLOTS + spare_blocks * MOE_RB, D_MODEL), F32),
        compiler_params=_cparams(("arbitrary",), 56),
        name="moe",
    )(slot_tok, tables, xn, w_gate_up, w_down, b_gate_up, b_down)


def _combine_kernel(dest_ref, y_hbm, x1_ref, rg_ref, ps_ref, pp_ref, wple_ref, wpg_ref, gple_ref,
                    gfin_ref, ys_ref, yp_ref, gbuf_ref, gsem):
    m = pl.program_id(0)
    nm = pl.num_programs(0)
    tm = CMB_TM
    slot = m % 2

    def row_copy(tile, slot_, r, k):
        d = dest_ref[k * T + tile * tm + r]
        return pltpu.make_async_copy(y_hbm.at[pl.ds(d, 1), :], gbuf_ref.at[slot_, k, pl.ds(r, 1), :],
                                     gsem.at[slot_])

    def wait_tile(slot_):
        for k in range(TOP_K):
            pltpu.make_async_copy(y_hbm.at[pl.ds(0, tm), :], gbuf_ref.at[slot_, k], gsem.at[slot_]).wait()

    @pl.when(m == 0)
    def _():
        def body(c, carry):
            for j in range(GATHER_UNROLL // TOP_K):
                for k in range(TOP_K):
                    row_copy(0, 0, c * (GATHER_UNROLL // TOP_K) + j, k).start()
            return carry

        lax.fori_loop(0, tm // (GATHER_UNROLL // TOP_K), body, 0)

    wait_tile(slot)
    gates = rg_ref[...]
    moe = gates[:, 0:1] * gbuf_ref[slot, 0]
    for k in range(1, TOP_K):
        moe = moe + gates[:, k:k + 1] * gbuf_ref[slot, k]
    x2 = x1_ref[...] + moe
    is_s = m < CMB_NS
    p = jnp.where(is_s, ps_ref[...], pp_ref[...]).astype(BF16)
    hn = _rms(x2, gple_ref[...]).astype(BF16)
    nxt = jnp.minimum(m + 1, nm - 1)
    rows_per_chunk = tm // CMB_CHUNKS
    cw = D_MODEL // CMB_CHUNKS
    x3_parts = []
    for c in range(CMB_CHUNKS):
        cols = slice(c * cw, (c + 1) * cw)
        pe = jnp.dot(p, wple_ref[:, cols], preferred_element_type=F32)
        gate = jax.nn.sigmoid(jnp.dot(hn, wpg_ref[:, cols], preferred_element_type=F32))
        x3_parts.append(x2[:, cols] + pe * gate)
        for r in range(c * rows_per_chunk, (c + 1) * rows_per_chunk):
            for k in range(TOP_K):
                row_copy(nxt, 1 - slot, r, k).start()
    x3 = jnp.concatenate(x3_parts, axis=1)
    y = _rms(x3, gfin_ref[...])

    @pl.when(is_s)
    def _():
        ys_ref[...] = y

    @pl.when(jnp.logical_not(is_s))
    def _():
        yp_ref[...] = y

    @pl.when(m == nm - 1)
    def _():
        wait_tile(1 - slot)


def _combine_call(dest, y_sorted, x1, rg, ps, pp, wple, wpg, g_ple, g_final):
    tm = CMB_TM
    ns = CMB_NS
    s_idx = lambda m, d: (jnp.minimum(m, ns - 1), 0)
    p_idx = lambda m, d: (jnp.maximum(m - ns, 0), 0)
    row = lambda m, d: (m, 0)
    const2 = lambda m, d: (0, 0)
    grid_spec = pltpu.PrefetchScalarGridSpec(
        num_scalar_prefetch=1,
        grid=(T // tm,),
        in_specs=[
            pl.BlockSpec(memory_space=pl.ANY),
            pl.BlockSpec((tm, D_MODEL), row),
            pl.BlockSpec((tm, LANES), row),
            pl.BlockSpec((tm, PLE_DIM), s_idx),
            pl.BlockSpec((tm, PLE_DIM), p_idx),
            pl.BlockSpec((PLE_DIM, D_MODEL), const2),
            pl.BlockSpec((D_MODEL, D_MODEL), const2),
            pl.BlockSpec((1, D_MODEL), const2),
            pl.BlockSpec((1, D_MODEL), const2),
        ],
        out_specs=[
            pl.BlockSpec((tm, D_MODEL), s_idx),
            pl.BlockSpec((tm, D_MODEL), p_idx),
        ],
        scratch_shapes=[
            pltpu.VMEM((2, TOP_K, tm, D_MODEL), F32),
            pltpu.SemaphoreType.DMA((2,)),
        ],
    )
    return pl.pallas_call(
        _combine_kernel,
        grid_spec=grid_spec,
        out_shape=[
            jax.ShapeDtypeStruct((T_S, D_MODEL), F32),
            jax.ShapeDtypeStruct((T_P, D_MODEL), F32),
        ],
        compiler_params=_cparams(("arbitrary",), 56),
        name="combine",
    )(dest, y_sorted, x1, rg, ps, pp, wple, wpg, g_ple, g_final)


def _route_kernel(lgt_ref, dest_ref, tok_ref, tab_ref, rg_ref,
                  rit_ref, dvm_ref, zvm_ref, carry_ref, cnt_ref, gs_ref, sem):
    rb_shift = MOE_RB.bit_length() - 1
    tb = ROUTE_TB
    sub = tb // LANES

    a_i = lax.broadcasted_iota(I32, (LANES, LANES), 0)
    b_i = lax.broadcasted_iota(I32, (LANES, LANES), 1)
    earlier = jnp.where(a_i < b_i, 1.0, 0.0).astype(BF16)
    carry_ref[...] = jnp.zeros_like(carry_ref)

    def tile(c, loop_carry):
        c0 = pl.multiple_of(c * tb, tb)
        work = lgt_ref[:, pl.ds(c0, tb)]
        row = lax.broadcasted_iota(I32, (N_EXPERTS, tb), 0).astype(F32)
        vals, idxs, hots = [], [], []
        for _ in range(TOP_K):
            mx = jnp.max(work, axis=0, keepdims=True)
            idx = jnp.min(jnp.where(work == mx, row, float(N_EXPERTS)), axis=0, keepdims=True)
            hot = row == idx
            work = jnp.where(hot, -jnp.inf, work)
            vals.append(mx)
            idxs.append(idx)
            hots.append(hot)
        exps = [jnp.exp(vk - vals[0]) for vk in vals]
        den = exps[0] + exps[1] + exps[2] + exps[3]
        chosen = jnp.zeros((N_EXPERTS, tb), F32)
        for hot in hots:
            chosen = chosen + jnp.where(hot, 1.0, 0.0)
        counts = carry_ref[...]
        before = []
        for j in range(sub):
            cj = chosen[:, j * LANES:(j + 1) * LANES]
            before.append(jnp.dot(cj.astype(BF16), earlier, preferred_element_type=F32) + counts)
            counts = counts + jnp.sum(cj, axis=1, keepdims=True)
        carry_ref[...] = counts
        before = jnp.concatenate(before, axis=1)
        for k in range(TOP_K):
            rank = jnp.sum(jnp.where(hots[k], before, 0.0), axis=0, keepdims=True)
            rit_ref[k:k + 1, pl.ds(c0, tb)] = idxs[k].astype(I32)
            rit_ref[TOP_K + k:TOP_K + k + 1, pl.ds(c0, tb)] = rank.astype(I32)
        gates = jnp.concatenate([ek / den for ek in exps], axis=0)
        for j in range(sub):
            g_tile = jnp.concatenate([gates[:, j * LANES:(j + 1) * LANES],
                                      jnp.zeros((LANES - TOP_K, LANES), F32)], axis=0)
            rg_ref[pl.ds(pl.multiple_of(c0 + j * LANES, LANES), LANES), :] = jnp.transpose(g_tile)
        return loop_carry

    lax.fori_loop(0, T // tb, tile, 0)
    cnt_copy = pltpu.make_async_copy(carry_ref, cnt_ref, sem)
    cnt_copy.start()
    cnt_copy.wait()

    def clear(j, carry):
        tab_ref[j] = 0
        return carry

    lax.fori_loop(0, TAB_SIZE, clear, 0)

    def expert(e, carry):
        acc, item, used = carry
        n = lax.shift_right_logical(cnt_ref[e, 0].astype(I32) + (MOE_RB - 1), rb_shift)
        gs_ref[e] = acc

        def add_item(local, it):
            tab_ref[TAB_E + it] = e
            tab_ref[TAB_ROW0 + it] = acc + local * MOE_RMAX
            tab_ref[TAB_NBLK + it] = jnp.minimum(MOE_BMAX, n - local * MOE_BMAX)
            return it + 1

        item = lax.fori_loop(0, lax.div(n + (MOE_BMAX - 1), MOE_BMAX), add_item, item)
        return acc + n * MOE_RB, item, used + n

    _, n_items, used = lax.fori_loop(0, N_EXPERTS, expert, (jnp.int32(0), jnp.int32(0), jnp.int32(0)))
    tab_ref[TAB_USED] = used
    e_last = tab_ref[TAB_E + n_items - 1]

    def pad_item(it, carry):
        tab_ref[TAB_E + it] = e_last
        tab_ref[TAB_ROW0 + it] = 0
        tab_ref[TAB_NBLK + it] = 0
        return carry

    lax.fori_loop(n_items, MOE_NI + 1, pad_item, 0)

    def wait_units(it, prev):
        nb = tab_ref[TAB_NBLK + it]
        tab_ref[TAB_WAIT + it] = jnp.maximum(nb, prev)
        return nb

    lax.fori_loop(0, MOE_NI + 1, wait_units, jnp.int32(0))

    e_idx = rit_ref[0:TOP_K, :]
    d = rit_ref[TOP_K:2 * TOP_K, :]
    for e in range(N_EXPERTS):
        d = d + jnp.where(e_idx == e, gs_ref[e], 0)
    dvm_ref[0:TOP_K, :] = d
    zvm_ref[...] = jnp.zeros_like(zvm_ref)
    copies = [pltpu.make_async_copy(dvm_ref.at[k], dest_ref.at[pl.ds(k * T, T)], sem) for k in range(TOP_K)]
    copies.append(pltpu.make_async_copy(zvm_ref, tok_ref, sem))
    for c in copies:
        c.start()
    for c in copies:
        c.wait()

    for k in range(TOP_K):
        def scatter(c, carry):
            for j in range(GATHER_UNROLL):
                t = c * GATHER_UNROLL + j
                tok_ref[dest_ref[k * T + t]] = t
            return carry

        lax.fori_loop(0, T // GATHER_UNROLL, scatter, 0)


def _route_call(logits_t):
    smem = pl.BlockSpec(memory_space=pltpu.SMEM)
    vmem = pl.BlockSpec(memory_space=pltpu.VMEM)
    return pl.pallas_call(
        _route_kernel,
        in_specs=[vmem],
        out_specs=[smem, smem, smem, vmem],
        out_shape=[
            jax.ShapeDtypeStruct((TOP_K * T,), I32),
            jax.ShapeDtypeStruct((N_TOK_TAB,), I32),
            jax.ShapeDtypeStruct((TAB_SIZE,), I32),
            jax.ShapeDtypeStruct((T, LANES), F32),
        ],
        scratch_shapes=[
            pltpu.VMEM((2 * TOP_K, T), I32),
            pltpu.VMEM((2 * TOP_K, T), I32),
            pltpu.VMEM((N_TOK_TAB,), I32),
            pltpu.VMEM((N_EXPERTS, LANES), F32),
            pltpu.SMEM((N_EXPERTS, LANES), F32),
            pltpu.SMEM((N_EXPERTS,), I32),
            pltpu.SemaphoreType.DMA(()),
        ],
        compiler_params=pltpu.CompilerParams(vmem_limit_bytes=32 * MIB),
        name="route",
    )(logits_t)


def kernel(x_prompt, x_sample, state_conv, p_prompt, p_sample, g_mix, w_in, ln_v_g, ln_v_b, w_s, b_s,
           conv_w, conv_b, w_proj_a, w_proj_b, w_o, g_moe, w_router, b_router, w_gate_up, b_gate_up,
           w_down, b_down, g_ple, w_ple, w_ple_gate, g_final):
    assert g_mix.shape[0] == 1, "one layer"
    xs = x_sample.reshape(T_S, D_MODEL)
    xp = x_prompt.reshape(T_P, D_MODEL)

    tril = jnp.tril(jnp.ones((CHUNK, CHUNK), bool))
    w_prompt = jnp.where(tril[None], w_s[0], 0.0)
    small = jnp.where(tril[None, :DEC_SEQ, :DEC_SEQ], w_s[0, :, :DEC_SEQ, :DEC_SEQ], 0.0)
    reps = CHUNK // DEC_SEQ
    blockdiag = jnp.kron(jnp.eye(reps, dtype=F32), jnp.ones((DEC_SEQ, DEC_SEQ), F32))
    w_sample = jnp.tile(small, (1, reps, reps)) * blockdiag[None]
    wsg = jnp.stack([w_sample, w_prompt]).astype(BF16)
    bias_p = jnp.repeat(b_s[0].T, GW_A, axis=1)
    bias_s = jnp.tile(jnp.repeat(b_s[0, :, :DEC_SEQ].T, GW_A, axis=1), (reps, 1))
    bsg = jnp.stack([bias_s, bias_p])
    ext = jnp.pad(state_conv[0], ((0, 0), (0, DEC_SEQ - (CONV_W - 1)), (0, 0))).reshape(T_S, D_B)

    wr = jnp.pad(w_router[0], ((0, 0), (0, LANES - N_EXPERTS)))
    wr_hi = wr.astype(BF16)
    wr_lo = (wr - wr_hi.astype(F32)).astype(BF16)
    b_r = jnp.pad(b_router[0], (0, LANES - N_EXPERTS), constant_values=NEG_BIG).reshape(1, LANES)

    h = _norm_call(xs, xp, g_mix)
    z, vpre = _in_proj_call(h, w_in[0])
    x1, xn, logits_t, vln, cxs, tail = _mixer_call(
        xs, xp, z, vpre, ext, wsg, bsg, ln_v_g, ln_v_b, conv_w[0], conv_b,
        w_proj_a[0].astype(BF16), w_proj_b[0].astype(BF16), w_o[0].astype(BF16), g_moe,
        wr_hi, wr_lo, b_r)

    dest, slot_tok, tables, route_g = _route_call(logits_t)
    y_sorted = _moe_call(slot_tok, tables, xn, w_gate_up[0], w_down[0],
                         b_gate_up[0].reshape(N_EXPERTS, 2 * MOE_P1, MOE_TF),
                         b_down[0].reshape(N_EXPERTS, MOE_P2, MOE_TN))
    ys, yp = _combine_call(dest, y_sorted, x1, route_g,
                           p_sample[0].reshape(T_S, PLE_DIM), p_prompt[0].reshape(T_P, PLE_DIM),
                           w_ple[0].astype(BF16), w_ple_gate[0].astype(BF16), g_ple, g_final.reshape(1, D_MODEL))

    y_prompt = yp.reshape(BATCH, SEQ, D_MODEL)
    y_sample = ys.reshape(DEC_BATCH, DEC_SEQ, D_MODEL)
    last = tail[:MIX_NT - MIX_NS].reshape(BATCH, MIX_SEQ_TILES, 8, D_B)[:, -1, 8 - (CONV_W - 1):, :]
    state_conv_prompt = last[None]
    state_conv_sample = cxs[:T_S].reshape(DEC_BATCH, DEC_SEQ, D_B)[:, DEC_SEQ - (CONV_W - 1):, :][None]
    state_chunk_v_sample = vln[:T_S].reshape(DEC_BATCH, DEC_SEQ, D_A)[None]
    return (y_prompt, y_sample, state_conv_prompt, state_conv_sample, state_chunk_v_sample)
```

```python
import functools

import jax
import jax.numpy as jnp
from jax import lax
from jax.experimental import pallas as pl
from jax.experimental.pallas import tpu as pltpu

F32 = jnp.float32
BF16 = jnp.bfloat16
I32 = jnp.int32

D_MODEL = 2048
BATCH = 4
SEQ = 2048
DEC_BATCH = 128
DEC_SEQ = 8
CHUNK = 128
D_A = D_MODEL // 2
N_GROUPS_A = 8
GW_A = D_A // N_GROUPS_A
D_B = D_MODEL // 2
CONV_W = 3
N_EXPERTS = 32
TOP_K = 4
D_FF = D_MODEL
SWIGLU_LIMIT = 7.0
SWIGLU_ALPHA = 1.702
PLE_DIM = 256
EPS = 1e-6

T_S = DEC_BATCH * DEC_SEQ
T_P = BATCH * SEQ
T = T_S + T_P

LANES = 128
SUBLANES = 8
V7X_VMEM_BYTES = 64 * 1024 * 1024
MIB = 1024 * 1024

NORM_TM = 512
IN_TM = 1024
IN_TN = 1024
IN_SUB = 256
MIX_TM = CHUNK
MIX_NS = T_S // MIX_TM
MIX_SEQ_TILES = SEQ // MIX_TM
MIX_NT = T // MIX_TM
MOE_RB = 128
MOE_BMAX = 14
MOE_RMAX = MOE_RB * MOE_BMAX
MOE_CHUNK = 8
MOE_TF = 256
MOE_TN = 512
MOE_P1 = D_FF // MOE_TF
MOE_P2 = D_MODEL // MOE_TN
MOE_TILES = MOE_P1 + MOE_P2
MOE_W_PRIORITY = 1
MOE_WSLOTS = 3
MOE_WAHEAD = MOE_WSLOTS - 1
assert D_MODEL == D_FF and 2 * MOE_TF == MOE_TN and MOE_TILES % MOE_WSLOTS == 0
MOE_G = MOE_RB // MOE_P1
MOE_YSLOTS = 8
N_SLOTS = T * TOP_K + N_EXPERTS * MOE_RB
N_BLOCKS = N_SLOTS // MOE_RB
MOE_NI = (N_BLOCKS + N_EXPERTS * (MOE_BMAX - 1)) // MOE_BMAX
GATHER_UNROLL = 8
TAB_STRIDE = 64
TAB_E, TAB_ROW0, TAB_NBLK, TAB_WAIT, TAB_USED = 0, TAB_STRIDE, 2 * TAB_STRIDE, 3 * TAB_STRIDE, 4 * TAB_STRIDE
TAB_SIZE = 5 * TAB_STRIDE
assert MOE_NI + 1 <= TAB_STRIDE
N_TOK_TAB = -(-(N_SLOTS + MOE_RMAX) // 1024) * 1024
ROUTE_TB = 512
CMB_TM = 256
CMB_NS = T_S // CMB_TM
CMB_CHUNKS = 8
NEG_BIG = -1e30


def _rms(x, g):
    return x * lax.rsqrt(jnp.mean(x * x, axis=-1, keepdims=True) + EPS) * g


def _cparams(sem, vmem_mib):
    return pltpu.CompilerParams(dimension_semantics=sem, vmem_limit_bytes=vmem_mib * MIB)


def _resident(shape):
    zeros = (0,) * len(shape)
    return pl.BlockSpec(shape, lambda *_: zeros, pipeline_mode=pl.Buffered(1))


def _norm_kernel(xs_ref, xp_ref, g_ref, h_ref, *, ns):
    m = pl.program_id(0)
    x = jnp.where(m < ns, xs_ref[...], xp_ref[...])
    h_ref[...] = _rms(x, g_ref[...]).astype(BF16)


def _norm_call(xs, xp, g):
    ns = T_S // NORM_TM
    return pl.pallas_call(
        functools.partial(_norm_kernel, ns=ns),
        grid=(T // NORM_TM,),
        in_specs=[
            pl.BlockSpec((NORM_TM, D_MODEL), lambda m: (jnp.minimum(m, ns - 1), 0)),
            pl.BlockSpec((NORM_TM, D_MODEL), lambda m: (jnp.maximum(m - ns, 0), 0)),
            pl.BlockSpec((1, D_MODEL), lambda m: (0, 0)),
        ],
        out_specs=pl.BlockSpec((NORM_TM, D_MODEL), lambda m: (m, 0)),
        out_shape=jax.ShapeDtypeStruct((T, D_MODEL), BF16),
        compiler_params=_cparams(("arbitrary",), 32),
        name="norm",
    )(xs, xp, g)


IN_N_GELU = 2 * D_A // IN_TN
IN_N_V0 = D_A // IN_TN
IN_N_LIN = (2 * D_A + 3 * D_B) // IN_TN


def _in_proj_kernel(h_ref, w_ref, z_ref, vpre_ref, wb_ref):
    n = pl.program_id(0)

    @pl.when(pl.program_id(1) == 0)
    def _():
        wb_ref[...] = w_ref[...].astype(BF16)

    def blocks(epilogue):
        for b in range(IN_TM // IN_SUB):
            rows = pl.ds(b * IN_SUB, IN_SUB)
            epilogue(rows, jnp.dot(h_ref[rows, :], wb_ref[...], preferred_element_type=F32))

    @pl.when(n < IN_N_V0)
    def _():
        def ep(rows, acc):
            z_ref[rows, :] = jax.nn.gelu(acc, approximate=True).astype(BF16)

        blocks(ep)

    @pl.when((n >= IN_N_V0) & (n < IN_N_GELU))
    def _():
        def ep(rows, acc):
            g = jax.nn.gelu(acc, approximate=True)
            z_ref[rows, :] = g.astype(BF16)
            vpre_ref[rows, :] = g

        blocks(ep)

    @pl.when((n >= IN_N_GELU) & (n < IN_N_LIN))
    def _():
        def ep(rows, acc):
            z_ref[rows, :] = acc.astype(BF16)

        blocks(ep)

    @pl.when(n >= IN_N_LIN)
    def _():
        def ep(rows, acc):
            z_ref[rows, :] = jax.nn.sigmoid(acc).astype(BF16)

        blocks(ep)


def _in_proj_call(h, w_in):
    d_in = w_in.shape[1]
    n_m = T // IN_TM

    def vpre_map(n, m):
        row = jnp.where(n < IN_N_V0, 0, jnp.where(n < IN_N_GELU, m, n_m - 1))
        return (row, jnp.clip(n - IN_N_V0, 0, IN_N_GELU - IN_N_V0 - 1))

    return pl.pallas_call(
        _in_proj_kernel,
        grid=(d_in // IN_TN, n_m),
        in_specs=[
            pl.BlockSpec((IN_TM, D_MODEL), lambda n, m: (m, 0)),
            pl.BlockSpec((D_MODEL, IN_TN), lambda n, m: (0, n)),
        ],
        out_specs=[
            pl.BlockSpec((IN_TM, IN_TN), lambda n, m: (m, n)),
            pl.BlockSpec((IN_TM, IN_TN), vpre_map),
        ],
        out_shape=[
            jax.ShapeDtypeStruct((T, d_in), BF16),
            jax.ShapeDtypeStruct((T, D_A), F32),
        ],
        scratch_shapes=[pltpu.VMEM((D_MODEL, IN_TN), BF16)],
        compiler_params=_cparams(("arbitrary", "arbitrary"), 48),
        name="in_proj",
    )(h, w_in)


def _mixer_kernel(xs_ref, xp_ref, zu_ref, zb_ref, zc_ref, zx_ref, ga0_ref, ga1_ref, gb0_ref, gb1_ref,
                  vpre_ref, ext_ref, wsg_ref, bsg_ref, lng_ref, lnb_ref,
                  cw_ref, cb_ref, wa_ref, wb_ref, wo_ref, gmoe_ref, wrh_ref, wrl_ref, br_ref,
                  x1_ref, xn_ref, lgt_ref, vln_ref, cxs_ref, tail_ref,
                  prev_ref, ab_ref):
    m = pl.program_id(0)
    tm = MIX_TM
    a_tile = jnp.minimum(m, MIX_NT - 1)
    is_s = a_tile < MIX_NS
    b_is_s = (m - 1) < MIX_NS
    slot = m & 1

    @pl.when(m == 0)
    def _():
        prev_ref[...] = jnp.zeros_like(prev_ref)
        ab_ref[...] = jnp.zeros_like(ab_ref)

    y_a = jnp.dot(ab_ref[1 - slot, :, 0:D_A], wa_ref[...], preferred_element_type=F32)
    y_b = jnp.dot(ab_ref[1 - slot, :, D_A:D_A + D_B], wb_ref[...], preferred_element_type=F32)

    vg = vpre_ref[...]
    mu = jnp.mean(vg, axis=-1, keepdims=True)
    vc = vg - mu
    v = vc * lax.rsqrt(jnp.mean(vc * vc, axis=-1, keepdims=True) + EPS) * lng_ref[...] + lnb_ref[...]
    vln_ref[...] = v

    vb = v.astype(BF16)
    s_parts = []
    for g in range(N_GROUPS_A):
        s_parts.append(jnp.dot(wsg_ref[0, g], vb[:, g * GW_A:(g + 1) * GW_A], preferred_element_type=F32))
    s = jnp.concatenate(s_parts, axis=1) + bsg_ref[0]
    u = zu_ref[...].astype(F32)
    a_in = (u * s).astype(BF16)

    ga = jnp.concatenate([ga0_ref[...], ga1_ref[...]], axis=1).astype(F32)
    gb = jnp.concatenate([gb0_ref[...], gb1_ref[...]], axis=1).astype(F32)
    mix = (ga * y_a + gb * y_b).astype(BF16)
    x = jnp.where(b_is_s, xs_ref[...], xp_ref[...])
    x1 = x + jnp.dot(mix, wo_ref[...], preferred_element_type=F32)
    x1_ref[...] = x1

    bg = zb_ref[...].astype(F32)
    cg = zc_ref[...].astype(F32)
    xin = zx_ref[...].astype(F32)
    cx = cg * xin
    cxs_ref[...] = cx
    tail_ref[0] = cx[tm - 8:tm]

    row = lax.broadcasted_iota(I32, (tm, D_B), 0)
    seq_start = ((a_tile - MIX_NS) % MIX_SEQ_TILES) == 0
    prev = jnp.where(seq_start, 0.0, prev_ref[...])
    row8 = lax.broadcasted_iota(I32, (8, D_B), 0)
    top = jnp.where(row8 < CONV_W - 1, pltpu.roll(prev, CONV_W - 1, 0), 0.0)
    ext_p = jnp.concatenate([top, jnp.zeros((tm - 8, D_B), F32)], axis=0)
    ext = jnp.where(is_s, ext_ref[...], ext_p)
    t_in = jnp.where(is_s, row & (DEC_SEQ - 1), row)
    s1 = jnp.where(t_in < 1, pltpu.roll(ext, tm - 1, 0), pltpu.roll(cx, 1, 0))
    s2 = jnp.where(t_in < 2, ext, pltpu.roll(cx, 2, 0))
    prev_ref[...] = cx[tm - 8:tm]
    conv = cb_ref[...] + s2 * cw_ref[0:1, :] + s1 * cw_ref[1:2, :] + cx * cw_ref[2:3, :]
    b_in = (bg * conv).astype(BF16)

    xn = _rms(x1, gmoe_ref[...])
    xn_ref[...] = xn
    hi = xn.astype(BF16)
    lo = (xn - hi.astype(F32)).astype(BF16)
    logits = (jnp.dot(hi, wrh_ref[...], preferred_element_type=F32)
              + jnp.dot(lo, wrh_ref[...], preferred_element_type=F32)
              + jnp.dot(hi, wrl_ref[...], preferred_element_type=F32)) + br_ref[...]

    lgt_ref[...] = jnp.transpose(logits)[0:N_EXPERTS, :]

    ab_ref[slot, :, 0:D_A] = a_in
    ab_ref[slot, :, D_A:D_A + D_B] = b_in


def _mixer_call(xs, xp, z, vpre, ext, wsg, bsg, ln_g, ln_b, conv_w, conv_b, wa, wb, wo, g_moe,
                wr_hi, wr_lo, b_r):
    tm = MIX_TM
    ns = MIX_NS
    nt = MIX_NT
    assert D_A == D_B and D_MODEL == 2 * D_A and z.shape[1] == 9 * D_A
    front = lambda m: jnp.minimum(m, nt - 1)
    back = lambda m: jnp.maximum(m - 1, 0)
    zcol = lambda tile_of, c: (lambda m: (tile_of(m), c))
    row = lambda m: (back(m), 0)
    s_out = lambda m: (jnp.minimum(front(m), ns), 0)
    p_out = lambda m: (jnp.where(front(m) < ns, nt - ns, front(m) - ns), 0, 0)
    sel = lambda m: jnp.minimum(front(m) // ns, 1)
    return pl.pallas_call(
        _mixer_kernel,
        grid=(nt + 1,),
        in_specs=[
            pl.BlockSpec((tm, D_MODEL), lambda m: (jnp.minimum(back(m), ns - 1), 0)),
            pl.BlockSpec((tm, D_MODEL), lambda m: (jnp.maximum(back(m) - ns, 0), 0)),
            pl.BlockSpec((tm, D_A), zcol(front, 0)),
            pl.BlockSpec((tm, D_A), zcol(front, 2)),
            pl.BlockSpec((tm, D_A), zcol(front, 3)),
            pl.BlockSpec((tm, D_A), zcol(front, 4)),
            pl.BlockSpec((tm, D_A), zcol(back, 5)),
            pl.BlockSpec((tm, D_A), zcol(back, 6)),
            pl.BlockSpec((tm, D_A), zcol(back, 7)),
            pl.BlockSpec((tm, D_A), zcol(back, 8)),
            pl.BlockSpec((tm, D_A), lambda m: (front(m), 0)),
            pl.BlockSpec((tm, D_B), lambda m: (jnp.minimum(front(m), ns - 1), 0)),
            pl.BlockSpec((1, N_GROUPS_A, CHUNK, CHUNK), lambda m: (sel(m), 0, 0, 0)),
            pl.BlockSpec((1, CHUNK, D_A), lambda m: (sel(m), 0, 0)),
            _resident((1, D_A)),
            _resident((1, D_A)),
            _resident((CONV_W, D_B)),
            _resident((1, D_B)),
            _resident((D_A, D_MODEL)),
            _resident((D_B, D_MODEL)),
            _resident((D_MODEL, D_MODEL)),
            _resident((1, D_MODEL)),
            _resident((D_MODEL, LANES)),
            _resident((D_MODEL, LANES)),
            _resident((1, LANES)),
        ],
        out_specs=[
            pl.BlockSpec((tm, D_MODEL), row),
            pl.BlockSpec((tm, D_MODEL), row),
            pl.BlockSpec((N_EXPERTS, tm), lambda m: (0, back(m))),
            pl.BlockSpec((tm, D_A), s_out),
            pl.BlockSpec((tm, D_B), s_out),
            pl.BlockSpec((1, 8, D_B), p_out),
        ],
        out_shape=[
            jax.ShapeDtypeStruct((T, D_MODEL), F32),
            jax.ShapeDtypeStruct((T, D_MODEL), F32),
            jax.ShapeDtypeStruct((N_EXPERTS, T), F32),
            jax.ShapeDtypeStruct((T_S + tm, D_A), F32),
            jax.ShapeDtypeStruct((T_S + tm, D_B), F32),
            jax.ShapeDtypeStruct((nt - ns + 1, 8, D_B), F32),
        ],
        scratch_shapes=[pltpu.VMEM((8, D_B), F32), pltpu.VMEM((2, tm, D_A + D_B), BF16)],
        compiler_params=_cparams(("arbitrary",), 52),
        name="mixer",
    )(xs, xp, z, z, z, z, z, z, z, z, vpre, ext, wsg, bsg, ln_g, ln_b, conv_w, conv_b, wa, wb, wo, g_moe,
      wr_hi, wr_lo, b_r)


def _for_units(n, unit_fn, chunk_begin=None):
    def chunk(u0, count):
        if chunk_begin is not None:
            chunk_begin(range(count))
        for j in range(count):
            unit_fn(u0 + j, j)

    def body(c, carry):
        chunk(c * MOE_CHUNK, MOE_CHUNK)
        return carry

    n_big = lax.shift_right_logical(n, MOE_CHUNK.bit_length() - 1)
    lax.fori_loop(0, n_big, body, 0)
    base = n_big * MOE_CHUNK
    rest = n & (MOE_CHUNK - 1)
    for count in range(1, MOE_CHUNK):
        @pl.when(rest == count)
        def _():
            chunk(base, count)


def _moe_kernel(tok_ref, tab_ref,
                xn_hbm, wgu_hbm, wd_hbm, bgu_hbm, bd_hbm,
                y_hbm,
                xraw_ref, xb_ref, act_ref, wst_ref, wbf_ref, bgu_ref, bd_ref,
                ystage_ref, zbuf_ref, gsem, ysem, zsem, wsem, bsem):
    i = pl.program_id(0)
    nblk = tab_ref[TAB_NBLK +i]
    row0 = tab_ref[TAB_ROW0 +i]
    valid = nblk > 0
    par = i & 1

    def gu_copies(e, tile, slot):
        col = pl.multiple_of(tile * MOE_TF, MOE_TF)
        return (pltpu.make_async_copy(wgu_hbm.at[e, :, pl.ds(col, MOE_TF)],
                                      wst_ref.at[slot, :, pl.ds(0, MOE_TF)], wsem.at[slot]),
                pltpu.make_async_copy(wgu_hbm.at[e, :, pl.ds(D_FF + col, MOE_TF)],
                                      wst_ref.at[slot, :, pl.ds(MOE_TF, MOE_TF)], wsem.at[slot]))

    def d_copy(e, tile, slot):
        col = pl.multiple_of(tile * MOE_TN, MOE_TN)
        return pltpu.make_async_copy(wd_hbm.at[e, :, pl.ds(col, MOE_TN)], wst_ref.at[slot], wsem.at[slot])

    def start_tile(e, t):
        slot = lax.rem(t, MOE_WSLOTS)

        @pl.when(t < MOE_P1)
        def _():
            for c in gu_copies(e, t, slot):
                c.start(priority=MOE_W_PRIORITY)

        @pl.when(t >= MOE_P1)
        def _():
            d_copy(e, t - MOE_P1, slot).start(priority=MOE_W_PRIORITY)

    def bias_copies(e, slot):
        return (pltpu.make_async_copy(bgu_hbm.at[e], bgu_ref.at[slot], bsem.at[slot]),
                pltpu.make_async_copy(bd_hbm.at[e], bd_ref.at[slot], bsem.at[slot]))

    def tail_copy(b):
        r = pl.multiple_of(b * MOE_RB, MOE_RB)
        return pltpu.make_async_copy(zbuf_ref, y_hbm.at[pl.ds(r, MOE_RB), :], zsem)

    def row_copy(tok, group, sub):
        return pltpu.make_async_copy(xn_hbm.at[pl.ds(tok, 1), :], xraw_ref.at[group, pl.ds(sub, 1), :], gsem)

    def unit_wait():
        blk = xraw_ref.at[pl.ds(0, MOE_RB // SUBLANES)]
        return pltpu.make_async_copy(blk, blk, gsem)

    def gather_rows(item, lo_unit, hi_unit):
        base = tab_ref[TAB_ROW0 +item]

        def body(c, carry):
            for j in range(SUBLANES):
                row_copy(tok_ref[base + c * SUBLANES + j], c, j).start()
            return carry

        per_unit = MOE_RB // SUBLANES
        lax.fori_loop(lo_unit * per_unit, hi_unit * per_unit, body, 0)

    def y_copy(slot, u, col):
        r = pl.multiple_of(row0 + u * MOE_RB, MOE_RB)
        return pltpu.make_async_copy(ystage_ref.at[slot], y_hbm.at[pl.ds(r, MOE_RB), pl.ds(col, MOE_TN)],
                                     ysem.at[slot])

    def dump_copy(slot):
        r = N_SLOTS + (slot // MOE_P2) * MOE_RB
        c = (slot % MOE_P2) * MOE_TN
        return pltpu.make_async_copy(ystage_ref.at[slot], y_hbm.at[pl.ds(r, MOE_RB), pl.ds(c, MOE_TN)],
                                     ysem.at[slot])

    @pl.when(i == 0)
    def _():
        for t in range(MOE_WAHEAD):
            for c in gu_copies(tab_ref[TAB_E +0], t, t):
                c.start(priority=MOE_W_PRIORITY)
        for c in bias_copies(tab_ref[TAB_E +0], 0):
            c.start()
        gather_rows(0, 0, tab_ref[TAB_NBLK +0])
        ystage_ref[...] = jnp.zeros_like(ystage_ref)
        for slot in range(MOE_YSLOTS):
            dump_copy(slot).start()
        zbuf_ref[...] = jnp.zeros_like(zbuf_ref)

        def fill(b, carry):
            tail_copy(b).start()
            return carry

        lax.fori_loop(tab_ref[TAB_USED], N_BLOCKS, fill, 0)

    def wait_rows(b, carry):
        unit_wait().wait()
        return carry

    lax.fori_loop(0, tab_ref[TAB_WAIT +i], wait_rows, 0)

    def conv(u, carry):
        rows = pl.ds(pl.multiple_of(u * MOE_RB, MOE_RB), MOE_RB)
        groups = pl.ds(u * (MOE_RB // SUBLANES), MOE_RB // SUBLANES)
        xb_ref[rows, :] = xraw_ref[groups].reshape(MOE_RB, D_MODEL).astype(BF16)
        return carry

    lax.fori_loop(0, nblk, conv, 0)

    e = tab_ref[TAB_E +i]
    e_next = tab_ref[TAB_E +i + 1]
    next_valid = tab_ref[TAB_NBLK +i + 1] > 0
    next_base = tab_ref[TAB_ROW0 +i + 1]

    @pl.when(valid)
    def _():
        for c in bias_copies(e, par):
            c.wait()

    def gate_up_step(s, carry):
        wslot = lax.rem(s, MOE_WSLOTS)
        for c in gu_copies(e, s, wslot):
            c.wait()
        start_tile(e, s + MOE_WAHEAD)
        wbf_ref[...] = wst_ref[wslot].astype(BF16)
        b_g = bgu_ref[par, pl.ds(s, 1), :]
        b_u = bgu_ref[par, pl.ds(MOE_P1 + s, 1), :]

        def unit(u, slot):
            r = pl.multiple_of(u * MOE_RB, MOE_RB)
            first = pl.multiple_of(r + s * MOE_G, MOE_G)
            group = lax.shift_right_logical(first, SUBLANES.bit_length() - 1)
            for j in range(MOE_G):
                row_copy(tok_ref[next_base + first + j], group + j // SUBLANES, j % SUBLANES).start()
            gu = jnp.dot(xb_ref[pl.ds(r, MOE_RB), :], wbf_ref[...], preferred_element_type=F32)
            gate = jnp.minimum(gu[:, 0:MOE_TF] + b_g, SWIGLU_LIMIT)
            up = jnp.clip(gu[:, MOE_TF:2 * MOE_TF] + b_u, -SWIGLU_LIMIT, SWIGLU_LIMIT)
            act = (up + 1) * (gate * jax.nn.sigmoid(gate * SWIGLU_ALPHA))
            act_ref[s, pl.ds(r, MOE_RB), :] = act.astype(BF16)

        _for_units(nblk, unit)
        return carry

    def down_step(s, carry):
        t = MOE_P1 + s
        wslot = lax.rem(t, MOE_WSLOTS)
        d_copy(e, s, wslot).wait()

        @pl.when(t + MOE_WAHEAD < MOE_TILES)
        def _():
            start_tile(e, t + MOE_WAHEAD)

        @pl.when((t + MOE_WAHEAD >= MOE_TILES) & next_valid)
        def _():
            start_tile(e_next, t + MOE_WAHEAD - MOE_TILES)

        @pl.when((t + MOE_WAHEAD == MOE_TILES) & next_valid)
        def _():
            for c in bias_copies(e_next, 1 - par):
                c.start()

        wbf_ref[...] = wst_ref[wslot].astype(BF16)
        b_d = bd_ref[par, pl.ds(s, 1), :]
        col = pl.multiple_of(s * MOE_TN, MOE_TN)

        def free_slots(slots):
            for slot in slots:
                y_copy(slot, 0, col).wait()

        def unit(u, slot):
            r = pl.multiple_of(u * MOE_RB, MOE_RB)
            a = jnp.concatenate([act_ref[j, pl.ds(r, MOE_RB), :] for j in range(MOE_P1)], axis=1)
            ystage_ref[slot] = jnp.dot(a, wbf_ref[...], preferred_element_type=F32) + b_d
            y_copy(slot, u, col).start()

        _for_units(nblk, unit, free_slots)
        return carry

    @pl.when(valid)
    def _():
        lax.fori_loop(0, MOE_P1, gate_up_step, 0)
        gather_rows(i + 1, nblk, tab_ref[TAB_NBLK +i + 1])
        lax.fori_loop(0, MOE_P2, down_step, 0)

    @pl.when(i == MOE_NI - 1)
    def _():
        for slot in range(MOE_YSLOTS):
            dump_copy(slot).wait()

        lax.fori_loop(0, tab_ref[TAB_WAIT +MOE_NI], wait_rows, 0)

        def drain(b, carry):
            tail_copy(b).wait()
            return carry

        lax.fori_loop(tab_ref[TAB_USED], N_BLOCKS, drain, 0)


def _moe_call(slot_tok, tables, xn, w_gate_up, w_down, b_gate_up, b_down):
    any_spec = pl.BlockSpec(memory_space=pl.ANY)
    grid_spec = pltpu.PrefetchScalarGridSpec(
        num_scalar_prefetch=2,
        grid=(MOE_NI,),
        in_specs=[any_spec] * 5,
        out_specs=any_spec,
        scratch_shapes=[
            pltpu.VMEM((MOE_RMAX // SUBLANES, SUBLANES, D_MODEL), F32),
            pltpu.VMEM((MOE_RMAX, D_MODEL), BF16),
            pltpu.VMEM((MOE_P1, MOE_RMAX, MOE_TF), BF16),
            pltpu.VMEM((MOE_WSLOTS, D_MODEL, MOE_TN), F32),
            pltpu.VMEM((D_MODEL, MOE_TN), BF16),
            pltpu.VMEM((2, 2 * MOE_P1, MOE_TF), F32),
            pltpu.VMEM((2, MOE_P2, MOE_TN), F32),
            pltpu.VMEM((MOE_YSLOTS, MOE_RB, MOE_TN), F32),
            pltpu.VMEM((MOE_RB, D_MODEL), F32),
            pltpu.SemaphoreType.DMA(()),
            pltpu.SemaphoreType.DMA((MOE_YSLOTS,)),
            pltpu.SemaphoreType.DMA(()),
            pltpu.SemaphoreType.DMA((MOE_WSLOTS,)),
            pltpu.SemaphoreType.DMA((2,)),
        ],
    )
    spare_blocks = MOE_YSLOTS // MOE_P2
    return pl.pallas_call(
        _moe_kernel,
        grid_spec=grid_spec,
        out_shape=jax.ShapeDtypeStruct((N_SLOTS + spare_blocks * MOE_RB, D_MODEL), F32),
        compiler_params=_cparams(("arbitrary",), 56),
        name="moe",
    )(slot_tok, tables, xn, w_gate_up, w_down, b_gate_up, b_down)


def _combine_kernel(dest_ref, y_hbm, x1_ref, rg_ref, ps_ref, pp_ref, wple_ref, wpg_ref, gple_ref,
                    gfin_ref, ys_ref, yp_ref, gbuf_ref, gsem):
    m = pl.program_id(0)
    nm = pl.num_programs(0)
    tm = CMB_TM
    slot = m % 2

    def row_copy(tile, slot_, r, k):
        d = dest_ref[k * T + tile * tm + r]
        return pltpu.make_async_copy(y_hbm.at[pl.ds(d, 1), :], gbuf_ref.at[slot_, k, pl.ds(r, 1), :],
                                     gsem.at[slot_])

    def wait_tile(slot_):
        for k in range(TOP_K):
            pltpu.make_async_copy(y_hbm.at[pl.ds(0, tm), :], gbuf_ref.at[slot_, k], gsem.at[slot_]).wait()

    @pl.when(m == 0)
    def _():
        def body(c, carry):
            for j in range(GATHER_UNROLL // TOP_K):
                for k in range(TOP_K):
                    row_copy(0, 0, c * (GATHER_UNROLL // TOP_K) + j, k).start()
            return carry

        lax.fori_loop(0, tm // (GATHER_UNROLL // TOP_K), body, 0)

    wait_tile(slot)
    gates = rg_ref[...]
    moe = gates[:, 0:1] * gbuf_ref[slot, 0]
    for k in range(1, TOP_K):
        moe = moe + gates[:, k:k + 1] * gbuf_ref[slot, k]
    x2 = x1_ref[...] + moe
    is_s = m < CMB_NS
    p = jnp.where(is_s, ps_ref[...], pp_ref[...]).astype(BF16)
    hn = _rms(x2, gple_ref[...]).astype(BF16)
    nxt = jnp.minimum(m + 1, nm - 1)
    rows_per_chunk = tm // CMB_CHUNKS
    cw = D_MODEL // CMB_CHUNKS
    x3_parts = []
    for c in range(CMB_CHUNKS):
        cols = slice(c * cw, (c + 1) * cw)
        pe = jnp.dot(p, wple_ref[:, cols], preferred_element_type=F32)
        gate = jax.nn.sigmoid(jnp.dot(hn, wpg_ref[:, cols], preferred_element_type=F32))
        x3_parts.append(x2[:, cols] + pe * gate)
        for r in range(c * rows_per_chunk, (c + 1) * rows_per_chunk):
            for k in range(TOP_K):
                row_copy(nxt, 1 - slot, r, k).start()
    x3 = jnp.concatenate(x3_parts, axis=1)
    y = _rms(x3, gfin_ref[...])

    @pl.when(is_s)
    def _():
        ys_ref[...] = y

    @pl.when(jnp.logical_not(is_s))
    def _():
        yp_ref[...] = y

    @pl.when(m == nm - 1)
    def _():
        wait_tile(1 - slot)


def _combine_call(dest, y_sorted, x1, rg, ps, pp, wple, wpg, g_ple, g_final):
    tm = CMB_TM
    ns = CMB_NS
    s_idx = lambda m, d: (jnp.minimum(m, ns - 1), 0)
    p_idx = lambda m, d: (jnp.maximum(m - ns, 0), 0)
    row = lambda m, d: (m, 0)
    const2 = lambda m, d: (0, 0)
    grid_spec = pltpu.PrefetchScalarGridSpec(
        num_scalar_prefetch=1,
        grid=(T // tm,),
        in_specs=[
            pl.BlockSpec(memory_space=pl.ANY),
            pl.BlockSpec((tm, D_MODEL), row),
            pl.BlockSpec((tm, LANES), row),
            pl.BlockSpec((tm, PLE_DIM), s_idx),
            pl.BlockSpec((tm, PLE_DIM), p_idx),
            pl.BlockSpec((PLE_DIM, D_MODEL), const2),
            pl.BlockSpec((D_MODEL, D_MODEL), const2),
            pl.BlockSpec((1, D_MODEL), const2),
            pl.BlockSpec((1, D_MODEL), const2),
        ],
        out_specs=[
            pl.BlockSpec((tm, D_MODEL), s_idx),
            pl.BlockSpec((tm, D_MODEL), p_idx),
        ],
        scratch_shapes=[
            pltpu.VMEM((2, TOP_K, tm, D_MODEL), F32),
            pltpu.SemaphoreType.DMA((2,)),
        ],
    )
    return pl.pallas_call(
        _combine_kernel,
        grid_spec=grid_spec,
        out_shape=[
            jax.ShapeDtypeStruct((T_S, D_MODEL), F32),
            jax.ShapeDtypeStruct((T_P, D_MODEL), F32),
        ],
        compiler_params=_cparams(("arbitrary",), 56),
        name="combine",
    )(dest, y_sorted, x1, rg, ps, pp, wple, wpg, g_ple, g_final)


def _route_kernel(lgt_ref, dest_ref, tok_ref, tab_ref, rg_ref,
                  rit_ref, dvm_ref, zvm_ref, carry_ref, cnt_ref, gs_ref, sem):
    rb_shift = MOE_RB.bit_length() - 1
    tb = ROUTE_TB
    sub = tb // LANES

    a_i = lax.broadcasted_iota(I32, (LANES, LANES), 0)
    b_i = lax.broadcasted_iota(I32, (LANES, LANES), 1)
    earlier = jnp.where(a_i < b_i, 1.0, 0.0).astype(BF16)
    carry_ref[...] = jnp.zeros_like(carry_ref)

    def tile(c, loop_carry):
        c0 = pl.multiple_of(c * tb, tb)
        work = lgt_ref[:, pl.ds(c0, tb)]
        row = lax.broadcasted_iota(I32, (N_EXPERTS, tb), 0).astype(F32)
        vals, idxs, hots = [], [], []
        for _ in range(TOP_K):
            mx = jnp.max(work, axis=0, keepdims=True)
            idx = jnp.min(jnp.where(work == mx, row, float(N_EXPERTS)), axis=0, keepdims=True)
            hot = row == idx
            work = jnp.where(hot, -jnp.inf, work)
            vals.append(mx)
            idxs.append(idx)
            hots.append(hot)
        exps = [jnp.exp(vk - vals[0]) for vk in vals]
        den = exps[0] + exps[1] + exps[2] + exps[3]
        chosen = jnp.zeros((N_EXPERTS, tb), F32)
        for hot in hots:
            chosen = chosen + jnp.where(hot, 1.0, 0.0)
        counts = carry_ref[...]
        before = []
        for j in range(sub):
            cj = chosen[:, j * LANES:(j + 1) * LANES]
            before.append(jnp.dot(cj.astype(BF16), earlier, preferred_element_type=F32) + counts)
            counts = counts + jnp.sum(cj, axis=1, keepdims=True)
        carry_ref[...] = counts
        before = jnp.concatenate(before, axis=1)
        for k in range(TOP_K):
            rank = jnp.sum(jnp.where(hots[k], before, 0.0), axis=0, keepdims=True)
            rit_ref[k:k + 1, pl.ds(c0, tb)] = idxs[k].astype(I32)
            rit_ref[TOP_K + k:TOP_K + k + 1, pl.ds(c0, tb)] = rank.astype(I32)
        gates = jnp.concatenate([ek / den for ek in exps], axis=0)
        for j in range(sub):
            g_tile = jnp.concatenate([gates[:, j * LANES:(j + 1) * LANES],
                                      jnp.zeros((LANES - TOP_K, LANES), F32)], axis=0)
            rg_ref[pl.ds(pl.multiple_of(c0 + j * LANES, LANES), LANES), :] = jnp.transpose(g_tile)
        return loop_carry

    lax.fori_loop(0, T // tb, tile, 0)
    cnt_copy = pltpu.make_async_copy(carry_ref, cnt_ref, sem)
    cnt_copy.start()
    cnt_copy.wait()

    def clear(j, carry):
        tab_ref[j] = 0
        return carry

    lax.fori_loop(0, TAB_SIZE, clear, 0)

    def expert(e, carry):
        acc, item, used = carry
        n = lax.shift_right_logical(cnt_ref[e, 0].astype(I32) + (MOE_RB - 1), rb_shift)
        gs_ref[e] = acc

        def add_item(local, it):
            tab_ref[TAB_E + it] = e
            tab_ref[TAB_ROW0 + it] = acc + local * MOE_RMAX
            tab_ref[TAB_NBLK + it] = jnp.minimum(MOE_BMAX, n - local * MOE_BMAX)
            return it + 1

        item = lax.fori_loop(0, lax.div(n + (MOE_BMAX - 1), MOE_BMAX), add_item, item)
        return acc + n * MOE_RB, item, used + n

    _, n_items, used = lax.fori_loop(0, N_EXPERTS, expert, (jnp.int32(0), jnp.int32(0), jnp.int32(0)))
    tab_ref[TAB_USED] = used
    e_last = tab_ref[TAB_E + n_items - 1]

    def pad_item(it, carry):
        tab_ref[TAB_E + it] = e_last
        tab_ref[TAB_ROW0 + it] = 0
        tab_ref[TAB_NBLK + it] = 0
        return carry

    lax.fori_loop(n_items, MOE_NI + 1, pad_item, 0)

    def wait_units(it, prev):
        nb = tab_ref[TAB_NBLK + it]
        tab_ref[TAB_WAIT + it] = jnp.maximum(nb, prev)
        return nb

    lax.fori_loop(0, MOE_NI + 1, wait_units, jnp.int32(0))

    e_idx = rit_ref[0:TOP_K, :]
    d = rit_ref[TOP_K:2 * TOP_K, :]
    for e in range(N_EXPERTS):
        d = d + jnp.where(e_idx == e, gs_ref[e], 0)
    dvm_ref[0:TOP_K, :] = d
    zvm_ref[...] = jnp.zeros_like(zvm_ref)
    copies = [pltpu.make_async_copy(dvm_ref.at[k], dest_ref.at[pl.ds(k * T, T)], sem) for k in range(TOP_K)]
    copies.append(pltpu.make_async_copy(zvm_ref, tok_ref, sem))
    for c in copies:
        c.start()
    for c in copies:
        c.wait()

    for k in range(TOP_K):
        def scatter(c, carry):
            for j in range(GATHER_UNROLL):
                t = c * GATHER_UNROLL + j
                tok_ref[dest_ref[k * T + t]] = t
            return carry

        lax.fori_loop(0, T // GATHER_UNROLL, scatter, 0)


def _route_call(logits_t):
    smem = pl.BlockSpec(memory_space=pltpu.SMEM)
    vmem = pl.BlockSpec(memory_space=pltpu.VMEM)
    return pl.pallas_call(
        _route_kernel,
        in_specs=[vmem],
        out_specs=[smem, smem, smem, vmem],
        out_shape=[
            jax.ShapeDtypeStruct((TOP_K * T,), I32),
            jax.ShapeDtypeStruct((N_TOK_TAB,), I32),
            jax.ShapeDtypeStruct((TAB_SIZE,), I32),
            jax.ShapeDtypeStruct((T, LANES), F32),
        ],
        scratch_shapes=[
            pltpu.VMEM((2 * TOP_K, T), I32),
            pltpu.VMEM((2 * TOP_K, T), I32),
            pltpu.VMEM((N_TOK_TAB,), I32),
            pltpu.VMEM((N_EXPERTS, LANES), F32),
            pltpu.SMEM((N_EXPERTS, LANES), F32),
            pltpu.SMEM((N_EXPERTS,), I32),
            pltpu.SemaphoreType.DMA(()),
        ],
        compiler_params=pltpu.CompilerParams(vmem_limit_bytes=32 * MIB),
        name="route",
    )(logits_t)


def kernel(x_prompt, x_sample, state_conv, p_prompt, p_sample, g_mix, w_in, ln_v_g, ln_v_b, w_s, b_s,
           conv_w, conv_b, w_proj_a, w_proj_b, w_o, g_moe, w_router, b_router, w_gate_up, b_gate_up,
           w_down, b_down, g_ple, w_ple, w_ple_gate, g_final):
    assert g_mix.shape[0] == 1, "one layer"
    xs = x_sample.reshape(T_S, D_MODEL)
    xp = x_prompt.reshape(T_P, D_MODEL)

    tril = jnp.tril(jnp.ones((CHUNK, CHUNK), bool))
    w_prompt = jnp.where(tril[None], w_s[0], 0.0)
    small = jnp.where(tril[None, :DEC_SEQ, :DEC_SEQ], w_s[0, :, :DEC_SEQ, :DEC_SEQ], 0.0)
    reps = CHUNK // DEC_SEQ
    blockdiag = jnp.kron(jnp.eye(reps, dtype=F32), jnp.ones((DEC_SEQ, DEC_SEQ), F32))
    w_sample = jnp.tile(small, (1, reps, reps)) * blockdiag[None]
    wsg = jnp.stack([w_sample, w_prompt]).astype(BF16)
    bias_p = jnp.repeat(b_s[0].T, GW_A, axis=1)
    bias_s = jnp.tile(jnp.repeat(b_s[0, :, :DEC_SEQ].T, GW_A, axis=1), (reps, 1))
    bsg = jnp.stack([bias_s, bias_p])
    ext = jnp.pad(state_conv[0], ((0, 0), (0, DEC_SEQ - (CONV_W - 1)), (0, 0))).reshape(T_S, D_B)

    wr = jnp.pad(w_router[0], ((0, 0), (0, LANES - N_EXPERTS)))
    wr_hi = wr.astype(BF16)
    wr_lo = (wr - wr_hi.astype(F32)).astype(BF16)
    b_r = jnp.pad(b_router[0], (0, LANES - N_EXPERTS), constant_values=NEG_BIG).reshape(1, LANES)

    h = _norm_call(xs, xp, g_mix)
    z, vpre = _in_proj_call(h, w_in[0])
    x1, xn, logits_t, vln, cxs, tail = _mixer_call(
        xs, xp, z, vpre, ext, wsg, bsg, ln_v_g, ln_v_b, conv_w[0], conv_b,
        w_proj_a[0].astype(BF16), w_proj_b[0].astype(BF16), w_o[0].astype(BF16), g_moe,
        wr_hi, wr_lo, b_r)

    dest, slot_tok, tables, route_g = _route_call(logits_t)
    y_sorted = _moe_call(slot_tok, tables, xn, w_gate_up[0], w_down[0],
                         b_gate_up[0].reshape(N_EXPERTS, 2 * MOE_P1, MOE_TF),
                         b_down[0].reshape(N_EXPERTS, MOE_P2, MOE_TN))
    ys, yp = _combine_call(dest, y_sorted, x1, route_g,
                           p_sample[0].reshape(T_S, PLE_DIM), p_prompt[0].reshape(T_P, PLE_DIM),
                           w_ple[0].astype(BF16), w_ple_gate[0].astype(BF16), g_ple, g_final.reshape(1, D_MODEL))

    y_prompt = yp.reshape(BATCH, SEQ, D_MODEL)
    y_sample = ys.reshape(DEC_BATCH, DEC_SEQ, D_MODEL)
    last = tail[:MIX_NT - MIX_NS].reshape(BATCH, MIX_SEQ_TILES, 8, D_B)[:, -1, 8 - (CONV_W - 1):, :]
    state_conv_prompt = last[None]
    state_conv_sample = cxs[:T_S].reshape(DEC_BATCH, DEC_SEQ, D_B)[:, DEC_SEQ - (CONV_W - 1):, :][None]
    state_chunk_v_sample = vln[:T_S].reshape(DEC_BATCH, DEC_SEQ, D_A)[None]
    return (y_prompt, y_sample, state_conv_prompt, state_conv_sample, state_chunk_v_sample)
```

```python
import functools

import jax
import jax.numpy as jnp
from jax import lax
from jax.experimental import pallas as pl
from jax.experimental.pallas import tpu as pltpu

F32 = jnp.float32
BF16 = jnp.bfloat16
I32 = jnp.int32

D_MODEL = 2048
BATCH = 4
SEQ = 2048
DEC_BATCH = 128
DEC_SEQ = 8
CHUNK = 128
D_A = D_MODEL // 2
N_GROUPS_A = 8
GW_A = D_A // N_GROUPS_A
D_B = D_MODEL // 2
CONV_W = 3
N_EXPERTS = 32
TOP_K = 4
D_FF = D_MODEL
SWIGLU_LIMIT = 7.0
SWIGLU_ALPHA = 1.702
PLE_DIM = 256
EPS = 1e-6

T_S = DEC_BATCH * DEC_SEQ
T_P = BATCH * SEQ
T = T_S + T_P

LANES = 128
SUBLANES = 8
V7X_VMEM_BYTES = 64 * 1024 * 1024
MIB = 1024 * 1024

NORM_TM = 512
IN_TM = 1024
IN_TN = 1024
IN_SUB = 128
MIX_TM = CHUNK
MIX_NS = T_S // MIX_TM
MIX_SEQ_TILES = SEQ // MIX_TM
MIX_NT = T // MIX_TM
MOE_RB = 128
MOE_BMAX = 14
MOE_RMAX = MOE_RB * MOE_BMAX
MOE_CHUNK = 8
MOE_WHOLE = (9, 10, 11)
MOE_TF = 256
MOE_TN = 512
MOE_P1 = D_FF // MOE_TF
MOE_P2 = D_MODEL // MOE_TN
MOE_TILES = MOE_P1 + MOE_P2
MOE_W_PRIORITY = 1
MOE_WSLOTS = 3
MOE_WAHEAD = MOE_WSLOTS - 1
assert D_MODEL == D_FF and 2 * MOE_TF == MOE_TN and MOE_TILES % MOE_WSLOTS == 0
MOE_G = MOE_RB // MOE_P1
MOE_YSLOTS = 12
N_SLOTS = T * TOP_K + N_EXPERTS * MOE_RB
N_BLOCKS = N_SLOTS // MOE_RB
MOE_NI = (N_BLOCKS + N_EXPERTS * (MOE_BMAX - 1)) // MOE_BMAX
GATHER_UNROLL = 8
TAB_STRIDE = 64
TAB_E, TAB_ROW0, TAB_NBLK, TAB_WAIT, TAB_USED = 0, TAB_STRIDE, 2 * TAB_STRIDE, 3 * TAB_STRIDE, 4 * TAB_STRIDE
TAB_SIZE = 5 * TAB_STRIDE
assert MOE_NI + 1 <= TAB_STRIDE
N_TOK_TAB = -(-(N_SLOTS + MOE_RMAX) // 1024) * 1024
ROUTE_TB = 512
CMB_TM = 256
CMB_NS = T_S // CMB_TM
CMB_CHUNKS = 8
NEG_BIG = -1e30


def _rms(x, g):
    return x * lax.rsqrt(jnp.mean(x * x, axis=-1, keepdims=True) + EPS) * g


def _cparams(sem, vmem_mib):
    return pltpu.CompilerParams(dimension_semantics=sem, vmem_limit_bytes=vmem_mib * MIB)


def _resident(shape):
    zeros = (0,) * len(shape)
    return pl.BlockSpec(shape, lambda *_: zeros, pipeline_mode=pl.Buffered(1))


def _norm_kernel(xs_ref, xp_ref, g_ref, h_ref, *, ns):
    m = pl.program_id(0)
    x = jnp.where(m < ns, xs_ref[...], xp_ref[...])
    h_ref[...] = _rms(x, g_ref[...]).astype(BF16)


def _norm_call(xs, xp, g):
    ns = T_S // NORM_TM
    return pl.pallas_call(
        functools.partial(_norm_kernel, ns=ns),
        grid=(T // NORM_TM,),
        in_specs=[
            pl.BlockSpec((NORM_TM, D_MODEL), lambda m: (jnp.minimum(m, ns - 1), 0)),
            pl.BlockSpec((NORM_TM, D_MODEL), lambda m: (jnp.maximum(m - ns, 0), 0)),
            pl.BlockSpec((1, D_MODEL), lambda m: (0, 0)),
        ],
        out_specs=pl.BlockSpec((NORM_TM, D_MODEL), lambda m: (m, 0)),
        out_shape=jax.ShapeDtypeStruct((T, D_MODEL), BF16),
        compiler_params=_cparams(("arbitrary",), 32),
        name="norm",
    )(xs, xp, g)


IN_N_GELU = 2 * D_A // IN_TN
IN_N_V0 = D_A // IN_TN
IN_N_LIN = (2 * D_A + 3 * D_B) // IN_TN


def _in_proj_kernel(h_ref, w_ref, z_ref, vpre_ref, wb_ref):
    n = pl.program_id(0)

    @pl.when(pl.program_id(1) == 0)
    def _():
        wb_ref[...] = w_ref[...].astype(BF16)

    def blocks(epilogue):
        for b in range(IN_TM // IN_SUB):
            rows = pl.ds(b * IN_SUB, IN_SUB)
            epilogue(rows, jnp.dot(h_ref[rows, :], wb_ref[...], preferred_element_type=F32))

    @pl.when(n < IN_N_V0)
    def _():
        def ep(rows, acc):
            z_ref[rows, :] = jax.nn.gelu(acc, approximate=True).astype(BF16)

        blocks(ep)

    @pl.when((n >= IN_N_V0) & (n < IN_N_GELU))
    def _():
        def ep(rows, acc):
            g = jax.nn.gelu(acc, approximate=True)
            z_ref[rows, :] = g.astype(BF16)
            vpre_ref[rows, :] = g

        blocks(ep)

    @pl.when((n >= IN_N_GELU) & (n < IN_N_LIN))
    def _():
        def ep(rows, acc):
            z_ref[rows, :] = acc.astype(BF16)

        blocks(ep)

    @pl.when(n >= IN_N_LIN)
    def _():
        def ep(rows, acc):
            z_ref[rows, :] = jax.nn.sigmoid(acc).astype(BF16)

        blocks(ep)


def _in_proj_call(h, w_in):
    d_in = w_in.shape[1]
    n_m = T // IN_TM

    def vpre_map(n, m):
        row = jnp.where(n < IN_N_V0, 0, jnp.where(n < IN_N_GELU, m, n_m - 1))
        return (row, jnp.clip(n - IN_N_V0, 0, IN_N_GELU - IN_N_V0 - 1))

    return pl.pallas_call(
        _in_proj_kernel,
        grid=(d_in // IN_TN, n_m),
        in_specs=[
            pl.BlockSpec((IN_TM, D_MODEL), lambda n, m: (m, 0)),
            pl.BlockSpec((D_MODEL, IN_TN), lambda n, m: (0, n)),
        ],
        out_specs=[
            pl.BlockSpec((IN_TM, IN_TN), lambda n, m: (m, n)),
            pl.BlockSpec((IN_TM, IN_TN), vpre_map),
        ],
        out_shape=[
            jax.ShapeDtypeStruct((T, d_in), BF16),
            jax.ShapeDtypeStruct((T, D_A), F32),
        ],
        scratch_shapes=[pltpu.VMEM((D_MODEL, IN_TN), BF16)],
        compiler_params=_cparams(("arbitrary", "arbitrary"), 48),
        name="in_proj",
    )(h, w_in)


def _mixer_kernel(xs_ref, xp_ref, zu_ref, zb_ref, zc_ref, zx_ref, ga0_ref, ga1_ref, gb0_ref, gb1_ref,
                  vpre_ref, ext_ref, wsg_ref, bsg_ref, lng_ref, lnb_ref,
                  cw_ref, cb_ref, wa_ref, wb_ref, wo_ref, gmoe_ref, wrh_ref, wrl_ref, br_ref,
                  x1_ref, xn_ref, lgt_ref, vln_ref, cxs_ref, tail_ref,
                  prev_ref, ab_ref):
    m = pl.program_id(0)
    tm = MIX_TM
    a_tile = jnp.minimum(m, MIX_NT - 1)
    is_s = a_tile < MIX_NS
    b_is_s = (m - 1) < MIX_NS
    slot = m & 1

    @pl.when(m == 0)
    def _():
        prev_ref[...] = jnp.zeros_like(prev_ref)
        ab_ref[...] = jnp.zeros_like(ab_ref)

    y_a = jnp.dot(ab_ref[1 - slot, :, 0:D_A], wa_ref[...], preferred_element_type=F32)
    y_b = jnp.dot(ab_ref[1 - slot, :, D_A:D_A + D_B], wb_ref[...], preferred_element_type=F32)

    vg = vpre_ref[...]
    mu = jnp.mean(vg, axis=-1, keepdims=True)
    vc = vg - mu
    v = vc * lax.rsqrt(jnp.mean(vc * vc, axis=-1, keepdims=True) + EPS) * lng_ref[...] + lnb_ref[...]
    vln_ref[...] = v

    vb = v.astype(BF16)
    s_parts = []
    for g in range(N_GROUPS_A):
        s_parts.append(jnp.dot(wsg_ref[0, g], vb[:, g * GW_A:(g + 1) * GW_A], preferred_element_type=F32))
    s = jnp.concatenate(s_parts, axis=1) + bsg_ref[0]
    u = zu_ref[...].astype(F32)
    a_in = (u * s).astype(BF16)

    ga = jnp.concatenate([ga0_ref[...], ga1_ref[...]], axis=1).astype(F32)
    gb = jnp.concatenate([gb0_ref[...], gb1_ref[...]], axis=1).astype(F32)
    mix = (ga * y_a + gb * y_b).astype(BF16)
    x = jnp.where(b_is_s, xs_ref[...], xp_ref[...])
    x1 = x + jnp.dot(mix, wo_ref[...], preferred_element_type=F32)
    x1_ref[...] = x1

    bg = zb_ref[...].astype(F32)
    cg = zc_ref[...].astype(F32)
    xin = zx_ref[...].astype(F32)
    cx = cg * xin
    cxs_ref[...] = cx
    tail_ref[0] = cx[tm - 8:tm]

    row = lax.broadcasted_iota(I32, (tm, D_B), 0)
    seq_start = ((a_tile - MIX_NS) % MIX_SEQ_TILES) == 0
    prev = jnp.where(seq_start, 0.0, prev_ref[...])
    row8 = lax.broadcasted_iota(I32, (8, D_B), 0)
    top = jnp.where(row8 < CONV_W - 1, pltpu.roll(prev, CONV_W - 1, 0), 0.0)
    ext_p = jnp.concatenate([top, jnp.zeros((tm - 8, D_B), F32)], axis=0)
    ext = jnp.where(is_s, ext_ref[...], ext_p)
    t_in = jnp.where(is_s, row & (DEC_SEQ - 1), row)
    s1 = jnp.where(t_in < 1, pltpu.roll(ext, tm - 1, 0), pltpu.roll(cx, 1, 0))
    s2 = jnp.where(t_in < 2, ext, pltpu.roll(cx, 2, 0))
    prev_ref[...] = cx[tm - 8:tm]
    conv = cb_ref[...] + s2 * cw_ref[0:1, :] + s1 * cw_ref[1:2, :] + cx * cw_ref[2:3, :]
    b_in = (bg * conv).astype(BF16)

    xn = _rms(x1, gmoe_ref[...])
    xn_ref[...] = xn
    hi = xn.astype(BF16)
    lo = (xn - hi.astype(F32)).astype(BF16)
    logits = (jnp.dot(hi, wrh_ref[...], preferred_element_type=F32)
              + jnp.dot(lo, wrh_ref[...], preferred_element_type=F32)
              + jnp.dot(hi, wrl_ref[...], preferred_element_type=F32)) + br_ref[...]

    lgt_ref[...] = jnp.transpose(logits)[0:N_EXPERTS, :]

    ab_ref[slot, :, 0:D_A] = a_in
    ab_ref[slot, :, D_A:D_A + D_B] = b_in


def _mixer_call(xs, xp, z, vpre, ext, wsg, bsg, ln_g, ln_b, conv_w, conv_b, wa, wb, wo, g_moe,
                wr_hi, wr_lo, b_r):
    tm = MIX_TM
    ns = MIX_NS
    nt = MIX_NT
    assert D_A == D_B and D_MODEL == 2 * D_A and z.shape[1] == 9 * D_A
    front = lambda m: jnp.minimum(m, nt - 1)
    back = lambda m: jnp.maximum(m - 1, 0)
    zcol = lambda tile_of, c: (lambda m: (tile_of(m), c))
    row = lambda m: (back(m), 0)
    s_out = lambda m: (jnp.minimum(front(m), ns), 0)
    p_out = lambda m: (jnp.where(front(m) < ns, nt - ns, front(m) - ns), 0, 0)
    sel = lambda m: jnp.minimum(front(m) // ns, 1)
    return pl.pallas_call(
        _mixer_kernel,
        grid=(nt + 1,),
        in_specs=[
            pl.BlockSpec((tm, D_MODEL), lambda m: (jnp.minimum(back(m), ns - 1), 0)),
            pl.BlockSpec((tm, D_MODEL), lambda m: (jnp.maximum(back(m) - ns, 0), 0)),
            pl.BlockSpec((tm, D_A), zcol(front, 0)),
            pl.BlockSpec((tm, D_A), zcol(front, 2)),
            pl.BlockSpec((tm, D_A), zcol(front, 3)),
            pl.BlockSpec((tm, D_A), zcol(front, 4)),
            pl.BlockSpec((tm, D_A), zcol(back, 5)),
            pl.BlockSpec((tm, D_A), zcol(back, 6)),
            pl.BlockSpec((tm, D_A), zcol(back, 7)),
            pl.BlockSpec((tm, D_A), zcol(back, 8)),
            pl.BlockSpec((tm, D_A), lambda m: (front(m), 0)),
            pl.BlockSpec((tm, D_B), lambda m: (jnp.minimum(front(m), ns - 1), 0)),
            pl.BlockSpec((1, N_GROUPS_A, CHUNK, CHUNK), lambda m: (sel(m), 0, 0, 0)),
            pl.BlockSpec((1, CHUNK, D_A), lambda m: (sel(m), 0, 0)),
            _resident((1, D_A)),
            _resident((1, D_A)),
            _resident((CONV_W, D_B)),
            _resident((1, D_B)),
            _resident((D_A, D_MODEL)),
            _resident((D_B, D_MODEL)),
            _resident((D_MODEL, D_MODEL)),
            _resident((1, D_MODEL)),
            _resident((D_MODEL, LANES)),
            _resident((D_MODEL, LANES)),
            _resident((1, LANES)),
        ],
        out_specs=[
            pl.BlockSpec((tm, D_MODEL), row),
            pl.BlockSpec((tm, D_MODEL), row),
            pl.BlockSpec((N_EXPERTS, tm), lambda m: (0, back(m))),
            pl.BlockSpec((tm, D_A), s_out),
            pl.BlockSpec((tm, D_B), s_out),
            pl.BlockSpec((1, 8, D_B), p_out),
        ],
        out_shape=[
            jax.ShapeDtypeStruct((T, D_MODEL), F32),
            jax.ShapeDtypeStruct((T, D_MODEL), F32),
            jax.ShapeDtypeStruct((N_EXPERTS, T), F32),
            jax.ShapeDtypeStruct((T_S + tm, D_A), F32),
            jax.ShapeDtypeStruct((T_S + tm, D_B), F32),
            jax.ShapeDtypeStruct((nt - ns + 1, 8, D_B), F32),
        ],
        scratch_shapes=[pltpu.VMEM((8, D_B), F32), pltpu.VMEM((2, tm, D_A + D_B), BF16)],
        compiler_params=_cparams(("arbitrary",), 52),
        name="mixer",
    )(xs, xp, z, z, z, z, z, z, z, z, vpre, ext, wsg, bsg, ln_g, ln_b, conv_w, conv_b, wa, wb, wo, g_moe,
      wr_hi, wr_lo, b_r)


def _for_units(n, unit_fn, chunk_begin=None):
    def chunk(u0, count):
        if chunk_begin is not None:
            chunk_begin(range(count))
        for j in range(count):
            unit_fn(u0 + j, j)

    def body(c, carry):
        chunk(c * MOE_CHUNK, MOE_CHUNK)
        return carry

    whole = functools.reduce(jnp.logical_or, [n == count for count in MOE_WHOLE])
    for count in MOE_WHOLE:
        @pl.when(n == count)
        def _():
            chunk(0, count)

    n_big = jnp.where(whole, 0, lax.shift_right_logical(n, MOE_CHUNK.bit_length() - 1))
    lax.fori_loop(0, n_big, body, 0)
    base = n_big * MOE_CHUNK
    rest = jnp.where(whole, 0, n & (MOE_CHUNK - 1))
    for count in range(1, MOE_CHUNK):
        @pl.when(rest == count)
        def _():
            chunk(base, count)


def _moe_kernel(tok_ref, tab_ref,
                xn_hbm, wgu_hbm, wd_hbm, bgu_hbm, bd_hbm,
                y_hbm,
                xraw_ref, xb_ref, act_ref, wst_ref, wbf_ref, bgu_ref, bd_ref,
                ystage_ref, zbuf_ref, gsem, ysem, zsem, wsem, bsem):
    i = pl.program_id(0)
    nblk = tab_ref[TAB_NBLK +i]
    row0 = tab_ref[TAB_ROW0 +i]
    valid = nblk > 0
    par = i & 1

    def gu_copies(e, tile, slot):
        col = pl.multiple_of(tile * MOE_TF, MOE_TF)
        return (pltpu.make_async_copy(wgu_hbm.at[e, :, pl.ds(col, MOE_TF)],
                                      wst_ref.at[slot, :, pl.ds(0, MOE_TF)], wsem.at[slot]),
                pltpu.make_async_copy(wgu_hbm.at[e, :, pl.ds(D_FF + col, MOE_TF)],
                                      wst_ref.at[slot, :, pl.ds(MOE_TF, MOE_TF)], wsem.at[slot]))

    def d_copy(e, tile, slot):
        col = pl.multiple_of(tile * MOE_TN, MOE_TN)
        return pltpu.make_async_copy(wd_hbm.at[e, :, pl.ds(col, MOE_TN)], wst_ref.at[slot], wsem.at[slot])

    def start_tile(e, t):
        slot = lax.rem(t, MOE_WSLOTS)

        @pl.when(t < MOE_P1)
        def _():
            for c in gu_copies(e, t, slot):
                c.start(priority=MOE_W_PRIORITY)

        @pl.when(t >= MOE_P1)
        def _():
            d_copy(e, t - MOE_P1, slot).start(priority=MOE_W_PRIORITY)

    def bias_copies(e, slot):
        return (pltpu.make_async_copy(bgu_hbm.at[e], bgu_ref.at[slot], bsem.at[slot]),
                pltpu.make_async_copy(bd_hbm.at[e], bd_ref.at[slot], bsem.at[slot]))

    def tail_copy(b):
        r = pl.multiple_of(b * MOE_RB, MOE_RB)
        return pltpu.make_async_copy(zbuf_ref, y_hbm.at[pl.ds(r, MOE_RB), :], zsem)

    def row_copy(tok, group, sub):
        return pltpu.make_async_copy(xn_hbm.at[pl.ds(tok, 1), :], xraw_ref.at[group, pl.ds(sub, 1), :], gsem)

    def unit_wait():
        blk = xraw_ref.at[pl.ds(0, MOE_RB // SUBLANES)]
        return pltpu.make_async_copy(blk, blk, gsem)

    def gather_rows(item, lo_unit, hi_unit):
        base = tab_ref[TAB_ROW0 +item]

        def body(c, carry):
            for j in range(SUBLANES):
                row_copy(tok_ref[base + c * SUBLANES + j], c, j).start()
            return carry

        per_unit = MOE_RB // SUBLANES
        lax.fori_loop(lo_unit * per_unit, hi_unit * per_unit, body, 0)

    def y_copy(slot, u, col):
        r = pl.multiple_of(row0 + u * MOE_RB, MOE_RB)
        return pltpu.make_async_copy(ystage_ref.at[slot], y_hbm.at[pl.ds(r, MOE_RB), pl.ds(col, MOE_TN)],
                                     ysem.at[slot])

    def dump_copy(slot):
        r = N_SLOTS + (slot // MOE_P2) * MOE_RB
        c = (slot % MOE_P2) * MOE_TN
        return pltpu.make_async_copy(ystage_ref.at[slot], y_hbm.at[pl.ds(r, MOE_RB), pl.ds(c, MOE_TN)],
                                     ysem.at[slot])

    @pl.when(i == 0)
    def _():
        for t in range(MOE_WAHEAD):
            for c in gu_copies(tab_ref[TAB_E +0], t, t):
                c.start(priority=MOE_W_PRIORITY)
        for c in bias_copies(tab_ref[TAB_E +0], 0):
            c.start()
        gather_rows(0, 0, tab_ref[TAB_NBLK +0])
        ystage_ref[...] = jnp.zeros_like(ystage_ref)
        for slot in range(MOE_YSLOTS):
            dump_copy(slot).start()
        zbuf_ref[...] = jnp.zeros_like(zbuf_ref)

        def fill(b, carry):
            tail_copy(b).start()
            return carry

        lax.fori_loop(tab_ref[TAB_USED], N_BLOCKS, fill, 0)

    def wait_rows(b, carry):
        unit_wait().wait()
        return carry

    lax.fori_loop(0, tab_ref[TAB_WAIT +i], wait_rows, 0)

    def conv(u, carry):
        rows = pl.ds(pl.multiple_of(u * MOE_RB, MOE_RB), MOE_RB)
        groups = pl.ds(u * (MOE_RB // SUBLANES), MOE_RB // SUBLANES)
        xb_ref[rows, :] = xraw_ref[groups].reshape(MOE_RB, D_MODEL).astype(BF16)
        return carry

    lax.fori_loop(0, nblk, conv, 0)

    e = tab_ref[TAB_E +i]
    e_next = tab_ref[TAB_E +i + 1]
    next_valid = tab_ref[TAB_NBLK +i + 1] > 0
    next_base = tab_ref[TAB_ROW0 +i + 1]

    @pl.when(valid)
    def _():
        for c in bias_copies(e, par):
            c.wait()

    def gate_up_step(s, carry):
        wslot = lax.rem(s, MOE_WSLOTS)
        for c in gu_copies(e, s, wslot):
            c.wait()
        start_tile(e, s + MOE_WAHEAD)
        wbf_ref[...] = wst_ref[wslot].astype(BF16)
        b_g = bgu_ref[par, pl.ds(s, 1), :]
        b_u = bgu_ref[par, pl.ds(MOE_P1 + s, 1), :]

        def unit(u, slot):
            r = pl.multiple_of(u * MOE_RB, MOE_RB)
            first = pl.multiple_of(r + s * MOE_G, MOE_G)
            group = lax.shift_right_logical(first, SUBLANES.bit_length() - 1)
            for j in range(MOE_G):
                row_copy(tok_ref[next_base + first + j], group + j // SUBLANES, j % SUBLANES).start()
            gu = jnp.dot(xb_ref[pl.ds(r, MOE_RB), :], wbf_ref[...], preferred_element_type=F32)
            gate = jnp.minimum(gu[:, 0:MOE_TF] + b_g, SWIGLU_LIMIT)
            up = jnp.clip(gu[:, MOE_TF:2 * MOE_TF] + b_u, -SWIGLU_LIMIT, SWIGLU_LIMIT)
            act = (up + 1) * (gate * jax.nn.sigmoid(gate * SWIGLU_ALPHA))
            act_ref[s, pl.ds(r, MOE_RB), :] = act.astype(BF16)

        _for_units(nblk, unit)
        return carry

    def down_step(s, carry):
        t = MOE_P1 + s
        wslot = lax.rem(t, MOE_WSLOTS)
        d_copy(e, s, wslot).wait()

        @pl.when(t + MOE_WAHEAD < MOE_TILES)
        def _():
            start_tile(e, t + MOE_WAHEAD)

        @pl.when((t + MOE_WAHEAD >= MOE_TILES) & next_valid)
        def _():
            start_tile(e_next, t + MOE_WAHEAD - MOE_TILES)

        @pl.when((t + MOE_WAHEAD == MOE_TILES) & next_valid)
        def _():
            for c in bias_copies(e_next, 1 - par):
                c.start()

        wbf_ref[...] = wst_ref[wslot].astype(BF16)
        b_d = bd_ref[par, pl.ds(s, 1), :]
        col = pl.multiple_of(s * MOE_TN, MOE_TN)

        def free_slots(slots):
            for slot in slots:
                y_copy(slot, 0, col).wait()

        def unit(u, slot):
            r = pl.multiple_of(u * MOE_RB, MOE_RB)
            a = jnp.concatenate([act_ref[j, pl.ds(r, MOE_RB), :] for j in range(MOE_P1)], axis=1)
            ystage_ref[slot] = jnp.dot(a, wbf_ref[...], preferred_element_type=F32) + b_d
            y_copy(slot, u, col).start()

        _for_units(nblk, unit, free_slots)
        return carry

    @pl.when(valid)
    def _():
        lax.fori_loop(0, MOE_P1, gate_up_step, 0)
        gather_rows(i + 1, nblk, tab_ref[TAB_NBLK +i + 1])
        lax.fori_loop(0, MOE_P2, down_step, 0)

    @pl.when(i == MOE_NI - 1)
    def _():
        for slot in range(MOE_YSLOTS):
            dump_copy(slot).wait()

        lax.fori_loop(0, tab_ref[TAB_WAIT +MOE_NI], wait_rows, 0)

        def drain(b, carry):
            tail_copy(b).wait()
            return carry

        lax.fori_loop(tab_ref[TAB_USED], N_BLOCKS, drain, 0)


def _moe_call(slot_tok, tables, xn, w_gate_up, w_down, b_gate_up, b_down):
    any_spec = pl.BlockSpec(memory_space=pl.ANY)
    grid_spec = pltpu.PrefetchScalarGridSpec(
        num_scalar_prefetch=2,
        grid=(MOE_NI,),
        in_specs=[any_spec] * 5,
        out_specs=any_spec,
        scratch_shapes=[
            pltpu.VMEM((MOE_RMAX // SUBLANES, SUBLANES, D_MODEL), F32),
            pltpu.VMEM((MOE_RMAX, D_MODEL), BF16),
            pltpu.VMEM((MOE_P1, MOE_RMAX, MOE_TF), BF16),
            pltpu.VMEM((MOE_WSLOTS, D_MODEL, MOE_TN), F32),
            pltpu.VMEM((D_MODEL, MOE_TN), BF16),
            pltpu.VMEM((2, 2 * MOE_P1, MOE_TF), F32),
            pltpu.VMEM((2, MOE_P2, MOE_TN), F32),
            pltpu.VMEM((MOE_YSLOTS, MOE_RB, MOE_TN), F32),
            pltpu.VMEM((MOE_RB, D_MODEL), F32),
            pltpu.SemaphoreType.DMA(()),
            pltpu.SemaphoreType.DMA((MOE_YSLOTS,)),
            pltpu.SemaphoreType.DMA(()),
            pltpu.SemaphoreType.DMA((MOE_WSLOTS,)),
            pltpu.SemaphoreType.DMA((2,)),
        ],
    )
    spare_blocks = MOE_YSLOTS // MOE_P2
    return pl.pallas_call(
        _moe_kernel,
        grid_spec=grid_spec,
        out_shape=jax.ShapeDtypeStruct((N_SLOTS + spare_blocks * MOE_RB, D_MODEL), F32),
        compiler_params=_cparams(("arbitrary",), 56),
        name="moe",
    )(slot_tok, tables, xn, w_gate_up, w_down, b_gate_up, b_down)


def _combine_kernel(dest_ref, y_hbm, x1_ref, rg_ref, ps_ref, pp_ref, wple_ref, wpg_ref, gple_ref,
                    gfin_ref, ys_ref, yp_ref, gbuf_ref, gsem):
    m = pl.program_id(0)
    nm = pl.num_programs(0)
    tm = CMB_TM
    slot = m % 2

    def row_copy(tile, slot_, r, k):
        d = dest_ref[k * T + tile * tm + r]
        return pltpu.make_async_copy(y_hbm.at[pl.ds(d, 1), :], gbuf_ref.at[slot_, k, pl.ds(r, 1), :],
                                     gsem.at[slot_])

    def wait_tile(slot_):
        for k in range(TOP_K):
            pltpu.make_async_copy(y_hbm.at[pl.ds(0, tm), :], gbuf_ref.at[slot_, k], gsem.at[slot_]).wait()

    @pl.when(m == 0)
    def _():
        def body(c, carry):
            for j in range(GATHER_UNROLL // TOP_K):
                for k in range(TOP_K):
                    row_copy(0, 0, c * (GATHER_UNROLL // TOP_K) + j, k).start()
            return carry

        lax.fori_loop(0, tm // (GATHER_UNROLL // TOP_K), body, 0)

    wait_tile(slot)
    gates = rg_ref[...]
    moe = gates[:, 0:1] * gbuf_ref[slot, 0]
    for k in range(1, TOP_K):
        moe = moe + gates[:, k:k + 1] * gbuf_ref[slot, k]
    x2 = x1_ref[...] + moe
    is_s = m < CMB_NS
    p = jnp.where(is_s, ps_ref[...], pp_ref[...]).astype(BF16)
    hn = _rms(x2, gple_ref[...]).astype(BF16)
    nxt = jnp.minimum(m + 1, nm - 1)
    rows_per_chunk = tm // CMB_CHUNKS
    cw = D_MODEL // CMB_CHUNKS
    x3_parts = []
    for c in range(CMB_CHUNKS):
        cols = slice(c * cw, (c + 1) * cw)
        pe = jnp.dot(p, wple_ref[:, cols], preferred_element_type=F32)
        gate = jax.nn.sigmoid(jnp.dot(hn, wpg_ref[:, cols], preferred_element_type=F32))
        x3_parts.append(x2[:, cols] + pe * gate)
        for r in range(c * rows_per_chunk, (c + 1) * rows_per_chunk):
            for k in range(TOP_K):
                row_copy(nxt, 1 - slot, r, k).start()
    x3 = jnp.concatenate(x3_parts, axis=1)
    y = _rms(x3, gfin_ref[...])

    @pl.when(is_s)
    def _():
        ys_ref[...] = y

    @pl.when(jnp.logical_not(is_s))
    def _():
        yp_ref[...] = y

    @pl.when(m == nm - 1)
    def _():
        wait_tile(1 - slot)


def _combine_call(dest, y_sorted, x1, rg, ps, pp, wple, wpg, g_ple, g_final):
    tm = CMB_TM
    ns = CMB_NS
    s_idx = lambda m, d: (jnp.minimum(m, ns - 1), 0)
    p_idx = lambda m, d: (jnp.maximum(m - ns, 0), 0)
    row = lambda m, d: (m, 0)
    const2 = lambda m, d: (0, 0)
    grid_spec = pltpu.PrefetchScalarGridSpec(
        num_scalar_prefetch=1,
        grid=(T // tm,),
        in_specs=[
            pl.BlockSpec(memory_space=pl.ANY),
            pl.BlockSpec((tm, D_MODEL), row),
            pl.BlockSpec((tm, LANES), row),
            pl.BlockSpec((tm, PLE_DIM), s_idx),
            pl.BlockSpec((tm, PLE_DIM), p_idx),
            pl.BlockSpec((PLE_DIM, D_MODEL), const2),
            pl.BlockSpec((D_MODEL, D_MODEL), const2),
            pl.BlockSpec((1, D_MODEL), const2),
            pl.BlockSpec((1, D_MODEL), const2),
        ],
        out_specs=[
            pl.BlockSpec((tm, D_MODEL), s_idx),
            pl.BlockSpec((tm, D_MODEL), p_idx),
        ],
        scratch_shapes=[
            pltpu.VMEM((2, TOP_K, tm, D_MODEL), F32),
            pltpu.SemaphoreType.DMA((2,)),
        ],
    )
    return pl.pallas_call(
        _combine_kernel,
        grid_spec=grid_spec,
        out_shape=[
            jax.ShapeDtypeStruct((T_S, D_MODEL), F32),
            jax.ShapeDtypeStruct((T_P, D_MODEL), F32),
        ],
        compiler_params=_cparams(("arbitrary",), 56),
        name="combine",
    )(dest, y_sorted, x1, rg, ps, pp, wple, wpg, g_ple, g_final)


def _route_kernel(lgt_ref, dest_ref, tok_ref, tab_ref, rg_ref,
                  rit_ref, dvm_ref, zvm_ref, carry_ref, cnt_ref, gs_ref, sem):
    rb_shift = MOE_RB.bit_length() - 1
    tb = ROUTE_TB
    sub = tb // LANES

    a_i = lax.broadcasted_iota(I32, (LANES, LANES), 0)
    b_i = lax.broadcasted_iota(I32, (LANES, LANES), 1)
    earlier = jnp.where(a_i < b_i, 1.0, 0.0).astype(BF16)
    carry_ref[...] = jnp.zeros_like(carry_ref)

    def tile(c, loop_carry):
        c0 = pl.multiple_of(c * tb, tb)
        work = lgt_ref[:, pl.ds(c0, tb)]
        row = lax.broadcasted_iota(I32, (N_EXPERTS, tb), 0).astype(F32)
        vals, idxs, hots = [], [], []
        for _ in range(TOP_K):
            mx = jnp.max(work, axis=0, keepdims=True)
            idx = jnp.min(jnp.where(work == mx, row, float(N_EXPERTS)), axis=0, keepdims=True)
            hot = row == idx
            work = jnp.where(hot, -jnp.inf, work)
            vals.append(mx)
            idxs.append(idx)
            hots.append(hot)
        exps = [jnp.exp(vk - vals[0]) for vk in vals]
        den = exps[0] + exps[1] + exps[2] + exps[3]
        chosen = jnp.zeros((N_EXPERTS, tb), F32)
        for hot in hots:
            chosen = chosen + jnp.where(hot, 1.0, 0.0)
        counts = carry_ref[...]
        before = []
        for j in range(sub):
            cj = chosen[:, j * LANES:(j + 1) * LANES]
            before.append(jnp.dot(cj.astype(BF16), earlier, preferred_element_type=F32) + counts)
            counts = counts + jnp.sum(cj, axis=1, keepdims=True)
        carry_ref[...] = counts
        before = jnp.concatenate(before, axis=1)
        for k in range(TOP_K):
            rank = jnp.sum(jnp.where(hots[k], before, 0.0), axis=0, keepdims=True)
            rit_ref[k:k + 1, pl.ds(c0, tb)] = idxs[k].astype(I32)
            rit_ref[TOP_K + k:TOP_K + k + 1, pl.ds(c0, tb)] = rank.astype(I32)
        gates = jnp.concatenate([ek / den for ek in exps], axis=0)
        for j in range(sub):
            g_tile = jnp.concatenate([gates[:, j * LANES:(j + 1) * LANES],
                                      jnp.zeros((LANES - TOP_K, LANES), F32)], axis=0)
            rg_ref[pl.ds(pl.multiple_of(c0 + j * LANES, LANES), LANES), :] = jnp.transpose(g_tile)
        return loop_carry

    lax.fori_loop(0, T // tb, tile, 0)
    cnt_copy = pltpu.make_async_copy(carry_ref, cnt_ref, sem)
    cnt_copy.start()
    cnt_copy.wait()

    def clear(j, carry):
        tab_ref[j] = 0
        return carry

    lax.fori_loop(0, TAB_SIZE, clear, 0)

    def expert(e, carry):
        acc, item, used = carry
        n = lax.shift_right_logical(cnt_ref[e, 0].astype(I32) + (MOE_RB - 1), rb_shift)
        gs_ref[e] = acc

        def add_item(local, it):
            tab_ref[TAB_E + it] = e
            tab_ref[TAB_ROW0 + it] = acc + local * MOE_RMAX
            tab_ref[TAB_NBLK + it] = jnp.minimum(MOE_BMAX, n - local * MOE_BMAX)
            return it + 1

        item = lax.fori_loop(0, lax.div(n + (MOE_BMAX - 1), MOE_BMAX), add_item, item)
        return acc + n * MOE_RB, item, used + n

    _, n_items, used = lax.fori_loop(0, N_EXPERTS, expert, (jnp.int32(0), jnp.int32(0), jnp.int32(0)))
    tab_ref[TAB_USED] = used
    e_last = tab_ref[TAB_E + n_items - 1]

    def pad_item(it, carry):
        tab_ref[TAB_E + it] = e_last
        tab_ref[TAB_ROW0 + it] = 0
        tab_ref[TAB_NBLK + it] = 0
        return carry

    lax.fori_loop(n_items, MOE_NI + 1, pad_item, 0)

    def wait_units(it, prev):
        nb = tab_ref[TAB_NBLK + it]
        tab_ref[TAB_WAIT + it] = jnp.maximum(nb, prev)
        return nb

    lax.fori_loop(0, MOE_NI + 1, wait_units, jnp.int32(0))

    e_idx = rit_ref[0:TOP_K, :]
    d = rit_ref[TOP_K:2 * TOP_K, :]
    for e in range(N_EXPERTS):
        d = d + jnp.where(e_idx == e, gs_ref[e], 0)
    dvm_ref[0:TOP_K, :] = d
    zvm_ref[...] = jnp.zeros_like(zvm_ref)
    copies = [pltpu.make_async_copy(dvm_ref.at[k], dest_ref.at[pl.ds(k * T, T)], sem) for k in range(TOP_K)]
    copies.append(pltpu.make_async_copy(zvm_ref, tok_ref, sem))
    for c in copies:
        c.start()
    for c in copies:
        c.wait()

    for k in range(TOP_K):
        def scatter(c, carry):
            for j in range(GATHER_UNROLL):
                t = c * GATHER_UNROLL + j
                tok_ref[dest_ref[k * T + t]] = t
            return carry

        lax.fori_loop(0, T // GATHER_UNROLL, scatter, 0)


def _route_call(logits_t):
    smem = pl.BlockSpec(memory_space=pltpu.SMEM)
    vmem = pl.BlockSpec(memory_space=pltpu.VMEM)
    return pl.pallas_call(
        _route_kernel,
        in_specs=[vmem],
        out_specs=[smem, smem, smem, vmem],
        out_shape=[
            jax.ShapeDtypeStruct((TOP_K * T,), I32),
            jax.ShapeDtypeStruct((N_TOK_TAB,), I32),
            jax.ShapeDtypeStruct((TAB_SIZE,), I32),
            jax.ShapeDtypeStruct((T, LANES), F32),
        ],
        scratch_shapes=[
            pltpu.VMEM((2 * TOP_K, T), I32),
            pltpu.VMEM((2 * TOP_K, T), I32),
            pltpu.VMEM((N_TOK_TAB,), I32),
            pltpu.VMEM((N_EXPERTS, LANES), F32),
            pltpu.SMEM((N_EXPERTS, LANES), F32),
            pltpu.SMEM((N_EXPERTS,), I32),
            pltpu.SemaphoreType.DMA(()),
        ],
        compiler_params=pltpu.CompilerParams(vmem_limit_bytes=32 * MIB),
        name="route",
    )(logits_t)


def kernel(x_prompt, x_sample, state_conv, p_prompt, p_sample, g_mix, w_in, ln_v_g, ln_v_b, w_s, b_s,
           conv_w, conv_b, w_proj_a, w_proj_b, w_o, g_moe, w_router, b_router, w_gate_up, b_gate_up,
           w_down, b_down, g_ple, w_ple, w_ple_gate, g_final):
    assert g_mix.shape[0] == 1, "one layer"
    xs = x_sample.reshape(T_S, D_MODEL)
    xp = x_prompt.reshape(T_P, D_MODEL)

    tril = jnp.tril(jnp.ones((CHUNK, CHUNK), bool))
    w_prompt = jnp.where(tril[None], w_s[0], 0.0)
    small = jnp.where(tril[None, :DEC_SEQ, :DEC_SEQ], w_s[0, :, :DEC_SEQ, :DEC_SEQ], 0.0)
    reps = CHUNK // DEC_SEQ
    blockdiag = jnp.kron(jnp.eye(reps, dtype=F32), jnp.ones((DEC_SEQ, DEC_SEQ), F32))
    w_sample = jnp.tile(small, (1, reps, reps)) * blockdiag[None]
    wsg = jnp.stack([w_sample, w_prompt]).astype(BF16)
    bias_p = jnp.repeat(b_s[0].T, GW_A, axis=1)
    bias_s = jnp.tile(jnp.repeat(b_s[0, :, :DEC_SEQ].T, GW_A, axis=1), (reps, 1))
    bsg = jnp.stack([bias_s, bias_p])
    ext = jnp.pad(state_conv[0], ((0, 0), (0, DEC_SEQ - (CONV_W - 1)), (0, 0))).reshape(T_S, D_B)

    wr = jnp.pad(w_router[0], ((0, 0), (0, LANES - N_EXPERTS)))
    wr_hi = wr.astype(BF16)
    wr_lo = (wr - wr_hi.astype(F32)).astype(BF16)
    b_r = jnp.pad(b_router[0], (0, LANES - N_EXPERTS), constant_values=NEG_BIG).reshape(1, LANES)

    h = _norm_call(xs, xp, g_mix)
    z, vpre = _in_proj_call(h, w_in[0])
    x1, xn, logits_t, vln, cxs, tail = _mixer_call(
        xs, xp, z, vpre, ext, wsg, bsg, ln_v_g, ln_v_b, conv_w[0], conv_b,
        w_proj_a[0].astype(BF16), w_proj_b[0].astype(BF16), w_o[0].astype(BF16), g_moe,
        wr_hi, wr_lo, b_r)

    dest, slot_tok, tables, route_g = _route_call(logits_t)
    y_sorted = _moe_call(slot_tok, tables, xn, w_gate_up[0], w_down[0],
                         b_gate_up[0].reshape(N_EXPERTS, 2 * MOE_P1, MOE_TF),
                         b_down[0].reshape(N_EXPERTS, MOE_P2, MOE_TN))
    ys, yp = _combine_call(dest, y_sorted, x1, route_g,
                           p_sample[0].reshape(T_S, PLE_DIM), p_prompt[0].reshape(T_P, PLE_DIM),
                           w_ple[0].astype(BF16), w_ple_gate[0].astype(BF16), g_ple, g_final.reshape(1, D_MODEL))

    y_prompt = yp.reshape(BATCH, SEQ, D_MODEL)
    y_sample = ys.reshape(DEC_BATCH, DEC_SEQ, D_MODEL)
    last = tail[:MIX_NT - MIX_NS].reshape(BATCH, MIX_SEQ_TILES, 8, D_B)[:, -1, 8 - (CONV_W - 1):, :]
    state_conv_prompt = last[None]
    state_conv_sample = cxs[:T_S].reshape(DEC_BATCH, DEC_SEQ, D_B)[:, DEC_SEQ - (CONV_W - 1):, :][None]
    state_chunk_v_sample = vln[:T_S].reshape(DEC_BATCH, DEC_SEQ, D_A)[None]
    return (y_prompt, y_sample, state_conv_prompt, state_conv_sample, state_chunk_v_sample)
```

```python
import functools

import jax
import jax.numpy as jnp
from jax import lax
from jax.experimental import pallas as pl
from jax.experimental.pallas import tpu as pltpu

F32 = jnp.float32
BF16 = jnp.bfloat16
I32 = jnp.int32

D_MODEL = 2048
BATCH = 4
SEQ = 2048
DEC_BATCH = 128
DEC_SEQ = 8
CHUNK = 128
D_A = D_MODEL // 2
N_GROUPS_A = 8
GW_A = D_A // N_GROUPS_A
D_B = D_MODEL // 2
CONV_W = 3
N_EXPERTS = 32
TOP_K = 4
D_FF = D_MODEL
SWIGLU_LIMIT = 7.0
SWIGLU_ALPHA = 1.702
PLE_DIM = 256
EPS = 1e-6

T_S = DEC_BATCH * DEC_SEQ
T_P = BATCH * SEQ
T = T_S + T_P

LANES = 128
SUBLANES = 8
MIB = 1024 * 1024

NORM_TM = 512
IN_TM = 1024
IN_TN = 1024
IN_SUB = 256
MIX_TM = CHUNK
MIX_NS = T_S // MIX_TM
MIX_SEQ_TILES = SEQ // MIX_TM
MIX_NT = T // MIX_TM
MOE_RB = 128
MOE_BMAX = 14
MOE_RMAX = MOE_RB * MOE_BMAX
MOE_CHUNK = 8
MOE_TF = 256
MOE_TN = 512
MOE_P1 = D_FF // MOE_TF
MOE_P2 = D_MODEL // MOE_TN
MOE_TILES = MOE_P1 + MOE_P2
MOE_WSLOTS = 3
MOE_WAHEAD = MOE_WSLOTS - 1
assert D_MODEL == D_FF and 2 * MOE_TF == MOE_TN and MOE_TILES % MOE_WSLOTS == 0
MOE_G = MOE_RB // MOE_P1
MOE_YSLOTS = 8
N_SLOTS = T * TOP_K + N_EXPERTS * MOE_RB
N_BLOCKS = N_SLOTS // MOE_RB
MOE_NI = (N_BLOCKS + N_EXPERTS * (MOE_BMAX - 1)) // MOE_BMAX
GATHER_UNROLL = 8
TAB_STRIDE = 64
TAB_E, TAB_ROW0, TAB_NBLK, TAB_WAIT, TAB_USED = 0, TAB_STRIDE, 2 * TAB_STRIDE, 3 * TAB_STRIDE, 4 * TAB_STRIDE
TAB_SIZE = 5 * TAB_STRIDE
assert MOE_NI + 1 <= TAB_STRIDE
N_TOK_TAB = -(-(N_SLOTS + MOE_RMAX) // 1024) * 1024
ROUTE_TB = 512
CMB_TM = 256
CMB_NS = T_S // CMB_TM
CMB_CHUNKS = 8
NEG_BIG = -1e30


def _rms(x, g):
    return x * lax.rsqrt(jnp.mean(x * x, axis=-1, keepdims=True) + EPS) * g


def _cparams(sem, vmem_mib):
    return pltpu.CompilerParams(dimension_semantics=sem, vmem_limit_bytes=vmem_mib * MIB)


def _resident(shape):
    zeros = (0,) * len(shape)
    return pl.BlockSpec(shape, lambda *_: zeros, pipeline_mode=pl.Buffered(1))


def _norm_kernel(xs_ref, xp_ref, g_ref, h_ref, *, ns):
    m = pl.program_id(0)
    x = jnp.where(m < ns, xs_ref[...], xp_ref[...])
    h_ref[...] = _rms(x, g_ref[...]).astype(BF16)


def _norm_call(xs, xp, g):
    ns = T_S // NORM_TM
    return pl.pallas_call(
        functools.partial(_norm_kernel, ns=ns),
        grid=(T // NORM_TM,),
        in_specs=[
            pl.BlockSpec((NORM_TM, D_MODEL), lambda m: (jnp.minimum(m, ns - 1), 0)),
            pl.BlockSpec((NORM_TM, D_MODEL), lambda m: (jnp.maximum(m - ns, 0), 0)),
            pl.BlockSpec((1, D_MODEL), lambda m: (0, 0)),
        ],
        out_specs=pl.BlockSpec((NORM_TM, D_MODEL), lambda m: (m, 0)),
        out_shape=jax.ShapeDtypeStruct((T, D_MODEL), BF16),
        compiler_params=_cparams(("arbitrary",), 32),
        name="norm",
    )(xs, xp, g)


IN_N_GELU = 2 * D_A // IN_TN
IN_N_V0 = D_A // IN_TN
IN_N_LIN = (2 * D_A + 3 * D_B) // IN_TN


def _in_proj_kernel(h_ref, w_ref, z_ref, vpre_ref, wb_ref):
    n = pl.program_id(0)

    @pl.when(pl.program_id(1) == 0)
    def _():
        wb_ref[...] = w_ref[...].astype(BF16)

    def blocks(epilogue):
        for b in range(IN_TM // IN_SUB):
            rows = pl.ds(b * IN_SUB, IN_SUB)
            epilogue(rows, jnp.dot(h_ref[rows, :], wb_ref[...], preferred_element_type=F32))

    @pl.when(n < IN_N_V0)
    def _():
        def ep(rows, acc):
            z_ref[rows, :] = jax.nn.gelu(acc, approximate=True).astype(BF16)

        blocks(ep)

    @pl.when((n >= IN_N_V0) & (n < IN_N_GELU))
    def _():
        def ep(rows, acc):
            g = jax.nn.gelu(acc, approximate=True)
            z_ref[rows, :] = g.astype(BF16)
            vpre_ref[rows, :] = g

        blocks(ep)

    @pl.when((n >= IN_N_GELU) & (n < IN_N_LIN))
    def _():
        def ep(rows, acc):
            z_ref[rows, :] = acc.astype(BF16)

        blocks(ep)

    @pl.when(n >= IN_N_LIN)
    def _():
        def ep(rows, acc):
            z_ref[rows, :] = jax.nn.sigmoid(acc).astype(BF16)

        blocks(ep)


def _in_proj_call(h, w_in):
    d_in = w_in.shape[1]
    n_m = T // IN_TM

    def vpre_map(n, m):
        row = jnp.where(n < IN_N_V0, 0, jnp.where(n < IN_N_GELU, m, n_m - 1))
        return (row, jnp.clip(n - IN_N_V0, 0, IN_N_GELU - IN_N_V0 - 1))

    return pl.pallas_call(
        _in_proj_kernel,
        grid=(d_in // IN_TN, n_m),
        in_specs=[
            pl.BlockSpec((IN_TM, D_MODEL), lambda n, m: (m, 0)),
            pl.BlockSpec((D_MODEL, IN_TN), lambda n, m: (0, n)),
        ],
        out_specs=[
            pl.BlockSpec((IN_TM, IN_TN), lambda n, m: (m, n)),
            pl.BlockSpec((IN_TM, IN_TN), vpre_map),
        ],
        out_shape=[
            jax.ShapeDtypeStruct((T, d_in), BF16),
            jax.ShapeDtypeStruct((T, D_A), F32),
        ],
        scratch_shapes=[pltpu.VMEM((D_MODEL, IN_TN), BF16)],
        compiler_params=_cparams(("arbitrary", "arbitrary"), 48),
        name="in_proj",
    )(h, w_in)


def _mixer_kernel(xs_ref, xp_ref, zu_ref, zb_ref, zc_ref, zx_ref, ga0_ref, ga1_ref, gb0_ref, gb1_ref,
                  vpre_ref, ext_ref, wsg_ref, bsg_ref, lng_ref, lnb_ref,
                  cw_ref, cb_ref, wa_ref, wb_ref, wo_ref, gmoe_ref, wrh_ref, wrl_ref, br_ref,
                  x1_ref, xn_ref, lgt_ref, vln_ref, cxs_ref, tail_ref,
                  prev_ref, ab_ref):
    m = pl.program_id(0)
    tm = MIX_TM
    a_tile = jnp.minimum(m, MIX_NT - 1)
    is_s = a_tile < MIX_NS
    b_is_s = (m - 1) < MIX_NS
    slot = m & 1

    @pl.when(m == 0)
    def _():
        prev_ref[...] = jnp.zeros_like(prev_ref)
        ab_ref[...] = jnp.zeros_like(ab_ref)

    y_a = jnp.dot(ab_ref[1 - slot, :, 0:D_A], wa_ref[...], preferred_element_type=F32)
    y_b = jnp.dot(ab_ref[1 - slot, :, D_A:D_A + D_B], wb_ref[...], preferred_element_type=F32)

    vg = vpre_ref[...]
    mu = jnp.mean(vg, axis=-1, keepdims=True)
    vc = vg - mu
    v = vc * lax.rsqrt(jnp.mean(vc * vc, axis=-1, keepdims=True) + EPS) * lng_ref[...] + lnb_ref[...]
    vln_ref[...] = v

    vb = v.astype(BF16)
    s_parts = []
    for g in range(N_GROUPS_A):
        s_parts.append(jnp.dot(wsg_ref[0, g], vb[:, g * GW_A:(g + 1) * GW_A], preferred_element_type=F32))
    s = jnp.concatenate(s_parts, axis=1) + bsg_ref[0]
    u = zu_ref[...].astype(F32)
    a_in = (u * s).astype(BF16)

    ga = jnp.concatenate([ga0_ref[...], ga1_ref[...]], axis=1).astype(F32)
    gb = jnp.concatenate([gb0_ref[...], gb1_ref[...]], axis=1).astype(F32)
    mix = (ga * y_a + gb * y_b).astype(BF16)
    x = jnp.where(b_is_s, xs_ref[...], xp_ref[...])
    x1 = x + jnp.dot(mix, wo_ref[...], preferred_element_type=F32)
    x1_ref[...] = x1

    bg = zb_ref[...].astype(F32)
    cg = zc_ref[...].astype(F32)
    xin = zx_ref[...].astype(F32)
    cx = cg * xin
    cxs_ref[...] = cx
    tail_ref[0] = cx[tm - SUBLANES:tm]

    row = lax.broadcasted_iota(I32, (tm, D_B), 0)
    seq_start = ((a_tile - MIX_NS) % MIX_SEQ_TILES) == 0
    prev = jnp.where(seq_start, 0.0, prev_ref[...])
    row8 = lax.broadcasted_iota(I32, (SUBLANES, D_B), 0)
    top = jnp.where(row8 < CONV_W - 1, pltpu.roll(prev, CONV_W - 1, 0), 0.0)
    ext_p = jnp.concatenate([top, jnp.zeros((tm - SUBLANES, D_B), F32)], axis=0)
    ext = jnp.where(is_s, ext_ref[...], ext_p)
    t_in = jnp.where(is_s, row & (DEC_SEQ - 1), row)
    s1 = jnp.where(t_in < 1, pltpu.roll(ext, tm - 1, 0), pltpu.roll(cx, 1, 0))
    s2 = jnp.where(t_in < 2, ext, pltpu.roll(cx, 2, 0))
    prev_ref[...] = cx[tm - SUBLANES:tm]
    conv = cb_ref[...] + s2 * cw_ref[0:1, :] + s1 * cw_ref[1:2, :] + cx * cw_ref[2:3, :]
    b_in = (bg * conv).astype(BF16)

    xn = _rms(x1, gmoe_ref[...])
    xn_ref[...] = xn
    hi = xn.astype(BF16)
    lo = (xn - hi.astype(F32)).astype(BF16)
    logits = (jnp.dot(hi, wrh_ref[...], preferred_element_type=F32)
              + jnp.dot(lo, wrh_ref[...], preferred_element_type=F32)
              + jnp.dot(hi, wrl_ref[...], preferred_element_type=F32)) + br_ref[...]

    lgt_ref[...] = jnp.transpose(logits)[0:N_EXPERTS, :]

    ab_ref[slot, :, 0:D_A] = a_in
    ab_ref[slot, :, D_A:D_A + D_B] = b_in


def _mixer_call(xs, xp, z, vpre, ext, wsg, bsg, ln_g, ln_b, conv_w, conv_b, wa, wb, wo, g_moe,
                wr_hi, wr_lo, b_r):
    tm = MIX_TM
    ns = MIX_NS
    nt = MIX_NT
    assert D_A == D_B and D_MODEL == 2 * D_A and z.shape[1] == 9 * D_A
    front = lambda m: jnp.minimum(m, nt - 1)
    back = lambda m: jnp.maximum(m - 1, 0)
    zcol = lambda tile_of, c: (lambda m: (tile_of(m), c))
    row = lambda m: (back(m), 0)
    s_out = lambda m: (jnp.minimum(front(m), ns), 0)
    p_out = lambda m: (jnp.where(front(m) < ns, nt - ns, front(m) - ns), 0, 0)
    sel = lambda m: jnp.minimum(front(m) // ns, 1)
    return pl.pallas_call(
        _mixer_kernel,
        grid=(nt + 1,),
        in_specs=[
            pl.BlockSpec((tm, D_MODEL), lambda m: (jnp.minimum(back(m), ns - 1), 0)),
            pl.BlockSpec((tm, D_MODEL), lambda m: (jnp.maximum(back(m) - ns, 0), 0)),
            pl.BlockSpec((tm, D_A), zcol(front, 0)),
            pl.BlockSpec((tm, D_A), zcol(front, 2)),
            pl.BlockSpec((tm, D_A), zcol(front, 3)),
            pl.BlockSpec((tm, D_A), zcol(front, 4)),
            pl.BlockSpec((tm, D_A), zcol(back, 5)),
            pl.BlockSpec((tm, D_A), zcol(back, 6)),
            pl.BlockSpec((tm, D_A), zcol(back, 7)),
            pl.BlockSpec((tm, D_A), zcol(back, 8)),
            pl.BlockSpec((tm, D_A), lambda m: (front(m), 0)),
            pl.BlockSpec((tm, D_B), lambda m: (jnp.minimum(front(m), ns - 1), 0)),
            pl.BlockSpec((1, N_GROUPS_A, CHUNK, CHUNK), lambda m: (sel(m), 0, 0, 0)),
            pl.BlockSpec((1, CHUNK, D_A), lambda m: (sel(m), 0, 0)),
            _resident((1, D_A)),
            _resident((1, D_A)),
            _resident((CONV_W, D_B)),
            _resident((1, D_B)),
            _resident((D_A, D_MODEL)),
            _resident((D_B, D_MODEL)),
            _resident((D_MODEL, D_MODEL)),
            _resident((1, D_MODEL)),
            _resident((D_MODEL, LANES)),
            _resident((D_MODEL, LANES)),
            _resident((1, LANES)),
        ],
        out_specs=[
            pl.BlockSpec((tm, D_MODEL), row),
            pl.BlockSpec((tm, D_MODEL), row),
            pl.BlockSpec((N_EXPERTS, tm), lambda m: (0, back(m))),
            pl.BlockSpec((tm, D_A), s_out),
            pl.BlockSpec((tm, D_B), s_out),
            pl.BlockSpec((1, SUBLANES, D_B), p_out),
        ],
        out_shape=[
            jax.ShapeDtypeStruct((T, D_MODEL), F32),
            jax.ShapeDtypeStruct((T, D_MODEL), F32),
            jax.ShapeDtypeStruct((N_EXPERTS, T), F32),
            jax.ShapeDtypeStruct((T_S + tm, D_A), F32),
            jax.ShapeDtypeStruct((T_S + tm, D_B), F32),
            jax.ShapeDtypeStruct((nt - ns + 1, SUBLANES, D_B), F32),
        ],
        scratch_shapes=[pltpu.VMEM((SUBLANES, D_B), F32), pltpu.VMEM((2, tm, D_A + D_B), BF16)],
        compiler_params=_cparams(("arbitrary",), 52),
        name="mixer",
    )(xs, xp, z, z, z, z, z, z, z, z, vpre, ext, wsg, bsg, ln_g, ln_b, conv_w, conv_b, wa, wb, wo, g_moe,
      wr_hi, wr_lo, b_r)


def _for_units(n, unit_fn, chunk_begin=None):
    def chunk(u0, count):
        if chunk_begin is not None:
            chunk_begin(range(count))
        for j in range(count):
            unit_fn(u0 + j, j)

    def body(c, carry):
        chunk(c * MOE_CHUNK, MOE_CHUNK)
        return carry

    n_big = lax.shift_right_logical(n, MOE_CHUNK.bit_length() - 1)
    lax.fori_loop(0, n_big, body, 0)
    base = n_big * MOE_CHUNK
    rest = n & (MOE_CHUNK - 1)
    for count in range(1, MOE_CHUNK):
        @pl.when(rest == count)
        def _():
            chunk(base, count)


def _moe_kernel(tok_ref, tab_ref,
                xn_hbm, wgu_hbm, wd_hbm, bgu_hbm, bd_hbm,
                y_hbm,
                xraw_ref, xb_ref, act_ref, wst_ref, wbf_ref, bgu_ref, bd_ref,
                ystage_ref, zbuf_ref, gsem, ysem, zsem, wsem, bsem):
    i = pl.program_id(0)
    nblk = tab_ref[TAB_NBLK +i]
    row0 = tab_ref[TAB_ROW0 +i]
    valid = nblk > 0
    par = i & 1

    def gu_copies(e, tile, slot):
        col = pl.multiple_of(tile * MOE_TF, MOE_TF)
        return (pltpu.make_async_copy(wgu_hbm.at[e, :, pl.ds(col, MOE_TF)],
                                      wst_ref.at[slot, :, pl.ds(0, MOE_TF)], wsem.at[slot]),
                pltpu.make_async_copy(wgu_hbm.at[e, :, pl.ds(D_FF + col, MOE_TF)],
                                      wst_ref.at[slot, :, pl.ds(MOE_TF, MOE_TF)], wsem.at[slot]))

    def d_copy(e, tile, slot):
        col = pl.multiple_of(tile * MOE_TN, MOE_TN)
        return pltpu.make_async_copy(wd_hbm.at[e, :, pl.ds(col, MOE_TN)], wst_ref.at[slot], wsem.at[slot])

    def start_tile(e, t):
        slot = lax.rem(t, MOE_WSLOTS)

        @pl.when(t < MOE_P1)
        def _():
            for c in gu_copies(e, t, slot):
                c.start()

        @pl.when(t >= MOE_P1)
        def _():
            d_copy(e, t - MOE_P1, slot).start()

    def bias_copies(e, slot):
        return (pltpu.make_async_copy(bgu_hbm.at[e], bgu_ref.at[slot], bsem.at[slot]),
                pltpu.make_async_copy(bd_hbm.at[e], bd_ref.at[slot], bsem.at[slot]))

    def tail_copy(b):
        r = pl.multiple_of(b * MOE_RB, MOE_RB)
        return pltpu.make_async_copy(zbuf_ref, y_hbm.at[pl.ds(r, MOE_RB), :], zsem)

    def row_copy(tok, group, sub):
        return pltpu.make_async_copy(xn_hbm.at[pl.ds(tok, 1), :], xraw_ref.at[group, pl.ds(sub, 1), :], gsem)

    def unit_wait():
        blk = xraw_ref.at[pl.ds(0, MOE_RB // SUBLANES)]
        return pltpu.make_async_copy(blk, blk, gsem)

    def gather_rows(item, lo_unit, hi_unit):
        base = tab_ref[TAB_ROW0 +item]

        def body(c, carry):
            for j in range(SUBLANES):
                row_copy(tok_ref[base + c * SUBLANES + j], c, j).start()
            return carry

        per_unit = MOE_RB // SUBLANES
        lax.fori_loop(lo_unit * per_unit, hi_unit * per_unit, body, 0)

    def y_copy(slot, u, col):
        r = pl.multiple_of(row0 + u * MOE_RB, MOE_RB)
        return pltpu.make_async_copy(ystage_ref.at[slot], y_hbm.at[pl.ds(r, MOE_RB), pl.ds(col, MOE_TN)],
                                     ysem.at[slot])

    def dump_copy(slot):
        r = N_SLOTS + (slot // MOE_P2) * MOE_RB
        c = (slot % MOE_P2) * MOE_TN
        return pltpu.make_async_copy(ystage_ref.at[slot], y_hbm.at[pl.ds(r, MOE_RB), pl.ds(c, MOE_TN)],
                                     ysem.at[slot])

    @pl.when(i == 0)
    def _():
        for t in range(MOE_WAHEAD):
            for c in gu_copies(tab_ref[TAB_E +0], t, t):
                c.start()
        for c in bias_copies(tab_ref[TAB_E +0], 0):
            c.start()
        gather_rows(0, 0, tab_ref[TAB_NBLK +0])
        ystage_ref[...] = jnp.zeros_like(ystage_ref)
        for slot in range(MOE_YSLOTS):
            dump_copy(slot).start()
        zbuf_ref[...] = jnp.zeros_like(zbuf_ref)

        def fill(b, carry):
            tail_copy(b).start()
            return carry

        lax.fori_loop(tab_ref[TAB_USED], N_BLOCKS, fill, 0)

    def wait_rows(b, carry):
        unit_wait().wait()
        return carry

    lax.fori_loop(0, tab_ref[TAB_WAIT +i], wait_rows, 0)

    def conv(u, carry):
        rows = pl.ds(pl.multiple_of(u * MOE_RB, MOE_RB), MOE_RB)
        groups = pl.ds(u * (MOE_RB // SUBLANES), MOE_RB // SUBLANES)
        xb_ref[rows, :] = xraw_ref[groups].reshape(MOE_RB, D_MODEL).astype(BF16)
        return carry

    lax.fori_loop(0, nblk, conv, 0)

    e = tab_ref[TAB_E +i]
    e_next = tab_ref[TAB_E +i + 1]
    next_valid = tab_ref[TAB_NBLK +i + 1] > 0
    next_base = tab_ref[TAB_ROW0 +i + 1]

    @pl.when(valid)
    def _():
        for c in bias_copies(e, par):
            c.wait()

    def gate_up_step(s, carry):
        wslot = lax.rem(s, MOE_WSLOTS)
        for c in gu_copies(e, s, wslot):
            c.wait()
        start_tile(e, s + MOE_WAHEAD)
        wbf_ref[...] = wst_ref[wslot].astype(BF16)
        b_g = bgu_ref[par, pl.ds(s, 1), :]
        b_u = bgu_ref[par, pl.ds(MOE_P1 + s, 1), :]

        def unit(u, slot):
            r = pl.multiple_of(u * MOE_RB, MOE_RB)
            first = pl.multiple_of(r + s * MOE_G, MOE_G)
            group = lax.shift_right_logical(first, SUBLANES.bit_length() - 1)
            for j in range(MOE_G):
                row_copy(tok_ref[next_base + first + j], group + j // SUBLANES, j % SUBLANES).start()
            gu = jnp.dot(xb_ref[pl.ds(r, MOE_RB), :], wbf_ref[...], preferred_element_type=F32)
            gate = jnp.minimum(gu[:, 0:MOE_TF] + b_g, SWIGLU_LIMIT)
            up = jnp.clip(gu[:, MOE_TF:2 * MOE_TF] + b_u, -SWIGLU_LIMIT, SWIGLU_LIMIT)
            act = (up + 1) * (gate * jax.nn.sigmoid(gate * SWIGLU_ALPHA))
            act_ref[s, pl.ds(r, MOE_RB), :] = act.astype(BF16)

        _for_units(nblk, unit)
        return carry

    def down_step(s, carry):
        t = MOE_P1 + s
        wslot = lax.rem(t, MOE_WSLOTS)
        d_copy(e, s, wslot).wait()

        @pl.when(t + MOE_WAHEAD < MOE_TILES)
        def _():
            start_tile(e, t + MOE_WAHEAD)

        @pl.when((t + MOE_WAHEAD >= MOE_TILES) & next_valid)
        def _():
            start_tile(e_next, t + MOE_WAHEAD - MOE_TILES)

        @pl.when((t + MOE_WAHEAD == MOE_TILES) & next_valid)
        def _():
            for c in bias_copies(e_next, 1 - par):
                c.start()

        wbf_ref[...] = wst_ref[wslot].astype(BF16)
        b_d = bd_ref[par, pl.ds(s, 1), :]
        col = pl.multiple_of(s * MOE_TN, MOE_TN)

        def free_slots(slots):
            for slot in slots:
                y_copy(slot, 0, col).wait()

        def unit(u, slot):
            r = pl.multiple_of(u * MOE_RB, MOE_RB)
            a = jnp.concatenate([act_ref[j, pl.ds(r, MOE_RB), :] for j in range(MOE_P1)], axis=1)
            ystage_ref[slot] = jnp.dot(a, wbf_ref[...], preferred_element_type=F32) + b_d
            y_copy(slot, u, col).start()

        _for_units(nblk, unit, free_slots)
        return carry

    @pl.when(valid)
    def _():
        lax.fori_loop(0, MOE_P1, gate_up_step, 0)
        gather_rows(i + 1, nblk, tab_ref[TAB_NBLK +i + 1])
        lax.fori_loop(0, MOE_P2, down_step, 0)

    @pl.when(i == MOE_NI - 1)
    def _():
        for slot in range(MOE_YSLOTS):
            dump_copy(slot).wait()

        lax.fori_loop(0, tab_ref[TAB_WAIT +MOE_NI], wait_rows, 0)

        def drain(b, carry):
            tail_copy(b).wait()
            return carry

        lax.fori_loop(tab_ref[TAB_USED], N_BLOCKS, drain, 0)


def _moe_call(slot_tok, tables, xn, w_gate_up, w_down, b_gate_up, b_down):
    any_spec = pl.BlockSpec(memory_space=pl.ANY)
    grid_spec = pltpu.PrefetchScalarGridSpec(
        num_scalar_prefetch=2,
        grid=(MOE_NI,),
        in_specs=[any_spec] * 5,
        out_specs=any_spec,
        scratch_shapes=[
            pltpu.VMEM((MOE_RMAX // SUBLANES, SUBLANES, D_MODEL), F32),
            pltpu.VMEM((MOE_RMAX, D_MODEL), BF16),
            pltpu.VMEM((MOE_P1, MOE_RMAX, MOE_TF), BF16),
            pltpu.VMEM((MOE_WSLOTS, D_MODEL, MOE_TN), F32),
            pltpu.VMEM((D_MODEL, MOE_TN), BF16),
            pltpu.VMEM((2, 2 * MOE_P1, MOE_TF), F32),
            pltpu.VMEM((2, MOE_P2, MOE_TN), F32),
            pltpu.VMEM((MOE_YSLOTS, MOE_RB, MOE_TN), F32),
            pltpu.VMEM((MOE_RB, D_MODEL), F32),
            pltpu.SemaphoreType.DMA(()),
            pltpu.SemaphoreType.DMA((MOE_YSLOTS,)),
            pltpu.SemaphoreType.DMA(()),
            pltpu.SemaphoreType.DMA((MOE_WSLOTS,)),
            pltpu.SemaphoreType.DMA((2,)),
        ],
    )
    spare_blocks = MOE_YSLOTS // MOE_P2
    return pl.pallas_call(
        _moe_kernel,
        grid_spec=grid_spec,
        out_shape=jax.ShapeDtypeStruct((N_SLOTS + spare_blocks * MOE_RB, D_MODEL), F32),
        compiler_params=_cparams(("arbitrary",), 56),
        name="moe",
    )(slot_tok, tables, xn, w_gate_up, w_down, b_gate_up, b_down)


def _combine_kernel(dest_ref, y_hbm, x1_ref, rg_ref, ps_ref, pp_ref, wple_ref, wpg_ref, gple_ref,
                    gfin_ref, ys_ref, yp_ref, gbuf_ref, gsem):
    m = pl.program_id(0)
    nm = pl.num_programs(0)
    tm = CMB_TM
    slot = m % 2

    def row_copy(tile, slot_, r, k):
        d = dest_ref[k * T + tile * tm + r]
        return pltpu.make_async_copy(y_hbm.at[pl.ds(d, 1), :], gbuf_ref.at[slot_, k, pl.ds(r, 1), :],
                                     gsem.at[slot_])

    def wait_tile(slot_):
        for k in range(TOP_K):
            pltpu.make_async_copy(y_hbm.at[pl.ds(0, tm), :], gbuf_ref.at[slot_, k], gsem.at[slot_]).wait()

    @pl.when(m == 0)
    def _():
        def body(c, carry):
            for j in range(GATHER_UNROLL // TOP_K):
                for k in range(TOP_K):
                    row_copy(0, 0, c * (GATHER_UNROLL // TOP_K) + j, k).start()
            return carry

        lax.fori_loop(0, tm // (GATHER_UNROLL // TOP_K), body, 0)

    wait_tile(slot)
    gates = rg_ref[...]
    moe = gates[:, 0:1] * gbuf_ref[slot, 0]
    for k in range(1, TOP_K):
        moe = moe + gates[:, k:k + 1] * gbuf_ref[slot, k]
    x2 = x1_ref[...] + moe
    is_s = m < CMB_NS
    p = jnp.where(is_s, ps_ref[...], pp_ref[...]).astype(BF16)
    hn = _rms(x2, gple_ref[...]).astype(BF16)
    nxt = jnp.minimum(m + 1, nm - 1)
    rows_per_chunk = tm // CMB_CHUNKS
    cw = D_MODEL // CMB_CHUNKS
    x3_parts = []
    for c in range(CMB_CHUNKS):
        cols = slice(c * cw, (c + 1) * cw)
        pe = jnp.dot(p, wple_ref[:, cols], preferred_element_type=F32)
        gate = jax.nn.sigmoid(jnp.dot(hn, wpg_ref[:, cols], preferred_element_type=F32))
        x3_parts.append(x2[:, cols] + pe * gate)
        for r in range(c * rows_per_chunk, (c + 1) * rows_per_chunk):
            for k in range(TOP_K):
                row_copy(nxt, 1 - slot, r, k).start()
    x3 = jnp.concatenate(x3_parts, axis=1)
    y = _rms(x3, gfin_ref[...])

    @pl.when(is_s)
    def _():
        ys_ref[...] = y

    @pl.when(jnp.logical_not(is_s))
    def _():
        yp_ref[...] = y

    @pl.when(m == nm - 1)
    def _():
        wait_tile(1 - slot)


def _combine_call(dest, y_sorted, x1, rg, ps, pp, wple, wpg, g_ple, g_final):
    tm = CMB_TM
    ns = CMB_NS
    s_idx = lambda m, d: (jnp.minimum(m, ns - 1), 0)
    p_idx = lambda m, d: (jnp.maximum(m - ns, 0), 0)
    row = lambda m, d: (m, 0)
    const2 = lambda m, d: (0, 0)
    grid_spec = pltpu.PrefetchScalarGridSpec(
        num_scalar_prefetch=1,
        grid=(T // tm,),
        in_specs=[
            pl.BlockSpec(memory_space=pl.ANY),
            pl.BlockSpec((tm, D_MODEL), row),
            pl.BlockSpec((tm, LANES), row),
            pl.BlockSpec((tm, PLE_DIM), s_idx),
            pl.BlockSpec((tm, PLE_DIM), p_idx),
            pl.BlockSpec((PLE_DIM, D_MODEL), const2),
            pl.BlockSpec((D_MODEL, D_MODEL), const2),
            pl.BlockSpec((1, D_MODEL), const2),
            pl.BlockSpec((1, D_MODEL), const2),
        ],
        out_specs=[
            pl.BlockSpec((tm, D_MODEL), s_idx),
            pl.BlockSpec((tm, D_MODEL), p_idx),
        ],
        scratch_shapes=[
            pltpu.VMEM((2, TOP_K, tm, D_MODEL), F32),
            pltpu.SemaphoreType.DMA((2,)),
        ],
    )
    return pl.pallas_call(
        _combine_kernel,
        grid_spec=grid_spec,
        out_shape=[
            jax.ShapeDtypeStruct((T_S, D_MODEL), F32),
            jax.ShapeDtypeStruct((T_P, D_MODEL), F32),
        ],
        compiler_params=_cparams(("arbitrary",), 56),
        name="combine",
    )(dest, y_sorted, x1, rg, ps, pp, wple, wpg, g_ple, g_final)


def _route_kernel(lgt_ref, dest_ref, tok_ref, tab_ref, rg_ref,
                  rit_ref, dvm_ref, zvm_ref, carry_ref, cnt_ref, gs_ref, sem):
    rb_shift = MOE_RB.bit_length() - 1
    tb = ROUTE_TB
    sub = tb // LANES

    a_i = lax.broadcasted_iota(I32, (LANES, LANES), 0)
    b_i = lax.broadcasted_iota(I32, (LANES, LANES), 1)
    earlier = jnp.where(a_i < b_i, 1.0, 0.0).astype(BF16)
    carry_ref[...] = jnp.zeros_like(carry_ref)

    def tile(c, loop_carry):
        c0 = pl.multiple_of(c * tb, tb)
        work = lgt_ref[:, pl.ds(c0, tb)]
        row = lax.broadcasted_iota(I32, (N_EXPERTS, tb), 0).astype(F32)
        vals, idxs, hots = [], [], []
        for _ in range(TOP_K):
            mx = jnp.max(work, axis=0, keepdims=True)
            idx = jnp.min(jnp.where(work == mx, row, float(N_EXPERTS)), axis=0, keepdims=True)
            hot = row == idx
            work = jnp.where(hot, -jnp.inf, work)
            vals.append(mx)
            idxs.append(idx)
            hots.append(hot)
        exps = [jnp.exp(vk - vals[0]) for vk in vals]
        den = exps[0] + exps[1] + exps[2] + exps[3]
        chosen = jnp.zeros((N_EXPERTS, tb), F32)
        for hot in hots:
            chosen = chosen + jnp.where(hot, 1.0, 0.0)
        counts = carry_ref[...]
        before = []
        for j in range(sub):
            cj = chosen[:, j * LANES:(j + 1) * LANES]
            before.append(jnp.dot(cj.astype(BF16), earlier, preferred_element_type=F32) + counts)
            counts = counts + jnp.sum(cj, axis=1, keepdims=True)
        carry_ref[...] = counts
        before = jnp.concatenate(before, axis=1)
        for k in range(TOP_K):
            rank = jnp.sum(jnp.where(hots[k], before, 0.0), axis=0, keepdims=True)
            rit_ref[k:k + 1, pl.ds(c0, tb)] = idxs[k].astype(I32)
            rit_ref[TOP_K + k:TOP_K + k + 1, pl.ds(c0, tb)] = rank.astype(I32)
        gates = jnp.concatenate([ek / den for ek in exps], axis=0)
        for j in range(sub):
            g_tile = jnp.concatenate([gates[:, j * LANES:(j + 1) * LANES],
                                      jnp.zeros((LANES - TOP_K, LANES), F32)], axis=0)
            rg_ref[pl.ds(pl.multiple_of(c0 + j * LANES, LANES), LANES), :] = jnp.transpose(g_tile)
        return loop_carry

    lax.fori_loop(0, T // tb, tile, 0)
    cnt_copy = pltpu.make_async_copy(carry_ref, cnt_ref, sem)
    cnt_copy.start()
    cnt_copy.wait()

    def clear(j, carry):
        tab_ref[j] = 0
        return carry

    lax.fori_loop(0, TAB_SIZE, clear, 0)

    def expert(e, carry):
        acc, item, used = carry
        n = lax.shift_right_logical(cnt_ref[e, 0].astype(I32) + (MOE_RB - 1), rb_shift)
        gs_ref[e] = acc

        def add_item(local, it):
            tab_ref[TAB_E + it] = e
            tab_ref[TAB_ROW0 + it] = acc + local * MOE_RMAX
            tab_ref[TAB_NBLK + it] = jnp.minimum(MOE_BMAX, n - local * MOE_BMAX)
            return it + 1

        item = lax.fori_loop(0, lax.div(n + (MOE_BMAX - 1), MOE_BMAX), add_item, item)
        return acc + n * MOE_RB, item, used + n

    _, n_items, used = lax.fori_loop(0, N_EXPERTS, expert, (jnp.int32(0), jnp.int32(0), jnp.int32(0)))
    tab_ref[TAB_USED] = used
    e_last = tab_ref[TAB_E + n_items - 1]

    def pad_item(it, carry):
        tab_ref[TAB_E + it] = e_last
        tab_ref[TAB_ROW0 + it] = 0
        tab_ref[TAB_NBLK + it] = 0
        return carry

    lax.fori_loop(n_items, MOE_NI + 1, pad_item, 0)

    def wait_units(it, prev):
        nb = tab_ref[TAB_NBLK + it]
        tab_ref[TAB_WAIT + it] = jnp.maximum(nb, prev)
        return nb

    lax.fori_loop(0, MOE_NI + 1, wait_units, jnp.int32(0))

    e_idx = rit_ref[0:TOP_K, :]
    d = rit_ref[TOP_K:2 * TOP_K, :]
    for e in range(N_EXPERTS):
        d = d + jnp.where(e_idx == e, gs_ref[e], 0)
    dvm_ref[0:TOP_K, :] = d
    zvm_ref[...] = jnp.zeros_like(zvm_ref)
    copies = [pltpu.make_async_copy(dvm_ref.at[k], dest_ref.at[pl.ds(k * T, T)], sem) for k in range(TOP_K)]
    copies.append(pltpu.make_async_copy(zvm_ref, tok_ref, sem))
    for c in copies:
        c.start()
    for c in copies:
        c.wait()

    for k in range(TOP_K):
        def scatter(c, carry):
            for j in range(GATHER_UNROLL):
                t = c * GATHER_UNROLL + j
                tok_ref[dest_ref[k * T + t]] = t
            return carry

        lax.fori_loop(0, T // GATHER_UNROLL, scatter, 0)


def _route_call(logits_t):
    smem = pl.BlockSpec(memory_space=pltpu.SMEM)
    vmem = pl.BlockSpec(memory_space=pltpu.VMEM)
    return pl.pallas_call(
        _route_kernel,
        in_specs=[vmem],
        out_specs=[smem, smem, smem, vmem],
        out_shape=[
            jax.ShapeDtypeStruct((TOP_K * T,), I32),
            jax.ShapeDtypeStruct((N_TOK_TAB,), I32),
            jax.ShapeDtypeStruct((TAB_SIZE,), I32),
            jax.ShapeDtypeStruct((T, LANES), F32),
        ],
        scratch_shapes=[
            pltpu.VMEM((2 * TOP_K, T), I32),
            pltpu.VMEM((2 * TOP_K, T), I32),
            pltpu.VMEM((N_TOK_TAB,), I32),
            pltpu.VMEM((N_EXPERTS, LANES), F32),
            pltpu.SMEM((N_EXPERTS, LANES), F32),
            pltpu.SMEM((N_EXPERTS,), I32),
            pltpu.SemaphoreType.DMA(()),
        ],
        compiler_params=pltpu.CompilerParams(vmem_limit_bytes=32 * MIB),
        name="route",
    )(logits_t)


def kernel(x_prompt, x_sample, state_conv, p_prompt, p_sample, g_mix, w_in, ln_v_g, ln_v_b, w_s, b_s,
           conv_w, conv_b, w_proj_a, w_proj_b, w_o, g_moe, w_router, b_router, w_gate_up, b_gate_up,
           w_down, b_down, g_ple, w_ple, w_ple_gate, g_final):
    assert g_mix.shape[0] == 1, "one layer"
    xs = x_sample.reshape(T_S, D_MODEL)
    xp = x_prompt.reshape(T_P, D_MODEL)

    tril = jnp.tril(jnp.ones((CHUNK, CHUNK), bool))
    w_prompt = jnp.where(tril[None], w_s[0], 0.0)
    small = jnp.where(tril[None, :DEC_SEQ, :DEC_SEQ], w_s[0, :, :DEC_SEQ, :DEC_SEQ], 0.0)
    reps = CHUNK // DEC_SEQ
    blockdiag = jnp.kron(jnp.eye(reps, dtype=F32), jnp.ones((DEC_SEQ, DEC_SEQ), F32))
    w_sample = jnp.tile(small, (1, reps, reps)) * blockdiag[None]
    wsg = jnp.stack([w_sample, w_prompt]).astype(BF16)
    bias_p = jnp.repeat(b_s[0].T, GW_A, axis=1)
    bias_s = jnp.tile(jnp.repeat(b_s[0, :, :DEC_SEQ].T, GW_A, axis=1), (reps, 1))
    bsg = jnp.stack([bias_s, bias_p])
    ext = jnp.pad(state_conv[0], ((0, 0), (0, DEC_SEQ - (CONV_W - 1)), (0, 0))).reshape(T_S, D_B)

    wr = jnp.pad(w_router[0], ((0, 0), (0, LANES - N_EXPERTS)))
    wr_hi = wr.astype(BF16)
    wr_lo = (wr - wr_hi.astype(F32)).astype(BF16)
    b_r = jnp.pad(b_router[0], (0, LANES - N_EXPERTS), constant_values=NEG_BIG).reshape(1, LANES)

    h = _norm_call(xs, xp, g_mix)
    z, vpre = _in_proj_call(h, w_in[0])
    x1, xn, logits_t, vln, cxs, tail = _mixer_call(
        xs, xp, z, vpre, ext, wsg, bsg, ln_v_g, ln_v_b, conv_w[0], conv_b,
        w_proj_a[0].astype(BF16), w_proj_b[0].astype(BF16), w_o[0].astype(BF16), g_moe,
        wr_hi, wr_lo, b_r)

    dest, slot_tok, tables, route_g = _route_call(logits_t)
    y_sorted = _moe_call(slot_tok, tables, xn, w_gate_up[0], w_down[0],
                         b_gate_up[0].reshape(N_EXPERTS, 2 * MOE_P1, MOE_TF),
                         b_down[0].reshape(N_EXPERTS, MOE_P2, MOE_TN))
    ys, yp = _combine_call(dest, y_sorted, x1, route_g,
                           p_sample[0].reshape(T_S, PLE_DIM), p_prompt[0].reshape(T_P, PLE_DIM),
                           w_ple[0].astype(BF16), w_ple_gate[0].astype(BF16), g_ple, g_final.reshape(1, D_MODEL))

    y_prompt = yp.reshape(BATCH, SEQ, D_MODEL)
    y_sample = ys.reshape(DEC_BATCH, DEC_SEQ, D_MODEL)
    last = tail[:MIX_NT - MIX_NS].reshape(BATCH, MIX_SEQ_TILES, SUBLANES, D_B)[:, -1, SUBLANES - (CONV_W - 1):, :]
    state_conv_prompt = last[None]
    state_conv_sample = cxs[:T_S].reshape(DEC_BATCH, DEC_SEQ, D_B)[:, DEC_SEQ - (CONV_W - 1):, :][None]
    state_chunk_v_sample = vln[:T_S].reshape(DEC_BATCH, DEC_SEQ, D_A)[None]
    return (y_prompt, y_sample, state_conv_prompt, state_conv_sample, state_chunk_v_sample)
```

```python
import functools

import jax
import jax.numpy as jnp
from jax import lax
from jax.experimental import pallas as pl
from jax.experimental.pallas import tpu as pltpu

F32 = jnp.float32
BF16 = jnp.bfloat16
I32 = jnp.int32

D_MODEL = 2048
BATCH = 4
SEQ = 2048
DEC_BATCH = 128
DEC_SEQ = 8
CHUNK = 128
D_A = D_MODEL // 2
N_GROUPS_A = 8
GW_A = D_A // N_GROUPS_A
D_B = D_MODEL // 2
CONV_W = 3
N_EXPERTS = 32
TOP_K = 4
D_FF = D_MODEL
SWIGLU_LIMIT = 7.0
SWIGLU_ALPHA = 1.702
PLE_DIM = 256
EPS = 1e-6

T_S = DEC_BATCH * DEC_SEQ
T_P = BATCH * SEQ
T = T_S + T_P

LANES = 128
SUBLANES = 8
MIB = 1024 * 1024

NORM_TM = 512
IN_TM = 1024
IN_TN = 1024
IN_SUB = 256
MIX_TM = CHUNK
MIX_NS = T_S // MIX_TM
MIX_SEQ_TILES = SEQ // MIX_TM
MIX_NT = T // MIX_TM
MOE_RB = 128
MOE_BMAX = 14
MOE_RMAX = MOE_RB * MOE_BMAX
MOE_CHUNK = 8
MOE_TF = 256
MOE_TN = 512
MOE_P1 = D_FF // MOE_TF
MOE_P2 = D_MODEL // MOE_TN
MOE_TILES = MOE_P1 + MOE_P2
MOE_WSLOTS = 3
MOE_WAHEAD = MOE_WSLOTS - 1
assert D_MODEL == D_FF and 2 * MOE_TF == MOE_TN and MOE_TILES % MOE_WSLOTS == 0
MOE_G = MOE_RB // MOE_P1
MOE_YSLOTS = 8
N_SLOTS = T * TOP_K + N_EXPERTS * MOE_RB
N_BLOCKS = N_SLOTS // MOE_RB
MOE_NI = (N_BLOCKS + N_EXPERTS * (MOE_BMAX - 1)) // MOE_BMAX
GATHER_UNROLL = 8
TAB_STRIDE = 64
TAB_E, TAB_ROW0, TAB_NBLK, TAB_WAIT, TAB_USED = 0, TAB_STRIDE, 2 * TAB_STRIDE, 3 * TAB_STRIDE, 4 * TAB_STRIDE
TAB_SIZE = 5 * TAB_STRIDE
assert MOE_NI + 1 <= TAB_STRIDE
N_TOK_TAB = -(-(N_SLOTS + MOE_RMAX) // 1024) * 1024
ROUTE_TB = 512
CMB_TM = 256
CMB_NS = T_S // CMB_TM
CMB_CHUNKS = 8
NEG_BIG = -1e30


def _rms(x, g):
    return x * lax.rsqrt(jnp.mean(x * x, axis=-1, keepdims=True) + EPS) * g


def _cparams(sem, vmem_mib):
    return pltpu.CompilerParams(dimension_semantics=sem, vmem_limit_bytes=vmem_mib * MIB)


def _resident(shape):
    zeros = (0,) * len(shape)
    return pl.BlockSpec(shape, lambda *_: zeros, pipeline_mode=pl.Buffered(1))


def _norm_kernel(xs_ref, xp_ref, g_ref, h_ref, *, ns):
    m = pl.program_id(0)
    x = jnp.where(m < ns, xs_ref[...], xp_ref[...])
    h_ref[...] = _rms(x, g_ref[...]).astype(BF16)


def _norm_call(xs, xp, g):
    ns = T_S // NORM_TM
    return pl.pallas_call(
        functools.partial(_norm_kernel, ns=ns),
        grid=(T // NORM_TM,),
        in_specs=[
            pl.BlockSpec((NORM_TM, D_MODEL), lambda m: (jnp.minimum(m, ns - 1), 0)),
            pl.BlockSpec((NORM_TM, D_MODEL), lambda m: (jnp.maximum(m - ns, 0), 0)),
            pl.BlockSpec((1, D_MODEL), lambda m: (0, 0)),
        ],
        out_specs=pl.BlockSpec((NORM_TM, D_MODEL), lambda m: (m, 0)),
        out_shape=jax.ShapeDtypeStruct((T, D_MODEL), BF16),
        compiler_params=_cparams(("arbitrary",), 32),
        name="norm",
    )(xs, xp, g)


IN_N_GELU = 2 * D_A // IN_TN
IN_N_V0 = D_A // IN_TN
IN_N_LIN = (2 * D_A + 3 * D_B) // IN_TN


def _in_proj_kernel(h_ref, w_ref, z_ref, vpre_ref, wb_ref):
    n = pl.program_id(0)

    @pl.when(pl.program_id(1) == 0)
    def _():
        wb_ref[...] = w_ref[...].astype(BF16)

    def blocks(epilogue):
        for b in range(IN_TM // IN_SUB):
            rows = pl.ds(b * IN_SUB, IN_SUB)
            epilogue(rows, jnp.dot(h_ref[rows, :], wb_ref[...], preferred_element_type=F32))

    @pl.when(n < IN_N_V0)
    def _():
        def ep(rows, acc):
            z_ref[rows, :] = jax.nn.gelu(acc, approximate=True).astype(BF16)

        blocks(ep)

    @pl.when((n >= IN_N_V0) & (n < IN_N_GELU))
    def _():
        def ep(rows, acc):
            g = jax.nn.gelu(acc, approximate=True)
            z_ref[rows, :] = g.astype(BF16)
            vpre_ref[rows, :] = g

        blocks(ep)

    @pl.when((n >= IN_N_GELU) & (n < IN_N_LIN))
    def _():
        def ep(rows, acc):
            z_ref[rows, :] = acc.astype(BF16)

        blocks(ep)

    @pl.when(n >= IN_N_LIN)
    def _():
        def ep(rows, acc):
            z_ref[rows, :] = jax.nn.sigmoid(acc).astype(BF16)

        blocks(ep)


def _in_proj_call(h, w_in):
    d_in = w_in.shape[1]
    n_m = T // IN_TM

    def vpre_map(n, m):
        row = jnp.where(n < IN_N_V0, 0, jnp.where(n < IN_N_GELU, m, n_m - 1))
        return (row, jnp.clip(n - IN_N_V0, 0, IN_N_GELU - IN_N_V0 - 1))

    return pl.pallas_call(
        _in_proj_kernel,
        grid=(d_in // IN_TN, n_m),
        in_specs=[
            pl.BlockSpec((IN_TM, D_MODEL), lambda n, m: (m, 0)),
            pl.BlockSpec((D_MODEL, IN_TN), lambda n, m: (0, n)),
        ],
        out_specs=[
            pl.BlockSpec((IN_TM, IN_TN), lambda n, m: (m, n)),
            pl.BlockSpec((IN_TM, IN_TN), vpre_map),
        ],
        out_shape=[
            jax.ShapeDtypeStruct((T, d_in), BF16),
            jax.ShapeDtypeStruct((T, D_A), F32),
        ],
        scratch_shapes=[pltpu.VMEM((D_MODEL, IN_TN), BF16)],
        compiler_params=_cparams(("arbitrary", "arbitrary"), 48),
        name="in_proj",
    )(h, w_in)


def _mixer_kernel(xs_ref, xp_ref, zu_ref, zb_ref, zc_ref, zx_ref, ga0_ref, ga1_ref, gb0_ref, gb1_ref,
                  vpre_ref, ext_ref, wsg_ref, bsg_ref, lng_ref, lnb_ref,
                  cw_ref, cb_ref, wa_ref, wb_ref, wo_ref, gmoe_ref, wrh_ref, wrl_ref, br_ref,
                  x1_ref, xn_ref, lgt_ref, vln_ref, cxs_ref, tail_ref,
                  prev_ref, ab_ref):
    m = pl.program_id(0)
    tm = MIX_TM
    a_tile = jnp.minimum(m, MIX_NT - 1)
    is_s = a_tile < MIX_NS
    b_is_s = (m - 1) < MIX_NS
    slot = m & 1

    @pl.when(m == 0)
    def _():
        prev_ref[...] = jnp.zeros_like(prev_ref)
        ab_ref[...] = jnp.zeros_like(ab_ref)

    y_a = jnp.dot(ab_ref[1 - slot, :, 0:D_A], wa_ref[...], preferred_element_type=F32)
    y_b = jnp.dot(ab_ref[1 - slot, :, D_A:D_A + D_B], wb_ref[...], preferred_element_type=F32)

    vg = vpre_ref[...]
    mu = jnp.mean(vg, axis=-1, keepdims=True)
    vc = vg - mu
    v = vc * lax.rsqrt(jnp.mean(vc * vc, axis=-1, keepdims=True) + EPS) * lng_ref[...] + lnb_ref[...]
    vln_ref[...] = v

    vb = v.astype(BF16)
    s_parts = []
    for g in range(N_GROUPS_A):
        s_parts.append(jnp.dot(wsg_ref[0, g], vb[:, g * GW_A:(g + 1) * GW_A], preferred_element_type=F32))
    s = jnp.concatenate(s_parts, axis=1) + bsg_ref[0]
    u = zu_ref[...].astype(F32)
    a_in = (u * s).astype(BF16)

    ga = jnp.concatenate([ga0_ref[...], ga1_ref[...]], axis=1).astype(F32)
    gb = jnp.concatenate([gb0_ref[...], gb1_ref[...]], axis=1).astype(F32)
    mix = (ga * y_a + gb * y_b).astype(BF16)
    x = jnp.where(b_is_s, xs_ref[...], xp_ref[...])
    x1 = x + jnp.dot(mix, wo_ref[...], preferred_element_type=F32)
    x1_ref[...] = x1

    bg = zb_ref[...].astype(F32)
    cg = zc_ref[...].astype(F32)
    xin = zx_ref[...].astype(F32)
    cx = cg * xin
    cxs_ref[...] = cx
    tail_ref[0] = cx[tm - SUBLANES:tm]

    row = lax.broadcasted_iota(I32, (tm, D_B), 0)
    seq_start = ((a_tile - MIX_NS) % MIX_SEQ_TILES) == 0
    prev = jnp.where(seq_start, 0.0, prev_ref[...])
    row8 = lax.broadcasted_iota(I32, (SUBLANES, D_B), 0)
    top = jnp.where(row8 < CONV_W - 1, pltpu.roll(prev, CONV_W - 1, 0), 0.0)
    ext_p = jnp.concatenate([top, jnp.zeros((tm - SUBLANES, D_B), F32)], axis=0)
    ext = jnp.where(is_s, ext_ref[...], ext_p)
    t_in = jnp.where(is_s, row & (DEC_SEQ - 1), row)
    s1 = jnp.where(t_in < 1, pltpu.roll(ext, tm - 1, 0), pltpu.roll(cx, 1, 0))
    s2 = jnp.where(t_in < 2, ext, pltpu.roll(cx, 2, 0))
    prev_ref[...] = cx[tm - SUBLANES:tm]
    conv = cb_ref[...] + s2 * cw_ref[0:1, :] + s1 * cw_ref[1:2, :] + cx * cw_ref[2:3, :]
    b_in = (bg * conv).astype(BF16)

    xn = _rms(x1, gmoe_ref[...])
    xn_ref[...] = xn
    hi = xn.astype(BF16)
    lo = (xn - hi.astype(F32)).astype(BF16)
    logits = (jnp.dot(hi, wrh_ref[...], preferred_element_type=F32)
              + jnp.dot(lo, wrh_ref[...], preferred_element_type=F32)
              + jnp.dot(hi, wrl_ref[...], preferred_element_type=F32)) + br_ref[...]

    lgt_ref[...] = jnp.transpose(logits)[0:N_EXPERTS, :]

    ab_ref[slot, :, 0:D_A] = a_in
    ab_ref[slot, :, D_A:D_A + D_B] = b_in


def _mixer_call(xs, xp, z, vpre, ext, wsg, bsg, ln_g, ln_b, conv_w, conv_b, wa, wb, wo, g_moe,
                wr_hi, wr_lo, b_r):
    tm = MIX_TM
    ns = MIX_NS
    nt = MIX_NT
    assert D_A == D_B and D_MODEL == 2 * D_A and z.shape[1] == 9 * D_A
    front = lambda m: jnp.minimum(m, nt - 1)
    back = lambda m: jnp.maximum(m - 1, 0)
    zcol = lambda tile_of, c: (lambda m: (tile_of(m), c))
    row = lambda m: (back(m), 0)
    s_out = lambda m: (jnp.minimum(front(m), ns), 0)
    p_out = lambda m: (jnp.where(front(m) < ns, nt - ns, front(m) - ns), 0, 0)
    sel = lambda m: jnp.minimum(front(m) // ns, 1)
    return pl.pallas_call(
        _mixer_kernel,
        grid=(nt + 1,),
        in_specs=[
            pl.BlockSpec((tm, D_MODEL), lambda m: (jnp.minimum(back(m), ns - 1), 0)),
            pl.BlockSpec((tm, D_MODEL), lambda m: (jnp.maximum(back(m) - ns, 0), 0)),
            pl.BlockSpec((tm, D_A), zcol(front, 0)),
            pl.BlockSpec((tm, D_A), zcol(front, 2)),
            pl.BlockSpec((tm, D_A), zcol(front, 3)),
            pl.BlockSpec((tm, D_A), zcol(front, 4)),
            pl.BlockSpec((tm, D_A), zcol(back, 5)),
            pl.BlockSpec((tm, D_A), zcol(back, 6)),
            pl.BlockSpec((tm, D_A), zcol(back, 7)),
            pl.BlockSpec((tm, D_A), zcol(back, 8)),
            pl.BlockSpec((tm, D_A), lambda m: (front(m), 0)),
            pl.BlockSpec((tm, D_B), lambda m: (jnp.minimum(front(m), ns - 1), 0)),
            pl.BlockSpec((1, N_GROUPS_A, CHUNK, CHUNK), lambda m: (sel(m), 0, 0, 0)),
            pl.BlockSpec((1, CHUNK, D_A), lambda m: (sel(m), 0, 0)),
            _resident((1, D_A)),
            _resident((1, D_A)),
            _resident((CONV_W, D_B)),
            _resident((1, D_B)),
            _resident((D_A, D_MODEL)),
            _resident((D_B, D_MODEL)),
            _resident((D_MODEL, D_MODEL)),
            _resident((1, D_MODEL)),
            _resident((D_MODEL, LANES)),
            _resident((D_MODEL, LANES)),
            _resident((1, LANES)),
        ],
        out_specs=[
            pl.BlockSpec((tm, D_MODEL), row),
            pl.BlockSpec((tm, D_MODEL), row),
            pl.BlockSpec((N_EXPERTS, tm), lambda m: (0, back(m))),
            pl.BlockSpec((tm, D_A), s_out),
            pl.BlockSpec((tm, D_B), s_out),
            pl.BlockSpec((1, SUBLANES, D_B), p_out),
        ],
        out_shape=[
            jax.ShapeDtypeStruct((T, D_MODEL), F32),
            jax.ShapeDtypeStruct((T, D_MODEL), F32),
            jax.ShapeDtypeStruct((N_EXPERTS, T), F32),
            jax.ShapeDtypeStruct((T_S + tm, D_A), F32),
            jax.ShapeDtypeStruct((T_S + tm, D_B), F32),
            jax.ShapeDtypeStruct((nt - ns + 1, SUBLANES, D_B), F32),
        ],
        scratch_shapes=[pltpu.VMEM((SUBLANES, D_B), F32), pltpu.VMEM((2, tm, D_A + D_B), BF16)],
        compiler_params=_cparams(("arbitrary",), 52),
        name="mixer",
    )(xs, xp, z, z, z, z, z, z, z, z, vpre, ext, wsg, bsg, ln_g, ln_b, conv_w, conv_b, wa, wb, wo, g_moe,
      wr_hi, wr_lo, b_r)


def _for_units(n, unit_fn, chunk_begin=None):
    def chunk(u0, count):
        if chunk_begin is not None:
            chunk_begin(range(count))
        for j in range(count):
            unit_fn(u0 + j, j)

    def body(c, carry):
        chunk(c * MOE_CHUNK, MOE_CHUNK)
        return carry

    n_big = lax.shift_right_logical(n, MOE_CHUNK.bit_length() - 1)
    lax.fori_loop(0, n_big, body, 0)
    base = n_big * MOE_CHUNK
    rest = n & (MOE_CHUNK - 1)
    for count in range(1, MOE_CHUNK):
        @pl.when(rest == count)
        def _():
            chunk(base, count)


def _moe_kernel(tok_ref, tab_ref,
                xn_hbm, wgu_hbm, wd_hbm, bgu_hbm, bd_hbm,
                y_hbm,
                xraw_ref, xb_ref, act_ref, wst_ref, wbf_ref, bgu_ref, bd_ref,
                ystage_ref, zbuf_ref, gsem, ysem, zsem, wsem, bsem):
    i = pl.program_id(0)
    nblk = tab_ref[TAB_NBLK +i]
    row0 = tab_ref[TAB_ROW0 +i]
    valid = nblk > 0
    par = i & 1

    def gu_copies(e, tile, slot):
        col = pl.multiple_of(tile * MOE_TF, MOE_TF)
        return (pltpu.make_async_copy(wgu_hbm.at[e, :, pl.ds(col, MOE_TF)],
                                      wst_ref.at[slot, :, pl.ds(0, MOE_TF)], wsem.at[slot]),
                pltpu.make_async_copy(wgu_hbm.at[e, :, pl.ds(D_FF + col, MOE_TF)],
                                      wst_ref.at[slot, :, pl.ds(MOE_TF, MOE_TF)], wsem.at[slot]))

    def d_copy(e, tile, slot):
        col = pl.multiple_of(tile * MOE_TN, MOE_TN)
        return pltpu.make_async_copy(wd_hbm.at[e, :, pl.ds(col, MOE_TN)], wst_ref.at[slot], wsem.at[slot])

    def start_tile(e, t):
        slot = lax.rem(t, MOE_WSLOTS)

        @pl.when(t < MOE_P1)
        def _():
            for c in gu_copies(e, t, slot):
                c.start()

        @pl.when(t >= MOE_P1)
        def _():
            d_copy(e, t - MOE_P1, slot).start()

    def bias_copies(e, slot):
        return (pltpu.make_async_copy(bgu_hbm.at[e], bgu_ref.at[slot], bsem.at[slot]),
                pltpu.make_async_copy(bd_hbm.at[e], bd_ref.at[slot], bsem.at[slot]))

    def tail_copy(b):
        r = pl.multiple_of(b * MOE_RB, MOE_RB)
        return pltpu.make_async_copy(zbuf_ref, y_hbm.at[pl.ds(r, MOE_RB), :], zsem)

    def row_copy(tok, group, sub):
        return pltpu.make_async_copy(xn_hbm.at[pl.ds(tok, 1), :], xraw_ref.at[group, pl.ds(sub, 1), :], gsem)

    def unit_wait():
        blk = xraw_ref.at[pl.ds(0, MOE_RB // SUBLANES)]
        return pltpu.make_async_copy(blk, blk, gsem)

    def gather_rows(item, lo_unit, hi_unit):
        base = tab_ref[TAB_ROW0 +item]

        def body(c, carry):
            for j in range(SUBLANES):
                row_copy(tok_ref[base + c * SUBLANES + j], c, j).start()
            return carry

        per_unit = MOE_RB // SUBLANES
        lax.fori_loop(lo_unit * per_unit, hi_unit * per_unit, body, 0)

    def y_copy(slot, u, col):
        r = pl.multiple_of(row0 + u * MOE_RB, MOE_RB)
        return pltpu.make_async_copy(ystage_ref.at[slot], y_hbm.at[pl.ds(r, MOE_RB), pl.ds(col, MOE_TN)],
                                     ysem.at[slot])

    def dump_copy(slot):
        r = N_SLOTS + (slot // MOE_P2) * MOE_RB
        c = (slot % MOE_P2) * MOE_TN
        return pltpu.make_async_copy(ystage_ref.at[slot], y_hbm.at[pl.ds(r, MOE_RB), pl.ds(c, MOE_TN)],
                                     ysem.at[slot])

    @pl.when(i == 0)
    def _():
        for t in range(MOE_WAHEAD):
            for c in gu_copies(tab_ref[TAB_E +0], t, t):
                c.start()
        for c in bias_copies(tab_ref[TAB_E +0], 0):
            c.start()
        gather_rows(0, 0, tab_ref[TAB_NBLK +0])
        ystage_ref[...] = jnp.zeros_like(ystage_ref)
        for slot in range(MOE_YSLOTS):
            dump_copy(slot).start()
        zbuf_ref[...] = jnp.zeros_like(zbuf_ref)

        def fill(b, carry):
            tail_copy(b).start()
            return carry

        lax.fori_loop(tab_ref[TAB_USED], N_BLOCKS, fill, 0)

    def wait_rows(b, carry):
        unit_wait().wait()
        return carry

    lax.fori_loop(0, tab_ref[TAB_WAIT +i], wait_rows, 0)

    def conv(u, carry):
        rows = pl.ds(pl.multiple_of(u * MOE_RB, MOE_RB), MOE_RB)
        groups = pl.ds(u * (MOE_RB // SUBLANES), MOE_RB // SUBLANES)
        xb_ref[rows, :] = xraw_ref[groups].reshape(MOE_RB, D_MODEL).astype(BF16)
        return carry

    lax.fori_loop(0, nblk, conv, 0)

    e = tab_ref[TAB_E +i]
    e_next = tab_ref[TAB_E +i + 1]
    next_valid = tab_ref[TAB_NBLK +i + 1] > 0
    next_base = tab_ref[TAB_ROW0 +i + 1]

    @pl.when(valid)
    def _():
        for c in bias_copies(e, par):
            c.wait()

    def gate_up_step(s, carry):
        wslot = lax.rem(s, MOE_WSLOTS)
        for c in gu_copies(e, s, wslot):
            c.wait()
        start_tile(e, s + MOE_WAHEAD)
        wbf_ref[...] = wst_ref[wslot].astype(BF16)
        b_g = bgu_ref[par, pl.ds(s, 1), :]
        b_u = bgu_ref[par, pl.ds(MOE_P1 + s, 1), :]

        def unit(u, slot):
            r = pl.multiple_of(u * MOE_RB, MOE_RB)
            first = pl.multiple_of(r + s * MOE_G, MOE_G)
            group = lax.shift_right_logical(first, SUBLANES.bit_length() - 1)
            for j in range(MOE_G):
                row_copy(tok_ref[next_base + first + j], group + j // SUBLANES, j % SUBLANES).start()
            gu = jnp.dot(xb_ref[pl.ds(r, MOE_RB), :], wbf_ref[...], preferred_element_type=F32)
            gate = jnp.minimum(gu[:, 0:MOE_TF] + b_g, SWIGLU_LIMIT)
            up = jnp.clip(gu[:, MOE_TF:2 * MOE_TF] + b_u, -SWIGLU_LIMIT, SWIGLU_LIMIT)
            act = (up + 1) * (gate * jax.nn.sigmoid(gate * SWIGLU_ALPHA))
            act_ref[s, pl.ds(r, MOE_RB), :] = act.astype(BF16)

        _for_units(nblk, unit)
        return carry

    def down_step(s, carry):
        t = MOE_P1 + s
        wslot = lax.rem(t, MOE_WSLOTS)
        d_copy(e, s, wslot).wait()

        @pl.when(t + MOE_WAHEAD < MOE_TILES)
        def _():
            start_tile(e, t + MOE_WAHEAD)

        @pl.when((t + MOE_WAHEAD >= MOE_TILES) & next_valid)
        def _():
            start_tile(e_next, t + MOE_WAHEAD - MOE_TILES)

        @pl.when((t + MOE_WAHEAD == MOE_TILES) & next_valid)
        def _():
            for c in bias_copies(e_next, 1 - par):
                c.start()

        wbf_ref[...] = wst_ref[wslot].astype(BF16)
        b_d = bd_ref[par, pl.ds(s, 1), :]
        col = pl.multiple_of(s * MOE_TN, MOE_TN)

        def free_slots(slots):
            for slot in slots:
                y_copy(slot, 0, col).wait()

        def unit(u, slot):
            r = pl.multiple_of(u * MOE_RB, MOE_RB)
            a = jnp.concatenate([act_ref[j, pl.ds(r, MOE_RB), :] for j in range(MOE_P1)], axis=1)
            ystage_ref[slot] = jnp.dot(a, wbf_ref[...], preferred_element_type=F32) + b_d
            y_copy(slot, u, col).start()

        _for_units(nblk, unit, free_slots)
        return carry

    @pl.when(valid)
    def _():
        lax.fori_loop(0, MOE_P1, gate_up_step, 0)
        gather_rows(i + 1, nblk, tab_ref[TAB_NBLK +i + 1])
        lax.fori_loop(0, MOE_P2, down_step, 0)

    @pl.when(i == MOE_NI - 1)
    def _():
        for slot in range(MOE_YSLOTS):
            dump_copy(slot).wait()

        lax.fori_loop(0, tab_ref[TAB_WAIT +MOE_NI], wait_rows, 0)

        def drain(b, carry):
            tail_copy(b).wait()
            return carry

        lax.fori_loop(tab_ref[TAB_USED], N_BLOCKS, drain, 0)


def _moe_call(slot_tok, tables, xn, w_gate_up, w_down, b_gate_up, b_down):
    any_spec = pl.BlockSpec(memory_space=pl.ANY)
    grid_spec = pltpu.PrefetchScalarGridSpec(
        num_scalar_prefetch=2,
        grid=(MOE_NI,),
        in_specs=[any_spec] * 5,
        out_specs=any_spec,
        scratch_shapes=[
            pltpu.VMEM((MOE_RMAX // SUBLANES, SUBLANES, D_MODEL), F32),
            pltpu.VMEM((MOE_RMAX, D_MODEL), BF16),
            pltpu.VMEM((MOE_P1, MOE_RMAX, MOE_TF), BF16),
            pltpu.VMEM((MOE_WSLOTS, D_MODEL, MOE_TN), F32),
            pltpu.VMEM((D_MODEL, MOE_TN), BF16),
            pltpu.VMEM((2, 2 * MOE_P1, MOE_TF), F32),
            pltpu.VMEM((2, MOE_P2, MOE_TN), F32),
            pltpu.VMEM((MOE_YSLOTS, MOE_RB, MOE_TN), F32),
            pltpu.VMEM((MOE_RB, D_MODEL), F32),
            pltpu.SemaphoreType.DMA(()),
            pltpu.SemaphoreType.DMA((MOE_YSLOTS,)),
            pltpu.SemaphoreType.DMA(()),
            pltpu.SemaphoreType.DMA((MOE_WSLOTS,)),
            pltpu.SemaphoreType.DMA((2,)),
        ],
    )
    spare_blocks = MOE_YSLOTS // MOE_P2
    return pl.pallas_call(
        _moe_kernel,
        grid_spec=grid_spec,
        out_shape=jax.ShapeDtypeStruct((N_SLOTS + spare_blocks * MOE_RB, D_MODEL), F32),
        compiler_params=_cparams(("arbitrary",), 56),
        name="moe",
    )(slot_tok, tables, xn, w_gate_up, w_down, b_gate_up, b_down)


def _combine_kernel(dest_ref, y_hbm, x1_ref, rg_ref, ps_ref, pp_ref, wple_ref, wpg_ref, gple_ref,
                    gfin_ref, ys_ref, yp_ref, gbuf_ref, gsem):
    m = pl.program_id(0)
    nm = pl.num_programs(0)
    tm = CMB_TM
    slot = m % 2

    def row_copy(tile, slot_, r, k):
        d = dest_ref[k * T + tile * tm + r]
        return pltpu.make_async_copy(y_hbm.at[pl.ds(d, 1), :], gbuf_ref.at[slot_, k, pl.ds(r, 1), :],
                                     gsem.at[slot_])

    def wait_tile(slot_):
        for k in range(TOP_K):
            pltpu.make_async_copy(y_hbm.at[pl.ds(0, tm), :], gbuf_ref.at[slot_, k], gsem.at[slot_]).wait()

    @pl.when(m == 0)
    def _():
        def body(c, carry):
            for j in range(GATHER_UNROLL // TOP_K):
                for k in range(TOP_K):
                    row_copy(0, 0, c * (GATHER_UNROLL // TOP_K) + j, k).start()
            return carry

        lax.fori_loop(0, tm // (GATHER_UNROLL // TOP_K), body, 0)

    wait_tile(slot)
    gates = rg_ref[...]
    moe = gates[:, 0:1] * gbuf_ref[slot, 0]
    for k in range(1, TOP_K):
        moe = moe + gates[:, k:k + 1] * gbuf_ref[slot, k]
    x2 = x1_ref[...] + moe
    is_s = m < CMB_NS
    p = jnp.where(is_s, ps_ref[...], pp_ref[...]).astype(BF16)
    hn = _rms(x2, gple_ref[...]).astype(BF16)
    nxt = jnp.minimum(m + 1, nm - 1)
    rows_per_chunk = tm // CMB_CHUNKS
    cw = D_MODEL // CMB_CHUNKS
    x3_parts = []
    for c in range(CMB_CHUNKS):
        cols = slice(c * cw, (c + 1) * cw)
        pe = jnp.dot(p, wple_ref[:, cols], preferred_element_type=F32)
        gate = jax.nn.sigmoid(jnp.dot(hn, wpg_ref[:, cols], preferred_element_type=F32))
        x3_parts.append(x2[:, cols] + pe * gate)
        for r in range(c * rows_per_chunk, (c + 1) * rows_per_chunk):
            for k in range(TOP_K):
                row_copy(nxt, 1 - slot, r, k).start(priority=k % 2)
    x3 = jnp.concatenate(x3_parts, axis=1)
    y = _rms(x3, gfin_ref[...])

    @pl.when(is_s)
    def _():
        ys_ref[...] = y

    @pl.when(jnp.logical_not(is_s))
    def _():
        yp_ref[...] = y

    @pl.when(m == nm - 1)
    def _():
        wait_tile(1 - slot)


def _combine_call(dest, y_sorted, x1, rg, ps, pp, wple, wpg, g_ple, g_final):
    tm = CMB_TM
    ns = CMB_NS
    s_idx = lambda m, d: (jnp.minimum(m, ns - 1), 0)
    p_idx = lambda m, d: (jnp.maximum(m - ns, 0), 0)
    row = lambda m, d: (m, 0)
    const2 = lambda m, d: (0, 0)
    grid_spec = pltpu.PrefetchScalarGridSpec(
        num_scalar_prefetch=1,
        grid=(T // tm,),
        in_specs=[
            pl.BlockSpec(memory_space=pl.ANY),
            pl.BlockSpec((tm, D_MODEL), row),
            pl.BlockSpec((tm, LANES), row),
            pl.BlockSpec((tm, PLE_DIM), s_idx),
            pl.BlockSpec((tm, PLE_DIM), p_idx),
            pl.BlockSpec((PLE_DIM, D_MODEL), const2),
            pl.BlockSpec((D_MODEL, D_MODEL), const2),
            pl.BlockSpec((1, D_MODEL), const2),
            pl.BlockSpec((1, D_MODEL), const2),
        ],
        out_specs=[
            pl.BlockSpec((tm, D_MODEL), s_idx),
            pl.BlockSpec((tm, D_MODEL), p_idx),
        ],
        scratch_shapes=[
            pltpu.VMEM((2, TOP_K, tm, D_MODEL), F32),
            pltpu.SemaphoreType.DMA((2,)),
        ],
    )
    return pl.pallas_call(
        _combine_kernel,
        grid_spec=grid_spec,
        out_shape=[
            jax.ShapeDtypeStruct((T_S, D_MODEL), F32),
            jax.ShapeDtypeStruct((T_P, D_MODEL), F32),
        ],
        compiler_params=_cparams(("arbitrary",), 56),
        name="combine",
    )(dest, y_sorted, x1, rg, ps, pp, wple, wpg, g_ple, g_final)


def _route_kernel(lgt_ref, dest_ref, tok_ref, tab_ref, rg_ref,
                  rit_ref, dvm_ref, zvm_ref, carry_ref, cnt_ref, gs_ref, sem):
    rb_shift = MOE_RB.bit_length() - 1
    tb = ROUTE_TB
    sub = tb // LANES

    a_i = lax.broadcasted_iota(I32, (LANES, LANES), 0)
    b_i = lax.broadcasted_iota(I32, (LANES, LANES), 1)
    earlier = jnp.where(a_i < b_i, 1.0, 0.0).astype(BF16)
    carry_ref[...] = jnp.zeros_like(carry_ref)

    def tile(c, loop_carry):
        c0 = pl.multiple_of(c * tb, tb)
        work = lgt_ref[:, pl.ds(c0, tb)]
        row = lax.broadcasted_iota(I32, (N_EXPERTS, tb), 0).astype(F32)
        vals, idxs, hots = [], [], []
        for _ in range(TOP_K):
            mx = jnp.max(work, axis=0, keepdims=True)
            idx = jnp.min(jnp.where(work == mx, row, float(N_EXPERTS)), axis=0, keepdims=True)
            hot = row == idx
            work = jnp.where(hot, -jnp.inf, work)
            vals.append(mx)
            idxs.append(idx)
            hots.append(hot)
        exps = [jnp.exp(vk - vals[0]) for vk in vals]
        den = exps[0] + exps[1] + exps[2] + exps[3]
        chosen = jnp.zeros((N_EXPERTS, tb), F32)
        for hot in hots:
            chosen = chosen + jnp.where(hot, 1.0, 0.0)
        counts = carry_ref[...]
        before = []
        for j in range(sub):
            cj = chosen[:, j * LANES:(j + 1) * LANES]
            before.append(jnp.dot(cj.astype(BF16), earlier, preferred_element_type=F32) + counts)
            counts = counts + jnp.sum(cj, axis=1, keepdims=True)
        carry_ref[...] = counts
        before = jnp.concatenate(before, axis=1)
        for k in range(TOP_K):
            rank = jnp.sum(jnp.where(hots[k], before, 0.0), axis=0, keepdims=True)
            rit_ref[k:k + 1, pl.ds(c0, tb)] = idxs[k].astype(I32)
            rit_ref[TOP_K + k:TOP_K + k + 1, pl.ds(c0, tb)] = rank.astype(I32)
        gates = jnp.concatenate([ek / den for ek in exps], axis=0)
        for j in range(sub):
            g_tile = jnp.concatenate([gates[:, j * LANES:(j + 1) * LANES],
                                      jnp.zeros((LANES - TOP_K, LANES), F32)], axis=0)
            rg_ref[pl.ds(pl.multiple_of(c0 + j * LANES, LANES), LANES), :] = jnp.transpose(g_tile)
        return loop_carry

    lax.fori_loop(0, T // tb, tile, 0)
    cnt_copy = pltpu.make_async_copy(carry_ref, cnt_ref, sem)
    cnt_copy.start()
    cnt_copy.wait()

    def clear(j, carry):
        tab_ref[j] = 0
        return carry

    lax.fori_loop(0, TAB_SIZE, clear, 0)

    def expert(e, carry):
        acc, item, used = carry
        n = lax.shift_right_logical(cnt_ref[e, 0].astype(I32) + (MOE_RB - 1), rb_shift)
        gs_ref[e] = acc

        def add_item(local, it):
            tab_ref[TAB_E + it] = e
            tab_ref[TAB_ROW0 + it] = acc + local * MOE_RMAX
            tab_ref[TAB_NBLK + it] = jnp.minimum(MOE_BMAX, n - local * MOE_BMAX)
            return it + 1

        item = lax.fori_loop(0, lax.div(n + (MOE_BMAX - 1), MOE_BMAX), add_item, item)
        return acc + n * MOE_RB, item, used + n

    _, n_items, used = lax.fori_loop(0, N_EXPERTS, expert, (jnp.int32(0), jnp.int32(0), jnp.int32(0)))
    tab_ref[TAB_USED] = used
    e_last = tab_ref[TAB_E + n_items - 1]

    def pad_item(it, carry):
        tab_ref[TAB_E + it] = e_last
        tab_ref[TAB_ROW0 + it] = 0
        tab_ref[TAB_NBLK + it] = 0
        return carry

    lax.fori_loop(n_items, MOE_NI + 1, pad_item, 0)

    def wait_units(it, prev):
        nb = tab_ref[TAB_NBLK + it]
        tab_ref[TAB_WAIT + it] = jnp.maximum(nb, prev)
        return nb

    lax.fori_loop(0, MOE_NI + 1, wait_units, jnp.int32(0))

    e_idx = rit_ref[0:TOP_K, :]
    d = rit_ref[TOP_K:2 * TOP_K, :]
    for e in range(N_EXPERTS):
        d = d + jnp.where(e_idx == e, gs_ref[e], 0)
    dvm_ref[0:TOP_K, :] = d
    zvm_ref[...] = jnp.zeros_like(zvm_ref)
    copies = [pltpu.make_async_copy(dvm_ref.at[k], dest_ref.at[pl.ds(k * T, T)], sem) for k in range(TOP_K)]
    copies.append(pltpu.make_async_copy(zvm_ref, tok_ref, sem))
    for c in copies:
        c.start()
    for c in copies:
        c.wait()

    for k in range(TOP_K):
        def scatter(c, carry):
            for j in range(GATHER_UNROLL):
                t = c * GATHER_UNROLL + j
                tok_ref[dest_ref[k * T + t]] = t
            return carry

        lax.fori_loop(0, T // GATHER_UNROLL, scatter, 0)


def _route_call(logits_t):
    smem = pl.BlockSpec(memory_space=pltpu.SMEM)
    vmem = pl.BlockSpec(memory_space=pltpu.VMEM)
    return pl.pallas_call(
        _route_kernel,
        in_specs=[vmem],
        out_specs=[smem, smem, smem, vmem],
        out_shape=[
            jax.ShapeDtypeStruct((TOP_K * T,), I32),
            jax.ShapeDtypeStruct((N_TOK_TAB,), I32),
            jax.ShapeDtypeStruct((TAB_SIZE,), I32),
            jax.ShapeDtypeStruct((T, LANES), F32),
        ],
        scratch_shapes=[
            pltpu.VMEM((2 * TOP_K, T), I32),
            pltpu.VMEM((2 * TOP_K, T), I32),
            pltpu.VMEM((N_TOK_TAB,), I32),
            pltpu.VMEM((N_EXPERTS, LANES), F32),
            pltpu.SMEM((N_EXPERTS, LANES), F32),
            pltpu.SMEM((N_EXPERTS,), I32),
            pltpu.SemaphoreType.DMA(()),
        ],
        compiler_params=pltpu.CompilerParams(vmem_limit_bytes=32 * MIB),
        name="route",
    )(logits_t)


def kernel(x_prompt, x_sample, state_conv, p_prompt, p_sample, g_mix, w_in, ln_v_g, ln_v_b, w_s, b_s,
           conv_w, conv_b, w_proj_a, w_proj_b, w_o, g_moe, w_router, b_router, w_gate_up, b_gate_up,
           w_down, b_down, g_ple, w_ple, w_ple_gate, g_final):
    assert g_mix.shape[0] == 1, "one layer"
    xs = x_sample.reshape(T_S, D_MODEL)
    xp = x_prompt.reshape(T_P, D_MODEL)

    tril = jnp.tril(jnp.ones((CHUNK, CHUNK), bool))
    w_prompt = jnp.where(tril[None], w_s[0], 0.0)
    small = jnp.where(tril[None, :DEC_SEQ, :DEC_SEQ], w_s[0, :, :DEC_SEQ, :DEC_SEQ], 0.0)
    reps = CHUNK // DEC_SEQ
    blockdiag = jnp.kron(jnp.eye(reps, dtype=F32), jnp.ones((DEC_SEQ, DEC_SEQ), F32))
    w_sample = jnp.tile(small, (1, reps, reps)) * blockdiag[None]
    wsg = jnp.stack([w_sample, w_prompt]).astype(BF16)
    bias_p = jnp.repeat(b_s[0].T, GW_A, axis=1)
    bias_s = jnp.tile(jnp.repeat(b_s[0, :, :DEC_SEQ].T, GW_A, axis=1), (reps, 1))
    bsg = jnp.stack([bias_s, bias_p])
    ext = jnp.pad(state_conv[0], ((0, 0), (0, DEC_SEQ - (CONV_W - 1)), (0, 0))).reshape(T_S, D_B)

    wr = jnp.pad(w_router[0], ((0, 0), (0, LANES - N_EXPERTS)))
    wr_hi = wr.astype(BF16)
    wr_lo = (wr - wr_hi.astype(F32)).astype(BF16)
    b_r = jnp.pad(b_router[0], (0, LANES - N_EXPERTS), constant_values=NEG_BIG).reshape(1, LANES)

    h = _norm_call(xs, xp, g_mix)
    z, vpre = _in_proj_call(h, w_in[0])
    x1, xn, logits_t, vln, cxs, tail = _mixer_call(
        xs, xp, z, vpre, ext, wsg, bsg, ln_v_g, ln_v_b, conv_w[0], conv_b,
        w_proj_a[0].astype(BF16), w_proj_b[0].astype(BF16), w_o[0].astype(BF16), g_moe,
        wr_hi, wr_lo, b_r)

    dest, slot_tok, tables, route_g = _route_call(logits_t)
    y_sorted = _moe_call(slot_tok, tables, xn, w_gate_up[0], w_down[0],
                         b_gate_up[0].reshape(N_EXPERTS, 2 * MOE_P1, MOE_TF),
                         b_down[0].reshape(N_EXPERTS, MOE_P2, MOE_TN))
    ys, yp = _combine_call(dest, y_sorted, x1, route_g,
                           p_sample[0].reshape(T_S, PLE_DIM), p_prompt[0].reshape(T_P, PLE_DIM),
                           w_ple[0].astype(BF16), w_ple_gate[0].astype(BF16), g_ple, g_final.reshape(1, D_MODEL))

    y_prompt = yp.reshape(BATCH, SEQ, D_MODEL)
    y_sample = ys.reshape(DEC_BATCH, DEC_SEQ, D_MODEL)
    last = tail[:MIX_NT - MIX_NS].reshape(BATCH, MIX_SEQ_TILES, SUBLANES, D_B)[:, -1, SUBLANES - (CONV_W - 1):, :]
    state_conv_prompt = last[None]
    state_conv_sample = cxs[:T_S].reshape(DEC_BATCH, DEC_SEQ, D_B)[:, DEC_SEQ - (CONV_W - 1):, :][None]
    state_chunk_v_sample = vln[:T_S].reshape(DEC_BATCH, DEC_SEQ, D_A)[None]
    return (y_prompt, y_sample, state_conv_prompt, state_conv_sample, state_chunk_v_sample)
```

```python
import functools

import jax
import jax.numpy as jnp
from jax import lax
from jax.experimental import pallas as pl
from jax.experimental.pallas import tpu as pltpu

F32 = jnp.float32
BF16 = jnp.bfloat16
I32 = jnp.int32

D_MODEL = 2048
BATCH = 4
SEQ = 2048
DEC_BATCH = 128
DEC_SEQ = 8
CHUNK = 128
D_A = D_MODEL // 2
N_GROUPS_A = 8
GW_A = D_A // N_GROUPS_A
D_B = D_MODEL // 2
CONV_W = 3
N_EXPERTS = 32
TOP_K = 4
D_FF = D_MODEL
SWIGLU_LIMIT = 7.0
SWIGLU_ALPHA = 1.702
PLE_DIM = 256
EPS = 1e-6

T_S = DEC_BATCH * DEC_SEQ
T_P = BATCH * SEQ
T = T_S + T_P

LANES = 128
SUBLANES = 8
MIB = 1024 * 1024

NORM_TM = 512
IN_TM = 1024
IN_TN = 1024
IN_SUB = 256
MIX_TM = CHUNK
MIX_NS = T_S // MIX_TM
MIX_SEQ_TILES = SEQ // MIX_TM
MIX_NT = T // MIX_TM
MOE_RB = 128
MOE_BMAX = 14
MOE_RMAX = MOE_RB * MOE_BMAX
MOE_CHUNK = 8
MOE_TF = 256
MOE_TN = 512
MOE_P1 = D_FF // MOE_TF
MOE_P2 = D_MODEL // MOE_TN
MOE_TILES = MOE_P1 + MOE_P2
MOE_WSLOTS = 3
MOE_WAHEAD = MOE_WSLOTS - 1
assert D_MODEL == D_FF and 2 * MOE_TF == MOE_TN and MOE_TILES % MOE_WSLOTS == 0
MOE_G = MOE_RB // MOE_P1
MOE_YSLOTS = 8
N_SLOTS = T * TOP_K + N_EXPERTS * MOE_RB
N_BLOCKS = N_SLOTS // MOE_RB
MOE_NI = (N_BLOCKS + N_EXPERTS * (MOE_BMAX - 1)) // MOE_BMAX
GATHER_UNROLL = 8
TAB_STRIDE = 64
TAB_E, TAB_ROW0, TAB_NBLK, TAB_WAIT, TAB_USED = 0, TAB_STRIDE, 2 * TAB_STRIDE, 3 * TAB_STRIDE, 4 * TAB_STRIDE
TAB_SIZE = 5 * TAB_STRIDE
assert MOE_NI + 1 <= TAB_STRIDE
N_TOK_TAB = -(-(N_SLOTS + MOE_RMAX) // 1024) * 1024
ROUTE_TB = 512
CMB_TM = 256
CMB_NS = T_S // CMB_TM
CMB_CHUNKS = 8
NEG_BIG = -1e30


def _rms(x, g):
    return x * lax.rsqrt(jnp.mean(x * x, axis=-1, keepdims=True) + EPS) * g


def _cparams(sem, vmem_mib):
    return pltpu.CompilerParams(dimension_semantics=sem, vmem_limit_bytes=vmem_mib * MIB)


def _resident(shape):
    zeros = (0,) * len(shape)
    return pl.BlockSpec(shape, lambda *_: zeros, pipeline_mode=pl.Buffered(1))


def _norm_kernel(xs_ref, xp_ref, g_ref, h_ref, *, ns):
    m = pl.program_id(0)
    x = jnp.where(m < ns, xs_ref[...], xp_ref[...])
    h_ref[...] = _rms(x, g_ref[...]).astype(BF16)


def _norm_call(xs, xp, g):
    ns = T_S // NORM_TM
    return pl.pallas_call(
        functools.partial(_norm_kernel, ns=ns),
        grid=(T // NORM_TM,),
        in_specs=[
            pl.BlockSpec((NORM_TM, D_MODEL), lambda m: (jnp.minimum(m, ns - 1), 0)),
            pl.BlockSpec((NORM_TM, D_MODEL), lambda m: (jnp.maximum(m - ns, 0), 0)),
            pl.BlockSpec((1, D_MODEL), lambda m: (0, 0)),
        ],
        out_specs=pl.BlockSpec((NORM_TM, D_MODEL), lambda m: (m, 0)),
        out_shape=jax.ShapeDtypeStruct((T, D_MODEL), BF16),
        compiler_params=_cparams(("arbitrary",), 32),
        name="norm",
    )(xs, xp, g)


IN_N_GELU = 2 * D_A // IN_TN
IN_N_V0 = D_A // IN_TN
IN_N_LIN = (2 * D_A + 3 * D_B) // IN_TN


def _in_proj_kernel(h_ref, w_ref, z_ref, vpre_ref, wb_ref):
    n = pl.program_id(0)

    @pl.when(pl.program_id(1) == 0)
    def _():
        wb_ref[...] = w_ref[...].astype(BF16)

    def blocks(epilogue):
        for b in range(IN_TM // IN_SUB):
            rows = pl.ds(b * IN_SUB, IN_SUB)
            epilogue(rows, jnp.dot(h_ref[rows, :], wb_ref[...], preferred_element_type=F32))

    @pl.when(n < IN_N_V0)
    def _():
        def ep(rows, acc):
            z_ref[rows, :] = jax.nn.gelu(acc, approximate=True).astype(BF16)

        blocks(ep)

    @pl.when((n >= IN_N_V0) & (n < IN_N_GELU))
    def _():
        def ep(rows, acc):
            g = jax.nn.gelu(acc, approximate=True)
            z_ref[rows, :] = g.astype(BF16)
            vpre_ref[rows, :] = g

        blocks(ep)

    @pl.when((n >= IN_N_GELU) & (n < IN_N_LIN))
    def _():
        def ep(rows, acc):
            z_ref[rows, :] = acc.astype(BF16)

        blocks(ep)

    @pl.when(n >= IN_N_LIN)
    def _():
        def ep(rows, acc):
            z_ref[rows, :] = jax.nn.sigmoid(acc).astype(BF16)

        blocks(ep)


def _in_proj_call(h, w_in):
    d_in = w_in.shape[1]
    n_m = T // IN_TM

    def vpre_map(n, m):
        row = jnp.where(n < IN_N_V0, 0, jnp.where(n < IN_N_GELU, m, n_m - 1))
        return (row, jnp.clip(n - IN_N_V0, 0, IN_N_GELU - IN_N_V0 - 1))

    return pl.pallas_call(
        _in_proj_kernel,
        grid=(d_in // IN_TN, n_m),
        in_specs=[
            pl.BlockSpec((IN_TM, D_MODEL), lambda n, m: (m, 0)),
            pl.BlockSpec((D_MODEL, IN_TN), lambda n, m: (0, n)),
        ],
        out_specs=[
            pl.BlockSpec((IN_TM, IN_TN), lambda n, m: (m, n)),
            pl.BlockSpec((IN_TM, IN_TN), vpre_map),
        ],
        out_shape=[
            jax.ShapeDtypeStruct((T, d_in), BF16),
            jax.ShapeDtypeStruct((T, D_A), F32),
        ],
        scratch_shapes=[pltpu.VMEM((D_MODEL, IN_TN), BF16)],
        compiler_params=_cparams(("arbitrary", "arbitrary"), 48),
        name="in_proj",
    )(h, w_in)


def _mixer_kernel(xs_ref, xp_ref, zu_ref, zb_ref, zc_ref, zx_ref, ga0_ref, ga1_ref, gb0_ref, gb1_ref,
                  vpre_ref, ext_ref, wsg_ref, bsg_ref, lng_ref, lnb_ref,
                  cw_ref, cb_ref, wa_ref, wb_ref, wo_ref, gmoe_ref, wrh_ref, wrl_ref, br_ref,
                  x1_ref, xn_ref, lgt_ref, vln_ref, cxs_ref, tail_ref,
                  prev_ref, ab_ref):
    m = pl.program_id(0)
    tm = MIX_TM
    a_tile = jnp.minimum(m, MIX_NT - 1)
    is_s = a_tile < MIX_NS
    b_is_s = (m - 1) < MIX_NS
    slot = m & 1

    @pl.when(m == 0)
    def _():
        prev_ref[...] = jnp.zeros_like(prev_ref)
        ab_ref[...] = jnp.zeros_like(ab_ref)

    y_a = jnp.dot(ab_ref[1 - slot, :, 0:D_A], wa_ref[...], preferred_element_type=F32)
    y_b = jnp.dot(ab_ref[1 - slot, :, D_A:D_A + D_B], wb_ref[...], preferred_element_type=F32)

    vg = vpre_ref[...]
    mu = jnp.mean(vg, axis=-1, keepdims=True)
    vc = vg - mu
    v = vc * lax.rsqrt(jnp.mean(vc * vc, axis=-1, keepdims=True) + EPS) * lng_ref[...] + lnb_ref[...]
    vln_ref[...] = v

    vb = v.astype(BF16)
    s_parts = []
    for g in range(N_GROUPS_A):
        s_parts.append(jnp.dot(wsg_ref[0, g], vb[:, g * GW_A:(g + 1) * GW_A], preferred_element_type=F32))
    s = jnp.concatenate(s_parts, axis=1) + bsg_ref[0]
    u = zu_ref[...].astype(F32)
    a_in = (u * s).astype(BF16)

    ga = jnp.concatenate([ga0_ref[...], ga1_ref[...]], axis=1).astype(F32)
    gb = jnp.concatenate([gb0_ref[...], gb1_ref[...]], axis=1).astype(F32)
    mix = (ga * y_a + gb * y_b).astype(BF16)
    x = jnp.where(b_is_s, xs_ref[...], xp_ref[...])
    x1 = x + jnp.dot(mix, wo_ref[...], preferred_element_type=F32)
    x1_ref[...] = x1

    bg = zb_ref[...].astype(F32)
    cg = zc_ref[...].astype(F32)
    xin = zx_ref[...].astype(F32)
    cx = cg * xin
    cxs_ref[...] = cx
    tail_ref[0] = cx[tm - SUBLANES:tm]

    row = lax.broadcasted_iota(I32, (tm, D_B), 0)
    seq_start = ((a_tile - MIX_NS) % MIX_SEQ_TILES) == 0
    prev = jnp.where(seq_start, 0.0, prev_ref[...])
    row8 = lax.broadcasted_iota(I32, (SUBLANES, D_B), 0)
    top = jnp.where(row8 < CONV_W - 1, pltpu.roll(prev, CONV_W - 1, 0), 0.0)
    ext_p = jnp.concatenate([top, jnp.zeros((tm - SUBLANES, D_B), F32)], axis=0)
    ext = jnp.where(is_s, ext_ref[...], ext_p)
    t_in = jnp.where(is_s, row & (DEC_SEQ - 1), row)
    s1 = jnp.where(t_in < 1, pltpu.roll(ext, tm - 1, 0), pltpu.roll(cx, 1, 0))
    s2 = jnp.where(t_in < 2, ext, pltpu.roll(cx, 2, 0))
    prev_ref[...] = cx[tm - SUBLANES:tm]
    conv = cb_ref[...] + s2 * cw_ref[0:1, :] + s1 * cw_ref[1:2, :] + cx * cw_ref[2:3, :]
    b_in = (bg * conv).astype(BF16)

    xn = _rms(x1, gmoe_ref[...])
    xn_ref[...] = xn
    hi = xn.astype(BF16)
    lo = (xn - hi.astype(F32)).astype(BF16)
    logits = (jnp.dot(hi, wrh_ref[...], preferred_element_type=F32)
              + jnp.dot(lo, wrh_ref[...], preferred_element_type=F32)
              + jnp.dot(hi, wrl_ref[...], preferred_element_type=F32)) + br_ref[...]

    lgt_ref[...] = jnp.transpose(logits)[0:N_EXPERTS, :]

    ab_ref[slot, :, 0:D_A] = a_in
    ab_ref[slot, :, D_A:D_A + D_B] = b_in


def _mixer_call(xs, xp, z, vpre, ext, wsg, bsg, ln_g, ln_b, conv_w, conv_b, wa, wb, wo, g_moe,
                wr_hi, wr_lo, b_r):
    tm = MIX_TM
    ns = MIX_NS
    nt = MIX_NT
    assert D_A == D_B and D_MODEL == 2 * D_A and z.shape[1] == 9 * D_A
    front = lambda m: jnp.minimum(m, nt - 1)
    back = lambda m: jnp.maximum(m - 1, 0)
    zcol = lambda tile_of, c: (lambda m: (tile_of(m), c))
    row = lambda m: (back(m), 0)
    s_out = lambda m: (jnp.minimum(front(m), ns), 0)
    p_out = lambda m: (jnp.where(front(m) < ns, nt - ns, front(m) - ns), 0, 0)
    sel = lambda m: jnp.minimum(front(m) // ns, 1)
    return pl.pallas_call(
        _mixer_kernel,
        grid=(nt + 1,),
        in_specs=[
            pl.BlockSpec((tm, D_MODEL), lambda m: (jnp.minimum(back(m), ns - 1), 0)),
            pl.BlockSpec((tm, D_MODEL), lambda m: (jnp.maximum(back(m) - ns, 0), 0)),
            pl.BlockSpec((tm, D_A), zcol(front, 0)),
            pl.BlockSpec((tm, D_A), zcol(front, 2)),
            pl.BlockSpec((tm, D_A), zcol(front, 3)),
            pl.BlockSpec((tm, D_A), zcol(front, 4)),
            pl.BlockSpec((tm, D_A), zcol(back, 5)),
            pl.BlockSpec((tm, D_A), zcol(back, 6)),
            pl.BlockSpec((tm, D_A), zcol(back, 7)),
            pl.BlockSpec((tm, D_A), zcol(back, 8)),
            pl.BlockSpec((tm, D_A), lambda m: (front(m), 0)),
            pl.BlockSpec((tm, D_B), lambda m: (jnp.minimum(front(m), ns - 1), 0)),
            pl.BlockSpec((1, N_GROUPS_A, CHUNK, CHUNK), lambda m: (sel(m), 0, 0, 0)),
            pl.BlockSpec((1, CHUNK, D_A), lambda m: (sel(m), 0, 0)),
            _resident((1, D_A)),
            _resident((1, D_A)),
            _resident((CONV_W, D_B)),
            _resident((1, D_B)),
            _resident((D_A, D_MODEL)),
            _resident((D_B, D_MODEL)),
            _resident((D_MODEL, D_MODEL)),
            _resident((1, D_MODEL)),
            _resident((D_MODEL, LANES)),
            _resident((D_MODEL, LANES)),
            _resident((1, LANES)),
        ],
        out_specs=[
            pl.BlockSpec((tm, D_MODEL), row),
            pl.BlockSpec((tm, D_MODEL), row),
            pl.BlockSpec((N_EXPERTS, tm), lambda m: (0, back(m))),
            pl.BlockSpec((tm, D_A), s_out),
            pl.BlockSpec((tm, D_B), s_out),
            pl.BlockSpec((1, SUBLANES, D_B), p_out),
        ],
        out_shape=[
            jax.ShapeDtypeStruct((T, D_MODEL), F32),
            jax.ShapeDtypeStruct((T, D_MODEL), F32),
            jax.ShapeDtypeStruct((N_EXPERTS, T), F32),
            jax.ShapeDtypeStruct((T_S + tm, D_A), F32),
            jax.ShapeDtypeStruct((T_S + tm, D_B), F32),
            jax.ShapeDtypeStruct((nt - ns + 1, SUBLANES, D_B), F32),
        ],
        scratch_shapes=[pltpu.VMEM((SUBLANES, D_B), F32), pltpu.VMEM((2, tm, D_A + D_B), BF16)],
        compiler_params=_cparams(("arbitrary",), 52),
        name="mixer",
    )(xs, xp, z, z, z, z, z, z, z, z, vpre, ext, wsg, bsg, ln_g, ln_b, conv_w, conv_b, wa, wb, wo, g_moe,
      wr_hi, wr_lo, b_r)


def _for_units(n, unit_fn, chunk_begin=None):
    def chunk(u0, count):
        if chunk_begin is not None:
            chunk_begin(range(count))
        for j in range(count):
            unit_fn(u0 + j, j)

    def body(c, carry):
        chunk(c * MOE_CHUNK, MOE_CHUNK)
        return carry

    n_big = lax.shift_right_logical(n, MOE_CHUNK.bit_length() - 1)
    lax.fori_loop(0, n_big, body, 0)
    base = n_big * MOE_CHUNK
    rest = n & (MOE_CHUNK - 1)
    for count in range(1, MOE_CHUNK):
        @pl.when(rest == count)
        def _():
            chunk(base, count)


def _moe_kernel(tok_ref, tab_ref,
                xn_hbm, wgu_hbm, wd_hbm, bgu_hbm, bd_hbm,
                y_hbm,
                xraw_ref, xb_ref, act_ref, wst_ref, wbf_ref, bgu_ref, bd_ref,
                ystage_ref, zbuf_ref, gsem, ysem, zsem, wsem, bsem):
    i = pl.program_id(0)
    nblk = tab_ref[TAB_NBLK +i]
    row0 = tab_ref[TAB_ROW0 +i]
    valid = nblk > 0
    par = i & 1

    def gu_copies(e, tile, slot):
        col = pl.multiple_of(tile * MOE_TF, MOE_TF)
        return (pltpu.make_async_copy(wgu_hbm.at[e, :, pl.ds(col, MOE_TF)],
                                      wst_ref.at[slot, :, pl.ds(0, MOE_TF)], wsem.at[slot]),
                pltpu.make_async_copy(wgu_hbm.at[e, :, pl.ds(D_FF + col, MOE_TF)],
                                      wst_ref.at[slot, :, pl.ds(MOE_TF, MOE_TF)], wsem.at[slot]))

    def d_copy(e, tile, slot):
        col = pl.multiple_of(tile * MOE_TN, MOE_TN)
        return pltpu.make_async_copy(wd_hbm.at[e, :, pl.ds(col, MOE_TN)], wst_ref.at[slot], wsem.at[slot])

    def start_tile(e, t):
        slot = lax.rem(t, MOE_WSLOTS)

        @pl.when(t < MOE_P1)
        def _():
            for c in gu_copies(e, t, slot):
                c.start()

        @pl.when(t >= MOE_P1)
        def _():
            d_copy(e, t - MOE_P1, slot).start()

    def bias_copies(e, slot):
        return (pltpu.make_async_copy(bgu_hbm.at[e], bgu_ref.at[slot], bsem.at[slot]),
                pltpu.make_async_copy(bd_hbm.at[e], bd_ref.at[slot], bsem.at[slot]))

    def tail_copy(b):
        r = pl.multiple_of(b * MOE_RB, MOE_RB)
        return pltpu.make_async_copy(zbuf_ref, y_hbm.at[pl.ds(r, MOE_RB), :], zsem)

    def row_copy(tok, group, sub):
        return pltpu.make_async_copy(xn_hbm.at[pl.ds(tok, 1), :], xraw_ref.at[group, pl.ds(sub, 1), :], gsem)

    def unit_wait():
        blk = xraw_ref.at[pl.ds(0, MOE_RB // SUBLANES)]
        return pltpu.make_async_copy(blk, blk, gsem)

    def gather_rows(item, lo_unit, hi_unit):
        base = tab_ref[TAB_ROW0 +item]

        def body(c, carry):
            for j in range(SUBLANES):
                row_copy(tok_ref[base + c * SUBLANES + j], c, j).start()
            return carry

        per_unit = MOE_RB // SUBLANES
        lax.fori_loop(lo_unit * per_unit, hi_unit * per_unit, body, 0)

    def y_copy(slot, u, col):
        r = pl.multiple_of(row0 + u * MOE_RB, MOE_RB)
        return pltpu.make_async_copy(ystage_ref.at[slot], y_hbm.at[pl.ds(r, MOE_RB), pl.ds(col, MOE_TN)],
                                     ysem.at[slot])

    def dump_copy(slot):
        r = N_SLOTS + (slot // MOE_P2) * MOE_RB
        c = (slot % MOE_P2) * MOE_TN
        return pltpu.make_async_copy(ystage_ref.at[slot], y_hbm.at[pl.ds(r, MOE_RB), pl.ds(c, MOE_TN)],
                                     ysem.at[slot])

    @pl.when(i == 0)
    def _():
        for t in range(MOE_WAHEAD):
            for c in gu_copies(tab_ref[TAB_E +0], t, t):
                c.start()
        for c in bias_copies(tab_ref[TAB_E +0], 0):
            c.start()
        gather_rows(0, 0, tab_ref[TAB_NBLK +0])
        ystage_ref[...] = jnp.zeros_like(ystage_ref)
        for slot in range(MOE_YSLOTS):
            dump_copy(slot).start()
        zbuf_ref[...] = jnp.zeros_like(zbuf_ref)

        def fill(b, carry):
            tail_copy(b).start()
            return carry

        lax.fori_loop(tab_ref[TAB_USED], N_BLOCKS, fill, 0)

    def wait_rows(b, carry):
        unit_wait().wait()
        return carry

    lax.fori_loop(0, tab_ref[TAB_WAIT +i], wait_rows, 0)

    def conv(u, carry):
        rows = pl.ds(pl.multiple_of(u * MOE_RB, MOE_RB), MOE_RB)
        groups = pl.ds(u * (MOE_RB // SUBLANES), MOE_RB // SUBLANES)
        xb_ref[rows, :] = xraw_ref[groups].reshape(MOE_RB, D_MODEL).astype(BF16)
        return carry

    lax.fori_loop(0, nblk, conv, 0)

    e = tab_ref[TAB_E +i]
    e_next = tab_ref[TAB_E +i + 1]
    next_valid = tab_ref[TAB_NBLK +i + 1] > 0
    next_base = tab_ref[TAB_ROW0 +i + 1]

    @pl.when(valid)
    def _():
        for c in bias_copies(e, par):
            c.wait()

    def gate_up_step(s, carry):
        wslot = lax.rem(s, MOE_WSLOTS)
        for c in gu_copies(e, s, wslot):
            c.wait()
        start_tile(e, s + MOE_WAHEAD)
        wbf_ref[...] = wst_ref[wslot].astype(BF16)
        b_g = bgu_ref[par, pl.ds(s, 1), :]
        b_u = bgu_ref[par, pl.ds(MOE_P1 + s, 1), :]

        def unit(u, slot):
            r = pl.multiple_of(u * MOE_RB, MOE_RB)
            first = pl.multiple_of(r + s * MOE_G, MOE_G)
            group = lax.shift_right_logical(first, SUBLANES.bit_length() - 1)
            for j in range(MOE_G):
                row_copy(tok_ref[next_base + first + j], group + j // SUBLANES, j % SUBLANES).start(priority=j % 2)
            gu = jnp.dot(xb_ref[pl.ds(r, MOE_RB), :], wbf_ref[...], preferred_element_type=F32)
            gate = jnp.minimum(gu[:, 0:MOE_TF] + b_g, SWIGLU_LIMIT)
            up = jnp.clip(gu[:, MOE_TF:2 * MOE_TF] + b_u, -SWIGLU_LIMIT, SWIGLU_LIMIT)
            act = (up + 1) * (gate * jax.nn.sigmoid(gate * SWIGLU_ALPHA))
            act_ref[s, pl.ds(r, MOE_RB), :] = act.astype(BF16)

        _for_units(nblk, unit)
        return carry

    def down_step(s, carry):
        t = MOE_P1 + s
        wslot = lax.rem(t, MOE_WSLOTS)
        d_copy(e, s, wslot).wait()

        @pl.when(t + MOE_WAHEAD < MOE_TILES)
        def _():
            start_tile(e, t + MOE_WAHEAD)

        @pl.when((t + MOE_WAHEAD >= MOE_TILES) & next_valid)
        def _():
            start_tile(e_next, t + MOE_WAHEAD - MOE_TILES)

        @pl.when((t + MOE_WAHEAD == MOE_TILES) & next_valid)
        def _():
            for c in bias_copies(e_next, 1 - par):
                c.start()

        wbf_ref[...] = wst_ref[wslot].astype(BF16)
        b_d = bd_ref[par, pl.ds(s, 1), :]
        col = pl.multiple_of(s * MOE_TN, MOE_TN)

        def free_slots(slots):
            for slot in slots:
                y_copy(slot, 0, col).wait()

        def unit(u, slot):
            r = pl.multiple_of(u * MOE_RB, MOE_RB)
            a = jnp.concatenate([act_ref[j, pl.ds(r, MOE_RB), :] for j in range(MOE_P1)], axis=1)
            ystage_ref[slot] = jnp.dot(a, wbf_ref[...], preferred_element_type=F32) + b_d
            y_copy(slot, u, col).start()

        _for_units(nblk, unit, free_slots)
        return carry

    @pl.when(valid)
    def _():
        lax.fori_loop(0, MOE_P1, gate_up_step, 0)
        gather_rows(i + 1, nblk, tab_ref[TAB_NBLK +i + 1])
        lax.fori_loop(0, MOE_P2, down_step, 0)

    @pl.when(i == MOE_NI - 1)
    def _():
        for slot in range(MOE_YSLOTS):
            dump_copy(slot).wait()

        lax.fori_loop(0, tab_ref[TAB_WAIT +MOE_NI], wait_rows, 0)

        def drain(b, carry):
            tail_copy(b).wait()
            return carry

        lax.fori_loop(tab_ref[TAB_USED], N_BLOCKS, drain, 0)


def _moe_call(slot_tok, tables, xn, w_gate_up, w_down, b_gate_up, b_down):
    any_spec = pl.BlockSpec(memory_space=pl.ANY)
    grid_spec = pltpu.PrefetchScalarGridSpec(
        num_scalar_prefetch=2,
        grid=(MOE_NI,),
        in_specs=[any_spec] * 5,
        out_specs=any_spec,
        scratch_shapes=[
            pltpu.VMEM((MOE_RMAX // SUBLANES, SUBLANES, D_MODEL), F32),
            pltpu.VMEM((MOE_RMAX, D_MODEL), BF16),
            pltpu.VMEM((MOE_P1, MOE_RMAX, MOE_TF), BF16),
            pltpu.VMEM((MOE_WSLOTS, D_MODEL, MOE_TN), F32),
            pltpu.VMEM((D_MODEL, MOE_TN), BF16),
            pltpu.VMEM((2, 2 * MOE_P1, MOE_TF), F32),
            pltpu.VMEM((2, MOE_P2, MOE_TN), F32),
            pltpu.VMEM((MOE_YSLOTS, MOE_RB, MOE_TN), F32),
            pltpu.VMEM((MOE_RB, D_MODEL), F32),
            pltpu.SemaphoreType.DMA(()),
            pltpu.SemaphoreType.DMA((MOE_YSLOTS,)),
            pltpu.SemaphoreType.DMA(()),
            pltpu.SemaphoreType.DMA((MOE_WSLOTS,)),
            pltpu.SemaphoreType.DMA((2,)),
        ],
    )
    spare_blocks = MOE_YSLOTS // MOE_P2
    return pl.pallas_call(
        _moe_kernel,
        grid_spec=grid_spec,
        out_shape=jax.ShapeDtypeStruct((N_SLOTS + spare_blocks * MOE_RB, D_MODEL), F32),
        compiler_params=_cparams(("arbitrary",), 56),
        name="moe",
    )(slot_tok, tables, xn, w_gate_up, w_down, b_gate_up, b_down)


def _combine_kernel(dest_ref, y_hbm, x1_ref, rg_ref, ps_ref, pp_ref, wple_ref, wpg_ref, gple_ref,
                    gfin_ref, ys_ref, yp_ref, gbuf_ref, gsem):
    m = pl.program_id(0)
    nm = pl.num_programs(0)
    tm = CMB_TM
    slot = m % 2

    def row_copy(tile, slot_, r, k):
        d = dest_ref[k * T + tile * tm + r]
        return pltpu.make_async_copy(y_hbm.at[pl.ds(d, 1), :], gbuf_ref.at[slot_, k, pl.ds(r, 1), :],
                                     gsem.at[slot_])

    def wait_tile(slot_):
        for k in range(TOP_K):
            pltpu.make_async_copy(y_hbm.at[pl.ds(0, tm), :], gbuf_ref.at[slot_, k], gsem.at[slot_]).wait()

    @pl.when(m == 0)
    def _():
        def body(c, carry):
            for j in range(GATHER_UNROLL // TOP_K):
                for k in range(TOP_K):
                    row_copy(0, 0, c * (GATHER_UNROLL // TOP_K) + j, k).start()
            return carry

        lax.fori_loop(0, tm // (GATHER_UNROLL // TOP_K), body, 0)

    wait_tile(slot)
    gates = rg_ref[...]
    moe = gates[:, 0:1] * gbuf_ref[slot, 0]
    for k in range(1, TOP_K):
        moe = moe + gates[:, k:k + 1] * gbuf_ref[slot, k]
    x2 = x1_ref[...] + moe
    is_s = m < CMB_NS
    p = jnp.where(is_s, ps_ref[...], pp_ref[...]).astype(BF16)
    hn = _rms(x2, gple_ref[...]).astype(BF16)
    nxt = jnp.minimum(m + 1, nm - 1)
    rows_per_chunk = tm // CMB_CHUNKS
    cw = D_MODEL // CMB_CHUNKS
    x3_parts = []
    for c in range(CMB_CHUNKS):
        cols = slice(c * cw, (c + 1) * cw)
        pe = jnp.dot(p, wple_ref[:, cols], preferred_element_type=F32)
        gate = jax.nn.sigmoid(jnp.dot(hn, wpg_ref[:, cols], preferred_element_type=F32))
        x3_parts.append(x2[:, cols] + pe * gate)
        for r in range(c * rows_per_chunk, (c + 1) * rows_per_chunk):
            for k in range(TOP_K):
                row_copy(nxt, 1 - slot, r, k).start()
    x3 = jnp.concatenate(x3_parts, axis=1)
    y = _rms(x3, gfin_ref[...])

    @pl.when(is_s)
    def _():
        ys_ref[...] = y

    @pl.when(jnp.logical_not(is_s))
    def _():
        yp_ref[...] = y

    @pl.when(m == nm - 1)
    def _():
        wait_tile(1 - slot)


def _combine_call(dest, y_sorted, x1, rg, ps, pp, wple, wpg, g_ple, g_final):
    tm = CMB_TM
    ns = CMB_NS
    s_idx = lambda m, d: (jnp.minimum(m, ns - 1), 0)
    p_idx = lambda m, d: (jnp.maximum(m - ns, 0), 0)
    row = lambda m, d: (m, 0)
    const2 = lambda m, d: (0, 0)
    grid_spec = pltpu.PrefetchScalarGridSpec(
        num_scalar_prefetch=1,
        grid=(T // tm,),
        in_specs=[
            pl.BlockSpec(memory_space=pl.ANY),
            pl.BlockSpec((tm, D_MODEL), row),
            pl.BlockSpec((tm, LANES), row),
            pl.BlockSpec((tm, PLE_DIM), s_idx),
            pl.BlockSpec((tm, PLE_DIM), p_idx),
            pl.BlockSpec((PLE_DIM, D_MODEL), const2),
            pl.BlockSpec((D_MODEL, D_MODEL), const2),
            pl.BlockSpec((1, D_MODEL), const2),
            pl.BlockSpec((1, D_MODEL), const2),
        ],
        out_specs=[
            pl.BlockSpec((tm, D_MODEL), s_idx),
            pl.BlockSpec((tm, D_MODEL), p_idx),
        ],
        scratch_shapes=[
            pltpu.VMEM((2, TOP_K, tm, D_MODEL), F32),
            pltpu.SemaphoreType.DMA((2,)),
        ],
    )
    return pl.pallas_call(
        _combine_kernel,
        grid_spec=grid_spec,
        out_shape=[
            jax.ShapeDtypeStruct((T_S, D_MODEL), F32),
            jax.ShapeDtypeStruct((T_P, D_MODEL), F32),
        ],
        compiler_params=_cparams(("arbitrary",), 56),
        name="combine",
    )(dest, y_sorted, x1, rg, ps, pp, wple, wpg, g_ple, g_final)


def _route_kernel(lgt_ref, dest_ref, tok_ref, tab_ref, rg_ref,
                  rit_ref, dvm_ref, zvm_ref, carry_ref, cnt_ref, gs_ref, sem):
    rb_shift = MOE_RB.bit_length() - 1
    tb = ROUTE_TB
    sub = tb // LANES

    a_i = lax.broadcasted_iota(I32, (LANES, LANES), 0)
    b_i = lax.broadcasted_iota(I32, (LANES, LANES), 1)
    earlier = jnp.where(a_i < b_i, 1.0, 0.0).astype(BF16)
    carry_ref[...] = jnp.zeros_like(carry_ref)

    def tile(c, loop_carry):
        c0 = pl.multiple_of(c * tb, tb)
        work = lgt_ref[:, pl.ds(c0, tb)]
        row = lax.broadcasted_iota(I32, (N_EXPERTS, tb), 0).astype(F32)
        vals, idxs, hots = [], [], []
        for _ in range(TOP_K):
            mx = jnp.max(work, axis=0, keepdims=True)
            idx = jnp.min(jnp.where(work == mx, row, float(N_EXPERTS)), axis=0, keepdims=True)
            hot = row == idx
            work = jnp.where(hot, -jnp.inf, work)
            vals.append(mx)
            idxs.append(idx)
            hots.append(hot)
        exps = [jnp.exp(vk - vals[0]) for vk in vals]
        den = exps[0] + exps[1] + exps[2] + exps[3]
        chosen = jnp.zeros((N_EXPERTS, tb), F32)
        for hot in hots:
            chosen = chosen + jnp.where(hot, 1.0, 0.0)
        counts = carry_ref[...]
        before = []
        for j in range(sub):
            cj = chosen[:, j * LANES:(j + 1) * LANES]
            before.append(jnp.dot(cj.astype(BF16), earlier, preferred_element_type=F32) + counts)
            counts = counts + jnp.sum(cj, axis=1, keepdims=True)
        carry_ref[...] = counts
        before = jnp.concatenate(before, axis=1)
        for k in range(TOP_K):
            rank = jnp.sum(jnp.where(hots[k], before, 0.0), axis=0, keepdims=True)
            rit_ref[k:k + 1, pl.ds(c0, tb)] = idxs[k].astype(I32)
            rit_ref[TOP_K + k:TOP_K + k + 1, pl.ds(c0, tb)] = rank.astype(I32)
        gates = jnp.concatenate([ek / den for ek in exps], axis=0)
        for j in range(sub):
            g_tile = jnp.concatenate([gates[:, j * LANES:(j + 1) * LANES],
                                      jnp.zeros((LANES - TOP_K, LANES), F32)], axis=0)
            rg_ref[pl.ds(pl.multiple_of(c0 + j * LANES, LANES), LANES), :] = jnp.transpose(g_tile)
        return loop_carry

    lax.fori_loop(0, T // tb, tile, 0)
    cnt_copy = pltpu.make_async_copy(carry_ref, cnt_ref, sem)
    cnt_copy.start()
    cnt_copy.wait()

    def clear(j, carry):
        tab_ref[j] = 0
        return carry

    lax.fori_loop(0, TAB_SIZE, clear, 0)

    def expert(e, carry):
        acc, item, used = carry
        n = lax.shift_right_logical(cnt_ref[e, 0].astype(I32) + (MOE_RB - 1), rb_shift)
        gs_ref[e] = acc

        def add_item(local, it):
            tab_ref[TAB_E + it] = e
            tab_ref[TAB_ROW0 + it] = acc + local * MOE_RMAX
            tab_ref[TAB_NBLK + it] = jnp.minimum(MOE_BMAX, n - local * MOE_BMAX)
            return it + 1

        item = lax.fori_loop(0, lax.div(n + (MOE_BMAX - 1), MOE_BMAX), add_item, item)
        return acc + n * MOE_RB, item, used + n

    _, n_items, used = lax.fori_loop(0, N_EXPERTS, expert, (jnp.int32(0), jnp.int32(0), jnp.int32(0)))
    tab_ref[TAB_USED] = used
    e_last = tab_ref[TAB_E + n_items - 1]

    def pad_item(it, carry):
        tab_ref[TAB_E + it] = e_last
        tab_ref[TAB_ROW0 + it] = 0
        tab_ref[TAB_NBLK + it] = 0
        return carry

    lax.fori_loop(n_items, MOE_NI + 1, pad_item, 0)

    def wait_units(it, prev):
        nb = tab_ref[TAB_NBLK + it]
        tab_ref[TAB_WAIT + it] = jnp.maximum(nb, prev)
        return nb

    lax.fori_loop(0, MOE_NI + 1, wait_units, jnp.int32(0))

    e_idx = rit_ref[0:TOP_K, :]
    d = rit_ref[TOP_K:2 * TOP_K, :]
    for e in range(N_EXPERTS):
        d = d + jnp.where(e_idx == e, gs_ref[e], 0)
    dvm_ref[0:TOP_K, :] = d
    zvm_ref[...] = jnp.zeros_like(zvm_ref)
    copies = [pltpu.make_async_copy(dvm_ref.at[k], dest_ref.at[pl.ds(k * T, T)], sem) for k in range(TOP_K)]
    copies.append(pltpu.make_async_copy(zvm_ref, tok_ref, sem))
    for c in copies:
        c.start()
    for c in copies:
        c.wait()

    for k in range(TOP_K):
        def scatter(c, carry):
            for j in range(GATHER_UNROLL):
                t = c * GATHER_UNROLL + j
                tok_ref[dest_ref[k * T + t]] = t
            return carry

        lax.fori_loop(0, T // GATHER_UNROLL, scatter, 0)


def _route_call(logits_t):
    smem = pl.BlockSpec(memory_space=pltpu.SMEM)
    vmem = pl.BlockSpec(memory_space=pltpu.VMEM)
    return pl.pallas_call(
        _route_kernel,
        in_specs=[vmem],
        out_specs=[smem, smem, smem, vmem],
        out_shape=[
            jax.ShapeDtypeStruct((TOP_K * T,), I32),
            jax.ShapeDtypeStruct((N_TOK_TAB,), I32),
            jax.ShapeDtypeStruct((TAB_SIZE,), I32),
            jax.ShapeDtypeStruct((T, LANES), F32),
        ],
        scratch_shapes=[
            pltpu.VMEM((2 * TOP_K, T), I32),
            pltpu.VMEM((2 * TOP_K, T), I32),
            pltpu.VMEM((N_TOK_TAB,), I32),
            pltpu.VMEM((N_EXPERTS, LANES), F32),
            pltpu.SMEM((N_EXPERTS, LANES), F32),
            pltpu.SMEM((N_EXPERTS,), I32),
            pltpu.SemaphoreType.DMA(()),
        ],
        compiler_params=pltpu.CompilerParams(vmem_limit_bytes=32 * MIB),
        name="route",
    )(logits_t)


def kernel(x_prompt, x_sample, state_conv, p_prompt, p_sample, g_mix, w_in, ln_v_g, ln_v_b, w_s, b_s,
           conv_w, conv_b, w_proj_a, w_proj_b, w_o, g_moe, w_router, b_router, w_gate_up, b_gate_up,
           w_down, b_down, g_ple, w_ple, w_ple_gate, g_final):
    assert g_mix.shape[0] == 1, "one layer"
    xs = x_sample.reshape(T_S, D_MODEL)
    xp = x_prompt.reshape(T_P, D_MODEL)

    tril = jnp.tril(jnp.ones((CHUNK, CHUNK), bool))
    w_prompt = jnp.where(tril[None], w_s[0], 0.0)
    small = jnp.where(tril[None, :DEC_SEQ, :DEC_SEQ], w_s[0, :, :DEC_SEQ, :DEC_SEQ], 0.0)
    reps = CHUNK // DEC_SEQ
    blockdiag = jnp.kron(jnp.eye(reps, dtype=F32), jnp.ones((DEC_SEQ, DEC_SEQ), F32))
    w_sample = jnp.tile(small, (1, reps, reps)) * blockdiag[None]
    wsg = jnp.stack([w_sample, w_prompt]).astype(BF16)
    bias_p = jnp.repeat(b_s[0].T, GW_A, axis=1)
    bias_s = jnp.tile(jnp.repeat(b_s[0, :, :DEC_SEQ].T, GW_A, axis=1), (reps, 1))
    bsg = jnp.stack([bias_s, bias_p])
    ext = jnp.pad(state_conv[0], ((0, 0), (0, DEC_SEQ - (CONV_W - 1)), (0, 0))).reshape(T_S, D_B)

    wr = jnp.pad(w_router[0], ((0, 0), (0, LANES - N_EXPERTS)))
    wr_hi = wr.astype(BF16)
    wr_lo = (wr - wr_hi.astype(F32)).astype(BF16)
    b_r = jnp.pad(b_router[0], (0, LANES - N_EXPERTS), constant_values=NEG_BIG).reshape(1, LANES)

    h = _norm_call(xs, xp, g_mix)
    z, vpre = _in_proj_call(h, w_in[0])
    x1, xn, logits_t, vln, cxs, tail = _mixer_call(
        xs, xp, z, vpre, ext, wsg, bsg, ln_v_g, ln_v_b, conv_w[0], conv_b,
        w_proj_a[0].astype(BF16), w_proj_b[0].astype(BF16), w_o[0].astype(BF16), g_moe,
        wr_hi, wr_lo, b_r)

    dest, slot_tok, tables, route_g = _route_call(logits_t)
    y_sorted = _moe_call(slot_tok, tables, xn, w_gate_up[0], w_down[0],
                         b_gate_up[0].reshape(N_EXPERTS, 2 * MOE_P1, MOE_TF),
                         b_down[0].reshape(N_EXPERTS, MOE_P2, MOE_TN))
    ys, yp = _combine_call(dest, y_sorted, x1, route_g,
                           p_sample[0].reshape(T_S, PLE_DIM), p_prompt[0].reshape(T_P, PLE_DIM),
                           w_ple[0].astype(BF16), w_ple_gate[0].astype(BF16), g_ple, g_final.reshape(1, D_MODEL))

    y_prompt = yp.reshape(BATCH, SEQ, D_MODEL)
    y_sample = ys.reshape(DEC_BATCH, DEC_SEQ, D_MODEL)
    last = tail[:MIX_NT - MIX_NS].reshape(BATCH, MIX_SEQ_TILES, SUBLANES, D_B)[:, -1, SUBLANES - (CONV_W - 1):, :]
    state_conv_prompt = last[None]
    state_conv_sample = cxs[:T_S].reshape(DEC_BATCH, DEC_SEQ, D_B)[:, DEC_SEQ - (CONV_W - 1):, :][None]
    state_chunk_v_sample = vln[:T_S].reshape(DEC_BATCH, DEC_SEQ, D_A)[None]
    return (y_prompt, y_sample, state_conv_prompt, state_conv_sample, state_chunk_v_sample)
```
